```python
import jax, jax.numpy as jnp
from jax import lax
import numpy as np

D_MODEL = 1024
BATCH = 2
SEQ = 16384
DEPTH = 1
DEC_BATCH = 4
DEC_SEQ = 4096
PAST_LEN = 128

HEAD_DIM = 64
N_Q_HEADS = 16
N_KV_HEADS = 4
GQA_GROUP = N_Q_HEADS // N_KV_HEADS
WINDOW = 128
ATTN_BLOCK = 128
ROPE_THETA = 10000.0
CONV_DIM = D_MODEL
CONV_WIDTH = 31
CONV_PAD = CONV_WIDTH // 2
N_EXPERTS = 32
TOP_K = 4
D_FF = D_MODEL
SWIGLU_LIMIT = 7.0
SWIGLU_ALPHA = 1.702
MOE_BLOCK = 256
NORM_EPS = 1e-5

Q_COLS = N_Q_HEADS * HEAD_DIM
KV_COLS = N_KV_HEADS * HEAD_DIM
GLU_COLS = 2 * CONV_DIM
GATE_COLS = 2 * D_MODEL
IN_COLS = Q_COLS + 2 * KV_COLS + GLU_COLS + GATE_COLS

kernel_name = "hybrid_swa_conformer_moe_encoder"


def rmsnorm(x, g):
    xf = x.astype(jnp.float32)
    y = xf * lax.rsqrt(jnp.mean(xf * xf, axis=-1, keepdims=True) + NORM_EPS) * g.astype(jnp.float32)
    return y.astype(x.dtype)


def layernorm(x, g, b):
    xf = x.astype(jnp.float32)
    mu = jnp.mean(xf, axis=-1, keepdims=True)
    var = jnp.mean(jnp.square(xf - mu), axis=-1, keepdims=True)
    y = (xf - mu) * lax.rsqrt(var + NORM_EPS) * g.astype(jnp.float32) + b.astype(jnp.float32)
    return y.astype(x.dtype)


def rope(t):
    S = t.shape[1]
    half = HEAD_DIM // 2
    inv_freq = 1.0 / (ROPE_THETA ** (jnp.arange(half, dtype=jnp.float32) * (2.0 / HEAD_DIM)))
    ang = jnp.arange(S, dtype=jnp.float32)[:, None] * inv_freq[None, :]
    cos = jnp.cos(ang)[None, :, None, :]
    sin = jnp.sin(ang)[None, :, None, :]
    tf = t.astype(jnp.float32)
    t1, t2 = tf[..., :half], tf[..., half:]
    out = jnp.concatenate([t1 * cos - t2 * sin, t2 * cos + t1 * sin], axis=-1)
    return out.astype(t.dtype)


def key_windows(t, n_blocks):
    B = t.shape[0]
    tp = jnp.pad(t, ((0, 0), (ATTN_BLOCK, ATTN_BLOCK), (0, 0), (0, 0)))
    tb = tp.reshape(B, n_blocks + 2, ATTN_BLOCK, N_KV_HEADS, HEAD_DIM)
    return jnp.concatenate([tb[:, :-2], tb[:, 1:-1], tb[:, 2:]], axis=2)


def windowed_gqa(q, k, v, sink):
    B, S = q.shape[0], q.shape[1]
    nb = S // ATTN_BLOCK
    qb = q.reshape(B, nb, ATTN_BLOCK, N_KV_HEADS, GQA_GROUP, HEAD_DIM)
    kw = key_windows(k, nb)
    vw = key_windows(v, nb)
    scale = HEAD_DIM ** -0.5
    s = jnp.einsum('bnqhgd,bnkhd->bnhgqk', qb, kw,
                   preferred_element_type=jnp.float32) * scale
    blk = jnp.arange(nb)[:, None, None]
    q_pos = blk * ATTN_BLOCK + jnp.arange(ATTN_BLOCK)[None, :, None]
    k_pos = (blk - 1) * ATTN_BLOCK + jnp.arange(3 * ATTN_BLOCK)[None, None, :]
    mask = (jnp.abs(k_pos - q_pos) <= WINDOW) & (k_pos >= 0) & (k_pos < S)
    s = jnp.where(mask[None, :, None, None], s, -1e30)
    sink_l = sink.astype(jnp.float32).reshape(N_KV_HEADS, GQA_GROUP)[None, None, :, :, None, None]
    m = jnp.maximum(jnp.max(s, axis=-1, keepdims=True), sink_l)
    p = jnp.exp(s - m)
    p = p / (jnp.sum(p, axis=-1, keepdims=True) + jnp.exp(sink_l - m))
    o = jnp.einsum('bnhgqk,bnkhd->bnqhgd', p.astype(v.dtype), vw)
    return o.reshape(B, S, N_Q_HEADS * HEAD_DIM)


def conformer_conv(u, dw_w, dw_b, ln_g, ln_b, w_pw2, b_pw2):
    a, g = jnp.split(u, 2, axis=-1)
    z = a * jax.nn.sigmoid(g)
    z = lax.conv_general_dilated(
        z, dw_w[:, None, :].astype(z.dtype), window_strides=(1,),
        padding=[(CONV_PAD, CONV_PAD)], dimension_numbers=('NWC', 'WIO', 'NWC'),
        feature_group_count=CONV_DIM) + dw_b
    z = layernorm(z, ln_g, ln_b)
    z = z * jax.nn.sigmoid(z)
    return z @ w_pw2 + b_pw2


def moe(h, w_router, b_router, w_gu, b_gu, w_down, b_down):
    B, S, D = h.shape
    T = B * S
    A = T * TOP_K
    hf = h.reshape(T, D)
    logits = (hf @ w_router + b_router).astype(jnp.float32)
    top_vals, top_idx = lax.top_k(logits, TOP_K)
    gates = jax.nn.softmax(top_vals, axis=-1)
    flat_e = top_idx.reshape(A)
    flat_tok = jnp.repeat(jnp.arange(T, dtype=jnp.int32), TOP_K)
    order = jnp.argsort(flat_e, stable=True)
    sorted_e = flat_e[order]
    sorted_tok = flat_tok[order]
    sorted_gate = gates.reshape(A)[order]
    counts = jnp.bincount(flat_e, length=N_EXPERTS)
    padded = ((counts + MOE_BLOCK - 1) // MOE_BLOCK) * MOE_BLOCK
    pad_end = jnp.cumsum(padded)
    pad_start = pad_end - padded
    start = jnp.cumsum(counts) - counts
    dest = pad_start[sorted_e] + (jnp.arange(A) - start[sorted_e])
    n_blocks = -(-A // MOE_BLOCK) + N_EXPERTS
    P = n_blocks * MOE_BLOCK
    slot_tok = jnp.full((P,), T, jnp.int32).at[dest].set(sorted_tok)
    slot_gate = jnp.zeros((P,), jnp.float32).at[dest].set(sorted_gate)
    block_exp = jnp.minimum(
        jnp.searchsorted(pad_end, jnp.arange(n_blocks) * MOE_BLOCK, side='right'),
        N_EXPERTS - 1)
    h_pad = jnp.concatenate([hf, jnp.zeros((1, D), hf.dtype)], axis=0)

    def expert_block(args):
        tok, gate, e = args
        xb = jnp.take(h_pad, tok, axis=0)
        gu = xb @ w_gu[e] + b_gu[e]
        g_, u_ = jnp.split(gu, 2, axis=-1)
        g_ = jnp.minimum(g_, SWIGLU_LIMIT)
        u_ = jnp.clip(u_, -SWIGLU_LIMIT, SWIGLU_LIMIT)
        act = g_ * jax.nn.sigmoid(SWIGLU_ALPHA * g_) * (u_ + 1.0)
        out = act @ w_down[e] + b_down[e]
        return out * gate[:, None].astype(out.dtype)

    outs = lax.map(expert_block, (slot_tok.reshape(n_blocks, MOE_BLOCK),
                                  slot_gate.reshape(n_blocks, MOE_BLOCK), block_exp))
    y = jax.ops.segment_sum(outs.reshape(P, D), slot_tok, num_segments=T + 1)[:T]
    return y.reshape(B, S, D)


def encoder_layer(x, norm_mix_g, w_in, b_in, attn_sink, w_o_attn, conv_dw_w, conv_dw_b,
                  conv_ln_g, conv_ln_b, w_pw2, b_pw2, w_out, norm_ffn_g, w_router, b_router,
                  w_gu, b_gu, w_down, b_down):
    B, S, _ = x.shape
    h = rmsnorm(x, norm_mix_g)
    z = h @ w_in + b_in
    o1 = Q_COLS
    o2 = o1 + KV_COLS
    o3 = o2 + KV_COLS
    o4 = o3 + GLU_COLS
    q = rope(z[..., :o1].reshape(B, S, N_Q_HEADS, HEAD_DIM))
    k = rope(z[..., o1:o2].reshape(B, S, N_KV_HEADS, HEAD_DIM))
    v = z[..., o2:o3].reshape(B, S, N_KV_HEADS, HEAD_DIM)
    glu_in = z[..., o3:o4]
    g_attn, g_conv = jnp.split(jax.nn.sigmoid(z[..., o4:]), 2, axis=-1)
    attn_out = windowed_gqa(q, k, v, attn_sink) @ w_o_attn
    conv_out = conformer_conv(glu_in, conv_dw_w, conv_dw_b, conv_ln_g, conv_ln_b, w_pw2, b_pw2)
    x = x + (g_attn * attn_out + g_conv * conv_out) @ w_out
    x = x + moe(rmsnorm(x, norm_ffn_g), w_router, b_router, w_gu, b_gu, w_down, b_down)
    return x


def setup_inputs(seed: int = 0) -> dict:
    key = jax.random.key(seed)
    ks = jax.random.split(key, 24)
    f32 = jnp.float32

    def nrm(k, shape, scale):
        return jax.random.normal(k, shape, f32) * scale

    return {
        "x_prompt": nrm(ks[0], (BATCH, SEQ, D_MODEL), 1.0),
        "x_sample": nrm(ks[1], (DEC_BATCH, DEC_SEQ, D_MODEL), 1.0),
        "norm_mix_g": 1.0 + nrm(ks[2], (DEPTH, D_MODEL), 0.02),
        "w_in": nrm(ks[3], (DEPTH, D_MODEL, IN_COLS), D_MODEL ** -0.5),
        "b_in": nrm(ks[4], (DEPTH, IN_COLS), 0.02),
        "attn_sink": nrm(ks[5], (DEPTH, N_Q_HEADS), 0.5),
        "w_o_attn": nrm(ks[6], (DEPTH, Q_COLS, D_MODEL), Q_COLS ** -0.5),
        "conv_dw_w": nrm(ks[7], (DEPTH, CONV_WIDTH, CONV_DIM), CONV_WIDTH ** -0.5),
        "conv_dw_b": nrm(ks[8], (DEPTH, CONV_DIM), 0.02),
        "conv_ln_g": 1.0 + nrm(ks[9], (DEPTH, CONV_DIM), 0.02),
        "conv_ln_b": nrm(ks[10], (DEPTH, CONV_DIM), 0.02),
        "w_pw2": nrm(ks[11], (DEPTH, CONV_DIM, D_MODEL), CONV_DIM ** -0.5),
        "b_pw2": nrm(ks[12], (DEPTH, D_MODEL), 0.02),
        "w_out": nrm(ks[13], (DEPTH, D_MODEL, D_MODEL), D_MODEL ** -0.5),
        "norm_ffn_g": 1.0 + nrm(ks[14], (DEPTH, D_MODEL), 0.02),
        "w_router": nrm(ks[15], (DEPTH, D_MODEL, N_EXPERTS), D_MODEL ** -0.5),
        "b_router": nrm(ks[16], (DEPTH, N_EXPERTS), 0.01),
        "w_gu": nrm(ks[17], (DEPTH, N_EXPERTS, D_MODEL, 2 * D_FF), D_MODEL ** -0.5),
        "b_gu": nrm(ks[18], (DEPTH, N_EXPERTS, 2 * D_FF), 0.02),
        "w_down": nrm(ks[19], (DEPTH, N_EXPERTS, D_FF, D_MODEL), D_FF ** -0.5),
        "b_down": nrm(ks[20], (DEPTH, N_EXPERTS, D_MODEL), 0.02),
        "norm_final_g": 1.0 + nrm(ks[21], (D_MODEL,), 0.02),
    }


def reference(x_prompt, x_sample, norm_mix_g, w_in, b_in, attn_sink, w_o_attn, conv_dw_w,
              conv_dw_b, conv_ln_g, conv_ln_b, w_pw2, b_pw2, w_out, norm_ffn_g, w_router,
              b_router, w_gu, b_gu, w_down, b_down, norm_final_g):
    def trunk(x):
        for l in range(DEPTH):
            x = encoder_layer(x, norm_mix_g[l], w_in[l], b_in[l], attn_sink[l], w_o_attn[l],
                              conv_dw_w[l], conv_dw_b[l], conv_ln_g[l], conv_ln_b[l], w_pw2[l],
                              b_pw2[l], w_out[l], norm_ffn_g[l], w_router[l], b_router[l],
                              w_gu[l], b_gu[l], w_down[l], b_down[l])
        return rmsnorm(x, norm_final_g)

    y_prompt = trunk(x_prompt)
    y_sample = trunk(x_sample)
    return (y_prompt, y_sample)
```

```python
import functools
from typing import NamedTuple

import numpy as np
import jax
import jax.numpy as jnp
from jax import lax
from jax.experimental import pallas as pl
from jax.experimental.pallas import tpu as pltpu

F32 = jnp.float32
BF16 = jnp.bfloat16
I32 = jnp.int32

D_MODEL = 1024
HEAD_DIM = 64
N_Q_HEADS = 16
N_KV_HEADS = 4
WINDOW = 128
ATTN_BLOCK = 128
ROPE_THETA = 10000.0
CONV_WIDTH = 31
CONV_PAD = CONV_WIDTH // 2
N_EXPERTS = 32
TOP_K = 4
D_FF = D_MODEL
SWIGLU_LIMIT = 7.0
SWIGLU_ALPHA = 1.702
NORM_EPS = 1e-5
NEG_INF = -1e30

Q_COLS = N_Q_HEADS * HEAD_DIM
KV_COLS = N_KV_HEADS * HEAD_DIM
LANES = 128
SUBLANES = 8
KV_DUP_COLS = N_KV_HEADS * LANES
HALO = 16
VMEM_LIMIT = 56 * 1024 * 1024


class Geo(NamedTuple):
    n_a: int
    len_a: int
    n_b: int
    len_b: int

    @property
    def rows_a(self):
        return self.n_a * self.len_a

    @property
    def total(self):
        return self.rows_a + self.n_b * self.len_b


def _seq_bounds(geo, r):
    in_a = r < geo.rows_a
    start_a = (r // geo.len_a) * geo.len_a
    start_b = geo.rows_a + ((r - geo.rows_a) // geo.len_b) * geo.len_b
    start = jnp.where(in_a, start_a, start_b)
    end = start + jnp.where(in_a, geo.len_a, geo.len_b)
    return start, end


def _params(*sem):
    return pltpu.CompilerParams(dimension_semantics=sem, vmem_limit_bytes=VMEM_LIMIT)


_C_Q = 0
_C_K = _C_Q + Q_COLS
_C_V = _C_K + KV_DUP_COLS
_C_GA = _C_V + KV_DUP_COLS
_C_GG = _C_GA + D_MODEL
_C_GATE = _C_GG + D_MODEL
_N_IN = _C_GATE + 2 * D_MODEL
_PROJ_CHUNK = 512


def _in_proj_body(x_ref, g_ref, w_ref, b_ref, cos_ref, sin_ref,
                  q_ref, kd_ref, vd_ref, glu_ref, gate_ref):
    x = x_ref[...]
    ms = jnp.mean(x * x, axis=-1, keepdims=True)
    h = (x * lax.rsqrt(ms + NORM_EPS) * g_ref[...]).astype(BF16)
    cos = cos_ref[...]
    sin = sin_ref[...]

    def proj(c0):
        return (jnp.dot(h, w_ref[:, c0:c0 + _PROJ_CHUNK], preferred_element_type=F32)
                + b_ref[:, c0:c0 + _PROJ_CHUNK])

    def rope_store(z, out_ref, o0, scale):
        for c in range(_PROJ_CHUNK // LANES):
            zc = z[:, c * LANES:(c + 1) * LANES]
            r = zc * cos + pltpu.roll(zc, LANES // 2, 1) * sin
            if scale != 1.0:
                r = r * scale
            out_ref[:, o0 + c * LANES:o0 + (c + 1) * LANES] = r.astype(out_ref.dtype)

    for c in range(Q_COLS // _PROJ_CHUNK):
        rope_store(proj(_C_Q + c * _PROJ_CHUNK), q_ref, c * _PROJ_CHUNK, HEAD_DIM ** -0.5)
    for c in range(KV_DUP_COLS // _PROJ_CHUNK):
        rope_store(proj(_C_K + c * _PROJ_CHUNK), kd_ref, c * _PROJ_CHUNK, 1.0)
    for c in range(KV_DUP_COLS // _PROJ_CHUNK):
        vd_ref[:, c * _PROJ_CHUNK:(c + 1) * _PROJ_CHUNK] = proj(_C_V + c * _PROJ_CHUNK).astype(BF16)
    for c in range(D_MODEL // _PROJ_CHUNK):
        a = proj(_C_GA + c * _PROJ_CHUNK)
        g = proj(_C_GG + c * _PROJ_CHUNK)
        glu_ref[:, c * _PROJ_CHUNK:(c + 1) * _PROJ_CHUNK] = a * jax.nn.sigmoid(g)
    for c in range(2 * D_MODEL // _PROJ_CHUNK):
        gate_ref[:, c * _PROJ_CHUNK:(c + 1) * _PROJ_CHUNK] = jax.nn.sigmoid(
            proj(_C_GATE + c * _PROJ_CHUNK)).astype(BF16)


def _in_proj(x, g_mix, w_perm, b_perm, cos_t, sin_t, geo, tm):
    T = x.shape[0]

    def pos_map(i):
        r0 = i * tm
        start, _ = _seq_bounds(geo, r0)
        return ((r0 - start) // tm, 0)

    const = lambda i: (0, 0)
    row = lambda i: (i, 0)
    return pl.pallas_call(
        _in_proj_body,
        grid=(T // tm,),
        in_specs=[
            pl.BlockSpec((tm, D_MODEL), row),
            pl.BlockSpec((1, D_MODEL), const),
            pl.BlockSpec((D_MODEL, _N_IN), const, pipeline_mode=pl.Buffered(1)),
            pl.BlockSpec((1, _N_IN), const),
            pl.BlockSpec((tm, LANES), pos_map),
            pl.BlockSpec((tm, LANES), pos_map),
        ],
        out_specs=[
            pl.BlockSpec((tm, Q_COLS), row),
            pl.BlockSpec((tm, KV_DUP_COLS), row),
            pl.BlockSpec((tm, KV_DUP_COLS), row),
            pl.BlockSpec((tm, D_MODEL), row),
            pl.BlockSpec((tm, 2 * D_MODEL), row),
        ],
        out_shape=[
            jax.ShapeDtypeStruct((T, Q_COLS), BF16),
            jax.ShapeDtypeStruct((T, KV_DUP_COLS), BF16),
            jax.ShapeDtypeStruct((T, KV_DUP_COLS), BF16),
            jax.ShapeDtypeStruct((T, D_MODEL), F32),
            jax.ShapeDtypeStruct((T, 2 * D_MODEL), BF16),
        ],
        compiler_params=_params("parallel"),
        name="in_proj",
    )(x, g_mix, w_perm, b_perm, cos_t, sin_t)


def _attn_body(geo, sink_ref, q_ref, kp_ref, kc_ref, kn_ref, vp_ref, vc_ref, vn_ref, o_ref):
    i = pl.program_id(0)
    r0 = i * ATTN_BLOCK
    start, end = _seq_bounds(geo, r0)
    has_prev = r0 > start
    has_next = r0 + ATTN_BLOCK < end

    nk = 3 * ATTN_BLOCK
    row = lax.broadcasted_iota(I32, (ATTN_BLOCK, nk), 0)
    col = lax.broadcasted_iota(I32, (ATTN_BLOCK, nk), 1)
    valid = (col >= row) & (col <= row + 2 * WINDOW)
    valid = valid & (has_prev | (col >= ATTN_BLOCK)) & (has_next | (col < 2 * ATTN_BLOCK))
    group = N_Q_HEADS // N_KV_HEADS
    valid = jnp.concatenate([valid] * group, axis=0)

    lane = lax.broadcasted_iota(I32, (ATTN_BLOCK, LANES), 1)
    even_head = (lane % HEAD_DIM) < (HEAD_DIM // 2)
    low_half = lane < HEAD_DIM

    for g in range(N_KV_HEADS):
        ls = slice(g * LANES, (g + 1) * LANES)
        qa = q_ref[:, (2 * g) * LANES:(2 * g + 1) * LANES]
        qb = q_ref[:, (2 * g + 1) * LANES:(2 * g + 2) * LANES]
        zero = jnp.zeros_like(qa)
        q4 = jnp.concatenate([jnp.where(even_head, qa, zero), jnp.where(even_head, zero, qa),
                              jnp.where(even_head, qb, zero), jnp.where(even_head, zero, qb)], axis=0)
        k = jnp.concatenate([kp_ref[:, ls], kc_ref[:, ls], kn_ref[:, ls]], axis=0)
        v = jnp.concatenate([vp_ref[:, ls], vc_ref[:, ls], vn_ref[:, ls]], axis=0)
        s = lax.dot_general(q4, k, (((1,), (1,)), ((), ())), preferred_element_type=F32)
        s = jnp.where(valid, s, NEG_INF)
        sink = jnp.concatenate(
            [jnp.full((ATTN_BLOCK, 1), sink_ref[group * g + h], F32) for h in range(group)], axis=0)
        m = jnp.maximum(jnp.max(s, axis=-1, keepdims=True), sink)
        p = jnp.exp(s - m)
        denom = jnp.sum(p, axis=-1, keepdims=True) + jnp.exp(sink - m)
        o = jnp.dot(p.astype(BF16), v, preferred_element_type=F32) / denom
        b = ATTN_BLOCK
        o_ref[:, (2 * g) * LANES:(2 * g + 1) * LANES] = jnp.where(
            low_half, o[0:b], o[b:2 * b]).astype(BF16)
        o_ref[:, (2 * g + 1) * LANES:(2 * g + 2) * LANES] = jnp.where(
            low_half, o[2 * b:3 * b], o[3 * b:4 * b]).astype(BF16)


def _attention(q, kd, vd, sink, geo):
    T = q.shape[0]
    nb = T // ATTN_BLOCK
    cur = lambda i: (i, 0)
    prev = lambda i: (jnp.maximum(i - 1, 0), 0)
    nxt = lambda i: (jnp.minimum(i + 1, nb - 1), 0)
    kv_spec = lambda m: pl.BlockSpec((ATTN_BLOCK, KV_DUP_COLS), m)
    return pl.pallas_call(
        functools.partial(_attn_body, geo),
        grid=(nb,),
        in_specs=[
            pl.BlockSpec(memory_space=pltpu.SMEM),
            pl.BlockSpec((ATTN_BLOCK, Q_COLS), cur),
            kv_spec(prev), kv_spec(cur), kv_spec(nxt),
            kv_spec(prev), kv_spec(cur), kv_spec(nxt),
        ],
        out_specs=pl.BlockSpec((ATTN_BLOCK, Q_COLS), cur),
        out_shape=jax.ShapeDtypeStruct((T, Q_COLS), BF16),
        compiler_params=_params("parallel"),
        name="window_attn",
    )(sink, q, kd, kd, kd, vd, vd, vd)


_CONV_ROWS = 16


def _conv_body(geo, tc, z_ref, zp_ref, zn_ref, w_ref, dwb_ref, lng_ref, lnb_ref, o_ref, sh_ref):
    i = pl.program_id(0)
    r0 = i * tc
    start, end = _seq_bounds(geo, r0)
    has_prev = (r0 > start).astype(F32)
    has_next = (r0 + tc < end).astype(F32)
    rows = tc + 2 * HALO
    keep = rows - SUBLANES

    sh_ref[0, 0:HALO, :] = zp_ref[...] * has_prev
    sh_ref[0, HALO:HALO + tc, :] = z_ref[...]
    sh_ref[0, HALO + tc:rows, :] = zn_ref[...] * has_next
    for r in range(1, SUBLANES):
        for c in range(D_MODEL // LANES):
            ls = slice(c * LANES, (c + 1) * LANES)
            sh_ref[r, 0:keep, ls] = sh_ref[0, r:r + keep, ls]

    n_lane = D_MODEL // LANES

    def step(j, carry):
        s0 = pl.multiple_of(j * _CONV_ROWS, _CONV_ROWS)
        acc = [jnp.zeros((_CONV_ROWS, LANES), F32) for _ in range(n_lane)]
        for k in range(CONV_WIDTH):
            off = HALO - CONV_PAD + k
            a, r = off // SUBLANES, off % SUBLANES
            for c in range(n_lane):
                ls = slice(c * LANES, (c + 1) * LANES)
                zt = sh_ref[r, pl.ds(s0 + a * SUBLANES, _CONV_ROWS), ls]
                wt = w_ref[k * SUBLANES:(k + 1) * SUBLANES, ls]
                acc[c] = acc[c] + zt * jnp.concatenate([wt] * (_CONV_ROWS // SUBLANES), axis=0)
        y = jnp.concatenate(acc, axis=1) + dwb_ref[...]
        mu = jnp.mean(y, axis=-1, keepdims=True)
        yc = y - mu
        var = jnp.mean(yc * yc, axis=-1, keepdims=True)
        yn = yc * lax.rsqrt(var + NORM_EPS) * lng_ref[...] + lnb_ref[...]
        o_ref[pl.ds(s0, _CONV_ROWS), :] = (yn * jax.nn.sigmoid(yn)).astype(BF16)
        return carry

    lax.fori_loop(0, tc // _CONV_ROWS, step, 0)


def _conv_branch(glu, w_rep, dw_b, ln_g, ln_b, geo, tc):
    T = glu.shape[0]
    nh = T // HALO
    per = tc // HALO
    const = lambda i: (0, 0)
    return pl.pallas_call(
        functools.partial(_conv_body, geo, tc),
        grid=(T // tc,),
        in_specs=[
            pl.BlockSpec((tc, D_MODEL), lambda i: (i, 0)),
            pl.BlockSpec((HALO, D_MODEL), lambda i: (jnp.maximum(i * per - 1, 0), 0)),
            pl.BlockSpec((HALO, D_MODEL), lambda i: (jnp.minimum((i + 1) * per, nh - 1), 0)),
            pl.BlockSpec((CONV_WIDTH * SUBLANES, D_MODEL), const),
            pl.BlockSpec((1, D_MODEL), const),
            pl.BlockSpec((1, D_MODEL), const),
            pl.BlockSpec((1, D_MODEL), const),
        ],
        out_specs=pl.BlockSpec((tc, D_MODEL), lambda i: (i, 0)),
        out_shape=jax.ShapeDtypeStruct((T, D_MODEL), BF16),
        scratch_shapes=[pltpu.VMEM((SUBLANES, tc + 2 * HALO, D_MODEL), F32)],
        compiler_params=_params("parallel"),
        name="conv_branch",
    )(glu, glu, glu, w_rep, dw_b, ln_g, ln_b)


def _mix_body(x_ref, o_ref, c_ref, gate_ref, wo_ref, wpw_ref, bpw_ref, wout_ref, gffn_ref,
              wrh_ref, wrl_ref, br_ref, tri_ref,
              x1_ref, h2_ref, idx_ref, gt_ref, rank_ref, cnt_ref, carry_ref):
    tm = x_ref.shape[0]

    @pl.when(pl.program_id(0) == 0)
    def _():
        carry_ref[...] = jnp.zeros_like(carry_ref)

    attn = jnp.dot(o_ref[...], wo_ref[...], preferred_element_type=F32)
    conv = jnp.dot(c_ref[...], wpw_ref[...], preferred_element_type=F32) + bpw_ref[...]
    g_attn = gate_ref[:, 0:D_MODEL].astype(F32)
    g_conv = gate_ref[:, D_MODEL:2 * D_MODEL].astype(F32)
    mix = (g_attn * attn + g_conv * conv).astype(BF16)
    x1 = x_ref[...] + jnp.dot(mix, wout_ref[...], preferred_element_type=F32)
    x1_ref[...] = x1
    ms = jnp.mean(x1 * x1, axis=-1, keepdims=True)
    h2 = x1 * lax.rsqrt(ms + NORM_EPS) * gffn_ref[...]
    h2_ref[...] = h2

    h_hi = h2.astype(BF16)
    h_lo = (h2 - h_hi.astype(F32)).astype(BF16)
    nt = (((1,), (1,)), ((), ()))
    logits = (lax.dot_general(wrh_ref[...], h_hi, nt, preferred_element_type=F32)
              + lax.dot_general(wrh_ref[...], h_lo, nt, preferred_element_type=F32)
              + lax.dot_general(wrl_ref[...], h_hi, nt, preferred_element_type=F32)
              + br_ref[...])

    eidx = lax.broadcasted_iota(I32, (N_EXPERTS, tm), 0)
    vals = logits
    picked, top_vals, top_idx = [], [], []
    for _ in range(TOP_K):
        m = jnp.max(vals, axis=0, keepdims=True)
        idx = jnp.min(jnp.where(vals == m, eidx, N_EXPERTS), axis=0, keepdims=True)
        sel = eidx == idx
        vals = jnp.where(sel, -jnp.inf, vals)
        picked.append(sel)
        top_vals.append(m)
        top_idx.append(idx)

    exps = [jnp.exp(v - top_vals[0]) for v in top_vals]
    tot = exps[0] + exps[1] + exps[2] + exps[3]
    onehot = (picked[0] | picked[1] | picked[2] | picked[3])
    prefix = jnp.dot(onehot.astype(BF16), tri_ref[...], preferred_element_type=F32) + carry_ref[...]
    for j in range(TOP_K):
        idx_ref[j:j + 1, :] = top_idx[j]
        gt_ref[j:j + 1, :] = exps[j] / tot
        rank_ref[j:j + 1, :] = jnp.sum(jnp.where(picked[j], prefix, 0.0), axis=0,
                                       keepdims=True).astype(I32)
    carry_ref[...] = carry_ref[...] + jnp.sum(onehot.astype(F32), axis=1, keepdims=True)
    cnt_ref[...] = jnp.broadcast_to(carry_ref[...], cnt_ref.shape)


def _mix_route(x, o, c, gates, wo, wpw, bpw, wout, gffn, wr_hi, wr_lo, br, tri, tm):
    T = x.shape[0]
    row = lambda i: (i, 0)
    col = lambda i: (0, i)
    const = lambda i: (0, 0)
    wspec = lambda shape: pl.BlockSpec(shape, const, pipeline_mode=pl.Buffered(1))
    return pl.pallas_call(
        _mix_body,
        grid=(T // tm,),
        in_specs=[
            pl.BlockSpec((tm, D_MODEL), row),
            pl.BlockSpec((tm, Q_COLS), row),
            pl.BlockSpec((tm, D_MODEL), row),
            pl.BlockSpec((tm, 2 * D_MODEL), row),
            wspec((Q_COLS, D_MODEL)),
            wspec((D_MODEL, D_MODEL)),
            pl.BlockSpec((1, D_MODEL), const),
            wspec((D_MODEL, D_MODEL)),
            pl.BlockSpec((1, D_MODEL), const),
            pl.BlockSpec((N_EXPERTS, D_MODEL), const),
            pl.BlockSpec((N_EXPERTS, D_MODEL), const),
            pl.BlockSpec((N_EXPERTS, 1), const),
            pl.BlockSpec((tm, tm), const),
        ],
        out_specs=[
            pl.BlockSpec((tm, D_MODEL), row),
            pl.BlockSpec((tm, D_MODEL), row),
            pl.BlockSpec((TOP_K, tm), col),
            pl.BlockSpec((TOP_K, tm), col),
            pl.BlockSpec((TOP_K, tm), col),
            pl.BlockSpec((N_EXPERTS, LANES), const),
        ],
        out_shape=[
            jax.ShapeDtypeStruct((T, D_MODEL), F32),
            jax.ShapeDtypeStruct((T, D_MODEL), F32),
            jax.ShapeDtypeStruct((TOP_K, T), I32),
            jax.ShapeDtypeStruct((TOP_K, T), F32),
            jax.ShapeDtypeStruct((TOP_K, T), I32),
            jax.ShapeDtypeStruct((N_EXPERTS, LANES), F32),
        ],
        scratch_shapes=[pltpu.VMEM((N_EXPERTS, 1), F32)],
        compiler_params=_params("arbitrary"),
        name="mix_route",
    )(x, o, c, gates, wo, wpw, bpw, wout, gffn, wr_hi, wr_lo, br, tri)


def _dispatch_body(td, dest_ref, h_hbm, xs_in, xs_hbm, sem):
    del xs_in
    base = pl.program_id(0) * td

    def copy(t, j):
        return pltpu.make_async_copy(h_hbm.at[pl.ds(base + t, 1)],
                                     xs_hbm.at[pl.ds(dest_ref[j, t], 1)], sem)

    def issue(t, carry):
        for j in range(TOP_K):
            copy(t, j).start()
        return carry

    def drain(t, carry):
        for j in range(TOP_K):
            copy(t, j).wait()
        return carry

    lax.fori_loop(0, td, issue, 0)
    lax.fori_loop(0, td, drain, 0)


def _dispatch(h2, dest, xs_init, td):
    T = h2.shape[0]
    return pl.pallas_call(
        functools.partial(_dispatch_body, td),
        grid=(T // td,),
        in_specs=[
            pl.BlockSpec((TOP_K, td), lambda i: (0, i), memory_space=pltpu.SMEM),
            pl.BlockSpec(memory_space=pl.ANY),
            pl.BlockSpec(memory_space=pl.ANY),
        ],
        out_specs=pl.BlockSpec(memory_space=pl.ANY),
        out_shape=jax.ShapeDtypeStruct(xs_init.shape, xs_init.dtype),
        scratch_shapes=[pltpu.SemaphoreType.DMA],
        input_output_aliases={2: 0},
        compiler_params=pltpu.CompilerParams(dimension_semantics=("arbitrary",),
                                             has_side_effects=True),
        name="dispatch",
    )(dest, h2, xs_init)


def _expert_body(bexp_ref, nused_ref, xs_ref, wgu_ref, bgu_ref, wd_ref, bd_ref, o_ref):
    n = pl.program_id(0)

    @pl.when(n < nused_ref[0])
    def _():
        x = xs_ref[...].astype(BF16)
        gu = jnp.dot(x, wgu_ref[...], preferred_element_type=F32) + bgu_ref[...]
        g = jnp.minimum(gu[:, 0:D_FF], SWIGLU_LIMIT)
        u = jnp.clip(gu[:, D_FF:2 * D_FF], -SWIGLU_LIMIT, SWIGLU_LIMIT)
        act = g * jax.nn.sigmoid(SWIGLU_ALPHA * g) * (u + 1.0)
        o_ref[...] = jnp.dot(act.astype(BF16), wd_ref[...], preferred_element_type=F32) + bd_ref[...]

    @pl.when(n >= nused_ref[0])
    def _():
        o_ref[...] = jnp.zeros_like(o_ref)


def _experts(block_exp, n_used, xs, wgu, bgu, wd, bd, bm):
    P = xs.shape[0]
    nblk = P // bm
    xmap = lambda n, be, nu: (jnp.minimum(n, nu[0] - 1), 0)
    emap = lambda n, be, nu: (be[n], 0, 0)
    grid_spec = pltpu.PrefetchScalarGridSpec(
        num_scalar_prefetch=2,
        grid=(nblk,),
        in_specs=[
            pl.BlockSpec((bm, D_MODEL), xmap),
            pl.BlockSpec((None, D_MODEL, 2 * D_FF), emap),
            pl.BlockSpec((None, 1, 2 * D_FF), emap),
            pl.BlockSpec((None, D_FF, D_MODEL), emap),
            pl.BlockSpec((None, 1, D_MODEL), emap),
        ],
        out_specs=pl.BlockSpec((bm, D_MODEL), lambda n, be, nu: (n, 0)),
    )
    return pl.pallas_call(
        _expert_body,
        grid_spec=grid_spec,
        out_shape=jax.ShapeDtypeStruct((P, D_MODEL), F32),
        compiler_params=_params("arbitrary"),
        name="experts",
    )(block_exp, n_used, xs, wgu, bgu, wd, bd)


def _combine_body(tf, dest_ref, x1_ref, gate_ref, gfin_ref, ys_hbm, y_ref, buf_ref, sem):
    def copy(t, j):
        return pltpu.make_async_copy(ys_hbm.at[pl.ds(dest_ref[j, t], 1)],
                                     buf_ref.at[j, pl.ds(t, 1)], sem)

    def issue(t, carry):
        for j in range(TOP_K):
            copy(t, j).start()
        return carry

    def drain(t, carry):
        for j in range(TOP_K):
            copy(t, j).wait()
        return carry

    lax.fori_loop(0, tf, issue, 0)
    lax.fori_loop(0, tf, drain, 0)

    gate = gate_ref[...]
    y = x1_ref[...]
    for j in range(TOP_K):
        y = y + gate[:, j:j + 1] * buf_ref[j]
    ms = jnp.mean(y * y, axis=-1, keepdims=True)
    y_ref[...] = y * lax.rsqrt(ms + NORM_EPS) * gfin_ref[...]


def _combine(dest, x1, gate_tok, gfin, ys, tf):
    T = x1.shape[0]
    row = lambda i: (i, 0)
    return pl.pallas_call(
        functools.partial(_combine_body, tf),
        grid=(T // tf,),
        in_specs=[
            pl.BlockSpec((TOP_K, tf), lambda i: (0, i), memory_space=pltpu.SMEM),
            pl.BlockSpec((tf, D_MODEL), row),
            pl.BlockSpec((tf, TOP_K), row),
            pl.BlockSpec((1, D_MODEL), lambda i: (0, 0)),
            pl.BlockSpec(memory_space=pl.ANY),
        ],
        out_specs=pl.BlockSpec((tf, D_MODEL), row),
        out_shape=jax.ShapeDtypeStruct((T, D_MODEL), F32),
        scratch_shapes=[pltpu.VMEM((TOP_K, tf, D_MODEL), F32), pltpu.SemaphoreType.DMA],
        compiler_params=_params("arbitrary"),
        name="combine",
    )(dest, x1, gate_tok, gfin, ys)


def _in_proj_columns():
    lane = np.arange(LANES)
    half = HEAD_DIM // 2
    dim_rot = (lane // HEAD_DIM) * half + (lane % half)
    q_cols = np.concatenate([(2 * p + (lane % HEAD_DIM) // half) * HEAD_DIM + dim_rot
                             for p in range(N_Q_HEADS // 2)])
    k_cols = np.concatenate([Q_COLS + g * HEAD_DIM + dim_rot for g in range(N_KV_HEADS)])
    v_cols = np.concatenate([Q_COLS + KV_COLS + g * HEAD_DIM + (lane % HEAD_DIM)
                             for g in range(N_KV_HEADS)])
    rest = np.arange(Q_COLS + 2 * KV_COLS, Q_COLS + 2 * KV_COLS + 4 * D_MODEL)
    return np.concatenate([q_cols, k_cols, v_cols, rest]).astype(np.int32)


def _rope_tables(n_pos):
    half = HEAD_DIM // 2
    inv_freq = 1.0 / (ROPE_THETA ** (jnp.arange(half, dtype=F32) * (2.0 / HEAD_DIM)))
    ang = jnp.arange(n_pos, dtype=F32)[:, None] * inv_freq[None, :]
    cos = jnp.tile(jnp.cos(ang), (1, LANES // half))
    sin = jnp.tile(jnp.sin(ang), (1, LANES // half))
    sign = jnp.where(jnp.arange(LANES) < LANES // 2, -1.0, 1.0).astype(F32)
    return cos, sin * sign[None, :]


def _tiles(geo):
    unit = min(geo.len_a, geo.len_b)
    tm = min(512, unit)
    return dict(tm=tm, tc=min(512, unit), td=min(512, unit), tf=min(256, unit), bm=256)


def kernel(x_prompt, x_sample, norm_mix_g, w_in, b_in, attn_sink, w_o_attn, conv_dw_w, conv_dw_b,
           conv_ln_g, conv_ln_b, w_pw2, b_pw2, w_out, norm_ffn_g, w_router, b_router, w_gu, b_gu,
           w_down, b_down, norm_final_g):
    assert w_in.shape[0] == 1, "single trunk layer"
    geo = Geo(x_prompt.shape[0], x_prompt.shape[1], x_sample.shape[0], x_sample.shape[1])
    T = geo.total
    ts = _tiles(geo)
    x = jnp.concatenate([x_prompt.reshape(-1, D_MODEL), x_sample.reshape(-1, D_MODEL)], axis=0)

    cols = _in_proj_columns()
    w_perm = w_in[0][:, cols].astype(BF16)
    b_perm = b_in[0][cols][None, :]
    cos_t, sin_t = _rope_tables(max(geo.len_a, geo.len_b))

    q, kd, vd, glu, gates = _in_proj(x, norm_mix_g, w_perm, b_perm, cos_t, sin_t, geo, ts["tm"])
    attn = _attention(q, kd, vd, attn_sink[0], geo)
    w_rep = jnp.repeat(conv_dw_w[0], SUBLANES, axis=0)
    conv = _conv_branch(glu, w_rep, conv_dw_b, conv_ln_g, conv_ln_b, geo, ts["tc"])

    wr_t = w_router[0].T
    wr_hi = wr_t.astype(BF16)
    wr_lo = (wr_t - wr_hi.astype(F32)).astype(BF16)
    tri = jnp.triu(jnp.ones((ts["tm"], ts["tm"]), BF16), 1)
    x1, h2, idx, gate_t, rank, counts = _mix_route(
        x, attn, conv, gates, w_o_attn[0].astype(BF16), w_pw2[0].astype(BF16), b_pw2,
        w_out[0].astype(BF16), norm_ffn_g, wr_hi, wr_lo, b_router[0][:, None], tri, ts["tm"])

    bm = ts["bm"]
    n_blocks = (T * TOP_K) // bm + N_EXPERTS
    cnt = counts[:, 0].astype(I32)
    padded = ((cnt + bm - 1) // bm) * bm
    pad_end = jnp.cumsum(padded)
    pad_start = pad_end - padded
    dest = pad_start[idx] + rank
    block_exp = jnp.minimum(
        jnp.searchsorted(pad_end, jnp.arange(n_blocks, dtype=I32) * bm, side="right"),
        N_EXPERTS - 1).astype(I32)
    n_used = (pad_end[-1:] // bm).astype(I32)

    xs = _dispatch(h2, dest, jnp.zeros((n_blocks * bm, D_MODEL), F32), ts["td"])
    ys = _experts(block_exp, n_used, xs, w_gu[0].astype(BF16), b_gu[0][:, None, :],
                  w_down[0].astype(BF16), b_down[0][:, None, :], bm)
    y = _combine(dest, x1, gate_t.T, norm_final_g[None, :], ys, ts["tf"])

    y_prompt = y[:geo.rows_a].reshape(x_prompt.shape)
    y_sample = y[geo.rows_a:].reshape(x_sample.shape)
    return (y_prompt, y_sample)
```

```python
import functools
from typing import NamedTuple

import jax
import jax.numpy as jnp
from jax import lax
from jax.experimental import pallas as pl
from jax.experimental.pallas import tpu as pltpu

F32 = jnp.float32
BF16 = jnp.bfloat16
I32 = jnp.int32

D_MODEL = 1024
HEAD_DIM = 64
N_Q_HEADS = 16
N_KV_HEADS = 4
WINDOW = 128
ATTN_BLOCK = 128
ROPE_THETA = 10000.0
CONV_WIDTH = 31
CONV_PAD = CONV_WIDTH // 2
N_EXPERTS = 32
TOP_K = 4
D_FF = D_MODEL
SWIGLU_LIMIT = 7.0
SWIGLU_ALPHA = 1.702
NORM_EPS = 1e-5
NEG_INF = -1e30

Q_COLS = N_Q_HEADS * HEAD_DIM
KV_COLS = N_KV_HEADS * HEAD_DIM
LANES = 128
SUBLANES = 8
KV_DUP_COLS = N_KV_HEADS * LANES
HALO = 16
VMEM_LIMIT = 56 * 1024 * 1024


class Geo(NamedTuple):
    n_a: int
    len_a: int
    n_b: int
    len_b: int

    @property
    def rows_a(self):
        return self.n_a * self.len_a

    @property
    def total(self):
        return self.rows_a + self.n_b * self.len_b


def _seq_bounds(geo, r):
    in_a = r < geo.rows_a
    start_a = (r // geo.len_a) * geo.len_a
    start_b = geo.rows_a + ((r - geo.rows_a) // geo.len_b) * geo.len_b
    start = jnp.where(in_a, start_a, start_b)
    end = start + jnp.where(in_a, geo.len_a, geo.len_b)
    return start, end


def _params(*sem):
    return pltpu.CompilerParams(dimension_semantics=sem, vmem_limit_bytes=VMEM_LIMIT)


_C_Q = 0
_C_K = _C_Q + Q_COLS
_C_V = _C_K + KV_DUP_COLS
_C_GA = _C_V + KV_DUP_COLS
_C_GG = _C_GA + D_MODEL
_C_GATE = _C_GG + D_MODEL
_N_IN = _C_GATE + 2 * D_MODEL
_PROJ_CHUNK = 512


def _in_proj_body(x_ref, g_ref, w_ref, b_ref, cos_ref, sin_ref,
                  q_ref, kd_ref, vd_ref, glu_ref, gate_ref):
    x = x_ref[...]
    ms = jnp.mean(x * x, axis=-1, keepdims=True)
    h = (x * lax.rsqrt(ms + NORM_EPS) * g_ref[...]).astype(BF16)
    cos = cos_ref[...]
    sin = sin_ref[...]

    def proj(c0):
        return (jnp.dot(h, w_ref[:, c0:c0 + _PROJ_CHUNK], preferred_element_type=F32)
                + b_ref[:, c0:c0 + _PROJ_CHUNK])

    def rope_store(z, out_ref, o0, scale):
        for c in range(_PROJ_CHUNK // LANES):
            zc = z[:, c * LANES:(c + 1) * LANES]
            r = zc * cos + pltpu.roll(zc, LANES // 2, 1) * sin
            if scale != 1.0:
                r = r * scale
            out_ref[:, o0 + c * LANES:o0 + (c + 1) * LANES] = r.astype(out_ref.dtype)

    for c in range(Q_COLS // _PROJ_CHUNK):
        rope_store(proj(_C_Q + c * _PROJ_CHUNK), q_ref, c * _PROJ_CHUNK, HEAD_DIM ** -0.5)
    for c in range(KV_DUP_COLS // _PROJ_CHUNK):
        rope_store(proj(_C_K + c * _PROJ_CHUNK), kd_ref, c * _PROJ_CHUNK, 1.0)
    for c in range(KV_DUP_COLS // _PROJ_CHUNK):
        vd_ref[:, c * _PROJ_CHUNK:(c + 1) * _PROJ_CHUNK] = proj(_C_V + c * _PROJ_CHUNK).astype(BF16)
    for c in range(D_MODEL // _PROJ_CHUNK):
        a = proj(_C_GA + c * _PROJ_CHUNK)
        g = proj(_C_GG + c * _PROJ_CHUNK)
        glu_ref[:, c * _PROJ_CHUNK:(c + 1) * _PROJ_CHUNK] = a * jax.nn.sigmoid(g)
    for c in range(2 * D_MODEL // _PROJ_CHUNK):
        gate_ref[:, c * _PROJ_CHUNK:(c + 1) * _PROJ_CHUNK] = jax.nn.sigmoid(
            proj(_C_GATE + c * _PROJ_CHUNK)).astype(BF16)


def _in_proj(x, g_mix, w_perm, b_perm, cos_t, sin_t, geo, tm):
    T = x.shape[0]

    def pos_map(i):
        r0 = i * tm
        start, _ = _seq_bounds(geo, r0)
        return ((r0 - start) // tm, 0)

    const = lambda i: (0, 0)
    row = lambda i: (i, 0)
    return pl.pallas_call(
        _in_proj_body,
        grid=(T // tm,),
        in_specs=[
            pl.BlockSpec((tm, D_MODEL), row),
            pl.BlockSpec((1, D_MODEL), const),
            pl.BlockSpec((D_MODEL, _N_IN), const, pipeline_mode=pl.Buffered(1)),
            pl.BlockSpec((1, _N_IN), const),
            pl.BlockSpec((tm, LANES), pos_map),
            pl.BlockSpec((tm, LANES), pos_map),
        ],
        out_specs=[
            pl.BlockSpec((tm, Q_COLS), row),
            pl.BlockSpec((tm, KV_DUP_COLS), row),
            pl.BlockSpec((tm, KV_DUP_COLS), row),
            pl.BlockSpec((tm, D_MODEL), row),
            pl.BlockSpec((tm, 2 * D_MODEL), row),
        ],
        out_shape=[
            jax.ShapeDtypeStruct((T, Q_COLS), BF16),
            jax.ShapeDtypeStruct((T, KV_DUP_COLS), BF16),
            jax.ShapeDtypeStruct((T, KV_DUP_COLS), BF16),
            jax.ShapeDtypeStruct((T, D_MODEL), F32),
            jax.ShapeDtypeStruct((T, 2 * D_MODEL), BF16),
        ],
        compiler_params=_params("parallel"),
        name="in_proj",
    )(x, g_mix, w_perm, b_perm, cos_t, sin_t)


def _attn_body(geo, sink_ref, q_ref, kp_ref, kc_ref, kn_ref, vp_ref, vc_ref, vn_ref, o_ref):
    i = pl.program_id(0)
    r0 = i * ATTN_BLOCK
    start, end = _seq_bounds(geo, r0)
    has_prev = r0 > start
    has_next = r0 + ATTN_BLOCK < end

    nk = 3 * ATTN_BLOCK
    row = lax.broadcasted_iota(I32, (ATTN_BLOCK, nk), 0)
    col = lax.broadcasted_iota(I32, (ATTN_BLOCK, nk), 1)
    valid = (col >= row) & (col <= row + 2 * WINDOW)
    valid = valid & (has_prev | (col >= ATTN_BLOCK)) & (has_next | (col < 2 * ATTN_BLOCK))
    group = N_Q_HEADS // N_KV_HEADS
    valid = jnp.concatenate([valid] * group, axis=0)

    lane = lax.broadcasted_iota(I32, (ATTN_BLOCK, LANES), 1)
    even_head = (lane % HEAD_DIM) < (HEAD_DIM // 2)
    low_half = lane < HEAD_DIM

    for g in range(N_KV_HEADS):
        ls = slice(g * LANES, (g + 1) * LANES)
        qa = q_ref[:, (2 * g) * LANES:(2 * g + 1) * LANES]
        qb = q_ref[:, (2 * g + 1) * LANES:(2 * g + 2) * LANES]
        zero = jnp.zeros_like(qa)
        q4 = jnp.concatenate([jnp.where(even_head, qa, zero), jnp.where(even_head, zero, qa),
                              jnp.where(even_head, qb, zero), jnp.where(even_head, zero, qb)], axis=0)
        k = jnp.concatenate([kp_ref[:, ls], kc_ref[:, ls], kn_ref[:, ls]], axis=0)
        v = jnp.concatenate([vp_ref[:, ls], vc_ref[:, ls], vn_ref[:, ls]], axis=0)
        s = lax.dot_general(q4, k, (((1,), (1,)), ((), ())), preferred_element_type=F32)
        s = jnp.where(valid, s, NEG_INF)
        sink = jnp.concatenate(
            [jnp.full((ATTN_BLOCK, 1), sink_ref[group * g + h], F32) for h in range(group)], axis=0)
        m = jnp.maximum(jnp.max(s, axis=-1, keepdims=True), sink)
        p = jnp.exp(s - m)
        denom = jnp.sum(p, axis=-1, keepdims=True) + jnp.exp(sink - m)
        o = jnp.dot(p.astype(BF16), v, preferred_element_type=F32) / denom
        b = ATTN_BLOCK
        o_ref[:, (2 * g) * LANES:(2 * g + 1) * LANES] = jnp.where(
            low_half, o[0:b], o[b:2 * b]).astype(BF16)
        o_ref[:, (2 * g + 1) * LANES:(2 * g + 2) * LANES] = jnp.where(
            low_half, o[2 * b:3 * b], o[3 * b:4 * b]).astype(BF16)


def _attention(q, kd, vd, sink, geo):
    T = q.shape[0]
    nb = T // ATTN_BLOCK
    cur = lambda i: (i, 0)
    prev = lambda i: (jnp.maximum(i - 1, 0), 0)
    nxt = lambda i: (jnp.minimum(i + 1, nb - 1), 0)
    kv_spec = lambda m: pl.BlockSpec((ATTN_BLOCK, KV_DUP_COLS), m)
    return pl.pallas_call(
        functools.partial(_attn_body, geo),
        grid=(nb,),
        in_specs=[
            pl.BlockSpec(memory_space=pltpu.SMEM),
            pl.BlockSpec((ATTN_BLOCK, Q_COLS), cur),
            kv_spec(prev), kv_spec(cur), kv_spec(nxt),
            kv_spec(prev), kv_spec(cur), kv_spec(nxt),
        ],
        out_specs=pl.BlockSpec((ATTN_BLOCK, Q_COLS), cur),
        out_shape=jax.ShapeDtypeStruct((T, Q_COLS), BF16),
        compiler_params=_params("parallel"),
        name="window_attn",
    )(sink, q, kd, kd, kd, vd, vd, vd)


_CONV_ROWS = 16


def _conv_body(geo, tc, z_ref, zp_ref, zn_ref, w_ref, dwb_ref, lng_ref, lnb_ref, o_ref, sh_ref):
    i = pl.program_id(0)
    r0 = i * tc
    start, end = _seq_bounds(geo, r0)
    has_prev = (r0 > start).astype(F32)
    has_next = (r0 + tc < end).astype(F32)
    rows = tc + 2 * HALO
    keep = rows - SUBLANES

    sh_ref[0, 0:HALO, :] = zp_ref[...] * has_prev
    sh_ref[0, HALO:HALO + tc, :] = z_ref[...]
    sh_ref[0, HALO + tc:rows, :] = zn_ref[...] * has_next
    for r in range(1, SUBLANES):
        for c in range(D_MODEL // LANES):
            ls = slice(c * LANES, (c + 1) * LANES)
            sh_ref[r, 0:keep, ls] = sh_ref[0, r:r + keep, ls]

    n_lane = D_MODEL // LANES

    def step(j, carry):
        s0 = pl.multiple_of(j * _CONV_ROWS, _CONV_ROWS)
        acc = [jnp.zeros((_CONV_ROWS, LANES), F32) for _ in range(n_lane)]
        for k in range(CONV_WIDTH):
            off = HALO - CONV_PAD + k
            a, r = off // SUBLANES, off % SUBLANES
            for c in range(n_lane):
                ls = slice(c * LANES, (c + 1) * LANES)
                zt = sh_ref[r, pl.ds(s0 + a * SUBLANES, _CONV_ROWS), ls]
                wt = w_ref[k * SUBLANES:(k + 1) * SUBLANES, ls]
                acc[c] = acc[c] + zt * jnp.concatenate([wt] * (_CONV_ROWS // SUBLANES), axis=0)
        y = jnp.concatenate(acc, axis=1) + dwb_ref[...]
        mu = jnp.mean(y, axis=-1, keepdims=True)
        yc = y - mu
        var = jnp.mean(yc * yc, axis=-1, keepdims=True)
        yn = yc * lax.rsqrt(var + NORM_EPS) * lng_ref[...] + lnb_ref[...]
        o_ref[pl.ds(s0, _CONV_ROWS), :] = (yn * jax.nn.sigmoid(yn)).astype(BF16)
        return carry

    lax.fori_loop(0, tc // _CONV_ROWS, step, 0)


def _conv_branch(glu, w_rep, dw_b, ln_g, ln_b, geo, tc):
    T = glu.shape[0]
    nh = T // HALO
    per = tc // HALO
    const = lambda i: (0, 0)
    return pl.pallas_call(
        functools.partial(_conv_body, geo, tc),
        grid=(T // tc,),
        in_specs=[
            pl.BlockSpec((tc, D_MODEL), lambda i: (i, 0)),
            pl.BlockSpec((HALO, D_MODEL), lambda i: (jnp.maximum(i * per - 1, 0), 0)),
            pl.BlockSpec((HALO, D_MODEL), lambda i: (jnp.minimum((i + 1) * per, nh - 1), 0)),
            pl.BlockSpec((CONV_WIDTH * SUBLANES, D_MODEL), const),
            pl.BlockSpec((1, D_MODEL), const),
            pl.BlockSpec((1, D_MODEL), const),
            pl.BlockSpec((1, D_MODEL), const),
        ],
        out_specs=pl.BlockSpec((tc, D_MODEL), lambda i: (i, 0)),
        out_shape=jax.ShapeDtypeStruct((T, D_MODEL), BF16),
        scratch_shapes=[pltpu.VMEM((SUBLANES, tc + 2 * HALO, D_MODEL), F32)],
        compiler_params=_params("parallel"),
        name="conv_branch",
    )(glu, glu, glu, w_rep, dw_b, ln_g, ln_b)


def _mix_body(x_ref, o_ref, c_ref, gate_ref, wo_ref, wpw_ref, bpw_ref, wout_ref, gffn_ref,
              wrh_ref, wrl_ref, br_ref, tri_ref,
              x1_ref, h2_ref, idx_ref, gt_ref, rank_ref, cnt_ref, carry_ref):
    tm = x_ref.shape[0]

    @pl.when(pl.program_id(0) == 0)
    def _():
        carry_ref[...] = jnp.zeros_like(carry_ref)

    attn = jnp.dot(o_ref[...], wo_ref[...], preferred_element_type=F32)
    conv = jnp.dot(c_ref[...], wpw_ref[...], preferred_element_type=F32) + bpw_ref[...]
    g_attn = gate_ref[:, 0:D_MODEL].astype(F32)
    g_conv = gate_ref[:, D_MODEL:2 * D_MODEL].astype(F32)
    mix = (g_attn * attn + g_conv * conv).astype(BF16)
    x1 = x_ref[...] + jnp.dot(mix, wout_ref[...], preferred_element_type=F32)
    x1_ref[...] = x1
    ms = jnp.mean(x1 * x1, axis=-1, keepdims=True)
    h2 = x1 * lax.rsqrt(ms + NORM_EPS) * gffn_ref[...]
    h2_ref[...] = h2

    h_hi = h2.astype(BF16)
    h_lo = (h2 - h_hi.astype(F32)).astype(BF16)
    nt = (((1,), (1,)), ((), ()))
    logits = (lax.dot_general(wrh_ref[...], h_hi, nt, preferred_element_type=F32)
              + lax.dot_general(wrh_ref[...], h_lo, nt, preferred_element_type=F32)
              + lax.dot_general(wrl_ref[...], h_hi, nt, preferred_element_type=F32)
              + br_ref[...])

    eidx = lax.broadcasted_iota(I32, (N_EXPERTS, tm), 0)
    vals = logits
    picked, top_vals, top_idx = [], [], []
    for _ in range(TOP_K):
        m = jnp.max(vals, axis=0, keepdims=True)
        idx = jnp.min(jnp.where(vals == m, eidx, N_EXPERTS), axis=0, keepdims=True)
        sel = eidx == idx
        vals = jnp.where(sel, -jnp.inf, vals)
        picked.append(sel)
        top_vals.append(m)
        top_idx.append(idx)

    exps = [jnp.exp(v - top_vals[0]) for v in top_vals]
    tot = exps[0] + exps[1] + exps[2] + exps[3]
    onehot = (picked[0] | picked[1] | picked[2] | picked[3])
    prefix = jnp.dot(onehot.astype(BF16), tri_ref[...], preferred_element_type=F32) + carry_ref[...]
    for j in range(TOP_K):
        idx_ref[j:j + 1, :] = top_idx[j]
        gt_ref[j:j + 1, :] = exps[j] / tot
        rank_ref[j:j + 1, :] = jnp.sum(jnp.where(picked[j], prefix, 0.0), axis=0,
                                       keepdims=True).astype(I32)
    carry_ref[...] = carry_ref[...] + jnp.sum(onehot.astype(F32), axis=1, keepdims=True)
    cnt_ref[...] = jnp.broadcast_to(carry_ref[...], cnt_ref.shape)


def _mix_route(x, o, c, gates, wo, wpw, bpw, wout, gffn, wr_hi, wr_lo, br, tri, tm):
    T = x.shape[0]
    row = lambda i: (i, 0)
    col = lambda i: (0, i)
    const = lambda i: (0, 0)
    wspec = lambda shape: pl.BlockSpec(shape, const, pipeline_mode=pl.Buffered(1))
    return pl.pallas_call(
        _mix_body,
        grid=(T // tm,),
        in_specs=[
            pl.BlockSpec((tm, D_MODEL), row),
            pl.BlockSpec((tm, Q_COLS), row),
            pl.BlockSpec((tm, D_MODEL), row),
            pl.BlockSpec((tm, 2 * D_MODEL), row),
            wspec((Q_COLS, D_MODEL)),
            wspec((D_MODEL, D_MODEL)),
            pl.BlockSpec((1, D_MODEL), const),
            wspec((D_MODEL, D_MODEL)),
            pl.BlockSpec((1, D_MODEL), const),
            pl.BlockSpec((N_EXPERTS, D_MODEL), const),
            pl.BlockSpec((N_EXPERTS, D_MODEL), const),
            pl.BlockSpec((N_EXPERTS, 1), const),
            pl.BlockSpec((tm, tm), const),
        ],
        out_specs=[
            pl.BlockSpec((tm, D_MODEL), row),
            pl.BlockSpec((tm, D_MODEL), row),
            pl.BlockSpec((TOP_K, tm), col),
            pl.BlockSpec((TOP_K, tm), col),
            pl.BlockSpec((TOP_K, tm), col),
            pl.BlockSpec((N_EXPERTS, LANES), const),
        ],
        out_shape=[
            jax.ShapeDtypeStruct((T, D_MODEL), F32),
            jax.ShapeDtypeStruct((T, D_MODEL), F32),
            jax.ShapeDtypeStruct((TOP_K, T), I32),
            jax.ShapeDtypeStruct((TOP_K, T), F32),
            jax.ShapeDtypeStruct((TOP_K, T), I32),
            jax.ShapeDtypeStruct((N_EXPERTS, LANES), F32),
        ],
        scratch_shapes=[pltpu.VMEM((N_EXPERTS, 1), F32)],
        compiler_params=_params("arbitrary"),
        name="mix_route",
    )(x, o, c, gates, wo, wpw, bpw, wout, gffn, wr_hi, wr_lo, br, tri)


_ISSUE_UNROLL = 8


def _dispatch_body(td, pstart_ref, cnt_ref, pend_ref, dest_ref, h_ref, xs_hbm, zero_ref, sem, zsem):
    @pl.when(pl.program_id(0) == 0)
    def _():
        bm = zero_ref.shape[0]
        zero_ref[...] = jnp.zeros_like(zero_ref)

        def row_copy(r):
            return pltpu.make_async_copy(zero_ref.at[pl.ds(0, 1)], xs_hbm.at[pl.ds(r, 1)], zsem)

        def block_copy(n):
            return pltpu.make_async_copy(zero_ref, xs_hbm.at[pl.ds(n * bm, bm)], zsem)

        def run(lo, hi, copy):
            lax.fori_loop(lo, hi, lambda r, c: (copy(r).start(), c)[1], 0)
            lax.fori_loop(lo, hi, lambda r, c: (copy(r).wait(), c)[1], 0)

        def per_expert(e, carry):
            run(pstart_ref[e] + cnt_ref[e], pend_ref[e], row_copy)
            return carry

        lax.fori_loop(0, N_EXPERTS, per_expert, 0)
        run(pend_ref[N_EXPERTS - 1] // bm, xs_hbm.shape[0] // bm, block_copy)

    def issue(tb, carry):
        for u in range(_ISSUE_UNROLL):
            t = tb * _ISSUE_UNROLL + u
            for j in range(TOP_K):
                pltpu.make_async_copy(h_ref.at[pl.ds(t, 1)],
                                      xs_hbm.at[pl.ds(dest_ref[j, t], 1)], sem).start()
        return carry

    lax.fori_loop(0, td // _ISSUE_UNROLL, issue, 0)
    for _ in range(TOP_K):
        pltpu.make_async_copy(h_ref, xs_hbm.at[pl.ds(0, td)], sem).wait()


def _dispatch(h2, dest, pad_start, cnt, pad_end, n_blocks, bm, td):
    T = h2.shape[0]
    n_rows = n_blocks * bm
    grid_spec = pltpu.PrefetchScalarGridSpec(
        num_scalar_prefetch=3,
        grid=(T // td,),
        in_specs=[
            pl.BlockSpec((TOP_K, td), lambda i, *_: (0, i), memory_space=pltpu.SMEM),
            pl.BlockSpec((td, D_MODEL), lambda i, *_: (i, 0)),
        ],
        out_specs=pl.BlockSpec(memory_space=pl.ANY),
        scratch_shapes=[pltpu.VMEM((bm, D_MODEL), F32), pltpu.SemaphoreType.DMA,
                        pltpu.SemaphoreType.DMA],
    )
    return pl.pallas_call(
        functools.partial(_dispatch_body, td),
        grid_spec=grid_spec,
        out_shape=jax.ShapeDtypeStruct((n_rows, D_MODEL), F32),
        compiler_params=pltpu.CompilerParams(dimension_semantics=("arbitrary",),
                                             has_side_effects=True),
        name="dispatch",
    )(pad_start, cnt, pad_end, dest, h2)


def _expert_body(bexp_ref, nused_ref, xs_ref, wgu_ref, bgu_ref, wd_ref, bd_ref, o_ref):
    n = pl.program_id(0)

    @pl.when(n < nused_ref[0])
    def _():
        x = xs_ref[...].astype(BF16)
        gu = jnp.dot(x, wgu_ref[...], preferred_element_type=F32) + bgu_ref[...]
        g = jnp.minimum(gu[:, 0:D_FF], SWIGLU_LIMIT)
        u = jnp.clip(gu[:, D_FF:2 * D_FF], -SWIGLU_LIMIT, SWIGLU_LIMIT)
        act = g * jax.nn.sigmoid(SWIGLU_ALPHA * g) * (u + 1.0)
        o_ref[...] = jnp.dot(act.astype(BF16), wd_ref[...], preferred_element_type=F32) + bd_ref[...]

    @pl.when(n >= nused_ref[0])
    def _():
        o_ref[...] = jnp.zeros_like(o_ref)


def _experts(block_exp, n_used, xs, wgu, bgu, wd, bd, bm):
    P = xs.shape[0]
    nblk = P // bm
    xmap = lambda n, be, nu: (jnp.minimum(n, nu[0] - 1), 0)
    emap = lambda n, be, nu: (be[n], 0, 0)
    grid_spec = pltpu.PrefetchScalarGridSpec(
        num_scalar_prefetch=2,
        grid=(nblk,),
        in_specs=[
            pl.BlockSpec((bm, D_MODEL), xmap),
            pl.BlockSpec((None, D_MODEL, 2 * D_FF), emap),
            pl.BlockSpec((None, 1, 2 * D_FF), emap),
            pl.BlockSpec((None, D_FF, D_MODEL), emap),
            pl.BlockSpec((None, 1, D_MODEL), emap),
        ],
        out_specs=pl.BlockSpec((bm, D_MODEL), lambda n, be, nu: (n, 0)),
    )
    return pl.pallas_call(
        _expert_body,
        grid_spec=grid_spec,
        out_shape=jax.ShapeDtypeStruct((P, D_MODEL), F32),
        compiler_params=_params("arbitrary"),
        name="experts",
    )(block_exp, n_used, xs, wgu, bgu, wd, bd)


def _combine_body(tf, dest_ref, x1_ref, gate_ref, gfin_ref, ys_hbm, y_ref, buf_ref, sem):
    def copy(t, j):
        return pltpu.make_async_copy(ys_hbm.at[pl.ds(dest_ref[j, t], 1)],
                                     buf_ref.at[j, pl.ds(t, 1)], sem)

    def issue(t, carry):
        for j in range(TOP_K):
            copy(t, j).start()
        return carry

    def drain(t, carry):
        for j in range(TOP_K):
            copy(t, j).wait()
        return carry

    lax.fori_loop(0, tf, issue, 0)
    lax.fori_loop(0, tf, drain, 0)

    gate = gate_ref[...]
    y = x1_ref[...]
    for j in range(TOP_K):
        y = y + gate[:, j:j + 1] * buf_ref[j]
    ms = jnp.mean(y * y, axis=-1, keepdims=True)
    y_ref[...] = y * lax.rsqrt(ms + NORM_EPS) * gfin_ref[...]


def _combine(dest, x1, gate_tok, gfin, ys, tf):
    T = x1.shape[0]
    row = lambda i: (i, 0)
    return pl.pallas_call(
        functools.partial(_combine_body, tf),
        grid=(T // tf,),
        in_specs=[
            pl.BlockSpec((TOP_K, tf), lambda i: (0, i), memory_space=pltpu.SMEM),
            pl.BlockSpec((tf, D_MODEL), row),
            pl.BlockSpec((tf, TOP_K), row),
            pl.BlockSpec((1, D_MODEL), lambda i: (0, 0)),
            pl.BlockSpec(memory_space=pl.ANY),
        ],
        out_specs=pl.BlockSpec((tf, D_MODEL), row),
        out_shape=jax.ShapeDtypeStruct((T, D_MODEL), F32),
        scratch_shapes=[pltpu.VMEM((TOP_K, tf, D_MODEL), F32), pltpu.SemaphoreType.DMA],
        compiler_params=_params("arbitrary"),
        name="combine",
    )(dest, x1, gate_tok, gfin, ys)


def _permute_in_proj(w):
    lead = w.shape[:-1]
    half = HEAD_DIM // 2
    q = w[..., :Q_COLS].reshape(*lead, N_Q_HEADS // 2, 2, 2, half)
    q = jnp.swapaxes(q, -3, -2).reshape(*lead, Q_COLS)
    k = w[..., Q_COLS:Q_COLS + KV_COLS].reshape(*lead, N_KV_HEADS, 2, 1, half)
    k = jnp.broadcast_to(k, (*lead, N_KV_HEADS, 2, 2, half)).reshape(*lead, KV_DUP_COLS)
    v = w[..., Q_COLS + KV_COLS:Q_COLS + 2 * KV_COLS].reshape(*lead, N_KV_HEADS, 1, HEAD_DIM)
    v = jnp.broadcast_to(v, (*lead, N_KV_HEADS, 2, HEAD_DIM)).reshape(*lead, KV_DUP_COLS)
    return jnp.concatenate([q, k, v, w[..., Q_COLS + 2 * KV_COLS:]], axis=-1)


def _rope_tables(n_pos):
    half = HEAD_DIM // 2
    inv_freq = 1.0 / (ROPE_THETA ** (jnp.arange(half, dtype=F32) * (2.0 / HEAD_DIM)))
    ang = jnp.arange(n_pos, dtype=F32)[:, None] * inv_freq[None, :]
    cos = jnp.tile(jnp.cos(ang), (1, LANES // half))
    sin = jnp.tile(jnp.sin(ang), (1, LANES // half))
    sign = jnp.where(jnp.arange(LANES) < LANES // 2, -1.0, 1.0).astype(F32)
    return cos, sin * sign[None, :]


def _tiles(geo):
    unit = min(geo.len_a, geo.len_b)
    tm = min(512, unit)
    return dict(tm=tm, tc=min(512, unit), td=min(512, unit), tf=min(256, unit), bm=256)


def kernel(x_prompt, x_sample, norm_mix_g, w_in, b_in, attn_sink, w_o_attn, conv_dw_w, conv_dw_b,
           conv_ln_g, conv_ln_b, w_pw2, b_pw2, w_out, norm_ffn_g, w_router, b_router, w_gu, b_gu,
           w_down, b_down, norm_final_g):
    assert w_in.shape[0] == 1, "single trunk layer"
    geo = Geo(x_prompt.shape[0], x_prompt.shape[1], x_sample.shape[0], x_sample.shape[1])
    T = geo.total
    ts = _tiles(geo)
    x = jnp.concatenate([x_prompt.reshape(-1, D_MODEL), x_sample.reshape(-1, D_MODEL)], axis=0)

    w_perm = _permute_in_proj(w_in[0]).astype(BF16)
    b_perm = _permute_in_proj(b_in)
    cos_t, sin_t = _rope_tables(max(geo.len_a, geo.len_b))

    q, kd, vd, glu, gates = _in_proj(x, norm_mix_g, w_perm, b_perm, cos_t, sin_t, geo, ts["tm"])
    attn = _attention(q, kd, vd, attn_sink[0], geo)
    w_rep = jnp.repeat(conv_dw_w[0], SUBLANES, axis=0)
    conv = _conv_branch(glu, w_rep, conv_dw_b, conv_ln_g, conv_ln_b, geo, ts["tc"])

    wr_t = w_router[0].T
    wr_hi = wr_t.astype(BF16)
    wr_lo = (wr_t - wr_hi.astype(F32)).astype(BF16)
    tri = jnp.triu(jnp.ones((ts["tm"], ts["tm"]), BF16), 1)
    x1, h2, idx, gate_t, rank, counts = _mix_route(
        x, attn, conv, gates, w_o_attn[0].astype(BF16), w_pw2[0].astype(BF16), b_pw2,
        w_out[0].astype(BF16), norm_ffn_g, wr_hi, wr_lo, b_router[0][:, None], tri, ts["tm"])

    bm = ts["bm"]
    n_blocks = (T * TOP_K) // bm + N_EXPERTS
    cnt = counts[:, 0].astype(I32)
    padded = ((cnt + bm - 1) // bm) * bm
    pad_end = jnp.cumsum(padded)
    pad_start = pad_end - padded
    expert_ids = jnp.arange(N_EXPERTS, dtype=I32)[:, None, None]
    dest = rank + jnp.sum(jnp.where(idx[None] == expert_ids, pad_start[:, None, None], 0), axis=0)
    block_start = jnp.arange(n_blocks, dtype=I32) * bm
    block_exp = jnp.minimum(jnp.sum((pad_end[None, :] <= block_start[:, None]).astype(I32), axis=1),
                            N_EXPERTS - 1)
    n_used = (pad_end[-1:] // bm).astype(I32)

    xs = _dispatch(h2, dest, pad_start, cnt, pad_end, n_blocks, bm, ts["td"])
    ys = _experts(block_exp, n_used, xs, w_gu[0].astype(BF16), b_gu[0][:, None, :],
                  w_down[0].astype(BF16), b_down[0][:, None, :], bm)
    y = _combine(dest, x1, gate_t.T, norm_final_g[None, :], ys, ts["tf"])

    y_prompt = y[:geo.rows_a].reshape(x_prompt.shape)
    y_sample = y[geo.rows_a:].reshape(x_sample.shape)
    return (y_prompt, y_sample)
```

```python
import functools
from typing import NamedTuple

import jax
import jax.numpy as jnp
from jax import lax
from jax.experimental import pallas as pl
from jax.experimental.pallas import tpu as pltpu

F32 = jnp.float32
BF16 = jnp.bfloat16
I32 = jnp.int32

D_MODEL = 1024
HEAD_DIM = 64
N_Q_HEADS = 16
N_KV_HEADS = 4
WINDOW = 128
ATTN_BLOCK = 128
ROPE_THETA = 10000.0
CONV_WIDTH = 31
CONV_PAD = CONV_WIDTH // 2
N_EXPERTS = 32
TOP_K = 4
D_FF = D_MODEL
SWIGLU_LIMIT = 7.0
SWIGLU_ALPHA = 1.702
NORM_EPS = 1e-5
NEG_INF = -1e30

Q_COLS = N_Q_HEADS * HEAD_DIM
KV_COLS = N_KV_HEADS * HEAD_DIM
LANES = 128
SUBLANES = 8
KV_DUP_COLS = N_KV_HEADS * LANES
HALO = 16
VMEM_LIMIT = 56 * 1024 * 1024


class Geo(NamedTuple):
    n_a: int
    len_a: int
    n_b: int
    len_b: int

    @property
    def rows_a(self):
        return self.n_a * self.len_a

    @property
    def total(self):
        return self.rows_a + self.n_b * self.len_b


def _seq_bounds(geo, r):
    in_a = r < geo.rows_a
    start_a = (r // geo.len_a) * geo.len_a
    start_b = geo.rows_a + ((r - geo.rows_a) // geo.len_b) * geo.len_b
    start = jnp.where(in_a, start_a, start_b)
    end = start + jnp.where(in_a, geo.len_a, geo.len_b)
    return start, end


def _params(*sem):
    return pltpu.CompilerParams(dimension_semantics=sem, vmem_limit_bytes=VMEM_LIMIT)


_C_Q = 0
_C_K = _C_Q + Q_COLS
_C_GA = _C_K + KV_DUP_COLS
_C_GG = _C_GA + D_MODEL
_C_GATE = _C_GG + D_MODEL
_N_IN = _C_GATE + 2 * D_MODEL
_PROJ_CHUNK = 512
N_SLAB = D_MODEL // LANES
_NT = (((1,), (1,)), ((), ()))


def _in_proj_body(n_a, xa_ref, xb_ref, g_ref, w_ref, b_ref, wvt_ref, bvt_ref, cos_ref, sin_ref,
                  q_ref, kd_ref, vt_ref, glu_ref, gate_ref):
    x = jnp.where(pl.program_id(0) < n_a, xa_ref[...], xb_ref[...])
    ms = jnp.mean(x * x, axis=-1, keepdims=True)
    h = (x * lax.rsqrt(ms + NORM_EPS) * g_ref[...]).astype(BF16)
    cos = cos_ref[...]
    sin = sin_ref[...]

    def proj(c0):
        return (jnp.dot(h, w_ref[:, c0:c0 + _PROJ_CHUNK], preferred_element_type=F32)
                + b_ref[:, c0:c0 + _PROJ_CHUNK])

    def rope_store(z, out_ref, o0, scale):
        for c in range(_PROJ_CHUNK // LANES):
            zc = z[:, c * LANES:(c + 1) * LANES]
            r = zc * cos + pltpu.roll(zc, LANES // 2, 1) * sin
            if scale != 1.0:
                r = r * scale
            out_ref[:, o0 + c * LANES:o0 + (c + 1) * LANES] = r.astype(out_ref.dtype)

    for c in range(Q_COLS // _PROJ_CHUNK):
        rope_store(proj(_C_Q + c * _PROJ_CHUNK), q_ref, c * _PROJ_CHUNK, HEAD_DIM ** -0.5)
    for c in range(KV_DUP_COLS // _PROJ_CHUNK):
        rope_store(proj(_C_K + c * _PROJ_CHUNK), kd_ref, c * _PROJ_CHUNK, 1.0)
    vt = lax.dot_general(wvt_ref[...], h, _NT, preferred_element_type=F32) + bvt_ref[...]
    vt_ref[...] = vt.astype(BF16)
    per = _PROJ_CHUNK // LANES
    for c in range(D_MODEL // _PROJ_CHUNK):
        a = proj(_C_GA + c * _PROJ_CHUNK)
        g = proj(_C_GG + c * _PROJ_CHUNK)
        glu = a * jax.nn.sigmoid(g)
        for s in range(per):
            glu_ref[c * per + s] = glu[:, s * LANES:(s + 1) * LANES]
    for c in range(2 * D_MODEL // _PROJ_CHUNK):
        gate_ref[:, c * _PROJ_CHUNK:(c + 1) * _PROJ_CHUNK] = jax.nn.sigmoid(
            proj(_C_GATE + c * _PROJ_CHUNK)).astype(BF16)


def _two_group_specs(geo, tile, width):
    n_a = geo.rows_a // tile
    return n_a, [pl.BlockSpec((tile, width), lambda i: (jnp.minimum(i, n_a - 1), 0)),
                 pl.BlockSpec((tile, width), lambda i: (jnp.maximum(i - n_a, 0), 0))]


def _in_proj(xa, xb, g_mix, w_perm, b_perm, wvt, bvt, cos_t, sin_t, geo, tm):
    T = geo.total

    def pos_map(i):
        r0 = i * tm
        start, _ = _seq_bounds(geo, r0)
        return ((r0 - start) // tm, 0)

    const = lambda i: (0, 0)
    row = lambda i: (i, 0)
    n_a, x_specs = _two_group_specs(geo, tm, D_MODEL)
    return pl.pallas_call(
        functools.partial(_in_proj_body, n_a),
        grid=(T // tm,),
        in_specs=x_specs + [
            pl.BlockSpec((1, D_MODEL), const),
            pl.BlockSpec((D_MODEL, _N_IN), const, pipeline_mode=pl.Buffered(1)),
            pl.BlockSpec((1, _N_IN), const),
            pl.BlockSpec((KV_DUP_COLS, D_MODEL), const),
            pl.BlockSpec((KV_DUP_COLS, 1), const),
            pl.BlockSpec((tm, LANES), pos_map),
            pl.BlockSpec((tm, LANES), pos_map),
        ],
        out_specs=[
            pl.BlockSpec((tm, Q_COLS), row),
            pl.BlockSpec((tm, KV_DUP_COLS), row),
            pl.BlockSpec((KV_DUP_COLS, tm), lambda i: (0, i)),
            pl.BlockSpec((N_SLAB, tm, LANES), lambda i: (0, i, 0)),
            pl.BlockSpec((tm, 2 * D_MODEL), row),
        ],
        out_shape=[
            jax.ShapeDtypeStruct((T, Q_COLS), BF16),
            jax.ShapeDtypeStruct((T, KV_DUP_COLS), BF16),
            jax.ShapeDtypeStruct((KV_DUP_COLS, T), BF16),
            jax.ShapeDtypeStruct((N_SLAB, T, LANES), F32),
            jax.ShapeDtypeStruct((T, 2 * D_MODEL), BF16),
        ],
        compiler_params=_params("parallel"),
        name="in_proj",
    )(xa, xb, g_mix, w_perm, b_perm, wvt, bvt, cos_t, sin_t)


_ONES_ROWS = 16


def _attn_body(geo, sink_ref, q_ref, kp_ref, kc_ref, kn_ref, vp_ref, vc_ref, vn_ref, o_ref):
    i = pl.program_id(0)
    r0 = i * ATTN_BLOCK
    start, end = _seq_bounds(geo, r0)
    has_prev = r0 > start
    has_next = r0 + ATTN_BLOCK < end
    group = N_Q_HEADS // N_KV_HEADS
    nq = group * ATTN_BLOCK

    key = lax.broadcasted_iota(I32, (ATTN_BLOCK, nq), 0)
    col = lax.broadcasted_iota(I32, (ATTN_BLOCK, nq), 1)
    qry = col % ATTN_BLOCK
    bias_prev = jnp.where((key >= qry) & has_prev, 0.0, NEG_INF)
    bias_next = jnp.where((key <= qry) & has_next, 0.0, NEG_INF)
    head_of_col = lax.broadcasted_iota(I32, (1, nq), 1) // ATTN_BLOCK

    lane = lax.broadcasted_iota(I32, (ATTN_BLOCK, LANES), 1)
    even_head = (lane % HEAD_DIM) < (HEAD_DIM // 2)
    ones = jnp.ones((_ONES_ROWS, 3 * ATTN_BLOCK), BF16)
    b = ATTN_BLOCK
    hd = HEAD_DIM

    for g in range(N_KV_HEADS):
        ls = slice(g * LANES, (g + 1) * LANES)
        qa = q_ref[:, (2 * g) * LANES:(2 * g + 1) * LANES]
        qb = q_ref[:, (2 * g + 1) * LANES:(2 * g + 2) * LANES]
        zero = jnp.zeros_like(qa)
        q4 = jnp.concatenate([jnp.where(even_head, qa, zero), jnp.where(even_head, zero, qa),
                              jnp.where(even_head, qb, zero), jnp.where(even_head, zero, qb)], axis=0)
        k = jnp.concatenate([kp_ref[:, ls], kc_ref[:, ls], kn_ref[:, ls]], axis=0)
        st = lax.dot_general(k, q4, _NT, preferred_element_type=F32)
        s_prev = st[0:b] + bias_prev
        s_cur = st[b:2 * b]
        s_next = st[2 * b:3 * b] + bias_next
        sink = jnp.full((1, nq), sink_ref[group * g], F32)
        for h in range(1, group):
            sink = jnp.where(head_of_col == h, sink_ref[group * g + h], sink)
        m = jnp.maximum(jnp.maximum(jnp.max(s_prev, axis=0, keepdims=True),
                                    jnp.max(s_cur, axis=0, keepdims=True)),
                        jnp.maximum(jnp.max(s_next, axis=0, keepdims=True), sink))
        p = jnp.concatenate([jnp.exp(s_prev - m).astype(BF16), jnp.exp(s_cur - m).astype(BF16),
                             jnp.exp(s_next - m).astype(BF16)], axis=0)
        vt = jnp.concatenate([vp_ref[ls, :], vc_ref[ls, :], vn_ref[ls, :]], axis=1)
        ot = jnp.dot(jnp.concatenate([vt, ones], axis=0), p, preferred_element_type=F32)
        denom = ot[2 * hd:2 * hd + 1] + jnp.exp(sink - m)
        on = ot[0:2 * hd] * (1.0 / denom)
        pair_a = jnp.concatenate([on[0:hd, 0:b], on[hd:2 * hd, b:2 * b]], axis=0)
        pair_b = jnp.concatenate([on[0:hd, 2 * b:3 * b], on[hd:2 * hd, 3 * b:4 * b]], axis=0)
        o_ref[:, (2 * g) * LANES:(2 * g + 1) * LANES] = pair_a.T.astype(BF16)
        o_ref[:, (2 * g + 1) * LANES:(2 * g + 2) * LANES] = pair_b.T.astype(BF16)


def _attention(q, kd, vt, sink, geo):
    T = q.shape[0]
    nb = T // ATTN_BLOCK
    prev = lambda i: jnp.maximum(i - 1, 0)
    cur = lambda i: i
    nxt = lambda i: jnp.minimum(i + 1, nb - 1)
    k_spec = lambda m: pl.BlockSpec((ATTN_BLOCK, KV_DUP_COLS), lambda i: (m(i), 0))
    v_spec = lambda m: pl.BlockSpec((KV_DUP_COLS, ATTN_BLOCK), lambda i: (0, m(i)))
    return pl.pallas_call(
        functools.partial(_attn_body, geo),
        grid=(nb,),
        in_specs=[
            pl.BlockSpec(memory_space=pltpu.SMEM),
            pl.BlockSpec((ATTN_BLOCK, Q_COLS), lambda i: (i, 0)),
            k_spec(prev), k_spec(cur), k_spec(nxt),
            v_spec(prev), v_spec(cur), v_spec(nxt),
        ],
        out_specs=pl.BlockSpec((ATTN_BLOCK, Q_COLS), lambda i: (i, 0)),
        out_shape=jax.ShapeDtypeStruct((T, Q_COLS), BF16),
        compiler_params=_params("parallel"),
        name="window_attn",
    )(sink, q, kd, kd, kd, vt, vt, vt)


_CONV_ROWS = 64
_LN_ROWS = 32


def _conv_body(geo, tc, z_ref, zp_ref, zn_ref, w_ref, dwb_ref, lng_ref, lnb_ref, o_ref,
               buf_ref, y_ref):
    i = pl.program_id(0)
    r0 = i * tc
    start, end = _seq_bounds(geo, r0)
    has_prev = (r0 > start).astype(F32)
    has_next = (r0 + tc < end).astype(F32)
    rows = tc + 2 * HALO
    buf_ref[:, 0:HALO, :] = zp_ref[...] * has_prev
    buf_ref[:, HALO:HALO + tc, :] = z_ref[...]
    buf_ref[:, HALO + tc:rows, :] = zn_ref[...] * has_next

    rep = _CONV_ROWS // SUBLANES

    def conv_chunk(j):
        s0 = pl.multiple_of(j * _CONV_ROWS, _CONV_ROWS)
        for c in range(N_SLAB):
            ls = slice(c * LANES, (c + 1) * LANES)
            acc = jnp.zeros((_CONV_ROWS, LANES), F32)
            for k in range(CONV_WIDTH):
                zt = buf_ref[c, pl.ds(s0 + (HALO - CONV_PAD + k), _CONV_ROWS, stride=1), :]
                wt = w_ref[k * SUBLANES:(k + 1) * SUBLANES, ls]
                acc = acc + zt * jnp.concatenate([wt] * rep, axis=0)
            y_ref[pl.ds(s0, _CONV_ROWS), ls] = acc

    def ln_chunk(j):
        for h in range(_CONV_ROWS // _LN_ROWS):
            s0 = pl.multiple_of(j * _CONV_ROWS + h * _LN_ROWS, _LN_ROWS)
            y = y_ref[pl.ds(s0, _LN_ROWS), :] + dwb_ref[...]
            mu = jnp.mean(y, axis=-1, keepdims=True)
            yc = y - mu
            var = jnp.mean(yc * yc, axis=-1, keepdims=True)
            yn = yc * lax.rsqrt(var + NORM_EPS) * lng_ref[...] + lnb_ref[...]
            o_ref[pl.ds(s0, _LN_ROWS), :] = (yn * jax.nn.sigmoid(yn)).astype(BF16)

    n = tc // _CONV_ROWS
    conv_chunk(0)

    def body(j, carry):
        ln_chunk(j - 1)
        conv_chunk(j)
        return carry

    lax.fori_loop(1, n, body, 0)
    ln_chunk(n - 1)


def _conv_branch(glu, w_rep, dw_b, ln_g, ln_b, geo, tc):
    T = glu.shape[1]
    nh = T // HALO
    per = tc // HALO
    const = lambda i: (0, 0)
    return pl.pallas_call(
        functools.partial(_conv_body, geo, tc),
        grid=(T // tc,),
        in_specs=[
            pl.BlockSpec((N_SLAB, tc, LANES), lambda i: (0, i, 0)),
            pl.BlockSpec((N_SLAB, HALO, LANES), lambda i: (0, jnp.maximum(i * per - 1, 0), 0)),
            pl.BlockSpec((N_SLAB, HALO, LANES), lambda i: (0, jnp.minimum((i + 1) * per, nh - 1), 0)),
            pl.BlockSpec((CONV_WIDTH * SUBLANES, D_MODEL), const),
            pl.BlockSpec((1, D_MODEL), const),
            pl.BlockSpec((1, D_MODEL), const),
            pl.BlockSpec((1, D_MODEL), const),
        ],
        out_specs=pl.BlockSpec((tc, D_MODEL), lambda i: (i, 0)),
        out_shape=jax.ShapeDtypeStruct((T, D_MODEL), BF16),
        scratch_shapes=[pltpu.VMEM((N_SLAB, tc + 2 * HALO, LANES), F32),
                        pltpu.VMEM((tc, D_MODEL), F32)],
        compiler_params=_params("parallel"),
        name="conv_branch",
    )(glu, glu, glu, w_rep, dw_b, ln_g, ln_b)


def _mix_body(n_a, xa_ref, xb_ref, o_ref, c_ref, gate_ref, wo_ref, wpw_ref, bpw_ref, wout_ref,
              gffn_ref, wrh_ref, wrl_ref, br_ref, tri_ref,
              x1_ref, h2_ref, idx_ref, gt_ref, rank_ref, cnt_ref, carry_ref):
    tm = xa_ref.shape[0]
    x = jnp.where(pl.program_id(0) < n_a, xa_ref[...], xb_ref[...])

    @pl.when(pl.program_id(0) == 0)
    def _():
        carry_ref[...] = jnp.zeros_like(carry_ref)

    attn = jnp.dot(o_ref[...], wo_ref[...], preferred_element_type=F32)
    conv = jnp.dot(c_ref[...], wpw_ref[...], preferred_element_type=F32) + bpw_ref[...]
    g_attn = gate_ref[:, 0:D_MODEL].astype(F32)
    g_conv = gate_ref[:, D_MODEL:2 * D_MODEL].astype(F32)
    mix = (g_attn * attn + g_conv * conv).astype(BF16)
    x1 = x + jnp.dot(mix, wout_ref[...], preferred_element_type=F32)
    x1_ref[...] = x1
    ms = jnp.mean(x1 * x1, axis=-1, keepdims=True)
    h2 = x1 * lax.rsqrt(ms + NORM_EPS) * gffn_ref[...]
    h2_ref[...] = h2

    h_hi = h2.astype(BF16)
    h_lo = (h2 - h_hi.astype(F32)).astype(BF16)
    logits = (lax.dot_general(wrh_ref[...], h_hi, _NT, preferred_element_type=F32)
              + lax.dot_general(wrh_ref[...], h_lo, _NT, preferred_element_type=F32)
              + lax.dot_general(wrl_ref[...], h_hi, _NT, preferred_element_type=F32)
              + br_ref[...])

    eidx = lax.broadcasted_iota(I32, (N_EXPERTS, tm), 0)
    vals = logits
    picked, top_vals, top_idx = [], [], []
    for _ in range(TOP_K):
        m = jnp.max(vals, axis=0, keepdims=True)
        idx = jnp.min(jnp.where(vals == m, eidx, N_EXPERTS), axis=0, keepdims=True)
        sel = eidx == idx
        vals = jnp.where(sel, -jnp.inf, vals)
        picked.append(sel)
        top_vals.append(m)
        top_idx.append(idx)

    exps = [jnp.exp(v - top_vals[0]) for v in top_vals]
    tot = exps[0] + exps[1] + exps[2] + exps[3]
    onehot = (picked[0] | picked[1] | picked[2] | picked[3])
    prefix = jnp.dot(onehot.astype(BF16), tri_ref[...], preferred_element_type=F32) + carry_ref[...]
    for j in range(TOP_K):
        idx_ref[j:j + 1, :] = top_idx[j]
        gt_ref[j:j + 1, :] = exps[j] / tot
        rank_ref[j:j + 1, :] = jnp.sum(jnp.where(picked[j], prefix, 0.0), axis=0,
                                       keepdims=True).astype(I32)
    carry_ref[...] = carry_ref[...] + jnp.sum(onehot.astype(F32), axis=1, keepdims=True)
    cnt_ref[...] = jnp.broadcast_to(carry_ref[...], cnt_ref.shape)


def _mix_route(xa, xb, o, c, gates, wo, wpw, bpw, wout, gffn, wr_hi, wr_lo, br, tri, geo, tm):
    T = geo.total
    row = lambda i: (i, 0)
    col = lambda i: (0, i)
    const = lambda i: (0, 0)
    wspec = lambda shape: pl.BlockSpec(shape, const, pipeline_mode=pl.Buffered(1))
    n_a, x_specs = _two_group_specs(geo, tm, D_MODEL)
    return pl.pallas_call(
        functools.partial(_mix_body, n_a),
        grid=(T // tm,),
        in_specs=x_specs + [
            pl.BlockSpec((tm, Q_COLS), row),
            pl.BlockSpec((tm, D_MODEL), row),
            pl.BlockSpec((tm, 2 * D_MODEL), row),
            wspec((Q_COLS, D_MODEL)),
            wspec((D_MODEL, D_MODEL)),
            pl.BlockSpec((1, D_MODEL), const),
            wspec((D_MODEL, D_MODEL)),
            pl.BlockSpec((1, D_MODEL), const),
            pl.BlockSpec((N_EXPERTS, D_MODEL), const),
            pl.BlockSpec((N_EXPERTS, D_MODEL), const),
            pl.BlockSpec((N_EXPERTS, 1), const),
            pl.BlockSpec((tm, tm), const),
        ],
        out_specs=[
            pl.BlockSpec((tm, D_MODEL), row),
            pl.BlockSpec((tm, D_MODEL), row),
            pl.BlockSpec((TOP_K, tm), col),
            pl.BlockSpec((TOP_K, tm), col),
            pl.BlockSpec((TOP_K, tm), col),
            pl.BlockSpec((N_EXPERTS, LANES), const),
        ],
        out_shape=[
            jax.ShapeDtypeStruct((T, D_MODEL), F32),
            jax.ShapeDtypeStruct((T, D_MODEL), F32),
            jax.ShapeDtypeStruct((TOP_K, T), I32),
            jax.ShapeDtypeStruct((TOP_K, T), F32),
            jax.ShapeDtypeStruct((TOP_K, T), I32),
            jax.ShapeDtypeStruct((N_EXPERTS, LANES), F32),
        ],
        scratch_shapes=[pltpu.VMEM((N_EXPERTS, 1), F32)],
        compiler_params=_params("arbitrary"),
        name="mix_route",
    )(xa, xb, o, c, gates, wo, wpw, bpw, wout, gffn, wr_hi, wr_lo, br, tri)


_ISSUE_UNROLL = 8


def _dispatch_body(td, pstart_ref, cnt_ref, pend_ref, dest_ref, h_ref, xs_hbm, zero_ref, sem, zsem):
    @pl.when(pl.program_id(0) == 0)
    def _():
        bm = zero_ref.shape[0]
        zero_ref[...] = jnp.zeros_like(zero_ref)

        def row_copy(r):
            return pltpu.make_async_copy(zero_ref.at[pl.ds(0, 1)], xs_hbm.at[pl.ds(r, 1)], zsem)

        def block_copy(n):
            return pltpu.make_async_copy(zero_ref, xs_hbm.at[pl.ds(n * bm, bm)], zsem)

        def run(lo, hi, copy):
            lax.fori_loop(lo, hi, lambda r, c: (copy(r).start(), c)[1], 0)
            lax.fori_loop(lo, hi, lambda r, c: (copy(r).wait(), c)[1], 0)

        def per_expert(e, carry):
            run(pstart_ref[e] + cnt_ref[e], pend_ref[e], row_copy)
            return carry

        lax.fori_loop(0, N_EXPERTS, per_expert, 0)
        run(pend_ref[N_EXPERTS - 1] // bm, xs_hbm.shape[0] // bm, block_copy)

    def issue(tb, carry):
        for u in range(_ISSUE_UNROLL):
            t = tb * _ISSUE_UNROLL + u
            for j in range(TOP_K):
                pltpu.make_async_copy(h_ref.at[pl.ds(t, 1)],
                                      xs_hbm.at[pl.ds(dest_ref[j, t], 1)], sem).start()
        return carry

    lax.fori_loop(0, td // _ISSUE_UNROLL, issue, 0)
    for _ in range(TOP_K):
        pltpu.make_async_copy(h_ref, xs_hbm.at[pl.ds(0, td)], sem).wait()


def _dispatch(h2, dest, pad_start, cnt, pad_end, n_blocks, bm, td):
    T = h2.shape[0]
    n_rows = n_blocks * bm
    grid_spec = pltpu.PrefetchScalarGridSpec(
        num_scalar_prefetch=3,
        grid=(T // td,),
        in_specs=[
            pl.BlockSpec((TOP_K, td), lambda i, *_: (0, i), memory_space=pltpu.SMEM),
            pl.BlockSpec((td, D_MODEL), lambda i, *_: (i, 0)),
        ],
        out_specs=pl.BlockSpec(memory_space=pl.ANY),
        scratch_shapes=[pltpu.VMEM((bm, D_MODEL), F32), pltpu.SemaphoreType.DMA,
                        pltpu.SemaphoreType.DMA],
    )
    return pl.pallas_call(
        functools.partial(_dispatch_body, td),
        grid_spec=grid_spec,
        out_shape=jax.ShapeDtypeStruct((n_rows, D_MODEL), F32),
        compiler_params=pltpu.CompilerParams(dimension_semantics=("arbitrary",),
                                             has_side_effects=True),
        name="dispatch",
    )(pad_start, cnt, pad_end, dest, h2)


def _expert_body(bexp_ref, nused_ref, xs_ref, wgu_ref, bgu_ref, wd_ref, bd_ref, o_ref):
    n = pl.program_id(0)

    @pl.when(n < nused_ref[0])
    def _():
        x = xs_ref[...].astype(BF16)
        gu = jnp.dot(x, wgu_ref[...], preferred_element_type=F32) + bgu_ref[...]
        g = jnp.minimum(gu[:, 0:D_FF], SWIGLU_LIMIT)
        u = jnp.clip(gu[:, D_FF:2 * D_FF], -SWIGLU_LIMIT, SWIGLU_LIMIT)
        act = g * jax.nn.sigmoid(SWIGLU_ALPHA * g) * (u + 1.0)
        o_ref[...] = jnp.dot(act.astype(BF16), wd_ref[...], preferred_element_type=F32) + bd_ref[...]

    @pl.when(n >= nused_ref[0])
    def _():
        o_ref[...] = jnp.zeros_like(o_ref)


def _experts(block_exp, n_used, xs, wgu, bgu, wd, bd, bm):
    P = xs.shape[0]
    nblk = P // bm
    xmap = lambda n, be, nu: (jnp.minimum(n, nu[0] - 1), 0)
    emap = lambda n, be, nu: (be[n], 0, 0)
    grid_spec = pltpu.PrefetchScalarGridSpec(
        num_scalar_prefetch=2,
        grid=(nblk,),
        in_specs=[
            pl.BlockSpec((bm, D_MODEL), xmap),
            pl.BlockSpec((None, D_MODEL, 2 * D_FF), emap),
            pl.BlockSpec((None, 1, 2 * D_FF), emap),
            pl.BlockSpec((None, D_FF, D_MODEL), emap),
            pl.BlockSpec((None, 1, D_MODEL), emap),
        ],
        out_specs=pl.BlockSpec((bm, D_MODEL), lambda n, be, nu: (n, 0)),
    )
    return pl.pallas_call(
        _expert_body,
        grid_spec=grid_spec,
        out_shape=jax.ShapeDtypeStruct((P, D_MODEL), F32),
        compiler_params=_params("arbitrary"),
        name="experts",
    )(block_exp, n_used, xs, wgu, bgu, wd, bd)


def _combine_body(tf, n_a, dest_ref, x1_ref, gate_ref, gfin_ref, ys_hbm, ya_ref, yb_ref,
                  buf_ref, sem):
    def issue(tb, carry):
        for u in range(_ISSUE_UNROLL):
            t = tb * _ISSUE_UNROLL + u
            for j in range(TOP_K):
                pltpu.make_async_copy(ys_hbm.at[pl.ds(dest_ref[j, t], 1)],
                                      buf_ref.at[j, pl.ds(t, 1)], sem).start()
        return carry

    lax.fori_loop(0, tf // _ISSUE_UNROLL, issue, 0)
    for j in range(TOP_K):
        pltpu.make_async_copy(ys_hbm.at[pl.ds(0, tf)], buf_ref.at[j], sem).wait()

    gate = gate_ref[...]
    y = x1_ref[...]
    for j in range(TOP_K):
        y = y + gate[:, j:j + 1] * buf_ref[j]
    ms = jnp.mean(y * y, axis=-1, keepdims=True)
    y = y * lax.rsqrt(ms + NORM_EPS) * gfin_ref[...]

    @pl.when(pl.program_id(0) < n_a)
    def _():
        ya_ref[...] = y

    @pl.when(pl.program_id(0) >= n_a)
    def _():
        yb_ref[...] = y


def _combine(dest, x1, gate_tok, gfin, ys, geo, tf):
    T = x1.shape[0]
    row = lambda i: (i, 0)
    n_a, y_specs = _two_group_specs(geo, tf, D_MODEL)
    return pl.pallas_call(
        functools.partial(_combine_body, tf, n_a),
        grid=(T // tf,),
        in_specs=[
            pl.BlockSpec((TOP_K, tf), lambda i: (0, i), memory_space=pltpu.SMEM),
            pl.BlockSpec((tf, D_MODEL), row),
            pl.BlockSpec((tf, TOP_K), row),
            pl.BlockSpec((1, D_MODEL), lambda i: (0, 0)),
            pl.BlockSpec(memory_space=pl.ANY),
        ],
        out_specs=y_specs,
        out_shape=[jax.ShapeDtypeStruct((geo.rows_a, D_MODEL), F32),
                   jax.ShapeDtypeStruct((T - geo.rows_a, D_MODEL), F32)],
        scratch_shapes=[pltpu.VMEM((TOP_K, tf, D_MODEL), F32), pltpu.SemaphoreType.DMA],
        compiler_params=_params("arbitrary"),
        name="combine",
    )(dest, x1, gate_tok, gfin, ys)


def _permute_in_proj(w):
    lead = w.shape[:-1]
    half = HEAD_DIM // 2
    q = w[..., :Q_COLS].reshape(*lead, N_Q_HEADS // 2, 2, 2, half)
    q = jnp.swapaxes(q, -3, -2).reshape(*lead, Q_COLS)
    k = w[..., Q_COLS:Q_COLS + KV_COLS].reshape(*lead, N_KV_HEADS, 2, 1, half)
    k = jnp.broadcast_to(k, (*lead, N_KV_HEADS, 2, 2, half)).reshape(*lead, KV_DUP_COLS)
    v = w[..., Q_COLS + KV_COLS:Q_COLS + 2 * KV_COLS].reshape(*lead, N_KV_HEADS, 1, HEAD_DIM)
    v = jnp.broadcast_to(v, (*lead, N_KV_HEADS, 2, HEAD_DIM)).reshape(*lead, KV_DUP_COLS)
    return jnp.concatenate([q, k, w[..., Q_COLS + 2 * KV_COLS:]], axis=-1), v


def _rope_tables(n_pos):
    half = HEAD_DIM // 2
    inv_freq = 1.0 / (ROPE_THETA ** (jnp.arange(half, dtype=F32) * (2.0 / HEAD_DIM)))
    ang = jnp.arange(n_pos, dtype=F32)[:, None] * inv_freq[None, :]
    cos = jnp.tile(jnp.cos(ang), (1, LANES // half))
    sin = jnp.tile(jnp.sin(ang), (1, LANES // half))
    sign = jnp.where(jnp.arange(LANES) < LANES // 2, -1.0, 1.0).astype(F32)
    return cos, sin * sign[None, :]


def _tiles(geo):
    unit = min(geo.len_a, geo.len_b)
    tm = min(512, unit)
    return dict(tm=tm, tc=min(512, unit), td=min(512, unit), tf=min(256, unit), bm=512)


def kernel(x_prompt, x_sample, norm_mix_g, w_in, b_in, attn_sink, w_o_attn, conv_dw_w, conv_dw_b,
           conv_ln_g, conv_ln_b, w_pw2, b_pw2, w_out, norm_ffn_g, w_router, b_router, w_gu, b_gu,
           w_down, b_down, norm_final_g):
    assert w_in.shape[0] == 1, "single trunk layer"
    geo = Geo(x_prompt.shape[0], x_prompt.shape[1], x_sample.shape[0], x_sample.shape[1])
    T = geo.total
    ts = _tiles(geo)
    xa = x_prompt.reshape(-1, D_MODEL)
    xb = x_sample.reshape(-1, D_MODEL)

    w_perm, w_v = _permute_in_proj(w_in[0])
    b_perm, b_v = _permute_in_proj(b_in)
    cos_t, sin_t = _rope_tables(max(geo.len_a, geo.len_b))

    q, kd, vt, glu, gates = _in_proj(xa, xb, norm_mix_g, w_perm.astype(BF16), b_perm,
                                     w_v.T.astype(BF16), b_v.T, cos_t, sin_t, geo, ts["tm"])
    attn = _attention(q, kd, vt, attn_sink[0], geo)
    w_rep = jnp.repeat(conv_dw_w[0], SUBLANES, axis=0)
    conv = _conv_branch(glu, w_rep, conv_dw_b, conv_ln_g, conv_ln_b, geo, ts["tc"])

    wr_t = w_router[0].T
    wr_hi = wr_t.astype(BF16)
    wr_lo = (wr_t - wr_hi.astype(F32)).astype(BF16)
    tri = jnp.triu(jnp.ones((ts["tm"], ts["tm"]), BF16), 1)
    x1, h2, idx, gate_t, rank, counts = _mix_route(
        xa, xb, attn, conv, gates, w_o_attn[0].astype(BF16), w_pw2[0].astype(BF16), b_pw2,
        w_out[0].astype(BF16), norm_ffn_g, wr_hi, wr_lo, b_router[0][:, None], tri, geo, ts["tm"])

    bm = ts["bm"]
    n_blocks = (T * TOP_K) // bm + N_EXPERTS
    cnt = counts[:, 0].astype(I32)
    padded = ((cnt + bm - 1) // bm) * bm
    pad_end = jnp.cumsum(padded)
    pad_start = pad_end - padded
    expert_ids = jnp.arange(N_EXPERTS, dtype=I32)[:, None, None]
    dest = rank + jnp.sum(jnp.where(idx[None] == expert_ids, pad_start[:, None, None], 0), axis=0)
    block_start = jnp.arange(n_blocks, dtype=I32) * bm
    block_exp = jnp.minimum(jnp.sum((pad_end[None, :] <= block_start[:, None]).astype(I32), axis=1),
                            N_EXPERTS - 1)
    n_used = (pad_end[-1:] // bm).astype(I32)

    xs = _dispatch(h2, dest, pad_start, cnt, pad_end, n_blocks, bm, ts["td"])
    ys = _experts(block_exp, n_used, xs, w_gu[0].astype(BF16), b_gu[0][:, None, :],
                  w_down[0].astype(BF16), b_down[0][:, None, :], bm)
    ya, yb = _combine(dest, x1, gate_t.T, norm_final_g[None, :], ys, geo, ts["tf"])
    return (ya.reshape(x_prompt.shape), yb.reshape(x_sample.shape))
```

```python
import functools
from typing import NamedTuple

import jax
import jax.numpy as jnp
from jax import lax
from jax.experimental import pallas as pl
from jax.experimental.pallas import tpu as pltpu
from jax.experimental.pallas import tpu_sc as plsc

F32 = jnp.float32
BF16 = jnp.bfloat16
I32 = jnp.int32

D_MODEL = 1024
HEAD_DIM = 64
N_Q_HEADS = 16
N_KV_HEADS = 4
WINDOW = 128
ATTN_BLOCK = 128
ROPE_THETA = 10000.0
CONV_WIDTH = 31
CONV_PAD = CONV_WIDTH // 2
N_EXPERTS = 32
TOP_K = 4
D_FF = D_MODEL
SWIGLU_LIMIT = 7.0
SWIGLU_ALPHA = 1.702
NORM_EPS = 1e-5
NEG_INF = -1e30

Q_COLS = N_Q_HEADS * HEAD_DIM
KV_COLS = N_KV_HEADS * HEAD_DIM
LANES = 128
SUBLANES = 8
KV_DUP_COLS = N_KV_HEADS * LANES
HALO = 16
VMEM_LIMIT = 56 * 1024 * 1024
SC_ROWS_PER_STEP = 64


class Geo(NamedTuple):
    n_a: int
    len_a: int
    n_b: int
    len_b: int

    @property
    def rows_a(self):
        return self.n_a * self.len_a

    @property
    def total(self):
        return self.rows_a + self.n_b * self.len_b


def _seq_bounds(geo, r):
    in_a = r < geo.rows_a
    start_a = (r // geo.len_a) * geo.len_a
    start_b = geo.rows_a + ((r - geo.rows_a) // geo.len_b) * geo.len_b
    start = jnp.where(in_a, start_a, start_b)
    end = start + jnp.where(in_a, geo.len_a, geo.len_b)
    return start, end


def _params(*sem):
    return pltpu.CompilerParams(dimension_semantics=sem, vmem_limit_bytes=VMEM_LIMIT)


def _two_group_specs(geo, tile, width):
    n_a = geo.rows_a // tile
    return n_a, [pl.BlockSpec((tile, width), lambda i: (jnp.minimum(i, n_a - 1), 0)),
                 pl.BlockSpec((tile, width), lambda i: (jnp.maximum(i - n_a, 0), 0))]


_C_Q = 0
_C_K = _C_Q + Q_COLS
_C_GA = _C_K + KV_DUP_COLS
_C_GG = _C_GA + D_MODEL
_C_GATE = _C_GG + D_MODEL
_N_IN = _C_GATE + 2 * D_MODEL
_PROJ_CHUNK = 512
N_SLAB = D_MODEL // LANES
_NT = (((1,), (1,)), ((), ()))


def _in_proj_body(n_a, xa_ref, xb_ref, g_ref, w_ref, b_ref, wvt_ref, bvt_ref, cos_ref, sin_ref,
                  q_ref, kd_ref, vt_ref, glu_ref, gate_ref):
    x = jnp.where(pl.program_id(0) < n_a, xa_ref[...], xb_ref[...])
    ms = jnp.mean(x * x, axis=-1, keepdims=True)
    h = (x * lax.rsqrt(ms + NORM_EPS) * g_ref[...]).astype(BF16)
    cos = cos_ref[...]
    sin = sin_ref[...]

    def proj(c0):
        return (jnp.dot(h, w_ref[:, c0:c0 + _PROJ_CHUNK], preferred_element_type=F32)
                + b_ref[:, c0:c0 + _PROJ_CHUNK])

    def rope_store(z, out_ref, o0, scale):
        for c in range(_PROJ_CHUNK // LANES):
            zc = z[:, c * LANES:(c + 1) * LANES]
            r = zc * cos + pltpu.roll(zc, LANES // 2, 1) * sin
            if scale != 1.0:
                r = r * scale
            out_ref[:, o0 + c * LANES:o0 + (c + 1) * LANES] = r.astype(out_ref.dtype)

    for c in range(Q_COLS // _PROJ_CHUNK):
        rope_store(proj(_C_Q + c * _PROJ_CHUNK), q_ref, c * _PROJ_CHUNK, HEAD_DIM ** -0.5)
    for c in range(KV_DUP_COLS // _PROJ_CHUNK):
        rope_store(proj(_C_K + c * _PROJ_CHUNK), kd_ref, c * _PROJ_CHUNK, 1.0)
    vt = lax.dot_general(wvt_ref[...], h, _NT, preferred_element_type=F32) + bvt_ref[...]
    vt_ref[...] = vt.astype(BF16)
    per = _PROJ_CHUNK // LANES
    for c in range(D_MODEL // _PROJ_CHUNK):
        a = proj(_C_GA + c * _PROJ_CHUNK)
        g = proj(_C_GG + c * _PROJ_CHUNK)
        glu = a * jax.nn.sigmoid(g)
        for s in range(per):
            glu_ref[c * per + s] = glu[:, s * LANES:(s + 1) * LANES]
    for c in range(2 * D_MODEL // _PROJ_CHUNK):
        gate_ref[:, c * _PROJ_CHUNK:(c + 1) * _PROJ_CHUNK] = jax.nn.sigmoid(
            proj(_C_GATE + c * _PROJ_CHUNK)).astype(BF16)


def _in_proj(xa, xb, g_mix, w_perm, b_perm, wvt, bvt, cos_t, sin_t, geo, tm):
    T = geo.total

    def pos_map(i):
        r0 = i * tm
        start, _ = _seq_bounds(geo, r0)
        return ((r0 - start) // tm, 0)

    const = lambda i: (0, 0)
    row = lambda i: (i, 0)
    n_a, x_specs = _two_group_specs(geo, tm, D_MODEL)
    return pl.pallas_call(
        functools.partial(_in_proj_body, n_a),
        grid=(T // tm,),
        in_specs=x_specs + [
            pl.BlockSpec((1, D_MODEL), const),
            pl.BlockSpec((D_MODEL, _N_IN), const, pipeline_mode=pl.Buffered(1)),
            pl.BlockSpec((1, _N_IN), const),
            pl.BlockSpec((KV_DUP_COLS, D_MODEL), const),
            pl.BlockSpec((KV_DUP_COLS, 1), const),
            pl.BlockSpec((tm, LANES), pos_map),
            pl.BlockSpec((tm, LANES), pos_map),
        ],
        out_specs=[
            pl.BlockSpec((tm, Q_COLS), row),
            pl.BlockSpec((tm, KV_DUP_COLS), row),
            pl.BlockSpec((KV_DUP_COLS, tm), lambda i: (0, i)),
            pl.BlockSpec((N_SLAB, tm, LANES), lambda i: (0, i, 0)),
            pl.BlockSpec((tm, 2 * D_MODEL), row),
        ],
        out_shape=[
            jax.ShapeDtypeStruct((T, Q_COLS), BF16),
            jax.ShapeDtypeStruct((T, KV_DUP_COLS), BF16),
            jax.ShapeDtypeStruct((KV_DUP_COLS, T), BF16),
            jax.ShapeDtypeStruct((N_SLAB, T, LANES), F32),
            jax.ShapeDtypeStruct((T, 2 * D_MODEL), BF16),
        ],
        compiler_params=_params("parallel"),
        name="in_proj",
    )(xa, xb, g_mix, w_perm, b_perm, wvt, bvt, cos_t, sin_t)


_ONES_ROWS = 16


def _attn_body(geo, sink_ref, q_ref, kp_ref, kc_ref, kn_ref, vp_ref, vc_ref, vn_ref, o_ref):
    i = pl.program_id(0)
    r0 = i * ATTN_BLOCK
    start, end = _seq_bounds(geo, r0)
    has_prev = r0 > start
    has_next = r0 + ATTN_BLOCK < end
    group = N_Q_HEADS // N_KV_HEADS
    nq = group * ATTN_BLOCK

    key = lax.broadcasted_iota(I32, (ATTN_BLOCK, nq), 0)
    col = lax.broadcasted_iota(I32, (ATTN_BLOCK, nq), 1)
    qry = col % ATTN_BLOCK
    bias_prev = jnp.where((key >= qry) & has_prev, 0.0, NEG_INF)
    bias_next = jnp.where((key <= qry) & has_next, 0.0, NEG_INF)
    head_of_col = lax.broadcasted_iota(I32, (1, nq), 1) // ATTN_BLOCK

    lane = lax.broadcasted_iota(I32, (ATTN_BLOCK, LANES), 1)
    even_head = (lane % HEAD_DIM) < (HEAD_DIM // 2)
    ones = jnp.ones((_ONES_ROWS, 3 * ATTN_BLOCK), BF16)
    b = ATTN_BLOCK
    hd = HEAD_DIM

    for g in range(N_KV_HEADS):
        ls = slice(g * LANES, (g + 1) * LANES)
        qa = q_ref[:, (2 * g) * LANES:(2 * g + 1) * LANES]
        qb = q_ref[:, (2 * g + 1) * LANES:(2 * g + 2) * LANES]
        zero = jnp.zeros_like(qa)
        q4 = jnp.concatenate([jnp.where(even_head, qa, zero), jnp.where(even_head, zero, qa),
                              jnp.where(even_head, qb, zero), jnp.where(even_head, zero, qb)], axis=0)
        k = jnp.concatenate([kp_ref[:, ls], kc_ref[:, ls], kn_ref[:, ls]], axis=0)
        st = lax.dot_general(k, q4, _NT, preferred_element_type=F32)
        s_prev = st[0:b] + bias_prev
        s_cur = st[b:2 * b]
        s_next = st[2 * b:3 * b] + bias_next
        sink = jnp.full((1, nq), sink_ref[group * g], F32)
        for h in range(1, group):
            sink = jnp.where(head_of_col == h, sink_ref[group * g + h], sink)
        m = jnp.maximum(jnp.maximum(jnp.max(s_prev, axis=0, keepdims=True),
                                    jnp.max(s_cur, axis=0, keepdims=True)),
                        jnp.maximum(jnp.max(s_next, axis=0, keepdims=True), sink))
        p = jnp.concatenate([jnp.exp(s_prev - m).astype(BF16), jnp.exp(s_cur - m).astype(BF16),
                             jnp.exp(s_next - m).astype(BF16)], axis=0)
        vt = jnp.concatenate([vp_ref[ls, :], vc_ref[ls, :], vn_ref[ls, :]], axis=1)
        ot = jnp.dot(jnp.concatenate([vt, ones], axis=0), p, preferred_element_type=F32)
        denom = ot[2 * hd:2 * hd + 1] + jnp.exp(sink - m)
        on = ot[0:2 * hd] * (1.0 / denom)
        pair_a = jnp.concatenate([on[0:hd, 0:b], on[hd:2 * hd, b:2 * b]], axis=0)
        pair_b = jnp.concatenate([on[0:hd, 2 * b:3 * b], on[hd:2 * hd, 3 * b:4 * b]], axis=0)
        o_ref[:, (2 * g) * LANES:(2 * g + 1) * LANES] = pair_a.T.astype(BF16)
        o_ref[:, (2 * g + 1) * LANES:(2 * g + 2) * LANES] = pair_b.T.astype(BF16)


def _attention(q, kd, vt, sink, geo):
    T = q.shape[0]
    nb = T // ATTN_BLOCK
    prev = lambda i: jnp.maximum(i - 1, 0)
    cur = lambda i: i
    nxt = lambda i: jnp.minimum(i + 1, nb - 1)
    k_spec = lambda m: pl.BlockSpec((ATTN_BLOCK, KV_DUP_COLS), lambda i: (m(i), 0))
    v_spec = lambda m: pl.BlockSpec((KV_DUP_COLS, ATTN_BLOCK), lambda i: (0, m(i)))
    return pl.pallas_call(
        functools.partial(_attn_body, geo),
        grid=(nb,),
        in_specs=[
            pl.BlockSpec(memory_space=pltpu.SMEM),
            pl.BlockSpec((ATTN_BLOCK, Q_COLS), lambda i: (i, 0)),
            k_spec(prev), k_spec(cur), k_spec(nxt),
            v_spec(prev), v_spec(cur), v_spec(nxt),
        ],
        out_specs=pl.BlockSpec((ATTN_BLOCK, Q_COLS), lambda i: (i, 0)),
        out_shape=jax.ShapeDtypeStruct((T, Q_COLS), BF16),
        compiler_params=_params("parallel"),
        name="window_attn",
    )(sink, q, kd, kd, kd, vt, vt, vt)


_CONV_ROWS = 64
_LN_ROWS = 32


def _conv_body(geo, tc, z_ref, zp_ref, zn_ref, w_ref, dwb_ref, lng_ref, lnb_ref, o_ref,
               buf_ref, y_ref):
    i = pl.program_id(0)
    r0 = i * tc
    start, end = _seq_bounds(geo, r0)
    has_prev = (r0 > start).astype(F32)
    has_next = (r0 + tc < end).astype(F32)
    rows = tc + 2 * HALO
    buf_ref[:, 0:HALO, :] = zp_ref[...] * has_prev
    buf_ref[:, HALO:HALO + tc, :] = z_ref[...]
    buf_ref[:, HALO + tc:rows, :] = zn_ref[...] * has_next

    rep = _CONV_ROWS // SUBLANES

    def conv_chunk(j):
        s0 = pl.multiple_of(j * _CONV_ROWS, _CONV_ROWS)
        for c in range(N_SLAB):
            ls = slice(c * LANES, (c + 1) * LANES)
            acc = jnp.zeros((_CONV_ROWS, LANES), F32)
            for k in range(CONV_WIDTH):
                zt = buf_ref[c, pl.ds(s0 + (HALO - CONV_PAD + k), _CONV_ROWS, stride=1), :]
                wt = w_ref[k * SUBLANES:(k + 1) * SUBLANES, ls]
                acc = acc + zt * jnp.concatenate([wt] * rep, axis=0)
            y_ref[pl.ds(s0, _CONV_ROWS), ls] = acc

    def ln_chunk(j):
        for h in range(_CONV_ROWS // _LN_ROWS):
            s0 = pl.multiple_of(j * _CONV_ROWS + h * _LN_ROWS, _LN_ROWS)
            y = y_ref[pl.ds(s0, _LN_ROWS), :] + dwb_ref[...]
            mu = jnp.mean(y, axis=-1, keepdims=True)
            yc = y - mu
            var = jnp.mean(yc * yc, axis=-1, keepdims=True)
            yn = yc * lax.rsqrt(var + NORM_EPS) * lng_ref[...] + lnb_ref[...]
            o_ref[pl.ds(s0, _LN_ROWS), :] = (yn * jax.nn.sigmoid(yn)).astype(BF16)

    n = tc // _CONV_ROWS
    conv_chunk(0)

    def body(j, carry):
        ln_chunk(j - 1)
        conv_chunk(j)
        return carry

    lax.fori_loop(1, n, body, 0)
    ln_chunk(n - 1)


def _conv_branch(glu, w_rep, dw_b, ln_g, ln_b, geo, tc):
    T = glu.shape[1]
    nh = T // HALO
    per = tc // HALO
    const = lambda i: (0, 0)
    return pl.pallas_call(
        functools.partial(_conv_body, geo, tc),
        grid=(T // tc,),
        in_specs=[
            pl.BlockSpec((N_SLAB, tc, LANES), lambda i: (0, i, 0)),
            pl.BlockSpec((N_SLAB, HALO, LANES), lambda i: (0, jnp.maximum(i * per - 1, 0), 0)),
            pl.BlockSpec((N_SLAB, HALO, LANES), lambda i: (0, jnp.minimum((i + 1) * per, nh - 1), 0)),
            pl.BlockSpec((CONV_WIDTH * SUBLANES, D_MODEL), const),
            pl.BlockSpec((1, D_MODEL), const),
            pl.BlockSpec((1, D_MODEL), const),
            pl.BlockSpec((1, D_MODEL), const),
        ],
        out_specs=pl.BlockSpec((tc, D_MODEL), lambda i: (i, 0)),
        out_shape=jax.ShapeDtypeStruct((T, D_MODEL), BF16),
        scratch_shapes=[pltpu.VMEM((N_SLAB, tc + 2 * HALO, LANES), F32),
                        pltpu.VMEM((tc, D_MODEL), F32)],
        compiler_params=_params("parallel"),
        name="conv_branch",
    )(glu, glu, glu, w_rep, dw_b, ln_g, ln_b)


def _mix_body(n_a, xa_ref, xb_ref, o_ref, c_ref, gate_ref, wo_ref, wpw_ref, bpw_ref, wout_ref,
              gffn_ref, wrh_ref, wrl_ref, br_ref, tri_ref,
              x1_ref, h2_ref, idx_ref, gt_ref, rank_ref, cnt_ref, carry_ref):
    tm = xa_ref.shape[0]
    x = jnp.where(pl.program_id(0) < n_a, xa_ref[...], xb_ref[...])

    @pl.when(pl.program_id(0) == 0)
    def _():
        carry_ref[...] = jnp.zeros_like(carry_ref)

    attn = jnp.dot(o_ref[...], wo_ref[...], preferred_element_type=F32)
    conv = jnp.dot(c_ref[...], wpw_ref[...], preferred_element_type=F32) + bpw_ref[...]
    g_attn = gate_ref[:, 0:D_MODEL].astype(F32)
    g_conv = gate_ref[:, D_MODEL:2 * D_MODEL].astype(F32)
    mix = (g_attn * attn + g_conv * conv).astype(BF16)
    x1 = x + jnp.dot(mix, wout_ref[...], preferred_element_type=F32)
    x1_ref[...] = x1
    ms = jnp.mean(x1 * x1, axis=-1, keepdims=True)
    h2 = x1 * lax.rsqrt(ms + NORM_EPS) * gffn_ref[...]
    h2_ref[...] = h2

    h_hi = h2.astype(BF16)
    h_lo = (h2 - h_hi.astype(F32)).astype(BF16)
    logits = (lax.dot_general(wrh_ref[...], h_hi, _NT, preferred_element_type=F32)
              + lax.dot_general(wrh_ref[...], h_lo, _NT, preferred_element_type=F32)
              + lax.dot_general(wrl_ref[...], h_hi, _NT, preferred_element_type=F32)
              + br_ref[...])

    eidx = lax.broadcasted_iota(I32, (N_EXPERTS, tm), 0)
    vals = logits
    picked, top_vals, top_idx = [], [], []
    for _ in range(TOP_K):
        m = jnp.max(vals, axis=0, keepdims=True)
        idx = jnp.min(jnp.where(vals == m, eidx, N_EXPERTS), axis=0, keepdims=True)
        sel = eidx == idx
        vals = jnp.where(sel, -jnp.inf, vals)
        picked.append(sel)
        top_vals.append(m)
        top_idx.append(idx)

    exps = [jnp.exp(v - top_vals[0]) for v in top_vals]
    tot = exps[0] + exps[1] + exps[2] + exps[3]
    onehot = (picked[0] | picked[1] | picked[2] | picked[3])
    prefix = jnp.dot(onehot.astype(BF16), tri_ref[...], preferred_element_type=F32) + carry_ref[...]
    for j in range(TOP_K):
        idx_ref[j:j + 1, :] = top_idx[j]
        gt_ref[j:j + 1, :] = exps[j] / tot
        rank_ref[j:j + 1, :] = jnp.sum(jnp.where(picked[j], prefix, 0.0), axis=0,
                                       keepdims=True).astype(I32)
    carry_ref[...] = carry_ref[...] + jnp.sum(onehot.astype(F32), axis=1, keepdims=True)
    cnt_ref[...] = jnp.broadcast_to(carry_ref[...], cnt_ref.shape)


def _mix_route(xa, xb, o, c, gates, wo, wpw, bpw, wout, gffn, wr_hi, wr_lo, br, tri, geo, tm):
    T = geo.total
    row = lambda i: (i, 0)
    col = lambda i: (0, i)
    const = lambda i: (0, 0)
    wspec = lambda shape: pl.BlockSpec(shape, const, pipeline_mode=pl.Buffered(1))
    n_a, x_specs = _two_group_specs(geo, tm, D_MODEL)
    return pl.pallas_call(
        functools.partial(_mix_body, n_a),
        grid=(T // tm,),
        in_specs=x_specs + [
            pl.BlockSpec((tm, Q_COLS), row),
            pl.BlockSpec((tm, D_MODEL), row),
            pl.BlockSpec((tm, 2 * D_MODEL), row),
            wspec((Q_COLS, D_MODEL)),
            wspec((D_MODEL, D_MODEL)),
            pl.BlockSpec((1, D_MODEL), const),
            wspec((D_MODEL, D_MODEL)),
            pl.BlockSpec((1, D_MODEL), const),
            pl.BlockSpec((N_EXPERTS, D_MODEL), const),
            pl.BlockSpec((N_EXPERTS, D_MODEL), const),
            pl.BlockSpec((N_EXPERTS, 1), const),
            pl.BlockSpec((tm, tm), const),
        ],
        out_specs=[
            pl.BlockSpec((tm, D_MODEL), row),
            pl.BlockSpec((tm, D_MODEL), row),
            pl.BlockSpec((TOP_K, tm), col),
            pl.BlockSpec((TOP_K, tm), col),
            pl.BlockSpec((TOP_K, tm), col),
            pl.BlockSpec((N_EXPERTS, LANES), const),
        ],
        out_shape=[
            jax.ShapeDtypeStruct((T, D_MODEL), F32),
            jax.ShapeDtypeStruct((T, D_MODEL), F32),
            jax.ShapeDtypeStruct((TOP_K, T), I32),
            jax.ShapeDtypeStruct((TOP_K, T), F32),
            jax.ShapeDtypeStruct((TOP_K, T), I32),
            jax.ShapeDtypeStruct((N_EXPERTS, LANES), F32),
        ],
        scratch_shapes=[pltpu.VMEM((N_EXPERTS, 1), F32)],
        compiler_params=_params("arbitrary"),
        name="mix_route",
    )(xa, xb, o, c, gates, wo, wpw, bpw, wout, gffn, wr_hi, wr_lo, br, tri)


def _sc_workers():
    info = plsc.get_sparse_core_info()
    return info.num_cores, info.num_cores * info.num_subcores


def _sc_dispatch(h, dest, n_out):
    t_rows, width = h.shape
    nc, nw = _sc_workers()
    per_w = t_rows // nw
    step = SC_ROWS_PER_STEP
    assert per_w * nw == t_rows and per_w % step == 0
    mesh = plsc.VectorSubcoreMesh(core_axis_name="c", subcore_axis_name="s")

    @functools.partial(
        pl.kernel, mesh=mesh,
        out_type=jax.ShapeDtypeStruct((n_out, width), h.dtype),
        scratch_types=[pltpu.VMEM((step,), I32)] * TOP_K
        + [pltpu.VMEM((step, width), h.dtype), pltpu.SemaphoreType.DMA],
    )
    def scatter_rows(h_hbm, dest_hbm, out_hbm, i0, i1, i2, i3, rows_v, sem):
        del sem
        base = (lax.axis_index("s") * nc + lax.axis_index("c")) * per_w

        @pl.loop(0, per_w // step)
        def _(i):
            off = base + i * step
            pltpu.sync_copy(h_hbm.at[pl.ds(off, step)], rows_v)
            for j, idx_v in enumerate((i0, i1, i2, i3)):
                pltpu.sync_copy(dest_hbm.at[j, pl.ds(off, step)], idx_v)
                pltpu.sync_copy(rows_v, out_hbm.at[idx_v])

    return scatter_rows(h, dest)


def _sc_gather(table, idx):
    n = idx.shape[0]
    width = table.shape[1]
    nc, nw = _sc_workers()
    per_w = n // nw
    step = SC_ROWS_PER_STEP
    assert per_w * nw == n and per_w % step == 0
    mesh = plsc.VectorSubcoreMesh(core_axis_name="c", subcore_axis_name="s")

    @functools.partial(
        pl.kernel, mesh=mesh,
        out_type=jax.ShapeDtypeStruct((n, width), table.dtype),
        scratch_types=[pltpu.VMEM((step,), I32), pltpu.VMEM((step, width), table.dtype),
                       pltpu.SemaphoreType.DMA],
    )
    def gather_rows(table_hbm, idx_hbm, out_hbm, idx_v, rows_v, sem):
        base = (lax.axis_index("s") * nc + lax.axis_index("c")) * per_w

        @pl.loop(0, per_w // step)
        def _(i):
            off = base + i * step
            pltpu.sync_copy(idx_hbm.at[pl.ds(off, step)], idx_v)
            pltpu.async_copy(table_hbm.at[idx_v], rows_v, sem).wait()
            pltpu.sync_copy(rows_v, out_hbm.at[pl.ds(off, step)])

    return gather_rows(table, idx)


def _expert_body(bexp_ref, nused_ref, valid_ref, xs_ref, wgu_ref, bgu_ref, wd_ref, bd_ref, o_ref):
    n = pl.program_id(0)

    @pl.when(n < nused_ref[0])
    def _():
        row = lax.broadcasted_iota(I32, (xs_ref.shape[0], 1), 0)
        x = jnp.where(row < valid_ref[n], xs_ref[...], 0.0).astype(BF16)
        gu = jnp.dot(x, wgu_ref[...], preferred_element_type=F32) + bgu_ref[...]
        g = jnp.minimum(gu[:, 0:D_FF], SWIGLU_LIMIT)
        u = jnp.clip(gu[:, D_FF:2 * D_FF], -SWIGLU_LIMIT, SWIGLU_LIMIT)
        act = g * jax.nn.sigmoid(SWIGLU_ALPHA * g) * (u + 1.0)
        o_ref[...] = jnp.dot(act.astype(BF16), wd_ref[...], preferred_element_type=F32) + bd_ref[...]

    @pl.when(n >= nused_ref[0])
    def _():
        o_ref[...] = jnp.zeros_like(o_ref)


def _experts(block_exp, n_used, block_valid, xs, wgu, bgu, wd, bd, bm):
    P = xs.shape[0]
    nblk = P // bm
    xmap = lambda n, be, nu, bv: (jnp.minimum(n, nu[0] - 1), 0)
    emap = lambda n, be, nu, bv: (be[n], 0, 0)
    grid_spec = pltpu.PrefetchScalarGridSpec(
        num_scalar_prefetch=3,
        grid=(nblk,),
        in_specs=[
            pl.BlockSpec((bm, D_MODEL), xmap),
            pl.BlockSpec((None, D_MODEL, 2 * D_FF), emap),
            pl.BlockSpec((None, 1, 2 * D_FF), emap),
            pl.BlockSpec((None, D_FF, D_MODEL), emap),
            pl.BlockSpec((None, 1, D_MODEL), emap),
        ],
        out_specs=pl.BlockSpec((bm, D_MODEL), lambda n, be, nu, bv: (n, 0)),
    )
    return pl.pallas_call(
        _expert_body,
        grid_spec=grid_spec,
        out_shape=jax.ShapeDtypeStruct((P, D_MODEL), F32),
        compiler_params=_params("arbitrary"),
        name="experts",
    )(block_exp, n_used, block_valid, xs, wgu, bgu, wd, bd)


def _combine_body(n_a, x1_ref, g0_ref, g1_ref, g2_ref, g3_ref, gate_ref, gfin_ref, ya_ref, yb_ref):
    gate = gate_ref[...]
    y = x1_ref[...]
    for j, g_ref in enumerate((g0_ref, g1_ref, g2_ref, g3_ref)):
        y = y + gate[:, j:j + 1] * g_ref[...]
    ms = jnp.mean(y * y, axis=-1, keepdims=True)
    y = y * lax.rsqrt(ms + NORM_EPS) * gfin_ref[...]

    @pl.when(pl.program_id(0) < n_a)
    def _():
        ya_ref[...] = y

    @pl.when(pl.program_id(0) >= n_a)
    def _():
        yb_ref[...] = y


def _combine(x1, gathered, gate_tok, gfin, geo, tf):
    T = x1.shape[0]
    nt = T // tf
    row = lambda i: (i, 0)
    n_a, y_specs = _two_group_specs(geo, tf, D_MODEL)
    choice = lambda j: pl.BlockSpec((tf, D_MODEL), lambda i: (j * nt + i, 0))
    return pl.pallas_call(
        functools.partial(_combine_body, n_a),
        grid=(nt,),
        in_specs=[pl.BlockSpec((tf, D_MODEL), row)] + [choice(j) for j in range(TOP_K)] + [
            pl.BlockSpec((tf, TOP_K), row),
            pl.BlockSpec((1, D_MODEL), lambda i: (0, 0)),
        ],
        out_specs=y_specs,
        out_shape=[jax.ShapeDtypeStruct((geo.rows_a, D_MODEL), F32),
                   jax.ShapeDtypeStruct((T - geo.rows_a, D_MODEL), F32)],
        compiler_params=_params("arbitrary"),
        name="combine",
    )(x1, gathered, gathered, gathered, gathered, gate_tok, gfin)


def _permute_in_proj(w):
    lead = w.shape[:-1]
    half = HEAD_DIM // 2
    q = w[..., :Q_COLS].reshape(*lead, N_Q_HEADS // 2, 2, 2, half)
    q = jnp.swapaxes(q, -3, -2).reshape(*lead, Q_COLS)
    k = w[..., Q_COLS:Q_COLS + KV_COLS].reshape(*lead, N_KV_HEADS, 2, 1, half)
    k = jnp.broadcast_to(k, (*lead, N_KV_HEADS, 2, 2, half)).reshape(*lead, KV_DUP_COLS)
    v = w[..., Q_COLS + KV_COLS:Q_COLS + 2 * KV_COLS].reshape(*lead, N_KV_HEADS, 1, HEAD_DIM)
    v = jnp.broadcast_to(v, (*lead, N_KV_HEADS, 2, HEAD_DIM)).reshape(*lead, KV_DUP_COLS)
    return jnp.concatenate([q, k, w[..., Q_COLS + 2 * KV_COLS:]], axis=-1), v


def _rope_tables(n_pos):
    half = HEAD_DIM // 2
    inv_freq = 1.0 / (ROPE_THETA ** (jnp.arange(half, dtype=F32) * (2.0 / HEAD_DIM)))
    ang = jnp.arange(n_pos, dtype=F32)[:, None] * inv_freq[None, :]
    cos = jnp.tile(jnp.cos(ang), (1, LANES // half))
    sin = jnp.tile(jnp.sin(ang), (1, LANES // half))
    sign = jnp.where(jnp.arange(LANES) < LANES // 2, -1.0, 1.0).astype(F32)
    return cos, sin * sign[None, :]


def _tiles(geo):
    unit = min(geo.len_a, geo.len_b)
    tile = min(512, unit)
    return dict(tm=tile, tc=tile, tf=tile, bm=512)


def kernel(x_prompt, x_sample, norm_mix_g, w_in, b_in, attn_sink, w_o_attn, conv_dw_w, conv_dw_b,
           conv_ln_g, conv_ln_b, w_pw2, b_pw2, w_out, norm_ffn_g, w_router, b_router, w_gu, b_gu,
           w_down, b_down, norm_final_g):
    assert w_in.shape[0] == 1, "single trunk layer"
    geo = Geo(x_prompt.shape[0], x_prompt.shape[1], x_sample.shape[0], x_sample.shape[1])
    T = geo.total
    ts = _tiles(geo)
    xa = x_prompt.reshape(-1, D_MODEL)
    xb = x_sample.reshape(-1, D_MODEL)

    w_perm, w_v = _permute_in_proj(w_in[0])
    b_perm, b_v = _permute_in_proj(b_in)
    cos_t, sin_t = _rope_tables(max(geo.len_a, geo.len_b))

    q, kd, vt, glu, gates = _in_proj(xa, xb, norm_mix_g, w_perm.astype(BF16), b_perm,
                                     w_v.T.astype(BF16), b_v.T, cos_t, sin_t, geo, ts["tm"])
    attn = _attention(q, kd, vt, attn_sink[0], geo)
    w_rep = jnp.repeat(conv_dw_w[0], SUBLANES, axis=0)
    conv = _conv_branch(glu, w_rep, conv_dw_b, conv_ln_g, conv_ln_b, geo, ts["tc"])

    wr_t = w_router[0].T
    wr_hi = wr_t.astype(BF16)
    wr_lo = (wr_t - wr_hi.astype(F32)).astype(BF16)
    tri = jnp.triu(jnp.ones((ts["tm"], ts["tm"]), BF16), 1)
    x1, h2, idx, gate_t, rank, counts = _mix_route(
        xa, xb, attn, conv, gates, w_o_attn[0].astype(BF16), w_pw2[0].astype(BF16), b_pw2,
        w_out[0].astype(BF16), norm_ffn_g, wr_hi, wr_lo, b_router[0][:, None], tri, geo, ts["tm"])

    bm = ts["bm"]
    n_blocks = (T * TOP_K) // bm + N_EXPERTS
    cnt = counts[:, 0].astype(I32)
    padded = ((cnt + bm - 1) // bm) * bm
    pad_end = jnp.cumsum(padded)
    pad_start = pad_end - padded
    expert_ids = jnp.arange(N_EXPERTS, dtype=I32)
    dest = rank + jnp.sum(jnp.where(idx[None] == expert_ids[:, None, None],
                                    pad_start[:, None, None], 0), axis=0)
    block_start = jnp.arange(n_blocks, dtype=I32) * bm
    block_exp = jnp.minimum(jnp.sum((pad_end[None, :] <= block_start[:, None]).astype(I32), axis=1),
                            N_EXPERTS - 1)
    n_used = (pad_end[-1:] // bm).astype(I32)
    seg_end = jnp.sum(jnp.where(block_exp[:, None] == expert_ids[None, :],
                                (pad_start + cnt)[None, :], 0), axis=1)
    block_valid = jnp.clip(seg_end - block_start, 0, bm).astype(I32)

    xs = _sc_dispatch(h2, dest, n_blocks * bm)
    ys = _experts(block_exp, n_used, block_valid, xs, w_gu[0].astype(BF16), b_gu[0][:, None, :],
                  w_down[0].astype(BF16), b_down[0][:, None, :], bm)
    gathered = _sc_gather(ys, dest.reshape(-1))
    ya, yb = _combine(x1, gathered, gate_t.T, norm_final_g[None, :], geo, ts["tf"])
    return (ya.reshape(x_prompt.shape), yb.reshape(x_sample.shape))
```

```python
import functools
from typing import NamedTuple

import jax
import jax.numpy as jnp
from jax import lax
from jax.experimental import pallas as pl
from jax.experimental.pallas import tpu as pltpu
from jax.experimental.pallas import tpu_sc as plsc

F32 = jnp.float32
BF16 = jnp.bfloat16
I32 = jnp.int32

D_MODEL = 1024
HEAD_DIM = 64
N_Q_HEADS = 16
N_KV_HEADS = 4
WINDOW = 128
ATTN_BLOCK = 128
ROPE_THETA = 10000.0
CONV_WIDTH = 31
CONV_PAD = CONV_WIDTH // 2
N_EXPERTS = 32
TOP_K = 4
D_FF = D_MODEL
SWIGLU_LIMIT = 7.0
SWIGLU_ALPHA = 1.702
NORM_EPS = 1e-5
NEG_INF = -1e30

Q_COLS = N_Q_HEADS * HEAD_DIM
KV_COLS = N_KV_HEADS * HEAD_DIM
LANES = 128
SUBLANES = 8
KV_DUP_COLS = N_KV_HEADS * LANES
HALO = 16
VMEM_LIMIT = 56 * 1024 * 1024
SC_ROWS_PER_STEP = 128


class Geo(NamedTuple):
    n_a: int
    len_a: int
    n_b: int
    len_b: int

    @property
    def rows_a(self):
        return self.n_a * self.len_a

    @property
    def total(self):
        return self.rows_a + self.n_b * self.len_b


def _seq_bounds(geo, r):
    in_a = r < geo.rows_a
    start_a = (r // geo.len_a) * geo.len_a
    start_b = geo.rows_a + ((r - geo.rows_a) // geo.len_b) * geo.len_b
    start = jnp.where(in_a, start_a, start_b)
    end = start + jnp.where(in_a, geo.len_a, geo.len_b)
    return start, end


def _params(*sem):
    return pltpu.CompilerParams(dimension_semantics=sem, vmem_limit_bytes=VMEM_LIMIT)


PACKED = D_MODEL // 2


def _pack_halves(x_bf16):
    hi = lax.bitcast_convert_type(x_bf16[:, :PACKED].astype(F32), I32)
    lo = lax.bitcast_convert_type(x_bf16[:, PACKED:].astype(F32), I32)
    return hi | lax.shift_right_logical(lo, 16)


def _unpack_halves(words, dtype):
    hi = lax.bitcast_convert_type(words & jnp.int32(-65536), F32)
    lo = lax.bitcast_convert_type(lax.shift_left(words, 16), F32)
    return jnp.concatenate([hi.astype(dtype), lo.astype(dtype)], axis=1)


def _two_group_specs(geo, tile, width):
    n_a = geo.rows_a // tile
    return n_a, [pl.BlockSpec((tile, width), lambda i: (jnp.minimum(i, n_a - 1), 0)),
                 pl.BlockSpec((tile, width), lambda i: (jnp.maximum(i - n_a, 0), 0))]


_C_Q = 0
_C_K = _C_Q + Q_COLS
_C_GA = _C_K + KV_DUP_COLS
_C_GG = _C_GA + D_MODEL
_C_GATE = _C_GG + D_MODEL
_N_IN = _C_GATE + 2 * D_MODEL
_PROJ_CHUNK = 512
N_SLAB = D_MODEL // LANES
_NT = (((1,), (1,)), ((), ()))


def _in_proj_body(n_a, xa_ref, xb_ref, g_ref, w_ref, b_ref, wvt_ref, bvt_ref, cos_ref, sin_ref,
                  q_ref, kd_ref, vt_ref, glu_ref, gate_ref):
    x = jnp.where(pl.program_id(0) < n_a, xa_ref[...], xb_ref[...])
    ms = jnp.mean(x * x, axis=-1, keepdims=True)
    h = (x * lax.rsqrt(ms + NORM_EPS) * g_ref[...]).astype(BF16)
    cos = cos_ref[...]
    sin = sin_ref[...]

    def proj(c0):
        return (jnp.dot(h, w_ref[:, c0:c0 + _PROJ_CHUNK], preferred_element_type=F32)
                + b_ref[:, c0:c0 + _PROJ_CHUNK])

    def rope_store(z, out_ref, o0, scale):
        for c in range(_PROJ_CHUNK // LANES):
            zc = z[:, c * LANES:(c + 1) * LANES]
            r = zc * cos + pltpu.roll(zc, LANES // 2, 1) * sin
            if scale != 1.0:
                r = r * scale
            out_ref[:, o0 + c * LANES:o0 + (c + 1) * LANES] = r.astype(out_ref.dtype)

    for c in range(Q_COLS // _PROJ_CHUNK):
        rope_store(proj(_C_Q + c * _PROJ_CHUNK), q_ref, c * _PROJ_CHUNK, HEAD_DIM ** -0.5)
    for c in range(KV_DUP_COLS // _PROJ_CHUNK):
        rope_store(proj(_C_K + c * _PROJ_CHUNK), kd_ref, c * _PROJ_CHUNK, 1.0)
    vt = lax.dot_general(wvt_ref[...], h, _NT, preferred_element_type=F32) + bvt_ref[...]
    vt_ref[...] = vt.astype(BF16)
    per = _PROJ_CHUNK // LANES
    for c in range(D_MODEL // _PROJ_CHUNK):
        a = proj(_C_GA + c * _PROJ_CHUNK)
        g = proj(_C_GG + c * _PROJ_CHUNK)
        glu = a * jax.nn.sigmoid(g)
        for s in range(per):
            glu_ref[c * per + s] = glu[:, s * LANES:(s + 1) * LANES]
    for c in range(2 * D_MODEL // _PROJ_CHUNK):
        gate_ref[:, c * _PROJ_CHUNK:(c + 1) * _PROJ_CHUNK] = jax.nn.sigmoid(
            proj(_C_GATE + c * _PROJ_CHUNK)).astype(BF16)


def _in_proj(xa, xb, g_mix, w_perm, b_perm, wvt, bvt, cos_t, sin_t, geo, tm):
    T = geo.total

    def pos_map(i):
        r0 = i * tm
        start, _ = _seq_bounds(geo, r0)
        return ((r0 - start) // tm, 0)

    const = lambda i: (0, 0)
    row = lambda i: (i, 0)
    n_a, x_specs = _two_group_specs(geo, tm, D_MODEL)
    return pl.pallas_call(
        functools.partial(_in_proj_body, n_a),
        grid=(T // tm,),
        in_specs=x_specs + [
            pl.BlockSpec((1, D_MODEL), const),
            pl.BlockSpec((D_MODEL, _N_IN), const, pipeline_mode=pl.Buffered(1)),
            pl.BlockSpec((1, _N_IN), const),
            pl.BlockSpec((KV_DUP_COLS, D_MODEL), const),
            pl.BlockSpec((KV_DUP_COLS, 1), const),
            pl.BlockSpec((tm, LANES), pos_map),
            pl.BlockSpec((tm, LANES), pos_map),
        ],
        out_specs=[
            pl.BlockSpec((tm, Q_COLS), row),
            pl.BlockSpec((tm, KV_DUP_COLS), row),
            pl.BlockSpec((KV_DUP_COLS, tm), lambda i: (0, i)),
            pl.BlockSpec((N_SLAB, tm, LANES), lambda i: (0, i, 0)),
            pl.BlockSpec((tm, 2 * D_MODEL), row),
        ],
        out_shape=[
            jax.ShapeDtypeStruct((T, Q_COLS), BF16),
            jax.ShapeDtypeStruct((T, KV_DUP_COLS), BF16),
            jax.ShapeDtypeStruct((KV_DUP_COLS, T), BF16),
            jax.ShapeDtypeStruct((N_SLAB, T, LANES), F32),
            jax.ShapeDtypeStruct((T, 2 * D_MODEL), BF16),
        ],
        compiler_params=_params("parallel"),
        name="in_proj",
    )(xa, xb, g_mix, w_perm, b_perm, wvt, bvt, cos_t, sin_t)


_ONES_ROWS = 16
_ATTN_STEP_BLOCKS = 2


def _attn_body(geo, sink_ref, q_ref, kp_ref, kc_ref, kn_ref, vp_ref, vc_ref, vn_ref, o_ref):
    group = N_Q_HEADS // N_KV_HEADS
    nq = group * ATTN_BLOCK
    b = ATTN_BLOCK
    hd = HEAD_DIM
    nsub = _ATTN_STEP_BLOCKS

    key = lax.broadcasted_iota(I32, (b, nq), 0)
    qry = lax.broadcasted_iota(I32, (b, nq), 1) % b
    head_of_col = lax.broadcasted_iota(I32, (1, nq), 1) // b
    lane = lax.broadcasted_iota(I32, (b, LANES), 1)
    even_head = (lane % hd) < (hd // 2)
    ones = jnp.ones((_ONES_ROWS, 3 * b), BF16)

    for s in range(nsub):
        rows = slice(s * b, (s + 1) * b)
        r0 = (pl.program_id(0) * nsub + s) * b
        start, end = _seq_bounds(geo, r0)
        bias_prev = jnp.where((key >= qry) & (r0 > start), 0.0, NEG_INF)
        bias_next = jnp.where((key <= qry) & (r0 + b < end), 0.0, NEG_INF)

        for g in range(N_KV_HEADS):
            ls = slice(g * LANES, (g + 1) * LANES)
            k_prev = kp_ref[:, ls] if s == 0 else kc_ref[(s - 1) * b:s * b, ls]
            k_next = kn_ref[:, ls] if s == nsub - 1 else kc_ref[(s + 1) * b:(s + 2) * b, ls]
            v_prev = vp_ref[ls, :] if s == 0 else vc_ref[ls, (s - 1) * b:s * b]
            v_next = vn_ref[ls, :] if s == nsub - 1 else vc_ref[ls, (s + 1) * b:(s + 2) * b]

            qa = q_ref[rows, (2 * g) * LANES:(2 * g + 1) * LANES]
            qb = q_ref[rows, (2 * g + 1) * LANES:(2 * g + 2) * LANES]
            zero = jnp.zeros_like(qa)
            q4 = jnp.concatenate([jnp.where(even_head, qa, zero), jnp.where(even_head, zero, qa),
                                  jnp.where(even_head, qb, zero), jnp.where(even_head, zero, qb)],
                                 axis=0)
            k = jnp.concatenate([k_prev, kc_ref[rows, ls], k_next], axis=0)
            st = lax.dot_general(k, q4, _NT, preferred_element_type=F32)
            s_prev = st[0:b] + bias_prev
            s_cur = st[b:2 * b]
            s_next = st[2 * b:3 * b] + bias_next
            sink = jnp.full((1, nq), sink_ref[group * g], F32)
            for h in range(1, group):
                sink = jnp.where(head_of_col == h, sink_ref[group * g + h], sink)
            m = jnp.maximum(jnp.maximum(jnp.max(s_prev, axis=0, keepdims=True),
                                        jnp.max(s_cur, axis=0, keepdims=True)),
                            jnp.maximum(jnp.max(s_next, axis=0, keepdims=True), sink))
            p = jnp.concatenate([jnp.exp(s_prev - m).astype(BF16), jnp.exp(s_cur - m).astype(BF16),
                                 jnp.exp(s_next - m).astype(BF16)], axis=0)
            vt = jnp.concatenate([v_prev, vc_ref[ls, rows], v_next], axis=1)
            ot = jnp.dot(jnp.concatenate([vt, ones], axis=0), p, preferred_element_type=F32)
            denom = ot[2 * hd:2 * hd + 1] + jnp.exp(sink - m)
            on = ot[0:2 * hd] * (1.0 / denom)
            pair_a = jnp.concatenate([on[0:hd, 0:b], on[hd:2 * hd, b:2 * b]], axis=0)
            pair_b = jnp.concatenate([on[0:hd, 2 * b:3 * b], on[hd:2 * hd, 3 * b:4 * b]], axis=0)
            o_ref[rows, (2 * g) * LANES:(2 * g + 1) * LANES] = pair_a.T.astype(BF16)
            o_ref[rows, (2 * g + 1) * LANES:(2 * g + 2) * LANES] = pair_b.T.astype(BF16)


def _attention(q, kd, vt, sink, geo):
    T = q.shape[0]
    nsub = _ATTN_STEP_BLOCKS
    nb = T // ATTN_BLOCK
    step_rows = nsub * ATTN_BLOCK
    prev = lambda i: jnp.maximum(i * nsub - 1, 0)
    nxt = lambda i: jnp.minimum((i + 1) * nsub, nb - 1)
    k_edge = lambda m: pl.BlockSpec((ATTN_BLOCK, KV_DUP_COLS), lambda i: (m(i), 0))
    v_edge = lambda m: pl.BlockSpec((KV_DUP_COLS, ATTN_BLOCK), lambda i: (0, m(i)))
    return pl.pallas_call(
        functools.partial(_attn_body, geo),
        grid=(T // step_rows,),
        in_specs=[
            pl.BlockSpec(memory_space=pltpu.SMEM),
            pl.BlockSpec((step_rows, Q_COLS), lambda i: (i, 0)),
            k_edge(prev), pl.BlockSpec((step_rows, KV_DUP_COLS), lambda i: (i, 0)), k_edge(nxt),
            v_edge(prev), pl.BlockSpec((KV_DUP_COLS, step_rows), lambda i: (0, i)), v_edge(nxt),
        ],
        out_specs=pl.BlockSpec((step_rows, Q_COLS), lambda i: (i, 0)),
        out_shape=jax.ShapeDtypeStruct((T, Q_COLS), BF16),
        compiler_params=_params("parallel"),
        name="window_attn",
    )(sink, q, kd, kd, kd, vt, vt, vt)


_CONV_ROWS = 64
_LN_ROWS = 32


def _conv_body(geo, tc, z_ref, zp_ref, zn_ref, w_ref, dwb_ref, lng_ref, lnb_ref, o_ref,
               buf_ref, y_ref):
    i = pl.program_id(0)
    r0 = i * tc
    start, end = _seq_bounds(geo, r0)
    has_prev = (r0 > start).astype(F32)
    has_next = (r0 + tc < end).astype(F32)
    rows = tc + 2 * HALO
    buf_ref[:, 0:HALO, :] = zp_ref[...] * has_prev
    buf_ref[:, HALO:HALO + tc, :] = z_ref[...]
    buf_ref[:, HALO + tc:rows, :] = zn_ref[...] * has_next

    rep = _CONV_ROWS // SUBLANES

    def conv_chunk(j):
        s0 = pl.multiple_of(j * _CONV_ROWS, _CONV_ROWS)
        for c in range(N_SLAB):
            ls = slice(c * LANES, (c + 1) * LANES)
            acc = jnp.zeros((_CONV_ROWS, LANES), F32)
            for k in range(CONV_WIDTH):
                zt = buf_ref[c, pl.ds(s0 + (HALO - CONV_PAD + k), _CONV_ROWS, stride=1), :]
                wt = w_ref[k * SUBLANES:(k + 1) * SUBLANES, ls]
                acc = acc + zt * jnp.concatenate([wt] * rep, axis=0)
            y_ref[pl.ds(s0, _CONV_ROWS), ls] = acc

    def ln_chunk(j):
        for h in range(_CONV_ROWS // _LN_ROWS):
            s0 = pl.multiple_of(j * _CONV_ROWS + h * _LN_ROWS, _LN_ROWS)
            y = y_ref[pl.ds(s0, _LN_ROWS), :] + dwb_ref[...]
            mu = jnp.mean(y, axis=-1, keepdims=True)
            yc = y - mu
            var = jnp.mean(yc * yc, axis=-1, keepdims=True)
            yn = yc * lax.rsqrt(var + NORM_EPS) * lng_ref[...] + lnb_ref[...]
            o_ref[pl.ds(s0, _LN_ROWS), :] = (yn * jax.nn.sigmoid(yn)).astype(BF16)

    n = tc // _CONV_ROWS
    conv_chunk(0)

    def body(j, carry):
        ln_chunk(j - 1)
        conv_chunk(j)
        return carry

    lax.fori_loop(1, n, body, 0)
    ln_chunk(n - 1)


def _conv_branch(glu, w_rep, dw_b, ln_g, ln_b, geo, tc):
    T = glu.shape[1]
    nh = T // HALO
    per = tc // HALO
    const = lambda i: (0, 0)
    return pl.pallas_call(
        functools.partial(_conv_body, geo, tc),
        grid=(T // tc,),
        in_specs=[
            pl.BlockSpec((N_SLAB, tc, LANES), lambda i: (0, i, 0)),
            pl.BlockSpec((N_SLAB, HALO, LANES), lambda i: (0, jnp.maximum(i * per - 1, 0), 0)),
            pl.BlockSpec((N_SLAB, HALO, LANES), lambda i: (0, jnp.minimum((i + 1) * per, nh - 1), 0)),
            pl.BlockSpec((CONV_WIDTH * SUBLANES, D_MODEL), const),
            pl.BlockSpec((1, D_MODEL), const),
            pl.BlockSpec((1, D_MODEL), const),
            pl.BlockSpec((1, D_MODEL), const),
        ],
        out_specs=pl.BlockSpec((tc, D_MODEL), lambda i: (i, 0)),
        out_shape=jax.ShapeDtypeStruct((T, D_MODEL), BF16),
        scratch_shapes=[pltpu.VMEM((N_SLAB, tc + 2 * HALO, LANES), F32),
                        pltpu.VMEM((tc, D_MODEL), F32)],
        compiler_params=_params("parallel"),
        name="conv_branch",
    )(glu, glu, glu, w_rep, dw_b, ln_g, ln_b)


def _mix_body(n_a, xa_ref, xb_ref, o_ref, c_ref, gate_ref, wo_ref, wpw_ref, bpw_ref, wout_ref,
              gffn_ref, wrh_ref, wrl_ref, br_ref, tri_ref,
              x1_ref, h2_ref, idx_ref, gt_ref, rank_ref, cnt_ref, carry_ref):
    tm = xa_ref.shape[0]
    x = jnp.where(pl.program_id(0) < n_a, xa_ref[...], xb_ref[...])

    @pl.when(pl.program_id(0) == 0)
    def _():
        carry_ref[...] = jnp.zeros_like(carry_ref)

    attn = jnp.dot(o_ref[...], wo_ref[...], preferred_element_type=F32)
    conv = jnp.dot(c_ref[...], wpw_ref[...], preferred_element_type=F32) + bpw_ref[...]
    g_attn = gate_ref[:, 0:D_MODEL].astype(F32)
    g_conv = gate_ref[:, D_MODEL:2 * D_MODEL].astype(F32)
    mix = (g_attn * attn + g_conv * conv).astype(BF16)
    x1 = x + jnp.dot(mix, wout_ref[...], preferred_element_type=F32)
    x1_ref[...] = x1
    ms = jnp.mean(x1 * x1, axis=-1, keepdims=True)
    h2 = x1 * lax.rsqrt(ms + NORM_EPS) * gffn_ref[...]
    h_hi = h2.astype(BF16)
    h2_ref[...] = _pack_halves(h_hi)

    h_lo = (h2 - h_hi.astype(F32)).astype(BF16)
    logits = (lax.dot_general(wrh_ref[...], h_hi, _NT, preferred_element_type=F32)
              + lax.dot_general(wrh_ref[...], h_lo, _NT, preferred_element_type=F32)
              + lax.dot_general(wrl_ref[...], h_hi, _NT, preferred_element_type=F32)
              + br_ref[...])

    eidx = lax.broadcasted_iota(I32, (N_EXPERTS, tm), 0)
    vals = logits
    picked, top_vals, top_idx = [], [], []
    for _ in range(TOP_K):
        m = jnp.max(vals, axis=0, keepdims=True)
        idx = jnp.min(jnp.where(vals == m, eidx, N_EXPERTS), axis=0, keepdims=True)
        sel = eidx == idx
        vals = jnp.where(sel, -jnp.inf, vals)
        picked.append(sel)
        top_vals.append(m)
        top_idx.append(idx)

    exps = [jnp.exp(v - top_vals[0]) for v in top_vals]
    tot = exps[0] + exps[1] + exps[2] + exps[3]
    onehot = (picked[0] | picked[1] | picked[2] | picked[3])
    prefix = jnp.dot(onehot.astype(BF16), tri_ref[...], preferred_element_type=F32) + carry_ref[...]
    for j in range(TOP_K):
        idx_ref[j:j + 1, :] = top_idx[j]
        gt_ref[j:j + 1, :] = exps[j] / tot
        rank_ref[j:j + 1, :] = jnp.sum(jnp.where(picked[j], prefix, 0.0), axis=0,
                                       keepdims=True).astype(I32)
    carry_ref[...] = carry_ref[...] + jnp.sum(onehot.astype(F32), axis=1, keepdims=True)
    cnt_ref[...] = jnp.broadcast_to(carry_ref[...], cnt_ref.shape)


def _mix_route(xa, xb, o, c, gates, wo, wpw, bpw, wout, gffn, wr_hi, wr_lo, br, tri, geo, tm):
    T = geo.total
    row = lambda i: (i, 0)
    col = lambda i: (0, i)
    const = lambda i: (0, 0)
    wspec = lambda shape: pl.BlockSpec(shape, const, pipeline_mode=pl.Buffered(1))
    n_a, x_specs = _two_group_specs(geo, tm, D_MODEL)
    return pl.pallas_call(
        functools.partial(_mix_body, n_a),
        grid=(T // tm,),
        in_specs=x_specs + [
            pl.BlockSpec((tm, Q_COLS), row),
            pl.BlockSpec((tm, D_MODEL), row),
            pl.BlockSpec((tm, 2 * D_MODEL), row),
            wspec((Q_COLS, D_MODEL)),
            wspec((D_MODEL, D_MODEL)),
            pl.BlockSpec((1, D_MODEL), const),
            wspec((D_MODEL, D_MODEL)),
            pl.BlockSpec((1, D_MODEL), const),
            pl.BlockSpec((N_EXPERTS, D_MODEL), const),
            pl.BlockSpec((N_EXPERTS, D_MODEL), const),
            pl.BlockSpec((N_EXPERTS, 1), const),
            pl.BlockSpec((tm, tm), const),
        ],
        out_specs=[
            pl.BlockSpec((tm, D_MODEL), row),
            pl.BlockSpec((tm, PACKED), row),
            pl.BlockSpec((TOP_K, tm), col),
            pl.BlockSpec((TOP_K, tm), col),
            pl.BlockSpec((TOP_K, tm), col),
            pl.BlockSpec((N_EXPERTS, LANES), const),
        ],
        out_shape=[
            jax.ShapeDtypeStruct((T, D_MODEL), F32),
            jax.ShapeDtypeStruct((T, PACKED), I32),
            jax.ShapeDtypeStruct((TOP_K, T), I32),
            jax.ShapeDtypeStruct((TOP_K, T), F32),
            jax.ShapeDtypeStruct((TOP_K, T), I32),
            jax.ShapeDtypeStruct((N_EXPERTS, LANES), F32),
        ],
        scratch_shapes=[pltpu.VMEM((N_EXPERTS, 1), F32)],
        compiler_params=_params("arbitrary"),
        name="mix_route",
    )(xa, xb, o, c, gates, wo, wpw, bpw, wout, gffn, wr_hi, wr_lo, br, tri)


def _sc_workers():
    info = plsc.get_sparse_core_info()
    return info.num_cores, info.num_cores * info.num_subcores


def _sc_dispatch(h, dest, n_out):
    t_rows, width = h.shape
    nc, nw = _sc_workers()
    per_w = t_rows // nw
    step = SC_ROWS_PER_STEP
    assert per_w * nw == t_rows and per_w % step == 0
    mesh = plsc.VectorSubcoreMesh(core_axis_name="c", subcore_axis_name="s")

    @functools.partial(
        pl.kernel, mesh=mesh,
        out_type=jax.ShapeDtypeStruct((n_out, width), h.dtype),
        scratch_types=[pltpu.VMEM((step,), I32)] * TOP_K
        + [pltpu.VMEM((step, width), h.dtype), pltpu.SemaphoreType.DMA],
    )
    def scatter_rows(h_hbm, dest_hbm, out_hbm, i0, i1, i2, i3, rows_v, sem):
        del sem
        base = (lax.axis_index("s") * nc + lax.axis_index("c")) * per_w

        @pl.loop(0, per_w // step)
        def _(i):
            off = base + i * step
            pltpu.sync_copy(h_hbm.at[pl.ds(off, step)], rows_v)
            for j, idx_v in enumerate((i0, i1, i2, i3)):
                pltpu.sync_copy(dest_hbm.at[j, pl.ds(off, step)], idx_v)
                pltpu.sync_copy(rows_v, out_hbm.at[idx_v])

    return scatter_rows(h, dest)


def _sc_gather(table, idx):
    n = idx.shape[0]
    width = table.shape[1]
    nc, nw = _sc_workers()
    per_w = n // nw
    step = SC_ROWS_PER_STEP
    assert per_w * nw == n and per_w % step == 0
    mesh = plsc.VectorSubcoreMesh(core_axis_name="c", subcore_axis_name="s")

    @functools.partial(
        pl.kernel, mesh=mesh,
        out_type=jax.ShapeDtypeStruct((n, width), table.dtype),
        scratch_types=[pltpu.VMEM((step,), I32), pltpu.VMEM((step, width), table.dtype),
                       pltpu.SemaphoreType.DMA],
    )
    def gather_rows(table_hbm, idx_hbm, out_hbm, idx_v, rows_v, sem):
        base = (lax.axis_index("s") * nc + lax.axis_index("c")) * per_w

        @pl.loop(0, per_w // step)
        def _(i):
            off = base + i * step
            pltpu.sync_copy(idx_hbm.at[pl.ds(off, step)], idx_v)
            pltpu.async_copy(table_hbm.at[idx_v], rows_v, sem).wait()
            pltpu.sync_copy(rows_v, out_hbm.at[pl.ds(off, step)])

    return gather_rows(table, idx)


def _expert_body(bexp_ref, nused_ref, valid_ref, xs_ref, wgu_ref, bgu_ref, wd_ref, bd_ref, o_ref,
                 wgu_bf, wd_bf):
    n = pl.program_id(0)
    active = n < nused_ref[0]
    new_expert = (n == 0) | (bexp_ref[n] != bexp_ref[jnp.maximum(n - 1, 0)])

    @pl.when(active & new_expert)
    def _():
        wgu_bf[...] = wgu_ref[...].astype(BF16)
        wd_bf[...] = wd_ref[...].astype(BF16)

    @pl.when(active)
    def _():
        row = lax.broadcasted_iota(I32, (xs_ref.shape[0], 1), 0)
        x = _unpack_halves(jnp.where(row < valid_ref[n], xs_ref[...], 0), BF16)
        gu = jnp.dot(x, wgu_bf[...], preferred_element_type=F32) + bgu_ref[...]
        g = jnp.minimum(gu[:, 0:D_FF], SWIGLU_LIMIT)
        u = jnp.clip(gu[:, D_FF:2 * D_FF], -SWIGLU_LIMIT, SWIGLU_LIMIT)
        act = g * jax.nn.sigmoid(SWIGLU_ALPHA * g) * (u + 1.0)
        out = jnp.dot(act.astype(BF16), wd_bf[...], preferred_element_type=F32) + bd_ref[...]
        o_ref[...] = _pack_halves(out.astype(BF16))

    @pl.when(jnp.logical_not(active))
    def _():
        o_ref[...] = jnp.zeros_like(o_ref)


def _experts(block_exp, n_used, block_valid, xs, wgu, bgu, wd, bd, bm):
    P = xs.shape[0]
    nblk = P // bm
    xmap = lambda n, be, nu, bv: (jnp.minimum(n, nu[0] - 1), 0)
    emap = lambda n, be, nu, bv: (be[n], 0, 0)
    grid_spec = pltpu.PrefetchScalarGridSpec(
        num_scalar_prefetch=3,
        grid=(nblk,),
        in_specs=[
            pl.BlockSpec((bm, PACKED), xmap),
            pl.BlockSpec((None, D_MODEL, 2 * D_FF), emap),
            pl.BlockSpec((None, 1, 2 * D_FF), emap),
            pl.BlockSpec((None, D_FF, D_MODEL), emap),
            pl.BlockSpec((None, 1, D_MODEL), emap),
        ],
        out_specs=pl.BlockSpec((bm, PACKED), lambda n, be, nu, bv: (n, 0)),
        scratch_shapes=[pltpu.VMEM((D_MODEL, 2 * D_FF), BF16), pltpu.VMEM((D_FF, D_MODEL), BF16)],
    )
    return pl.pallas_call(
        _expert_body,
        grid_spec=grid_spec,
        out_shape=jax.ShapeDtypeStruct((P, PACKED), I32),
        compiler_params=_params("arbitrary"),
        name="experts",
    )(block_exp, n_used, block_valid, xs, wgu, bgu, wd, bd)


def _combine_body(n_a, x1_ref, g0_ref, g1_ref, g2_ref, g3_ref, gate_ref, gfin_ref, ya_ref, yb_ref):
    gate = gate_ref[...]
    y = x1_ref[...]
    for j, g_ref in enumerate((g0_ref, g1_ref, g2_ref, g3_ref)):
        y = y + gate[:, j:j + 1] * _unpack_halves(g_ref[...], F32)
    ms = jnp.mean(y * y, axis=-1, keepdims=True)
    y = y * lax.rsqrt(ms + NORM_EPS) * gfin_ref[...]

    @pl.when(pl.program_id(0) < n_a)
    def _():
        ya_ref[...] = y

    @pl.when(pl.program_id(0) >= n_a)
    def _():
        yb_ref[...] = y


def _combine(x1, gathered, gate_tok, gfin, geo, tf):
    T = x1.shape[0]
    nt = T // tf
    row = lambda i: (i, 0)
    n_a, y_specs = _two_group_specs(geo, tf, D_MODEL)
    choice = lambda j: pl.BlockSpec((tf, PACKED), lambda i: (j * nt + i, 0))
    return pl.pallas_call(
        functools.partial(_combine_body, n_a),
        grid=(nt,),
        in_specs=[pl.BlockSpec((tf, D_MODEL), row)] + [choice(j) for j in range(TOP_K)] + [
            pl.BlockSpec((tf, TOP_K), row),
            pl.BlockSpec((1, D_MODEL), lambda i: (0, 0)),
        ],
        out_specs=y_specs,
        out_shape=[jax.ShapeDtypeStruct((geo.rows_a, D_MODEL), F32),
                   jax.ShapeDtypeStruct((T - geo.rows_a, D_MODEL), F32)],
        compiler_params=_params("arbitrary"),
        name="combine",
    )(x1, gathered, gathered, gathered, gathered, gate_tok, gfin)


def _permute_in_proj(w):
    lead = w.shape[:-1]
    half = HEAD_DIM // 2
    q = w[..., :Q_COLS].reshape(*lead, N_Q_HEADS // 2, 2, 2, half)
    q = jnp.swapaxes(q, -3, -2).reshape(*lead, Q_COLS)
    k = w[..., Q_COLS:Q_COLS + KV_COLS].reshape(*lead, N_KV_HEADS, 2, 1, half)
    k = jnp.broadcast_to(k, (*lead, N_KV_HEADS, 2, 2, half)).reshape(*lead, KV_DUP_COLS)
    v = w[..., Q_COLS + KV_COLS:Q_COLS + 2 * KV_COLS].reshape(*lead, N_KV_HEADS, 1, HEAD_DIM)
    v = jnp.broadcast_to(v, (*lead, N_KV_HEADS, 2, HEAD_DIM)).reshape(*lead, KV_DUP_COLS)
    return jnp.concatenate([q, k, w[..., Q_COLS + 2 * KV_COLS:]], axis=-1), v


def _rope_tables(n_pos):
    half = HEAD_DIM // 2
    inv_freq = 1.0 / (ROPE_THETA ** (jnp.arange(half, dtype=F32) * (2.0 / HEAD_DIM)))
    ang = jnp.arange(n_pos, dtype=F32)[:, None] * inv_freq[None, :]
    cos = jnp.tile(jnp.cos(ang), (1, LANES // half))
    sin = jnp.tile(jnp.sin(ang), (1, LANES // half))
    sign = jnp.where(jnp.arange(LANES) < LANES // 2, -1.0, 1.0).astype(F32)
    return cos, sin * sign[None, :]


def _tiles(geo):
    unit = min(geo.len_a, geo.len_b)
    tile = min(512, unit)
    return dict(tm=tile, tc=tile, tf=tile, bm=512)


def kernel(x_prompt, x_sample, norm_mix_g, w_in, b_in, attn_sink, w_o_attn, conv_dw_w, conv_dw_b,
           conv_ln_g, conv_ln_b, w_pw2, b_pw2, w_out, norm_ffn_g, w_router, b_router, w_gu, b_gu,
           w_down, b_down, norm_final_g):
    assert w_in.shape[0] == 1, "single trunk layer"
    geo = Geo(x_prompt.shape[0], x_prompt.shape[1], x_sample.shape[0], x_sample.shape[1])
    T = geo.total
    ts = _tiles(geo)
    xa = x_prompt.reshape(-1, D_MODEL)
    xb = x_sample.reshape(-1, D_MODEL)

    w_perm, w_v = _permute_in_proj(w_in[0])
    b_perm, b_v = _permute_in_proj(b_in)
    cos_t, sin_t = _rope_tables(max(geo.len_a, geo.len_b))

    q, kd, vt, glu, gates = _in_proj(xa, xb, norm_mix_g, w_perm.astype(BF16), b_perm,
                                     w_v.T.astype(BF16), b_v.T, cos_t, sin_t, geo, ts["tm"])
    attn = _attention(q, kd, vt, attn_sink[0], geo)
    w_rep = jnp.repeat(conv_dw_w[0], SUBLANES, axis=0)
    conv = _conv_branch(glu, w_rep, conv_dw_b, conv_ln_g, conv_ln_b, geo, ts["tc"])

    wr_t = w_router[0].T
    wr_hi = wr_t.astype(BF16)
    wr_lo = (wr_t - wr_hi.astype(F32)).astype(BF16)
    tri = jnp.triu(jnp.ones((ts["tm"], ts["tm"]), BF16), 1)
    x1, h2, idx, gate_t, rank, counts = _mix_route(
        xa, xb, attn, conv, gates, w_o_attn[0].astype(BF16), w_pw2[0].astype(BF16), b_pw2,
        w_out[0].astype(BF16), norm_ffn_g, wr_hi, wr_lo, b_router[0][:, None], tri, geo, ts["tm"])

    bm = ts["bm"]
    n_blocks = (T * TOP_K) // bm + N_EXPERTS
    cnt = counts[:, 0].astype(I32)
    padded = ((cnt + bm - 1) // bm) * bm
    pad_end = jnp.cumsum(padded)
    pad_start = pad_end - padded
    expert_ids = jnp.arange(N_EXPERTS, dtype=I32)
    dest = rank + jnp.sum(jnp.where(idx[None] == expert_ids[:, None, None],
                                    pad_start[:, None, None], 0), axis=0)
    block_start = jnp.arange(n_blocks, dtype=I32) * bm
    block_exp = jnp.minimum(jnp.sum((pad_end[None, :] <= block_start[:, None]).astype(I32), axis=1),
                            N_EXPERTS - 1)
    n_used = (pad_end[-1:] // bm).astype(I32)
    seg_end = jnp.sum(jnp.where(block_exp[:, None] == expert_ids[None, :],
                                (pad_start + cnt)[None, :], 0), axis=1)
    block_valid = jnp.clip(seg_end - block_start, 0, bm).astype(I32)

    xs = _sc_dispatch(h2, dest, n_blocks * bm)
    ys = _experts(block_exp, n_used, block_valid, xs, w_gu[0], b_gu[0][:, None, :],
                  w_down[0], b_down[0][:, None, :], bm)
    gathered = _sc_gather(ys, dest.reshape(-1))
    ya, yb = _combine(x1, gathered, gate_t.T, norm_final_g[None, :], geo, ts["tf"])
    return (ya.reshape(x_prompt.shape), yb.reshape(x_sample.shape))
```

```python
import functools
import math
from typing import NamedTuple

import jax
import jax.numpy as jnp
from jax import lax
from jax.experimental import pallas as pl
from jax.experimental.pallas import tpu as pltpu
from jax.experimental.pallas import tpu_sc as plsc

F32 = jnp.float32
BF16 = jnp.bfloat16
I32 = jnp.int32

D_MODEL = 1024
HEAD_DIM = 64
N_Q_HEADS = 16
N_KV_HEADS = 4
WINDOW = 128
ATTN_BLOCK = 128
ROPE_THETA = 10000.0
CONV_WIDTH = 31
CONV_PAD = CONV_WIDTH // 2
N_EXPERTS = 32
TOP_K = 4
D_FF = D_MODEL
SWIGLU_LIMIT = 7.0
SWIGLU_ALPHA = 1.702
NORM_EPS = 1e-5
NEG_INF = -1e30

Q_COLS = N_Q_HEADS * HEAD_DIM
KV_COLS = N_KV_HEADS * HEAD_DIM
LANES = 128
SUBLANES = 8
KV_DUP_COLS = N_KV_HEADS * LANES
HALO = 16
VMEM_LIMIT = 56 * 1024 * 1024
SC_ROWS_PER_STEP = 128


class Geo(NamedTuple):
    n_a: int
    len_a: int
    n_b: int
    len_b: int

    @property
    def rows_a(self):
        return self.n_a * self.len_a

    @property
    def total(self):
        return self.rows_a + self.n_b * self.len_b


def _seq_bounds(geo, r):
    in_a = r < geo.rows_a
    start_a = (r // geo.len_a) * geo.len_a
    start_b = geo.rows_a + ((r - geo.rows_a) // geo.len_b) * geo.len_b
    start = jnp.where(in_a, start_a, start_b)
    end = start + jnp.where(in_a, geo.len_a, geo.len_b)
    return start, end


def _params(*sem):
    return pltpu.CompilerParams(dimension_semantics=sem, vmem_limit_bytes=VMEM_LIMIT)


PACKED = D_MODEL // 2


def _pack_halves(x_bf16):
    hi = lax.bitcast_convert_type(x_bf16[:, :PACKED].astype(F32), I32)
    lo = lax.bitcast_convert_type(x_bf16[:, PACKED:].astype(F32), I32)
    return hi | lax.shift_right_logical(lo, 16)


def _unpack_halves(words, dtype):
    hi = lax.bitcast_convert_type(words & jnp.int32(-65536), F32)
    lo = lax.bitcast_convert_type(lax.shift_left(words, 16), F32)
    return jnp.concatenate([hi.astype(dtype), lo.astype(dtype)], axis=1)


def _two_group_specs(geo, tile, width):
    n_a = geo.rows_a // tile
    return n_a, [pl.BlockSpec((tile, width), lambda i: (jnp.minimum(i, n_a - 1), 0)),
                 pl.BlockSpec((tile, width), lambda i: (jnp.maximum(i - n_a, 0), 0))]


_C_Q = 0
_C_K = _C_Q + Q_COLS
_C_GA = _C_K + KV_DUP_COLS
_C_GG = _C_GA + D_MODEL
_C_GATE = _C_GG + D_MODEL
_N_IN = _C_GATE + 2 * D_MODEL
_PROJ_CHUNK = 512
N_SLAB = D_MODEL // LANES
_NT = (((1,), (1,)), ((), ()))
_LOG2E = math.log2(math.e)
_Q_SCALE = HEAD_DIM ** -0.5 * _LOG2E


def _in_proj_body(n_a, xa_ref, xb_ref, g_ref, w_ref, b_ref, wvt_ref, bvt_ref, cos_ref, sin_ref,
                  q_ref, kd_ref, vt_ref, glu_ref, gate_ref):
    x = jnp.where(pl.program_id(0) < n_a, xa_ref[...], xb_ref[...])
    ms = jnp.mean(x * x, axis=-1, keepdims=True)
    h = (x * lax.rsqrt(ms + NORM_EPS) * g_ref[...]).astype(BF16)
    cos = cos_ref[...]
    sin = sin_ref[...]

    def proj(c0):
        return (jnp.dot(h, w_ref[:, c0:c0 + _PROJ_CHUNK], preferred_element_type=F32)
                + b_ref[:, c0:c0 + _PROJ_CHUNK])

    def rope_store(z, out_ref, o0, scale):
        for c in range(_PROJ_CHUNK // LANES):
            zc = z[:, c * LANES:(c + 1) * LANES]
            r = zc * cos + pltpu.roll(zc, LANES // 2, 1) * sin
            if scale != 1.0:
                r = r * scale
            out_ref[:, o0 + c * LANES:o0 + (c + 1) * LANES] = r.astype(out_ref.dtype)

    for c in range(Q_COLS // _PROJ_CHUNK):
        rope_store(proj(_C_Q + c * _PROJ_CHUNK), q_ref, c * _PROJ_CHUNK, _Q_SCALE)
    for c in range(KV_DUP_COLS // _PROJ_CHUNK):
        rope_store(proj(_C_K + c * _PROJ_CHUNK), kd_ref, c * _PROJ_CHUNK, 1.0)
    vt = lax.dot_general(wvt_ref[...], h, _NT, preferred_element_type=F32) + bvt_ref[...]
    vt_ref[...] = vt.astype(BF16)
    per = _PROJ_CHUNK // LANES
    for c in range(D_MODEL // _PROJ_CHUNK):
        a = proj(_C_GA + c * _PROJ_CHUNK)
        g = proj(_C_GG + c * _PROJ_CHUNK)
        glu = a * jax.nn.sigmoid(g)
        for s in range(per):
            glu_ref[c * per + s] = glu[:, s * LANES:(s + 1) * LANES]
    for c in range(2 * D_MODEL // _PROJ_CHUNK):
        gate_ref[:, c * _PROJ_CHUNK:(c + 1) * _PROJ_CHUNK] = jax.nn.sigmoid(
            proj(_C_GATE + c * _PROJ_CHUNK)).astype(BF16)


def _in_proj(xa, xb, g_mix, w_perm, b_perm, wvt, bvt, cos_t, sin_t, geo, tm):
    T = geo.total

    def pos_map(i):
        r0 = i * tm
        start, _ = _seq_bounds(geo, r0)
        return ((r0 - start) // tm, 0)

    const = lambda i: (0, 0)
    row = lambda i: (i, 0)
    n_a, x_specs = _two_group_specs(geo, tm, D_MODEL)
    return pl.pallas_call(
        functools.partial(_in_proj_body, n_a),
        grid=(T // tm,),
        in_specs=x_specs + [
            pl.BlockSpec((1, D_MODEL), const),
            pl.BlockSpec((D_MODEL, _N_IN), const, pipeline_mode=pl.Buffered(1)),
            pl.BlockSpec((1, _N_IN), const),
            pl.BlockSpec((KV_DUP_COLS, D_MODEL), const),
            pl.BlockSpec((KV_DUP_COLS, 1), const),
            pl.BlockSpec((tm, LANES), pos_map),
            pl.BlockSpec((tm, LANES), pos_map),
        ],
        out_specs=[
            pl.BlockSpec((tm, Q_COLS), row),
            pl.BlockSpec((tm, KV_DUP_COLS), row),
            pl.BlockSpec((KV_DUP_COLS, tm), lambda i: (0, i)),
            pl.BlockSpec((N_SLAB, tm, LANES), lambda i: (0, i, 0)),
            pl.BlockSpec((tm, 2 * D_MODEL), row),
        ],
        out_shape=[
            jax.ShapeDtypeStruct((T, Q_COLS), BF16),
            jax.ShapeDtypeStruct((T, KV_DUP_COLS), BF16),
            jax.ShapeDtypeStruct((KV_DUP_COLS, T), BF16),
            jax.ShapeDtypeStruct((N_SLAB, T, LANES), F32),
            jax.ShapeDtypeStruct((T, 2 * D_MODEL), BF16),
        ],
        compiler_params=_params("parallel"),
        name="in_proj",
    )(xa, xb, g_mix, w_perm, b_perm, wvt, bvt, cos_t, sin_t)


_ONES_ROWS = 16
_ATTN_STEP_BLOCKS = 4


def _attn_body(geo, sink_ref, q_ref, kp_ref, kc_ref, kn_ref, vp_ref, vc_ref, vn_ref, o_ref):
    group = N_Q_HEADS // N_KV_HEADS
    nq = group * ATTN_BLOCK
    b = ATTN_BLOCK
    hd = HEAD_DIM
    nsub = _ATTN_STEP_BLOCKS

    key = lax.broadcasted_iota(I32, (b, nq), 0)
    qry = lax.broadcasted_iota(I32, (b, nq), 1) % b
    head_of_col = lax.broadcasted_iota(I32, (1, nq), 1) // b
    lane = lax.broadcasted_iota(I32, (b, LANES), 1)
    even_head = (lane % hd) < (hd // 2)
    ones = jnp.ones((_ONES_ROWS, 3 * b), BF16)

    for s in range(nsub):
        rows = slice(s * b, (s + 1) * b)
        r0 = (pl.program_id(0) * nsub + s) * b
        start, end = _seq_bounds(geo, r0)
        bias_prev = jnp.where((key >= qry) & (r0 > start), 0.0, NEG_INF)
        bias_next = jnp.where((key <= qry) & (r0 + b < end), 0.0, NEG_INF)

        for g in range(N_KV_HEADS):
            ls = slice(g * LANES, (g + 1) * LANES)
            k_prev = kp_ref[:, ls] if s == 0 else kc_ref[(s - 1) * b:s * b, ls]
            k_next = kn_ref[:, ls] if s == nsub - 1 else kc_ref[(s + 1) * b:(s + 2) * b, ls]
            v_prev = vp_ref[ls, :] if s == 0 else vc_ref[ls, (s - 1) * b:s * b]
            v_next = vn_ref[ls, :] if s == nsub - 1 else vc_ref[ls, (s + 1) * b:(s + 2) * b]

            qa = q_ref[rows, (2 * g) * LANES:(2 * g + 1) * LANES]
            qb = q_ref[rows, (2 * g + 1) * LANES:(2 * g + 2) * LANES]
            zero = jnp.zeros_like(qa)
            q4 = jnp.concatenate([jnp.where(even_head, qa, zero), jnp.where(even_head, zero, qa),
                                  jnp.where(even_head, qb, zero), jnp.where(even_head, zero, qb)],
                                 axis=0)
            k = jnp.concatenate([k_prev, kc_ref[rows, ls], k_next], axis=0)
            st = lax.dot_general(k, q4, _NT, preferred_element_type=F32)
            s_prev = st[0:b] + bias_prev
            s_cur = st[b:2 * b]
            s_next = st[2 * b:3 * b] + bias_next
            sink = jnp.full((1, nq), sink_ref[group * g] * _LOG2E, F32)
            for h in range(1, group):
                sink = jnp.where(head_of_col == h, sink_ref[group * g + h] * _LOG2E, sink)
            m = jnp.maximum(jnp.maximum(jnp.max(s_prev, axis=0, keepdims=True),
                                        jnp.max(s_cur, axis=0, keepdims=True)),
                            jnp.maximum(jnp.max(s_next, axis=0, keepdims=True), sink))
            p = jnp.concatenate([jnp.exp2(s_prev - m).astype(BF16), jnp.exp2(s_cur - m).astype(BF16),
                                 jnp.exp2(s_next - m).astype(BF16)], axis=0)
            vt = jnp.concatenate([v_prev, vc_ref[ls, rows], v_next], axis=1)
            ot = jnp.dot(jnp.concatenate([vt, ones], axis=0), p, preferred_element_type=F32)
            denom = ot[2 * hd:2 * hd + 1] + jnp.exp2(sink - m)
            on = ot[0:2 * hd] * (1.0 / denom)
            pair_a = jnp.concatenate([on[0:hd, 0:b], on[hd:2 * hd, b:2 * b]], axis=0)
            pair_b = jnp.concatenate([on[0:hd, 2 * b:3 * b], on[hd:2 * hd, 3 * b:4 * b]], axis=0)
            o_ref[rows, (2 * g) * LANES:(2 * g + 1) * LANES] = pair_a.T.astype(BF16)
            o_ref[rows, (2 * g + 1) * LANES:(2 * g + 2) * LANES] = pair_b.T.astype(BF16)


def _attention(q, kd, vt, sink, geo):
    T = q.shape[0]
    nsub = _ATTN_STEP_BLOCKS
    nb = T // ATTN_BLOCK
    step_rows = nsub * ATTN_BLOCK
    prev = lambda i: jnp.maximum(i * nsub - 1, 0)
    nxt = lambda i: jnp.minimum((i + 1) * nsub, nb - 1)
    k_edge = lambda m: pl.BlockSpec((ATTN_BLOCK, KV_DUP_COLS), lambda i: (m(i), 0))
    v_edge = lambda m: pl.BlockSpec((KV_DUP_COLS, ATTN_BLOCK), lambda i: (0, m(i)))
    return pl.pallas_call(
        functools.partial(_attn_body, geo),
        grid=(T // step_rows,),
        in_specs=[
            pl.BlockSpec(memory_space=pltpu.SMEM),
            pl.BlockSpec((step_rows, Q_COLS), lambda i: (i, 0)),
            k_edge(prev), pl.BlockSpec((step_rows, KV_DUP_COLS), lambda i: (i, 0)), k_edge(nxt),
            v_edge(prev), pl.BlockSpec((KV_DUP_COLS, step_rows), lambda i: (0, i)), v_edge(nxt),
        ],
        out_specs=pl.BlockSpec((step_rows, Q_COLS), lambda i: (i, 0)),
        out_shape=jax.ShapeDtypeStruct((T, Q_COLS), BF16),
        compiler_params=_params("parallel"),
        name="window_attn",
    )(sink, q, kd, kd, kd, vt, vt, vt)


_CONV_ROWS = 64
_LN_ROWS = 32


def _conv_body(geo, tc, z_ref, zp_ref, zn_ref, w_ref, dwb_ref, lng_ref, lnb_ref, o_ref,
               buf_ref, y_ref):
    i = pl.program_id(0)
    r0 = i * tc
    start, end = _seq_bounds(geo, r0)
    has_prev = (r0 > start).astype(F32)
    has_next = (r0 + tc < end).astype(F32)
    rows = tc + 2 * HALO
    buf_ref[:, 0:HALO, :] = zp_ref[...] * has_prev
    buf_ref[:, HALO:HALO + tc, :] = z_ref[...]
    buf_ref[:, HALO + tc:rows, :] = zn_ref[...] * has_next

    rep = _CONV_ROWS // SUBLANES

    def conv_chunk(j):
        s0 = pl.multiple_of(j * _CONV_ROWS, _CONV_ROWS)
        for c in range(N_SLAB):
            ls = slice(c * LANES, (c + 1) * LANES)
            acc = jnp.zeros((_CONV_ROWS, LANES), F32)
            for k in range(CONV_WIDTH):
                zt = buf_ref[c, pl.ds(s0 + (HALO - CONV_PAD + k), _CONV_ROWS, stride=1), :]
                wt = w_ref[k * SUBLANES:(k + 1) * SUBLANES, ls]
                acc = acc + zt * jnp.concatenate([wt] * rep, axis=0)
            y_ref[pl.ds(s0, _CONV_ROWS), ls] = acc

    def ln_chunk(j):
        for h in range(_CONV_ROWS // _LN_ROWS):
            s0 = pl.multiple_of(j * _CONV_ROWS + h * _LN_ROWS, _LN_ROWS)
            y = y_ref[pl.ds(s0, _LN_ROWS), :] + dwb_ref[...]
            mu = jnp.mean(y, axis=-1, keepdims=True)
            yc = y - mu
            var = jnp.mean(yc * yc, axis=-1, keepdims=True)
            yn = yc * lax.rsqrt(var + NORM_EPS) * lng_ref[...] + lnb_ref[...]
            o_ref[pl.ds(s0, _LN_ROWS), :] = (yn * jax.nn.sigmoid(yn)).astype(BF16)

    n = tc // _CONV_ROWS
    conv_chunk(0)

    def body(j, carry):
        ln_chunk(j - 1)
        conv_chunk(j)
        return carry

    lax.fori_loop(1, n, body, 0)
    ln_chunk(n - 1)


def _conv_branch(glu, w_rep, dw_b, ln_g, ln_b, geo, tc):
    T = glu.shape[1]
    nh = T // HALO
    per = tc // HALO
    const = lambda i: (0, 0)
    return pl.pallas_call(
        functools.partial(_conv_body, geo, tc),
        grid=(T // tc,),
        in_specs=[
            pl.BlockSpec((N_SLAB, tc, LANES), lambda i: (0, i, 0)),
            pl.BlockSpec((N_SLAB, HALO, LANES), lambda i: (0, jnp.maximum(i * per - 1, 0), 0)),
            pl.BlockSpec((N_SLAB, HALO, LANES), lambda i: (0, jnp.minimum((i + 1) * per, nh - 1), 0)),
            pl.BlockSpec((CONV_WIDTH * SUBLANES, D_MODEL), const),
            pl.BlockSpec((1, D_MODEL), const),
            pl.BlockSpec((1, D_MODEL), const),
            pl.BlockSpec((1, D_MODEL), const),
        ],
        out_specs=pl.BlockSpec((tc, D_MODEL), lambda i: (i, 0)),
        out_shape=jax.ShapeDtypeStruct((T, D_MODEL), BF16),
        scratch_shapes=[pltpu.VMEM((N_SLAB, tc + 2 * HALO, LANES), F32),
                        pltpu.VMEM((tc, D_MODEL), F32)],
        compiler_params=_params("parallel"),
        name="conv_branch",
    )(glu, glu, glu, w_rep, dw_b, ln_g, ln_b)


def _mix_body(n_a, xa_ref, xb_ref, o_ref, c_ref, gate_ref, wo_ref, wpw_ref, bpw_ref, wout_ref,
              gffn_ref, wrh_ref, wrl_ref, br_ref, tri_ref,
              x1_ref, h2_ref, idx_ref, gt_ref, rank_ref, cnt_ref, carry_ref):
    tm = xa_ref.shape[0]
    x = jnp.where(pl.program_id(0) < n_a, xa_ref[...], xb_ref[...])

    @pl.when(pl.program_id(0) == 0)
    def _():
        carry_ref[...] = jnp.zeros_like(carry_ref)

    attn = jnp.dot(o_ref[...], wo_ref[...], preferred_element_type=F32)
    conv = jnp.dot(c_ref[...], wpw_ref[...], preferred_element_type=F32) + bpw_ref[...]
    g_attn = gate_ref[:, 0:D_MODEL].astype(F32)
    g_conv = gate_ref[:, D_MODEL:2 * D_MODEL].astype(F32)
    mix = (g_attn * attn + g_conv * conv).astype(BF16)
    x1 = x + jnp.dot(mix, wout_ref[...], preferred_element_type=F32)
    x1_ref[...] = x1
    ms = jnp.mean(x1 * x1, axis=-1, keepdims=True)
    h2 = x1 * lax.rsqrt(ms + NORM_EPS) * gffn_ref[...]
    h_hi = h2.astype(BF16)
    h2_ref[...] = _pack_halves(h_hi)

    h_lo = (h2 - h_hi.astype(F32)).astype(BF16)
    logits = (lax.dot_general(wrh_ref[...], h_hi, _NT, preferred_element_type=F32)
              + lax.dot_general(wrh_ref[...], h_lo, _NT, preferred_element_type=F32)
              + lax.dot_general(wrl_ref[...], h_hi, _NT, preferred_element_type=F32)
              + br_ref[...])

    eidx = lax.broadcasted_iota(I32, (N_EXPERTS, tm), 0)
    vals = logits
    picked, top_vals, top_idx = [], [], []
    for _ in range(TOP_K):
        m = jnp.max(vals, axis=0, keepdims=True)
        idx = jnp.min(jnp.where(vals == m, eidx, N_EXPERTS), axis=0, keepdims=True)
        sel = eidx == idx
        vals = jnp.where(sel, -jnp.inf, vals)
        picked.append(sel)
        top_vals.append(m)
        top_idx.append(idx)

    exps = [jnp.exp(v - top_vals[0]) for v in top_vals]
    tot = exps[0] + exps[1] + exps[2] + exps[3]
    onehot = (picked[0] | picked[1] | picked[2] | picked[3])
    prefix = jnp.dot(onehot.astype(BF16), tri_ref[...], preferred_element_type=F32) + carry_ref[...]
    for j in range(TOP_K):
        idx_ref[j:j + 1, :] = top_idx[j]
        gt_ref[j:j + 1, :] = exps[j] / tot
        rank_ref[j:j + 1, :] = jnp.sum(jnp.where(picked[j], prefix, 0.0), axis=0,
                                       keepdims=True).astype(I32)
    carry_ref[...] = carry_ref[...] + jnp.sum(onehot.astype(F32), axis=1, keepdims=True)
    cnt_ref[...] = jnp.broadcast_to(carry_ref[...], cnt_ref.shape)


def _mix_route(xa, xb, o, c, gates, wo, wpw, bpw, wout, gffn, wr_hi, wr_lo, br, tri, geo, tm):
    T = geo.total
    row = lambda i: (i, 0)
    col = lambda i: (0, i)
    const = lambda i: (0, 0)
    wspec = lambda shape: pl.BlockSpec(shape, const, pipeline_mode=pl.Buffered(1))
    n_a, x_specs = _two_group_specs(geo, tm, D_MODEL)
    return pl.pallas_call(
        functools.partial(_mix_body, n_a),
        grid=(T // tm,),
        in_specs=x_specs + [
            pl.BlockSpec((tm, Q_COLS), row),
            pl.BlockSpec((tm, D_MODEL), row),
            pl.BlockSpec((tm, 2 * D_MODEL), row),
            wspec((Q_COLS, D_MODEL)),
            wspec((D_MODEL, D_MODEL)),
            pl.BlockSpec((1, D_MODEL), const),
            wspec((D_MODEL, D_MODEL)),
            pl.BlockSpec((1, D_MODEL), const),
            pl.BlockSpec((N_EXPERTS, D_MODEL), const),
            pl.BlockSpec((N_EXPERTS, D_MODEL), const),
            pl.BlockSpec((N_EXPERTS, 1), const),
            pl.BlockSpec((tm, tm), const),
        ],
        out_specs=[
            pl.BlockSpec((tm, D_MODEL), row),
            pl.BlockSpec((tm, PACKED), row),
            pl.BlockSpec((TOP_K, tm), col),
            pl.BlockSpec((TOP_K, tm), col),
            pl.BlockSpec((TOP_K, tm), col),
            pl.BlockSpec((N_EXPERTS, LANES), const),
        ],
        out_shape=[
            jax.ShapeDtypeStruct((T, D_MODEL), F32),
            jax.ShapeDtypeStruct((T, PACKED), I32),
            jax.ShapeDtypeStruct((TOP_K, T), I32),
            jax.ShapeDtypeStruct((TOP_K, T), F32),
            jax.ShapeDtypeStruct((TOP_K, T), I32),
            jax.ShapeDtypeStruct((N_EXPERTS, LANES), F32),
        ],
        scratch_shapes=[pltpu.VMEM((N_EXPERTS, 1), F32)],
        compiler_params=_params("arbitrary"),
        name="mix_route",
    )(xa, xb, o, c, gates, wo, wpw, bpw, wout, gffn, wr_hi, wr_lo, br, tri)


def _sc_workers():
    info = plsc.get_sparse_core_info()
    return info.num_cores, info.num_cores * info.num_subcores


def _sc_dispatch(h, dest, n_out):
    t_rows, width = h.shape
    nc, nw = _sc_workers()
    per_w = t_rows // nw
    step = SC_ROWS_PER_STEP
    assert per_w * nw == t_rows and per_w % step == 0
    mesh = plsc.VectorSubcoreMesh(core_axis_name="c", subcore_axis_name="s")

    @functools.partial(
        pl.kernel, mesh=mesh,
        out_type=jax.ShapeDtypeStruct((n_out, width), h.dtype),
        scratch_types=[pltpu.VMEM((step,), I32)] * TOP_K
        + [pltpu.VMEM((step, width), h.dtype), pltpu.SemaphoreType.DMA],
    )
    def scatter_rows(h_hbm, dest_hbm, out_hbm, i0, i1, i2, i3, rows_v, sem):
        del sem
        base = (lax.axis_index("s") * nc + lax.axis_index("c")) * per_w

        @pl.loop(0, per_w // step)
        def _(i):
            off = base + i * step
            pltpu.sync_copy(h_hbm.at[pl.ds(off, step)], rows_v)
            for j, idx_v in enumerate((i0, i1, i2, i3)):
                pltpu.sync_copy(dest_hbm.at[j, pl.ds(off, step)], idx_v)
                pltpu.sync_copy(rows_v, out_hbm.at[idx_v])

    return scatter_rows(h, dest)


def _sc_gather(table, idx):
    n = idx.shape[0]
    width = table.shape[1]
    nc, nw = _sc_workers()
    per_w = n // nw
    step = SC_ROWS_PER_STEP
    assert per_w * nw == n and per_w % step == 0
    mesh = plsc.VectorSubcoreMesh(core_axis_name="c", subcore_axis_name="s")

    @functools.partial(
        pl.kernel, mesh=mesh,
        out_type=jax.ShapeDtypeStruct((n, width), table.dtype),
        scratch_types=[pltpu.VMEM((step,), I32), pltpu.VMEM((step, width), table.dtype),
                       pltpu.SemaphoreType.DMA],
    )
    def gather_rows(table_hbm, idx_hbm, out_hbm, idx_v, rows_v, sem):
        base = (lax.axis_index("s") * nc + lax.axis_index("c")) * per_w

        @pl.loop(0, per_w // step)
        def _(i):
            off = base + i * step
            pltpu.sync_copy(idx_hbm.at[pl.ds(off, step)], idx_v)
            pltpu.async_copy(table_hbm.at[idx_v], rows_v, sem).wait()
            pltpu.sync_copy(rows_v, out_hbm.at[pl.ds(off, step)])

    return gather_rows(table, idx)


def _expert_body(bexp_ref, nused_ref, valid_ref, xs_ref, wgu_ref, bgu_ref, wd_ref, bd_ref, o_ref,
                 wgu_bf, wd_bf):
    n = pl.program_id(0)
    active = n < nused_ref[0]
    new_expert = (n == 0) | (bexp_ref[n] != bexp_ref[jnp.maximum(n - 1, 0)])

    @pl.when(active & new_expert)
    def _():
        wgu_bf[...] = wgu_ref[...].astype(BF16)
        wd_bf[...] = wd_ref[...].astype(BF16)

    @pl.when(active)
    def _():
        row = lax.broadcasted_iota(I32, (xs_ref.shape[0], 1), 0)
        x = _unpack_halves(jnp.where(row < valid_ref[n], xs_ref[...], 0), BF16)
        gu = jnp.dot(x, wgu_bf[...], preferred_element_type=F32) + bgu_ref[...]
        g = jnp.minimum(gu[:, 0:D_FF], SWIGLU_LIMIT)
        u = jnp.clip(gu[:, D_FF:2 * D_FF], -SWIGLU_LIMIT, SWIGLU_LIMIT)
        act = g * jax.nn.sigmoid(SWIGLU_ALPHA * g) * (u + 1.0)
        out = jnp.dot(act.astype(BF16), wd_bf[...], preferred_element_type=F32) + bd_ref[...]
        o_ref[...] = _pack_halves(out.astype(BF16))

    @pl.when(jnp.logical_not(active))
    def _():
        o_ref[...] = jnp.zeros_like(o_ref)


def _experts(block_exp, n_used, block_valid, xs, wgu, bgu, wd, bd, bm):
    P = xs.shape[0]
    nblk = P // bm
    xmap = lambda n, be, nu, bv: (jnp.minimum(n, nu[0] - 1), 0)
    emap = lambda n, be, nu, bv: (be[n], 0, 0)
    grid_spec = pltpu.PrefetchScalarGridSpec(
        num_scalar_prefetch=3,
        grid=(nblk,),
        in_specs=[
            pl.BlockSpec((bm, PACKED), xmap),
            pl.BlockSpec((None, D_MODEL, 2 * D_FF), emap),
            pl.BlockSpec((None, 1, 2 * D_FF), emap),
            pl.BlockSpec((None, D_FF, D_MODEL), emap),
            pl.BlockSpec((None, 1, D_MODEL), emap),
        ],
        out_specs=pl.BlockSpec((bm, PACKED), lambda n, be, nu, bv: (n, 0)),
        scratch_shapes=[pltpu.VMEM((D_MODEL, 2 * D_FF), BF16), pltpu.VMEM((D_FF, D_MODEL), BF16)],
    )
    return pl.pallas_call(
        _expert_body,
        grid_spec=grid_spec,
        out_shape=jax.ShapeDtypeStruct((P, PACKED), I32),
        compiler_params=_params("arbitrary"),
        name="experts",
    )(block_exp, n_used, block_valid, xs, wgu, bgu, wd, bd)


def _combine_body(n_a, x1_ref, g0_ref, g1_ref, g2_ref, g3_ref, gate_ref, gfin_ref, ya_ref, yb_ref):
    gate = gate_ref[...]
    y = x1_ref[...]
    for j, g_ref in enumerate((g0_ref, g1_ref, g2_ref, g3_ref)):
        y = y + gate[:, j:j + 1] * _unpack_halves(g_ref[...], F32)
    ms = jnp.mean(y * y, axis=-1, keepdims=True)
    y = y * lax.rsqrt(ms + NORM_EPS) * gfin_ref[...]

    @pl.when(pl.program_id(0) < n_a)
    def _():
        ya_ref[...] = y

    @pl.when(pl.program_id(0) >= n_a)
    def _():
        yb_ref[...] = y


def _combine(x1, gathered, gate_tok, gfin, geo, tf):
    T = x1.shape[0]
    nt = T // tf
    row = lambda i: (i, 0)
    n_a, y_specs = _two_group_specs(geo, tf, D_MODEL)
    choice = lambda j: pl.BlockSpec((tf, PACKED), lambda i: (j * nt + i, 0))
    return pl.pallas_call(
        functools.partial(_combine_body, n_a),
        grid=(nt,),
        in_specs=[pl.BlockSpec((tf, D_MODEL), row)] + [choice(j) for j in range(TOP_K)] + [
            pl.BlockSpec((tf, TOP_K), row),
            pl.BlockSpec((1, D_MODEL), lambda i: (0, 0)),
        ],
        out_specs=y_specs,
        out_shape=[jax.ShapeDtypeStruct((geo.rows_a, D_MODEL), F32),
                   jax.ShapeDtypeStruct((T - geo.rows_a, D_MODEL), F32)],
        compiler_params=_params("arbitrary"),
        name="combine",
    )(x1, gathered, gathered, gathered, gathered, gate_tok, gfin)


def _permute_in_proj(w):
    lead = w.shape[:-1]
    half = HEAD_DIM // 2
    q = w[..., :Q_COLS].reshape(*lead, N_Q_HEADS // 2, 2, 2, half)
    q = jnp.swapaxes(q, -3, -2).reshape(*lead, Q_COLS)
    k = w[..., Q_COLS:Q_COLS + KV_COLS].reshape(*lead, N_KV_HEADS, 2, 1, half)
    k = jnp.broadcast_to(k, (*lead, N_KV_HEADS, 2, 2, half)).reshape(*lead, KV_DUP_COLS)
    v = w[..., Q_COLS + KV_COLS:Q_COLS + 2 * KV_COLS].reshape(*lead, N_KV_HEADS, 1, HEAD_DIM)
    v = jnp.broadcast_to(v, (*lead, N_KV_HEADS, 2, HEAD_DIM)).reshape(*lead, KV_DUP_COLS)
    return jnp.concatenate([q, k, w[..., Q_COLS + 2 * KV_COLS:]], axis=-1), v


def _rope_tables(n_pos):
    half = HEAD_DIM // 2
    inv_freq = 1.0 / (ROPE_THETA ** (jnp.arange(half, dtype=F32) * (2.0 / HEAD_DIM)))
    ang = jnp.arange(n_pos, dtype=F32)[:, None] * inv_freq[None, :]
    cos = jnp.tile(jnp.cos(ang), (1, LANES // half))
    sin = jnp.tile(jnp.sin(ang), (1, LANES // half))
    sign = jnp.where(jnp.arange(LANES) < LANES // 2, -1.0, 1.0).astype(F32)
    return cos, sin * sign[None, :]


def _tiles(geo):
    unit = min(geo.len_a, geo.len_b)
    tile = min(512, unit)
    return dict(tm=tile, tc=tile, tf=tile, bm=512)


def kernel(x_prompt, x_sample, norm_mix_g, w_in, b_in, attn_sink, w_o_attn, conv_dw_w, conv_dw_b,
           conv_ln_g, conv_ln_b, w_pw2, b_pw2, w_out, norm_ffn_g, w_router, b_router, w_gu, b_gu,
           w_down, b_down, norm_final_g):
    assert w_in.shape[0] == 1, "single trunk layer"
    geo = Geo(x_prompt.shape[0], x_prompt.shape[1], x_sample.shape[0], x_sample.shape[1])
    T = geo.total
    ts = _tiles(geo)
    xa = x_prompt.reshape(-1, D_MODEL)
    xb = x_sample.reshape(-1, D_MODEL)

    w_perm, w_v = _permute_in_proj(w_in[0])
    b_perm, b_v = _permute_in_proj(b_in)
    cos_t, sin_t = _rope_tables(max(geo.len_a, geo.len_b))

    q, kd, vt, glu, gates = _in_proj(xa, xb, norm_mix_g, w_perm.astype(BF16), b_perm,
                                     w_v.T.astype(BF16), b_v.T, cos_t, sin_t, geo, ts["tm"])
    attn = _attention(q, kd, vt, attn_sink[0], geo)
    w_rep = jnp.repeat(conv_dw_w[0], SUBLANES, axis=0)
    conv = _conv_branch(glu, w_rep, conv_dw_b, conv_ln_g, conv_ln_b, geo, ts["tc"])

    wr_t = w_router[0].T
    wr_hi = wr_t.astype(BF16)
    wr_lo = (wr_t - wr_hi.astype(F32)).astype(BF16)
    tri = jnp.triu(jnp.ones((ts["tm"], ts["tm"]), BF16), 1)
    x1, h2, idx, gate_t, rank, counts = _mix_route(
        xa, xb, attn, conv, gates, w_o_attn[0].astype(BF16), w_pw2[0].astype(BF16), b_pw2,
        w_out[0].astype(BF16), norm_ffn_g, wr_hi, wr_lo, b_router[0][:, None], tri, geo, ts["tm"])

    bm = ts["bm"]
    n_blocks = (T * TOP_K) // bm + N_EXPERTS
    cnt = counts[:, 0].astype(I32)
    padded = ((cnt + bm - 1) // bm) * bm
    pad_end = jnp.cumsum(padded)
    pad_start = pad_end - padded
    expert_ids = jnp.arange(N_EXPERTS, dtype=I32)
    dest = rank + jnp.sum(jnp.where(idx[None] == expert_ids[:, None, None],
                                    pad_start[:, None, None], 0), axis=0)
    block_start = jnp.arange(n_blocks, dtype=I32) * bm
    block_exp = jnp.minimum(jnp.sum((pad_end[None, :] <= block_start[:, None]).astype(I32), axis=1),
                            N_EXPERTS - 1)
    n_used = (pad_end[-1:] // bm).astype(I32)
    seg_end = jnp.sum(jnp.where(block_exp[:, None] == expert_ids[None, :],
                                (pad_start + cnt)[None, :], 0), axis=1)
    block_valid = jnp.clip(seg_end - block_start, 0, bm).astype(I32)

    xs = _sc_dispatch(h2, dest, n_blocks * bm)
    ys = _experts(block_exp, n_used, block_valid, xs, w_gu[0], b_gu[0][:, None, :],
                  w_down[0], b_down[0][:, None, :], bm)
    gathered = _sc_gather(ys, dest.reshape(-1))
    ya, yb = _combine(x1, gathered, gate_t.T, norm_final_g[None, :], geo, ts["tf"])
    return (ya.reshape(x_prompt.shape), yb.reshape(x_sample.shape))
```

```python
import functools
import math
from typing import NamedTuple

import jax
import jax.numpy as jnp
from jax import lax
from jax.experimental import pallas as pl
from jax.experimental.pallas import tpu as pltpu
from jax.experimental.pallas import tpu_sc as plsc

F32 = jnp.float32
BF16 = jnp.bfloat16
I32 = jnp.int32

D_MODEL = 1024
HEAD_DIM = 64
N_Q_HEADS = 16
N_KV_HEADS = 4
WINDOW = 128
ATTN_BLOCK = 128
ROPE_THETA = 10000.0
CONV_WIDTH = 31
CONV_PAD = CONV_WIDTH // 2
N_EXPERTS = 32
TOP_K = 4
D_FF = D_MODEL
SWIGLU_LIMIT = 7.0
SWIGLU_ALPHA = 1.702
NORM_EPS = 1e-5
NEG_INF = -1e30

Q_COLS = N_Q_HEADS * HEAD_DIM
KV_COLS = N_KV_HEADS * HEAD_DIM
LANES = 128
SUBLANES = 8
KV_DUP_COLS = N_KV_HEADS * LANES
HALO = 16
VMEM_LIMIT = 56 * 1024 * 1024
SC_ROWS_PER_STEP = 128


class Geo(NamedTuple):
    n_a: int
    len_a: int
    n_b: int
    len_b: int

    @property
    def rows_a(self):
        return self.n_a * self.len_a

    @property
    def total(self):
        return self.rows_a + self.n_b * self.len_b


def _seq_bounds(geo, r):
    in_a = r < geo.rows_a
    start_a = (r // geo.len_a) * geo.len_a
    start_b = geo.rows_a + ((r - geo.rows_a) // geo.len_b) * geo.len_b
    start = jnp.where(in_a, start_a, start_b)
    end = start + jnp.where(in_a, geo.len_a, geo.len_b)
    return start, end


def _params(*sem):
    return pltpu.CompilerParams(dimension_semantics=sem, vmem_limit_bytes=VMEM_LIMIT)


PACKED = D_MODEL // 2


def _pack_halves(x_bf16):
    hi = lax.bitcast_convert_type(x_bf16[:, :PACKED].astype(F32), I32)
    lo = lax.bitcast_convert_type(x_bf16[:, PACKED:].astype(F32), I32)
    return hi | lax.shift_right_logical(lo, 16)


def _unpack_halves(words, dtype):
    hi = lax.bitcast_convert_type(words & jnp.int32(-65536), F32)
    lo = lax.bitcast_convert_type(lax.shift_left(words, 16), F32)
    return jnp.concatenate([hi.astype(dtype), lo.astype(dtype)], axis=1)


def _two_group_specs(geo, tile, width):
    n_a = geo.rows_a // tile
    return n_a, [pl.BlockSpec((tile, width), lambda i: (jnp.minimum(i, n_a - 1), 0)),
                 pl.BlockSpec((tile, width), lambda i: (jnp.maximum(i - n_a, 0), 0))]


_C_Q = 0
_C_K = _C_Q + Q_COLS
_C_GA = _C_K + KV_DUP_COLS
_C_GG = _C_GA + D_MODEL
_C_GATE = _C_GG + D_MODEL
_N_IN = _C_GATE + 2 * D_MODEL
_PROJ_CHUNK = 512
N_SLAB = D_MODEL // LANES
_NT = (((1,), (1,)), ((), ()))
_LOG2E = math.log2(math.e)
_Q_SCALE = HEAD_DIM ** -0.5 * _LOG2E


def _in_proj_body(n_a, xa_ref, xb_ref, g_ref, w_ref, b_ref, wvt_ref, bvt_ref, cos_ref, sin_ref,
                  q_ref, kd_ref, vt_ref, glu_ref, gate_ref):
    x = jnp.where(pl.program_id(0) < n_a, xa_ref[...], xb_ref[...])
    ms = jnp.mean(x * x, axis=-1, keepdims=True)
    h = (x * lax.rsqrt(ms + NORM_EPS) * g_ref[...]).astype(BF16)
    cos = cos_ref[...]
    sin = sin_ref[...]

    def proj(c0):
        return (jnp.dot(h, w_ref[:, c0:c0 + _PROJ_CHUNK], preferred_element_type=F32)
                + b_ref[:, c0:c0 + _PROJ_CHUNK])

    def rope_store(z, out_ref, o0, scale):
        for c in range(_PROJ_CHUNK // LANES):
            zc = z[:, c * LANES:(c + 1) * LANES]
            r = zc * cos + pltpu.roll(zc, LANES // 2, 1) * sin
            if scale != 1.0:
                r = r * scale
            out_ref[:, o0 + c * LANES:o0 + (c + 1) * LANES] = r.astype(out_ref.dtype)

    for c in range(Q_COLS // _PROJ_CHUNK):
        rope_store(proj(_C_Q + c * _PROJ_CHUNK), q_ref, c * _PROJ_CHUNK, _Q_SCALE)
    for c in range(KV_DUP_COLS // _PROJ_CHUNK):
        rope_store(proj(_C_K + c * _PROJ_CHUNK), kd_ref, c * _PROJ_CHUNK, 1.0)
    vt = lax.dot_general(wvt_ref[...], h, _NT, preferred_element_type=F32) + bvt_ref[...]
    vt_ref[...] = vt.astype(BF16)
    per = _PROJ_CHUNK // LANES
    for c in range(D_MODEL // _PROJ_CHUNK):
        a = proj(_C_GA + c * _PROJ_CHUNK)
        g = proj(_C_GG + c * _PROJ_CHUNK)
        glu = a * jax.nn.sigmoid(g)
        for s in range(per):
            glu_ref[c * per + s] = glu[:, s * LANES:(s + 1) * LANES]
    for c in range(2 * D_MODEL // _PROJ_CHUNK):
        gate_ref[:, c * _PROJ_CHUNK:(c + 1) * _PROJ_CHUNK] = jax.nn.sigmoid(
            proj(_C_GATE + c * _PROJ_CHUNK)).astype(BF16)


def _in_proj(xa, xb, g_mix, w_perm, b_perm, wvt, bvt, cos_t, sin_t, geo, tm):
    T = geo.total

    def pos_map(i):
        r0 = i * tm
        start, _ = _seq_bounds(geo, r0)
        return ((r0 - start) // tm, 0)

    const = lambda i: (0, 0)
    row = lambda i: (i, 0)
    n_a, x_specs = _two_group_specs(geo, tm, D_MODEL)
    return pl.pallas_call(
        functools.partial(_in_proj_body, n_a),
        grid=(T // tm,),
        in_specs=x_specs + [
            pl.BlockSpec((1, D_MODEL), const),
            pl.BlockSpec((D_MODEL, _N_IN), const, pipeline_mode=pl.Buffered(1)),
            pl.BlockSpec((1, _N_IN), const),
            pl.BlockSpec((KV_DUP_COLS, D_MODEL), const),
            pl.BlockSpec((KV_DUP_COLS, 1), const),
            pl.BlockSpec((tm, LANES), pos_map),
            pl.BlockSpec((tm, LANES), pos_map),
        ],
        out_specs=[
            pl.BlockSpec((tm, Q_COLS), row),
            pl.BlockSpec((tm, KV_DUP_COLS), row),
            pl.BlockSpec((KV_DUP_COLS, tm), lambda i: (0, i)),
            pl.BlockSpec((N_SLAB, tm, LANES), lambda i: (0, i, 0)),
            pl.BlockSpec((tm, 2 * D_MODEL), row),
        ],
        out_shape=[
            jax.ShapeDtypeStruct((T, Q_COLS), BF16),
            jax.ShapeDtypeStruct((T, KV_DUP_COLS), BF16),
            jax.ShapeDtypeStruct((KV_DUP_COLS, T), BF16),
            jax.ShapeDtypeStruct((N_SLAB, T, LANES), F32),
            jax.ShapeDtypeStruct((T, 2 * D_MODEL), BF16),
        ],
        compiler_params=_params("parallel"),
        name="in_proj",
    )(xa, xb, g_mix, w_perm, b_perm, wvt, bvt, cos_t, sin_t)


_ONES_ROWS = 16
_ATTN_STEP_BLOCKS = 4


def _attn_body(geo, sink_ref, q_ref, kp_ref, kc_ref, kn_ref, vp_ref, vc_ref, vn_ref, o_ref):
    group = N_Q_HEADS // N_KV_HEADS
    nq = group * ATTN_BLOCK
    b = ATTN_BLOCK
    hd = HEAD_DIM
    nsub = _ATTN_STEP_BLOCKS

    key = lax.broadcasted_iota(I32, (b, nq), 0)
    qry = lax.broadcasted_iota(I32, (b, nq), 1) % b
    head_of_col = lax.broadcasted_iota(I32, (1, nq), 1) // b
    lane = lax.broadcasted_iota(I32, (b, LANES), 1)
    even_head = (lane % hd) < (hd // 2)
    ones = jnp.ones((_ONES_ROWS, 3 * b), BF16)

    for s in range(nsub):
        rows = slice(s * b, (s + 1) * b)
        r0 = (pl.program_id(0) * nsub + s) * b
        start, end = _seq_bounds(geo, r0)
        bias_prev = jnp.where((key >= qry) & (r0 > start), 0.0, NEG_INF)
        bias_next = jnp.where((key <= qry) & (r0 + b < end), 0.0, NEG_INF)

        for g in range(N_KV_HEADS):
            ls = slice(g * LANES, (g + 1) * LANES)
            k_prev = kp_ref[:, ls] if s == 0 else kc_ref[(s - 1) * b:s * b, ls]
            k_next = kn_ref[:, ls] if s == nsub - 1 else kc_ref[(s + 1) * b:(s + 2) * b, ls]
            v_prev = vp_ref[ls, :] if s == 0 else vc_ref[ls, (s - 1) * b:s * b]
            v_next = vn_ref[ls, :] if s == nsub - 1 else vc_ref[ls, (s + 1) * b:(s + 2) * b]

            qa = q_ref[rows, (2 * g) * LANES:(2 * g + 1) * LANES]
            qb = q_ref[rows, (2 * g + 1) * LANES:(2 * g + 2) * LANES]
            zero = jnp.zeros_like(qa)
            q4 = jnp.concatenate([jnp.where(even_head, qa, zero), jnp.where(even_head, zero, qa),
                                  jnp.where(even_head, qb, zero), jnp.where(even_head, zero, qb)],
                                 axis=0)
            k = jnp.concatenate([k_prev, kc_ref[rows, ls], k_next], axis=0)
            st = lax.dot_general(k, q4, _NT, preferred_element_type=F32)
            s_prev = st[0:b] + bias_prev
            s_cur = st[b:2 * b]
            s_next = st[2 * b:3 * b] + bias_next
            sink = jnp.full((1, nq), sink_ref[group * g] * _LOG2E, F32)
            for h in range(1, group):
                sink = jnp.where(head_of_col == h, sink_ref[group * g + h] * _LOG2E, sink)
            m = jnp.maximum(jnp.maximum(jnp.max(s_prev, axis=0, keepdims=True),
                                        jnp.max(s_cur, axis=0, keepdims=True)),
                            jnp.maximum(jnp.max(s_next, axis=0, keepdims=True), sink))
            p = jnp.concatenate([jnp.exp2(s_prev - m).astype(BF16), jnp.exp2(s_cur - m).astype(BF16),
                                 jnp.exp2(s_next - m).astype(BF16)], axis=0)
            vt = jnp.concatenate([v_prev, vc_ref[ls, rows], v_next], axis=1)
            ot = jnp.dot(jnp.concatenate([vt, ones], axis=0), p, preferred_element_type=F32)
            denom = ot[2 * hd:2 * hd + 1] + jnp.exp2(sink - m)
            on = ot[0:2 * hd] * (1.0 / denom)
            pair_a = jnp.concatenate([on[0:hd, 0:b], on[hd:2 * hd, b:2 * b]], axis=0)
            pair_b = jnp.concatenate([on[0:hd, 2 * b:3 * b], on[hd:2 * hd, 3 * b:4 * b]], axis=0)
            o_ref[rows, (2 * g) * LANES:(2 * g + 1) * LANES] = pair_a.T.astype(BF16)
            o_ref[rows, (2 * g + 1) * LANES:(2 * g + 2) * LANES] = pair_b.T.astype(BF16)


def _attention(q, kd, vt, sink, geo):
    T = q.shape[0]
    nsub = _ATTN_STEP_BLOCKS
    nb = T // ATTN_BLOCK
    step_rows = nsub * ATTN_BLOCK
    prev = lambda i: jnp.maximum(i * nsub - 1, 0)
    nxt = lambda i: jnp.minimum((i + 1) * nsub, nb - 1)
    k_edge = lambda m: pl.BlockSpec((ATTN_BLOCK, KV_DUP_COLS), lambda i: (m(i), 0))
    v_edge = lambda m: pl.BlockSpec((KV_DUP_COLS, ATTN_BLOCK), lambda i: (0, m(i)))
    return pl.pallas_call(
        functools.partial(_attn_body, geo),
        grid=(T // step_rows,),
        in_specs=[
            pl.BlockSpec(memory_space=pltpu.SMEM),
            pl.BlockSpec((step_rows, Q_COLS), lambda i: (i, 0)),
            k_edge(prev), pl.BlockSpec((step_rows, KV_DUP_COLS), lambda i: (i, 0)), k_edge(nxt),
            v_edge(prev), pl.BlockSpec((KV_DUP_COLS, step_rows), lambda i: (0, i)), v_edge(nxt),
        ],
        out_specs=pl.BlockSpec((step_rows, Q_COLS), lambda i: (i, 0)),
        out_shape=jax.ShapeDtypeStruct((T, Q_COLS), BF16),
        compiler_params=_params("parallel"),
        name="window_attn",
    )(sink, q, kd, kd, kd, vt, vt, vt)


_CONV_ROWS = 64
_LN_ROWS = 32


def _conv_body(geo, tc, z_ref, zp_ref, zn_ref, w_ref, dwb_ref, lng_ref, lnb_ref, o_ref,
               buf_ref, y_ref):
    i = pl.program_id(0)
    r0 = i * tc
    start, end = _seq_bounds(geo, r0)
    has_prev = (r0 > start).astype(F32)
    has_next = (r0 + tc < end).astype(F32)
    rows = tc + 2 * HALO
    buf_ref[:, 0:HALO, :] = zp_ref[...] * has_prev
    buf_ref[:, HALO:HALO + tc, :] = z_ref[...]
    buf_ref[:, HALO + tc:rows, :] = zn_ref[...] * has_next

    rep = _CONV_ROWS // SUBLANES

    def conv_chunk(j):
        s0 = pl.multiple_of(j * _CONV_ROWS, _CONV_ROWS)
        for c in range(N_SLAB):
            ls = slice(c * LANES, (c + 1) * LANES)
            acc = jnp.zeros((_CONV_ROWS, LANES), F32)
            for k in range(CONV_WIDTH):
                zt = buf_ref[c, pl.ds(s0 + (HALO - CONV_PAD + k), _CONV_ROWS, stride=1), :]
                wt = w_ref[k * SUBLANES:(k + 1) * SUBLANES, ls]
                acc = acc + zt * jnp.concatenate([wt] * rep, axis=0)
            y_ref[pl.ds(s0, _CONV_ROWS), ls] = acc

    def ln_chunk(j):
        for h in range(_CONV_ROWS // _LN_ROWS):
            s0 = pl.multiple_of(j * _CONV_ROWS + h * _LN_ROWS, _LN_ROWS)
            y = y_ref[pl.ds(s0, _LN_ROWS), :] + dwb_ref[...]
            mu = jnp.mean(y, axis=-1, keepdims=True)
            yc = y - mu
            var = jnp.mean(yc * yc, axis=-1, keepdims=True)
            yn = yc * lax.rsqrt(var + NORM_EPS) * lng_ref[...] + lnb_ref[...]
            o_ref[pl.ds(s0, _LN_ROWS), :] = (yn * jax.nn.sigmoid(yn)).astype(BF16)

    n = tc // _CONV_ROWS
    conv_chunk(0)

    def body(j, carry):
        ln_chunk(j - 1)
        conv_chunk(j)
        return carry

    lax.fori_loop(1, n, body, 0)
    ln_chunk(n - 1)


def _conv_branch(glu, w_rep, dw_b, ln_g, ln_b, geo, tc):
    T = glu.shape[1]
    nh = T // HALO
    per = tc // HALO
    const = lambda i: (0, 0)
    return pl.pallas_call(
        functools.partial(_conv_body, geo, tc),
        grid=(T // tc,),
        in_specs=[
            pl.BlockSpec((N_SLAB, tc, LANES), lambda i: (0, i, 0)),
            pl.BlockSpec((N_SLAB, HALO, LANES), lambda i: (0, jnp.maximum(i * per - 1, 0), 0)),
            pl.BlockSpec((N_SLAB, HALO, LANES), lambda i: (0, jnp.minimum((i + 1) * per, nh - 1), 0)),
            pl.BlockSpec((CONV_WIDTH * SUBLANES, D_MODEL), const),
            pl.BlockSpec((1, D_MODEL), const),
            pl.BlockSpec((1, D_MODEL), const),
            pl.BlockSpec((1, D_MODEL), const),
        ],
        out_specs=pl.BlockSpec((tc, D_MODEL), lambda i: (i, 0)),
        out_shape=jax.ShapeDtypeStruct((T, D_MODEL), BF16),
        scratch_shapes=[pltpu.VMEM((N_SLAB, tc + 2 * HALO, LANES), F32),
                        pltpu.VMEM((tc, D_MODEL), F32)],
        compiler_params=_params("parallel"),
        name="conv_branch",
    )(glu, glu, glu, w_rep, dw_b, ln_g, ln_b)


def _mix_body(n_a, xa_ref, xb_ref, o_ref, c_ref, gate_ref, wo_ref, wpw_ref, bpw_ref, wout_ref,
              gffn_ref, wrh_ref, wrl_ref, br_ref, tri_ref,
              x1_ref, h2_ref, idx_ref, gt_ref, rank_ref, cnt_ref, carry_ref):
    tm = xa_ref.shape[0]
    x = jnp.where(pl.program_id(0) < n_a, xa_ref[...], xb_ref[...])

    @pl.when(pl.program_id(0) == 0)
    def _():
        carry_ref[...] = jnp.zeros_like(carry_ref)

    attn = jnp.dot(o_ref[...], wo_ref[...], preferred_element_type=F32)
    conv = jnp.dot(c_ref[...], wpw_ref[...], preferred_element_type=F32) + bpw_ref[...]
    g_attn = gate_ref[:, 0:D_MODEL].astype(F32)
    g_conv = gate_ref[:, D_MODEL:2 * D_MODEL].astype(F32)
    mix = (g_attn * attn + g_conv * conv).astype(BF16)
    x1 = x + jnp.dot(mix, wout_ref[...], preferred_element_type=F32)
    x1_ref[...] = x1
    ms = jnp.mean(x1 * x1, axis=-1, keepdims=True)
    h2 = x1 * lax.rsqrt(ms + NORM_EPS) * gffn_ref[...]
    h_hi = h2.astype(BF16)
    h2_ref[...] = _pack_halves(h_hi)

    h_lo = (h2 - h_hi.astype(F32)).astype(BF16)
    logits = (lax.dot_general(wrh_ref[...], h_hi, _NT, preferred_element_type=F32)
              + lax.dot_general(wrh_ref[...], h_lo, _NT, preferred_element_type=F32)
              + lax.dot_general(wrl_ref[...], h_hi, _NT, preferred_element_type=F32)
              + br_ref[...])

    eidx = lax.broadcasted_iota(I32, (N_EXPERTS, tm), 0)
    vals = logits
    picked, top_vals, top_idx = [], [], []
    for _ in range(TOP_K):
        m = jnp.max(vals, axis=0, keepdims=True)
        idx = jnp.min(jnp.where(vals == m, eidx, N_EXPERTS), axis=0, keepdims=True)
        sel = eidx == idx
        vals = jnp.where(sel, -jnp.inf, vals)
        picked.append(sel)
        top_vals.append(m)
        top_idx.append(idx)

    exps = [jnp.exp(v - top_vals[0]) for v in top_vals]
    tot = exps[0] + exps[1] + exps[2] + exps[3]
    onehot = (picked[0] | picked[1] | picked[2] | picked[3])
    prefix = jnp.dot(onehot.astype(BF16), tri_ref[...], preferred_element_type=F32) + carry_ref[...]
    for j in range(TOP_K):
        idx_ref[j:j + 1, :] = top_idx[j]
        gt_ref[j:j + 1, :] = exps[j] / tot
        rank_ref[j:j + 1, :] = jnp.sum(jnp.where(picked[j], prefix, 0.0), axis=0,
                                       keepdims=True).astype(I32)
    carry_ref[...] = carry_ref[...] + jnp.sum(onehot.astype(F32), axis=1, keepdims=True)
    cnt_ref[...] = jnp.broadcast_to(carry_ref[...], cnt_ref.shape)


def _mix_route(xa, xb, o, c, gates, wo, wpw, bpw, wout, gffn, wr_hi, wr_lo, br, tri, geo, tm):
    T = geo.total
    row = lambda i: (i, 0)
    col = lambda i: (0, i)
    const = lambda i: (0, 0)
    wspec = lambda shape: pl.BlockSpec(shape, const, pipeline_mode=pl.Buffered(1))
    n_a, x_specs = _two_group_specs(geo, tm, D_MODEL)
    return pl.pallas_call(
        functools.partial(_mix_body, n_a),
        grid=(T // tm,),
        in_specs=x_specs + [
            pl.BlockSpec((tm, Q_COLS), row),
            pl.BlockSpec((tm, D_MODEL), row),
            pl.BlockSpec((tm, 2 * D_MODEL), row),
            wspec((Q_COLS, D_MODEL)),
            wspec((D_MODEL, D_MODEL)),
            pl.BlockSpec((1, D_MODEL), const),
            wspec((D_MODEL, D_MODEL)),
            pl.BlockSpec((1, D_MODEL), const),
            pl.BlockSpec((N_EXPERTS, D_MODEL), const),
            pl.BlockSpec((N_EXPERTS, D_MODEL), const),
            pl.BlockSpec((N_EXPERTS, 1), const),
            pl.BlockSpec((tm, tm), const),
        ],
        out_specs=[
            pl.BlockSpec((tm, D_MODEL), row),
            pl.BlockSpec((tm, PACKED), row),
            pl.BlockSpec((TOP_K, tm), col),
            pl.BlockSpec((TOP_K, tm), col),
            pl.BlockSpec((TOP_K, tm), col),
            pl.BlockSpec((N_EXPERTS, LANES), const),
        ],
        out_shape=[
            jax.ShapeDtypeStruct((T, D_MODEL), F32),
            jax.ShapeDtypeStruct((T, PACKED), I32),
            jax.ShapeDtypeStruct((TOP_K, T), I32),
            jax.ShapeDtypeStruct((TOP_K, T), F32),
            jax.ShapeDtypeStruct((TOP_K, T), I32),
            jax.ShapeDtypeStruct((N_EXPERTS, LANES), F32),
        ],
        scratch_shapes=[pltpu.VMEM((N_EXPERTS, 1), F32)],
        compiler_params=_params("arbitrary"),
        name="mix_route",
    )(xa, xb, o, c, gates, wo, wpw, bpw, wout, gffn, wr_hi, wr_lo, br, tri)


def _sc_workers():
    info = plsc.get_sparse_core_info()
    return info.num_cores, info.num_cores * info.num_subcores


def _sc_dispatch(h, dest, n_out):
    t_rows, width = h.shape
    nc, nw = _sc_workers()
    per_w = t_rows // nw
    step = SC_ROWS_PER_STEP
    assert per_w * nw == t_rows and per_w % step == 0
    mesh = plsc.VectorSubcoreMesh(core_axis_name="c", subcore_axis_name="s")

    @functools.partial(
        pl.kernel, mesh=mesh,
        out_type=jax.ShapeDtypeStruct((n_out, width), h.dtype),
        scratch_types=[pltpu.VMEM((step,), I32)] * TOP_K
        + [pltpu.VMEM((step, width), h.dtype), pltpu.SemaphoreType.DMA],
    )
    def scatter_rows(h_hbm, dest_hbm, out_hbm, i0, i1, i2, i3, rows_v, sem):
        del sem
        base = (lax.axis_index("s") * nc + lax.axis_index("c")) * per_w

        @pl.loop(0, per_w // step)
        def _(i):
            off = base + i * step
            pltpu.sync_copy(h_hbm.at[pl.ds(off, step)], rows_v)
            for j, idx_v in enumerate((i0, i1, i2, i3)):
                pltpu.sync_copy(dest_hbm.at[j, pl.ds(off, step)], idx_v)
                pltpu.sync_copy(rows_v, out_hbm.at[idx_v])

    return scatter_rows(h, dest)


def _sc_gather(table, idx):
    n = idx.shape[0]
    width = table.shape[1]
    nc, nw = _sc_workers()
    per_w = n // nw
    step = SC_ROWS_PER_STEP
    assert per_w * nw == n and per_w % step == 0
    mesh = plsc.VectorSubcoreMesh(core_axis_name="c", subcore_axis_name="s")

    @functools.partial(
        pl.kernel, mesh=mesh,
        out_type=jax.ShapeDtypeStruct((n, width), table.dtype),
        scratch_types=[pltpu.VMEM((step,), I32), pltpu.VMEM((step, width), table.dtype),
                       pltpu.SemaphoreType.DMA],
    )
    def gather_rows(table_hbm, idx_hbm, out_hbm, idx_v, rows_v, sem):
        base = (lax.axis_index("s") * nc + lax.axis_index("c")) * per_w

        @pl.loop(0, per_w // step)
        def _(i):
            off = base + i * step
            pltpu.sync_copy(idx_hbm.at[pl.ds(off, step)], idx_v)
            pltpu.async_copy(table_hbm.at[idx_v], rows_v, sem).wait()
            pltpu.sync_copy(rows_v, out_hbm.at[pl.ds(off, step)])

    return gather_rows(table, idx)


def _expert_body(bexp_ref, nused_ref, valid_ref, xs_ref, wgu_ref, bgu_ref, wd_ref, bd_ref, o_ref,
                 wgu_bf, wd_bf):
    n = pl.program_id(0)
    active = n < nused_ref[0]
    new_expert = (n == 0) | (bexp_ref[n] != bexp_ref[jnp.maximum(n - 1, 0)])

    @pl.when(active & new_expert)
    def _():
        wgu_bf[...] = wgu_ref[...].astype(BF16)
        wd_bf[...] = wd_ref[...].astype(BF16)

    @pl.when(active)
    def _():
        row = lax.broadcasted_iota(I32, (xs_ref.shape[0], 1), 0)
        x = _unpack_halves(jnp.where(row < valid_ref[n], xs_ref[...], 0), BF16)
        gu = jnp.dot(x, wgu_bf[...], preferred_element_type=F32) + bgu_ref[...]
        g = jnp.minimum(gu[:, 0:D_FF], SWIGLU_LIMIT)
        u = jnp.clip(gu[:, D_FF:2 * D_FF], -SWIGLU_LIMIT, SWIGLU_LIMIT)
        act = g * jax.nn.sigmoid(SWIGLU_ALPHA * g) * (u + 1.0)
        out = jnp.dot(act.astype(BF16), wd_bf[...], preferred_element_type=F32) + bd_ref[...]
        o_ref[...] = _pack_halves(out.astype(BF16))

    @pl.when(jnp.logical_not(active))
    def _():
        o_ref[...] = jnp.zeros_like(o_ref)


def _experts(block_exp, n_used, block_valid, xs, wgu, bgu, wd, bd, bm):
    P = xs.shape[0]
    nblk = P // bm
    xmap = lambda n, be, nu, bv: (jnp.minimum(n, nu[0] - 1), 0)
    emap = lambda n, be, nu, bv: (be[n], 0, 0)
    grid_spec = pltpu.PrefetchScalarGridSpec(
        num_scalar_prefetch=3,
        grid=(nblk,),
        in_specs=[
            pl.BlockSpec((bm, PACKED), xmap),
            pl.BlockSpec((None, D_MODEL, 2 * D_FF), emap),
            pl.BlockSpec((None, 1, 2 * D_FF), emap),
            pl.BlockSpec((None, D_FF, D_MODEL), emap),
            pl.BlockSpec((None, 1, D_MODEL), emap),
        ],
        out_specs=pl.BlockSpec((bm, PACKED), lambda n, be, nu, bv: (n, 0)),
        scratch_shapes=[pltpu.VMEM((D_MODEL, 2 * D_FF), BF16), pltpu.VMEM((D_FF, D_MODEL), BF16)],
    )
    return pl.pallas_call(
        _expert_body,
        grid_spec=grid_spec,
        out_shape=jax.ShapeDtypeStruct((P, PACKED), I32),
        compiler_params=_params("arbitrary"),
        name="experts",
    )(block_exp, n_used, block_valid, xs, wgu, bgu, wd, bd)


def _combine_body(x1_ref, g0_ref, g1_ref, g2_ref, g3_ref, gate_ref, gfin_ref, y_ref):
    gate = gate_ref[...]
    y = x1_ref[...]
    for j, g_ref in enumerate((g0_ref, g1_ref, g2_ref, g3_ref)):
        y = y + gate[:, j:j + 1] * _unpack_halves(g_ref[...], F32)
    ms = jnp.mean(y * y, axis=-1, keepdims=True)
    y_ref[...] = y * lax.rsqrt(ms + NORM_EPS) * gfin_ref[...]


def _combine(x1, gathered, gate_tok, gfin, row0, rows, tf):
    nt = rows // tf
    t0 = row0 // tf
    row = lambda i: (t0 + i, 0)
    choice = lambda j: pl.BlockSpec((tf, PACKED), lambda i: (j * nt + i, 0))
    return pl.pallas_call(
        _combine_body,
        grid=(nt,),
        in_specs=[pl.BlockSpec((tf, D_MODEL), row)] + [choice(j) for j in range(TOP_K)] + [
            pl.BlockSpec((tf, TOP_K), row),
            pl.BlockSpec((1, D_MODEL), lambda i: (0, 0)),
        ],
        out_specs=pl.BlockSpec((tf, D_MODEL), lambda i: (i, 0)),
        out_shape=jax.ShapeDtypeStruct((rows, D_MODEL), F32),
        compiler_params=_params("parallel"),
        name="combine",
    )(x1, gathered, gathered, gathered, gathered, gate_tok, gfin)


def _permute_in_proj(w):
    lead = w.shape[:-1]
    half = HEAD_DIM // 2
    q = w[..., :Q_COLS].reshape(*lead, N_Q_HEADS // 2, 2, 2, half)
    q = jnp.swapaxes(q, -3, -2).reshape(*lead, Q_COLS)
    k = w[..., Q_COLS:Q_COLS + KV_COLS].reshape(*lead, N_KV_HEADS, 2, 1, half)
    k = jnp.broadcast_to(k, (*lead, N_KV_HEADS, 2, 2, half)).reshape(*lead, KV_DUP_COLS)
    v = w[..., Q_COLS + KV_COLS:Q_COLS + 2 * KV_COLS].reshape(*lead, N_KV_HEADS, 1, HEAD_DIM)
    v = jnp.broadcast_to(v, (*lead, N_KV_HEADS, 2, HEAD_DIM)).reshape(*lead, KV_DUP_COLS)
    return jnp.concatenate([q, k, w[..., Q_COLS + 2 * KV_COLS:]], axis=-1), v


def _rope_tables(n_pos):
    half = HEAD_DIM // 2
    inv_freq = 1.0 / (ROPE_THETA ** (jnp.arange(half, dtype=F32) * (2.0 / HEAD_DIM)))
    ang = jnp.arange(n_pos, dtype=F32)[:, None] * inv_freq[None, :]
    cos = jnp.tile(jnp.cos(ang), (1, LANES // half))
    sin = jnp.tile(jnp.sin(ang), (1, LANES // half))
    sign = jnp.where(jnp.arange(LANES) < LANES // 2, -1.0, 1.0).astype(F32)
    return cos, sin * sign[None, :]


def _tiles(geo):
    unit = min(geo.len_a, geo.len_b)
    tile = min(512, unit)
    return dict(tm=tile, tc=tile, tf=tile, bm=512)


def kernel(x_prompt, x_sample, norm_mix_g, w_in, b_in, attn_sink, w_o_attn, conv_dw_w, conv_dw_b,
           conv_ln_g, conv_ln_b, w_pw2, b_pw2, w_out, norm_ffn_g, w_router, b_router, w_gu, b_gu,
           w_down, b_down, norm_final_g):
    assert w_in.shape[0] == 1, "single trunk layer"
    geo = Geo(x_prompt.shape[0], x_prompt.shape[1], x_sample.shape[0], x_sample.shape[1])
    T = geo.total
    ts = _tiles(geo)
    xa = x_prompt.reshape(-1, D_MODEL)
    xb = x_sample.reshape(-1, D_MODEL)

    w_perm, w_v = _permute_in_proj(w_in[0])
    b_perm, b_v = _permute_in_proj(b_in)
    cos_t, sin_t = _rope_tables(max(geo.len_a, geo.len_b))

    q, kd, vt, glu, gates = _in_proj(xa, xb, norm_mix_g, w_perm.astype(BF16), b_perm,
                                     w_v.T.astype(BF16), b_v.T, cos_t, sin_t, geo, ts["tm"])
    attn = _attention(q, kd, vt, attn_sink[0], geo)
    w_rep = jnp.repeat(conv_dw_w[0], SUBLANES, axis=0)
    conv = _conv_branch(glu, w_rep, conv_dw_b, conv_ln_g, conv_ln_b, geo, ts["tc"])

    wr_t = w_router[0].T
    wr_hi = wr_t.astype(BF16)
    wr_lo = (wr_t - wr_hi.astype(F32)).astype(BF16)
    tri = jnp.triu(jnp.ones((ts["tm"], ts["tm"]), BF16), 1)
    x1, h2, idx, gate_t, rank, counts = _mix_route(
        xa, xb, attn, conv, gates, w_o_attn[0].astype(BF16), w_pw2[0].astype(BF16), b_pw2,
        w_out[0].astype(BF16), norm_ffn_g, wr_hi, wr_lo, b_router[0][:, None], tri, geo, ts["tm"])

    bm = ts["bm"]
    n_blocks = (T * TOP_K) // bm + N_EXPERTS
    cnt = counts[:, 0].astype(I32)
    padded = ((cnt + bm - 1) // bm) * bm
    pad_end = jnp.cumsum(padded)
    pad_start = pad_end - padded
    expert_ids = jnp.arange(N_EXPERTS, dtype=I32)
    dest = rank + jnp.sum(jnp.where(idx[None] == expert_ids[:, None, None],
                                    pad_start[:, None, None], 0), axis=0)
    block_start = jnp.arange(n_blocks, dtype=I32) * bm
    block_exp = jnp.minimum(jnp.sum((pad_end[None, :] <= block_start[:, None]).astype(I32), axis=1),
                            N_EXPERTS - 1)
    n_used = (pad_end[-1:] // bm).astype(I32)
    seg_end = jnp.sum(jnp.where(block_exp[:, None] == expert_ids[None, :],
                                (pad_start + cnt)[None, :], 0), axis=1)
    block_valid = jnp.clip(seg_end - block_start, 0, bm).astype(I32)

    xs = _sc_dispatch(h2, dest, n_blocks * bm)
    ys = _experts(block_exp, n_used, block_valid, xs, w_gu[0], b_gu[0][:, None, :],
                  w_down[0], b_down[0][:, None, :], bm)
    gate_tok = gate_t.T
    outs = []
    for row0, rows in ((0, geo.rows_a), (geo.rows_a, T - geo.rows_a)):
        gathered = _sc_gather(ys, dest[:, row0:row0 + rows].reshape(-1))
        outs.append(_combine(x1, gathered, gate_tok, norm_final_g[None, :], row0, rows, ts["tf"]))
    return (outs[0].reshape(x_prompt.shape), outs[1].reshape(x_sample.shape))
```

```python
import functools
import math
from typing import NamedTuple

import jax
import jax.numpy as jnp
from jax import lax
from jax.experimental import pallas as pl
from jax.experimental.pallas import tpu as pltpu
from jax.experimental.pallas import tpu_sc as plsc

F32 = jnp.float32
BF16 = jnp.bfloat16
I32 = jnp.int32

D_MODEL = 1024
HEAD_DIM = 64
N_Q_HEADS = 16
N_KV_HEADS = 4
WINDOW = 128
ATTN_BLOCK = 128
ROPE_THETA = 10000.0
CONV_WIDTH = 31
CONV_PAD = CONV_WIDTH // 2
N_EXPERTS = 32
TOP_K = 4
D_FF = D_MODEL
SWIGLU_LIMIT = 7.0
SWIGLU_ALPHA = 1.702
NORM_EPS = 1e-5
NEG_INF = -1e30

Q_COLS = N_Q_HEADS * HEAD_DIM
KV_COLS = N_KV_HEADS * HEAD_DIM
LANES = 128
SUBLANES = 8
KV_DUP_COLS = N_KV_HEADS * LANES
HALO = 16
VMEM_LIMIT = 56 * 1024 * 1024
SC_ROWS_PER_STEP = 128


class Geo(NamedTuple):
    n_a: int
    len_a: int
    n_b: int
    len_b: int

    @property
    def rows_a(self):
        return self.n_a * self.len_a

    @property
    def total(self):
        return self.rows_a + self.n_b * self.len_b


def _seq_bounds(geo, r):
    in_a = r < geo.rows_a
    start_a = (r // geo.len_a) * geo.len_a
    start_b = geo.rows_a + ((r - geo.rows_a) // geo.len_b) * geo.len_b
    start = jnp.where(in_a, start_a, start_b)
    end = start + jnp.where(in_a, geo.len_a, geo.len_b)
    return start, end


def _params(*sem):
    return pltpu.CompilerParams(dimension_semantics=sem, vmem_limit_bytes=VMEM_LIMIT)


PACKED = D_MODEL // 2


def _pack_halves(x_bf16):
    hi = lax.bitcast_convert_type(x_bf16[:, :PACKED].astype(F32), I32)
    lo = lax.bitcast_convert_type(x_bf16[:, PACKED:].astype(F32), I32)
    return hi | lax.shift_right_logical(lo, 16)


def _unpack_halves(words, dtype):
    hi = lax.bitcast_convert_type(words & jnp.int32(-65536), F32)
    lo = lax.bitcast_convert_type(lax.shift_left(words, 16), F32)
    return jnp.concatenate([hi.astype(dtype), lo.astype(dtype)], axis=1)


def _two_group_specs(geo, tile, width):
    n_a = geo.rows_a // tile
    return n_a, [pl.BlockSpec((tile, width), lambda i: (jnp.minimum(i, n_a - 1), 0)),
                 pl.BlockSpec((tile, width), lambda i: (jnp.maximum(i - n_a, 0), 0))]


_C_Q = 0
_C_K = _C_Q + Q_COLS
_C_GA = _C_K + KV_DUP_COLS
_C_GG = _C_GA + D_MODEL
_C_GATE = _C_GG + D_MODEL
_N_IN = _C_GATE + 2 * D_MODEL
_PROJ_CHUNK = 512
N_SLAB = D_MODEL // LANES
_NT = (((1,), (1,)), ((), ()))
_LOG2E = math.log2(math.e)
_Q_SCALE = HEAD_DIM ** -0.5 * _LOG2E


def _in_proj_body(n_a, xa_ref, xb_ref, g_ref, w_ref, b_ref, wvt_ref, bvt_ref, cos_ref, sin_ref,
                  q_ref, kd_ref, vt_ref, glu_ref, gate_ref):
    x = jnp.where(pl.program_id(0) < n_a, xa_ref[...], xb_ref[...])
    ms = jnp.mean(x * x, axis=-1, keepdims=True)
    h = (x * lax.rsqrt(ms + NORM_EPS) * g_ref[...]).astype(BF16)
    cos = cos_ref[...]
    sin = sin_ref[...]

    def proj(c0):
        return (jnp.dot(h, w_ref[:, c0:c0 + _PROJ_CHUNK], preferred_element_type=F32)
                + b_ref[:, c0:c0 + _PROJ_CHUNK])

    def rope_store(z, out_ref, o0, scale):
        for c in range(_PROJ_CHUNK // LANES):
            zc = z[:, c * LANES:(c + 1) * LANES]
            r = zc * cos + pltpu.roll(zc, LANES // 2, 1) * sin
            if scale != 1.0:
                r = r * scale
            out_ref[:, o0 + c * LANES:o0 + (c + 1) * LANES] = r.astype(out_ref.dtype)

    for c in range(Q_COLS // _PROJ_CHUNK):
        rope_store(proj(_C_Q + c * _PROJ_CHUNK), q_ref, c * _PROJ_CHUNK, _Q_SCALE)
    for c in range(KV_DUP_COLS // _PROJ_CHUNK):
        rope_store(proj(_C_K + c * _PROJ_CHUNK), kd_ref, c * _PROJ_CHUNK, 1.0)
    vt = lax.dot_general(wvt_ref[...], h, _NT, preferred_element_type=F32) + bvt_ref[...]
    vt_ref[...] = vt.astype(BF16)
    per = _PROJ_CHUNK // LANES
    for c in range(D_MODEL // _PROJ_CHUNK):
        a = proj(_C_GA + c * _PROJ_CHUNK)
        g = proj(_C_GG + c * _PROJ_CHUNK)
        glu = a * jax.nn.sigmoid(g)
        for s in range(per):
            glu_ref[c * per + s] = glu[:, s * LANES:(s + 1) * LANES]
    for c in range(2 * D_MODEL // _PROJ_CHUNK):
        gate_ref[:, c * _PROJ_CHUNK:(c + 1) * _PROJ_CHUNK] = jax.nn.sigmoid(
            proj(_C_GATE + c * _PROJ_CHUNK)).astype(BF16)


def _in_proj(xa, xb, g_mix, w_perm, b_perm, wvt, bvt, cos_t, sin_t, geo, tm):
    T = geo.total

    def pos_map(i):
        r0 = i * tm
        start, _ = _seq_bounds(geo, r0)
        return ((r0 - start) // tm, 0)

    const = lambda i: (0, 0)
    row = lambda i: (i, 0)
    n_a, x_specs = _two_group_specs(geo, tm, D_MODEL)
    return pl.pallas_call(
        functools.partial(_in_proj_body, n_a),
        grid=(T // tm,),
        in_specs=x_specs + [
            pl.BlockSpec((1, D_MODEL), const),
            pl.BlockSpec((D_MODEL, _N_IN), const, pipeline_mode=pl.Buffered(1)),
            pl.BlockSpec((1, _N_IN), const),
            pl.BlockSpec((KV_COLS, D_MODEL), const),
            pl.BlockSpec((KV_COLS, 1), const),
            pl.BlockSpec((tm, LANES), pos_map),
            pl.BlockSpec((tm, LANES), pos_map),
        ],
        out_specs=[
            pl.BlockSpec((tm, Q_COLS), row),
            pl.BlockSpec((tm, KV_DUP_COLS), row),
            pl.BlockSpec((KV_COLS, tm), lambda i: (0, i)),
            pl.BlockSpec((N_SLAB, tm, LANES), lambda i: (0, i, 0)),
            pl.BlockSpec((tm, 2 * D_MODEL), row),
        ],
        out_shape=[
            jax.ShapeDtypeStruct((T, Q_COLS), BF16),
            jax.ShapeDtypeStruct((T, KV_DUP_COLS), BF16),
            jax.ShapeDtypeStruct((KV_COLS, T), BF16),
            jax.ShapeDtypeStruct((N_SLAB, T, LANES), F32),
            jax.ShapeDtypeStruct((T, 2 * D_MODEL), BF16),
        ],
        compiler_params=_params("parallel"),
        name="in_proj",
    )(xa, xb, g_mix, w_perm, b_perm, wvt, bvt, cos_t, sin_t)


_ONES_ROWS = 16
_ATTN_STEP_BLOCKS = 4


def _attn_body(geo, sink_ref, q_ref, kp_ref, kc_ref, kn_ref, vp_ref, vc_ref, vn_ref, o_ref):
    group = N_Q_HEADS // N_KV_HEADS
    nq = group * ATTN_BLOCK
    b = ATTN_BLOCK
    hd = HEAD_DIM
    nsub = _ATTN_STEP_BLOCKS

    key = lax.broadcasted_iota(I32, (b, nq), 0)
    qry = lax.broadcasted_iota(I32, (b, nq), 1) % b
    head_of_col = lax.broadcasted_iota(I32, (1, nq), 1) // b
    lane = lax.broadcasted_iota(I32, (b, LANES), 1)
    even_head = (lane % hd) < (hd // 2)
    ones = jnp.ones((_ONES_ROWS, 3 * b), BF16)

    for s in range(nsub):
        rows = slice(s * b, (s + 1) * b)
        r0 = (pl.program_id(0) * nsub + s) * b
        start, end = _seq_bounds(geo, r0)
        bias_prev = jnp.where((key >= qry) & (r0 > start), 0.0, NEG_INF)
        bias_next = jnp.where((key <= qry) & (r0 + b < end), 0.0, NEG_INF)

        for g in range(N_KV_HEADS):
            ls = slice(g * LANES, (g + 1) * LANES)
            k_prev = kp_ref[:, ls] if s == 0 else kc_ref[(s - 1) * b:s * b, ls]
            k_next = kn_ref[:, ls] if s == nsub - 1 else kc_ref[(s + 1) * b:(s + 2) * b, ls]
            vs = slice(g * hd, (g + 1) * hd)
            v_prev = vp_ref[vs, :] if s == 0 else vc_ref[vs, (s - 1) * b:s * b]
            v_next = vn_ref[vs, :] if s == nsub - 1 else vc_ref[vs, (s + 1) * b:(s + 2) * b]

            qa = q_ref[rows, (2 * g) * LANES:(2 * g + 1) * LANES]
            qb = q_ref[rows, (2 * g + 1) * LANES:(2 * g + 2) * LANES]
            zero = jnp.zeros_like(qa)
            q4 = jnp.concatenate([jnp.where(even_head, qa, zero), jnp.where(even_head, zero, qa),
                                  jnp.where(even_head, qb, zero), jnp.where(even_head, zero, qb)],
                                 axis=0)
            k = jnp.concatenate([k_prev, kc_ref[rows, ls], k_next], axis=0)
            st = lax.dot_general(k, q4, _NT, preferred_element_type=F32)
            s_prev = st[0:b] + bias_prev
            s_cur = st[b:2 * b]
            s_next = st[2 * b:3 * b] + bias_next
            sink = jnp.full((1, nq), sink_ref[group * g] * _LOG2E, F32)
            for h in range(1, group):
                sink = jnp.where(head_of_col == h, sink_ref[group * g + h] * _LOG2E, sink)
            m = jnp.maximum(jnp.maximum(jnp.max(s_prev, axis=0, keepdims=True),
                                        jnp.max(s_cur, axis=0, keepdims=True)),
                            jnp.maximum(jnp.max(s_next, axis=0, keepdims=True), sink))
            p = jnp.concatenate([jnp.exp2(s_prev - m).astype(BF16), jnp.exp2(s_cur - m).astype(BF16),
                                 jnp.exp2(s_next - m).astype(BF16)], axis=0)
            vt = jnp.concatenate([v_prev, vc_ref[vs, rows], v_next], axis=1)
            ot = jnp.dot(jnp.concatenate([vt, vt, ones], axis=0), p, preferred_element_type=F32)
            denom = ot[2 * hd:2 * hd + 1] + jnp.exp2(sink - m)
            on = ot[0:2 * hd] * (1.0 / denom)
            pair_a = jnp.concatenate([on[0:hd, 0:b], on[hd:2 * hd, b:2 * b]], axis=0)
            pair_b = jnp.concatenate([on[0:hd, 2 * b:3 * b], on[hd:2 * hd, 3 * b:4 * b]], axis=0)
            o_ref[rows, (2 * g) * LANES:(2 * g + 1) * LANES] = pair_a.T.astype(BF16)
            o_ref[rows, (2 * g + 1) * LANES:(2 * g + 2) * LANES] = pair_b.T.astype(BF16)


def _attention(q, kd, vt, sink, geo):
    T = q.shape[0]
    nsub = _ATTN_STEP_BLOCKS
    nb = T // ATTN_BLOCK
    step_rows = nsub * ATTN_BLOCK
    prev = lambda i: jnp.maximum(i * nsub - 1, 0)
    nxt = lambda i: jnp.minimum((i + 1) * nsub, nb - 1)
    k_edge = lambda m: pl.BlockSpec((ATTN_BLOCK, KV_DUP_COLS), lambda i: (m(i), 0))
    v_edge = lambda m: pl.BlockSpec((KV_COLS, ATTN_BLOCK), lambda i: (0, m(i)))
    return pl.pallas_call(
        functools.partial(_attn_body, geo),
        grid=(T // step_rows,),
        in_specs=[
            pl.BlockSpec(memory_space=pltpu.SMEM),
            pl.BlockSpec((step_rows, Q_COLS), lambda i: (i, 0)),
            k_edge(prev), pl.BlockSpec((step_rows, KV_DUP_COLS), lambda i: (i, 0)), k_edge(nxt),
            v_edge(prev), pl.BlockSpec((KV_COLS, step_rows), lambda i: (0, i)), v_edge(nxt),
        ],
        out_specs=pl.BlockSpec((step_rows, Q_COLS), lambda i: (i, 0)),
        out_shape=jax.ShapeDtypeStruct((T, Q_COLS), BF16),
        compiler_params=_params("parallel"),
        name="window_attn",
    )(sink, q, kd, kd, kd, vt, vt, vt)


_CONV_ROWS = 64
_LN_ROWS = 32


def _conv_body(geo, tc, z_ref, zp_ref, zn_ref, w_ref, dwb_ref, lng_ref, lnb_ref, o_ref,
               buf_ref, y_ref):
    i = pl.program_id(0)
    r0 = i * tc
    start, end = _seq_bounds(geo, r0)
    has_prev = (r0 > start).astype(F32)
    has_next = (r0 + tc < end).astype(F32)
    rows = tc + 2 * HALO
    buf_ref[:, 0:HALO, :] = zp_ref[...] * has_prev
    buf_ref[:, HALO:HALO + tc, :] = z_ref[...]
    buf_ref[:, HALO + tc:rows, :] = zn_ref[...] * has_next

    rep = _CONV_ROWS // SUBLANES

    def conv_chunk(j):
        s0 = pl.multiple_of(j * _CONV_ROWS, _CONV_ROWS)
        for c in range(N_SLAB):
            ls = slice(c * LANES, (c + 1) * LANES)
            acc = jnp.zeros((_CONV_ROWS, LANES), F32)
            for k in range(CONV_WIDTH):
                zt = buf_ref[c, pl.ds(s0 + (HALO - CONV_PAD + k), _CONV_ROWS, stride=1), :]
                wt = w_ref[k * SUBLANES:(k + 1) * SUBLANES, ls]
                acc = acc + zt * jnp.concatenate([wt] * rep, axis=0)
            y_ref[pl.ds(s0, _CONV_ROWS), ls] = acc

    def ln_chunk(j):
        for h in range(_CONV_ROWS // _LN_ROWS):
            s0 = pl.multiple_of(j * _CONV_ROWS + h * _LN_ROWS, _LN_ROWS)
            y = y_ref[pl.ds(s0, _LN_ROWS), :] + dwb_ref[...]
            mu = jnp.mean(y, axis=-1, keepdims=True)
            yc = y - mu
            var = jnp.mean(yc * yc, axis=-1, keepdims=True)
            yn = yc * lax.rsqrt(var + NORM_EPS) * lng_ref[...] + lnb_ref[...]
            o_ref[pl.ds(s0, _LN_ROWS), :] = (yn * jax.nn.sigmoid(yn)).astype(BF16)

    n = tc // _CONV_ROWS
    conv_chunk(0)

    def body(j, carry):
        ln_chunk(j - 1)
        conv_chunk(j)
        return carry

    lax.fori_loop(1, n, body, 0)
    ln_chunk(n - 1)


def _conv_branch(glu, w_rep, dw_b, ln_g, ln_b, geo, tc):
    T = glu.shape[1]
    nh = T // HALO
    per = tc // HALO
    const = lambda i: (0, 0)
    return pl.pallas_call(
        functools.partial(_conv_body, geo, tc),
        grid=(T // tc,),
        in_specs=[
            pl.BlockSpec((N_SLAB, tc, LANES), lambda i: (0, i, 0)),
            pl.BlockSpec((N_SLAB, HALO, LANES), lambda i: (0, jnp.maximum(i * per - 1, 0), 0)),
            pl.BlockSpec((N_SLAB, HALO, LANES), lambda i: (0, jnp.minimum((i + 1) * per, nh - 1), 0)),
            pl.BlockSpec((CONV_WIDTH * SUBLANES, D_MODEL), const),
            pl.BlockSpec((1, D_MODEL), const),
            pl.BlockSpec((1, D_MODEL), const),
            pl.BlockSpec((1, D_MODEL), const),
        ],
        out_specs=pl.BlockSpec((tc, D_MODEL), lambda i: (i, 0)),
        out_shape=jax.ShapeDtypeStruct((T, D_MODEL), BF16),
        scratch_shapes=[pltpu.VMEM((N_SLAB, tc + 2 * HALO, LANES), F32),
                        pltpu.VMEM((tc, D_MODEL), F32)],
        compiler_params=_params("parallel"),
        name="conv_branch",
    )(glu, glu, glu, w_rep, dw_b, ln_g, ln_b)


def _mix_body(n_a, xa_ref, xb_ref, o_ref, c_ref, gate_ref, wo_ref, wpw_ref, bpw_ref, wout_ref,
              gffn_ref, wrh_ref, wrl_ref, br_ref, tri_ref,
              x1_ref, h2_ref, idx_ref, gt_ref, rank_ref, cnt_ref, carry_ref):
    tm = xa_ref.shape[0]
    x = jnp.where(pl.program_id(0) < n_a, xa_ref[...], xb_ref[...])

    @pl.when(pl.program_id(0) == 0)
    def _():
        carry_ref[...] = jnp.zeros_like(carry_ref)

    attn = jnp.dot(o_ref[...], wo_ref[...], preferred_element_type=F32)
    conv = jnp.dot(c_ref[...], wpw_ref[...], preferred_element_type=F32) + bpw_ref[...]
    g_attn = gate_ref[:, 0:D_MODEL].astype(F32)
    g_conv = gate_ref[:, D_MODEL:2 * D_MODEL].astype(F32)
    mix = (g_attn * attn + g_conv * conv).astype(BF16)
    x1 = x + jnp.dot(mix, wout_ref[...], preferred_element_type=F32)
    x1_ref[...] = x1
    ms = jnp.mean(x1 * x1, axis=-1, keepdims=True)
    h2 = x1 * lax.rsqrt(ms + NORM_EPS) * gffn_ref[...]
    h_hi = h2.astype(BF16)
    h2_ref[...] = _pack_halves(h_hi)

    h_lo = (h2 - h_hi.astype(F32)).astype(BF16)
    logits = (lax.dot_general(wrh_ref[...], h_hi, _NT, preferred_element_type=F32)
              + lax.dot_general(wrh_ref[...], h_lo, _NT, preferred_element_type=F32)
              + lax.dot_general(wrl_ref[...], h_hi, _NT, preferred_element_type=F32)
              + br_ref[...])

    eidx = lax.broadcasted_iota(I32, (N_EXPERTS, tm), 0)
    vals = logits
    picked, top_vals, top_idx = [], [], []
    for _ in range(TOP_K):
        m = jnp.max(vals, axis=0, keepdims=True)
        idx = jnp.min(jnp.where(vals == m, eidx, N_EXPERTS), axis=0, keepdims=True)
        sel = eidx == idx
        vals = jnp.where(sel, -jnp.inf, vals)
        picked.append(sel)
        top_vals.append(m)
        top_idx.append(idx)

    exps = [jnp.exp(v - top_vals[0]) for v in top_vals]
    tot = exps[0] + exps[1] + exps[2] + exps[3]
    onehot = (picked[0] | picked[1] | picked[2] | picked[3])
    prefix = jnp.dot(onehot.astype(BF16), tri_ref[...], preferred_element_type=F32) + carry_ref[...]
    for j in range(TOP_K):
        idx_ref[j:j + 1, :] = top_idx[j]
        gt_ref[j:j + 1, :] = exps[j] / tot
        rank_ref[j:j + 1, :] = jnp.sum(jnp.where(picked[j], prefix, 0.0), axis=0,
                                       keepdims=True).astype(I32)
    carry_ref[...] = carry_ref[...] + jnp.sum(onehot.astype(F32), axis=1, keepdims=True)
    cnt_ref[...] = jnp.broadcast_to(carry_ref[...], cnt_ref.shape)


def _mix_route(xa, xb, o, c, gates, wo, wpw, bpw, wout, gffn, wr_hi, wr_lo, br, tri, geo, tm):
    T = geo.total
    row = lambda i: (i, 0)
    col = lambda i: (0, i)
    const = lambda i: (0, 0)
    wspec = lambda shape: pl.BlockSpec(shape, const, pipeline_mode=pl.Buffered(1))
    n_a, x_specs = _two_group_specs(geo, tm, D_MODEL)
    return pl.pallas_call(
        functools.partial(_mix_body, n_a),
        grid=(T // tm,),
        in_specs=x_specs + [
            pl.BlockSpec((tm, Q_COLS), row),
            pl.BlockSpec((tm, D_MODEL), row),
            pl.BlockSpec((tm, 2 * D_MODEL), row),
            wspec((Q_COLS, D_MODEL)),
            wspec((D_MODEL, D_MODEL)),
            pl.BlockSpec((1, D_MODEL), const),
            wspec((D_MODEL, D_MODEL)),
            pl.BlockSpec((1, D_MODEL), const),
            pl.BlockSpec((N_EXPERTS, D_MODEL), const),
            pl.BlockSpec((N_EXPERTS, D_MODEL), const),
            pl.BlockSpec((N_EXPERTS, 1), const),
            pl.BlockSpec((tm, tm), const),
        ],
        out_specs=[
            pl.BlockSpec((tm, D_MODEL), row),
            pl.BlockSpec((tm, PACKED), row),
            pl.BlockSpec((TOP_K, tm), col),
            pl.BlockSpec((TOP_K, tm), col),
            pl.BlockSpec((TOP_K, tm), col),
            pl.BlockSpec((N_EXPERTS, LANES), const),
        ],
        out_shape=[
            jax.ShapeDtypeStruct((T, D_MODEL), F32),
            jax.ShapeDtypeStruct((T, PACKED), I32),
            jax.ShapeDtypeStruct((TOP_K, T), I32),
            jax.ShapeDtypeStruct((TOP_K, T), F32),
            jax.ShapeDtypeStruct((TOP_K, T), I32),
            jax.ShapeDtypeStruct((N_EXPERTS, LANES), F32),
        ],
        scratch_shapes=[pltpu.VMEM((N_EXPERTS, 1), F32)],
        compiler_params=_params("arbitrary"),
        name="mix_route",
    )(xa, xb, o, c, gates, wo, wpw, bpw, wout, gffn, wr_hi, wr_lo, br, tri)


def _sc_workers():
    info = plsc.get_sparse_core_info()
    return info.num_cores, info.num_cores * info.num_subcores


def _sc_dispatch(h, dest, n_out):
    t_rows, width = h.shape
    nc, nw = _sc_workers()
    per_w = t_rows // nw
    step = SC_ROWS_PER_STEP
    assert per_w * nw == t_rows and per_w % step == 0
    mesh = plsc.VectorSubcoreMesh(core_axis_name="c", subcore_axis_name="s")

    @functools.partial(
        pl.kernel, mesh=mesh,
        out_type=jax.ShapeDtypeStruct((n_out, width), h.dtype),
        scratch_types=[pltpu.VMEM((step,), I32)] * TOP_K + [pltpu.VMEM((step, width), h.dtype)],
    )
    def scatter_rows(h_hbm, dest_hbm, out_hbm, i0, i1, i2, i3, rows_v):
        base = (lax.axis_index("s") * nc + lax.axis_index("c")) * per_w

        @pl.loop(0, per_w // step)
        def _(i):
            off = base + i * step
            pltpu.sync_copy(h_hbm.at[pl.ds(off, step)], rows_v)
            for j, idx_v in enumerate((i0, i1, i2, i3)):
                pltpu.sync_copy(dest_hbm.at[j, pl.ds(off, step)], idx_v)
                pltpu.sync_copy(rows_v, out_hbm.at[idx_v])

    return scatter_rows(h, dest)


def _sc_gather(table, idx):
    n = idx.shape[0]
    width = table.shape[1]
    nc, nw = _sc_workers()
    per_w = n // nw
    step = SC_ROWS_PER_STEP
    assert per_w * nw == n and per_w % step == 0
    mesh = plsc.VectorSubcoreMesh(core_axis_name="c", subcore_axis_name="s")

    @functools.partial(
        pl.kernel, mesh=mesh,
        out_type=jax.ShapeDtypeStruct((n, width), table.dtype),
        scratch_types=[pltpu.VMEM((step,), I32), pltpu.VMEM((step, width), table.dtype),
                       pltpu.SemaphoreType.DMA],
    )
    def gather_rows(table_hbm, idx_hbm, out_hbm, idx_v, rows_v, sem):
        base = (lax.axis_index("s") * nc + lax.axis_index("c")) * per_w

        @pl.loop(0, per_w // step)
        def _(i):
            off = base + i * step
            pltpu.sync_copy(idx_hbm.at[pl.ds(off, step)], idx_v)
            pltpu.async_copy(table_hbm.at[idx_v], rows_v, sem).wait()
            pltpu.sync_copy(rows_v, out_hbm.at[pl.ds(off, step)])

    return gather_rows(table, idx)


def _expert_body(bexp_ref, nused_ref, valid_ref, xs_ref, wgu_ref, bgu_ref, wd_ref, bd_ref, o_ref,
                 wgu_bf, wd_bf):
    n = pl.program_id(0)
    active = n < nused_ref[0]
    new_expert = (n == 0) | (bexp_ref[n] != bexp_ref[jnp.maximum(n - 1, 0)])

    @pl.when(active & new_expert)
    def _():
        wgu_bf[...] = wgu_ref[...].astype(BF16)
        wd_bf[...] = wd_ref[...].astype(BF16)

    @pl.when(active)
    def _():
        row = lax.broadcasted_iota(I32, (xs_ref.shape[0], 1), 0)
        x = _unpack_halves(jnp.where(row < valid_ref[n], xs_ref[...], 0), BF16)
        gu = jnp.dot(x, wgu_bf[...], preferred_element_type=F32) + bgu_ref[...]
        g = jnp.minimum(gu[:, 0:D_FF], SWIGLU_LIMIT)
        u = jnp.clip(gu[:, D_FF:2 * D_FF], -SWIGLU_LIMIT, SWIGLU_LIMIT)
        act = g * jax.nn.sigmoid(SWIGLU_ALPHA * g) * (u + 1.0)
        out = jnp.dot(act.astype(BF16), wd_bf[...], preferred_element_type=F32) + bd_ref[...]
        o_ref[...] = _pack_halves(out.astype(BF16))

    @pl.when(jnp.logical_not(active))
    def _():
        o_ref[...] = jnp.zeros_like(o_ref)


def _experts(block_exp, n_used, block_valid, xs, wgu, bgu, wd, bd, bm):
    P = xs.shape[0]
    nblk = P // bm
    xmap = lambda n, be, nu, bv: (jnp.minimum(n, nu[0] - 1), 0)
    emap = lambda n, be, nu, bv: (be[n], 0, 0)
    grid_spec = pltpu.PrefetchScalarGridSpec(
        num_scalar_prefetch=3,
        grid=(nblk,),
        in_specs=[
            pl.BlockSpec((bm, PACKED), xmap),
            pl.BlockSpec((None, D_MODEL, 2 * D_FF), emap),
            pl.BlockSpec((None, 1, 2 * D_FF), emap),
            pl.BlockSpec((None, D_FF, D_MODEL), emap),
            pl.BlockSpec((None, 1, D_MODEL), emap),
        ],
        out_specs=pl.BlockSpec((bm, PACKED), lambda n, be, nu, bv: (n, 0)),
        scratch_shapes=[pltpu.VMEM((D_MODEL, 2 * D_FF), BF16), pltpu.VMEM((D_FF, D_MODEL), BF16)],
    )
    return pl.pallas_call(
        _expert_body,
        grid_spec=grid_spec,
        out_shape=jax.ShapeDtypeStruct((P, PACKED), I32),
        compiler_params=_params("arbitrary"),
        name="experts",
    )(block_exp, n_used, block_valid, xs, wgu, bgu, wd, bd)


def _combine_body(x1_ref, g0_ref, g1_ref, g2_ref, g3_ref, gate_ref, gfin_ref, y_ref):
    gate = gate_ref[...]
    y = x1_ref[...]
    for j, g_ref in enumerate((g0_ref, g1_ref, g2_ref, g3_ref)):
        y = y + gate[:, j:j + 1] * _unpack_halves(g_ref[...], F32)
    ms = jnp.mean(y * y, axis=-1, keepdims=True)
    y_ref[...] = y * lax.rsqrt(ms + NORM_EPS) * gfin_ref[...]


def _combine(x1, gathered, gate_tok, gfin, row0, rows, tf):
    nt = rows // tf
    t0 = row0 // tf
    row = lambda i: (t0 + i, 0)
    choice = lambda j: pl.BlockSpec((tf, PACKED), lambda i: (j * nt + i, 0))
    return pl.pallas_call(
        _combine_body,
        grid=(nt,),
        in_specs=[pl.BlockSpec((tf, D_MODEL), row)] + [choice(j) for j in range(TOP_K)] + [
            pl.BlockSpec((tf, TOP_K), row),
            pl.BlockSpec((1, D_MODEL), lambda i: (0, 0)),
        ],
        out_specs=pl.BlockSpec((tf, D_MODEL), lambda i: (i, 0)),
        out_shape=jax.ShapeDtypeStruct((rows, D_MODEL), F32),
        compiler_params=_params("parallel"),
        name="combine",
    )(x1, gathered, gathered, gathered, gathered, gate_tok, gfin)


def _permute_in_proj(w):
    lead = w.shape[:-1]
    half = HEAD_DIM // 2
    q = w[..., :Q_COLS].reshape(*lead, N_Q_HEADS // 2, 2, 2, half)
    q = jnp.swapaxes(q, -3, -2).reshape(*lead, Q_COLS)
    k = w[..., Q_COLS:Q_COLS + KV_COLS].reshape(*lead, N_KV_HEADS, 2, 1, half)
    k = jnp.broadcast_to(k, (*lead, N_KV_HEADS, 2, 2, half)).reshape(*lead, KV_DUP_COLS)
    v = w[..., Q_COLS + KV_COLS:Q_COLS + 2 * KV_COLS]
    return jnp.concatenate([q, k, w[..., Q_COLS + 2 * KV_COLS:]], axis=-1), v


def _rope_tables(n_pos):
    half = HEAD_DIM // 2
    inv_freq = 1.0 / (ROPE_THETA ** (jnp.arange(half, dtype=F32) * (2.0 / HEAD_DIM)))
    ang = jnp.arange(n_pos, dtype=F32)[:, None] * inv_freq[None, :]
    cos = jnp.tile(jnp.cos(ang), (1, LANES // half))
    sin = jnp.tile(jnp.sin(ang), (1, LANES // half))
    sign = jnp.where(jnp.arange(LANES) < LANES // 2, -1.0, 1.0).astype(F32)
    return cos, sin * sign[None, :]


def _tiles(geo):
    unit = min(geo.len_a, geo.len_b)
    tile = min(512, unit)
    return dict(tm=tile, tc=tile, tf=tile, bm=512)


def kernel(x_prompt, x_sample, norm_mix_g, w_in, b_in, attn_sink, w_o_attn, conv_dw_w, conv_dw_b,
           conv_ln_g, conv_ln_b, w_pw2, b_pw2, w_out, norm_ffn_g, w_router, b_router, w_gu, b_gu,
           w_down, b_down, norm_final_g):
    assert w_in.shape[0] == 1, "single trunk layer"
    geo = Geo(x_prompt.shape[0], x_prompt.shape[1], x_sample.shape[0], x_sample.shape[1])
    T = geo.total
    ts = _tiles(geo)
    xa = x_prompt.reshape(-1, D_MODEL)
    xb = x_sample.reshape(-1, D_MODEL)

    w_perm, w_v = _permute_in_proj(w_in[0])
    b_perm, b_v = _permute_in_proj(b_in)
    cos_t, sin_t = _rope_tables(max(geo.len_a, geo.len_b))

    q, kd, vt, glu, gates = _in_proj(xa, xb, norm_mix_g, w_perm.astype(BF16), b_perm,
                                     w_v.T.astype(BF16), b_v.T, cos_t, sin_t, geo, ts["tm"])
    attn = _attention(q, kd, vt, attn_sink[0], geo)
    w_rep = jnp.repeat(conv_dw_w[0], SUBLANES, axis=0)
    conv = _conv_branch(glu, w_rep, conv_dw_b, conv_ln_g, conv_ln_b, geo, ts["tc"])

    wr_t = w_router[0].T
    wr_hi = wr_t.astype(BF16)
    wr_lo = (wr_t - wr_hi.astype(F32)).astype(BF16)
    tri = jnp.triu(jnp.ones((ts["tm"], ts["tm"]), BF16), 1)
    x1, h2, idx, gate_t, rank, counts = _mix_route(
        xa, xb, attn, conv, gates, w_o_attn[0].astype(BF16), w_pw2[0].astype(BF16), b_pw2,
        w_out[0].astype(BF16), norm_ffn_g, wr_hi, wr_lo, b_router[0][:, None], tri, geo, ts["tm"])

    bm = ts["bm"]
    n_blocks = (T * TOP_K) // bm + N_EXPERTS
    cnt = counts[:, 0].astype(I32)
    padded = ((cnt + bm - 1) // bm) * bm
    pad_end = jnp.cumsum(padded)
    pad_start = pad_end - padded
    expert_ids = jnp.arange(N_EXPERTS, dtype=I32)
    dest = rank + jnp.sum(jnp.where(idx[None] == expert_ids[:, None, None],
                                    pad_start[:, None, None], 0), axis=0)
    block_start = jnp.arange(n_blocks, dtype=I32) * bm
    block_exp = jnp.minimum(jnp.sum((pad_end[None, :] <= block_start[:, None]).astype(I32), axis=1),
                            N_EXPERTS - 1)
    n_used = (pad_end[-1:] // bm).astype(I32)
    seg_end = jnp.sum(jnp.where(block_exp[:, None] == expert_ids[None, :],
                                (pad_start + cnt)[None, :], 0), axis=1)
    block_valid = jnp.clip(seg_end - block_start, 0, bm).astype(I32)

    xs = _sc_dispatch(h2, dest, n_blocks * bm)
    ys = _experts(block_exp, n_used, block_valid, xs, w_gu[0], b_gu[0][:, None, :],
                  w_down[0], b_down[0][:, None, :], bm)
    gate_tok = gate_t.T
    outs = []
    for row0, rows in ((0, geo.rows_a), (geo.rows_a, T - geo.rows_a)):
        gathered = _sc_gather(ys, dest[:, row0:row0 + rows].reshape(-1))
        outs.append(_combine(x1, gathered, gate_tok, norm_final_g[None, :], row0, rows, ts["tf"]))
    return (outs[0].reshape(x_prompt.shape), outs[1].reshape(x_sample.shape))
```

```python
import functools
import math
from typing import NamedTuple

import jax
import jax.numpy as jnp
from jax import lax
from jax.experimental import pallas as pl
from jax.experimental.pallas import tpu as pltpu
from jax.experimental.pallas import tpu_sc as plsc

F32 = jnp.float32
BF16 = jnp.bfloat16
I32 = jnp.int32

D_MODEL = 1024
HEAD_DIM = 64
N_Q_HEADS = 16
N_KV_HEADS = 4
WINDOW = 128
ATTN_BLOCK = 128
ROPE_THETA = 10000.0
CONV_WIDTH = 31
CONV_PAD = CONV_WIDTH // 2
N_EXPERTS = 32
TOP_K = 4
D_FF = D_MODEL
SWIGLU_LIMIT = 7.0
SWIGLU_ALPHA = 1.702
NORM_EPS = 1e-5
NEG_INF = -1e30

Q_COLS = N_Q_HEADS * HEAD_DIM
KV_COLS = N_KV_HEADS * HEAD_DIM
LANES = 128
SUBLANES = 8
KV_DUP_COLS = N_KV_HEADS * LANES
HALO = 16
VMEM_LIMIT = 56 * 1024 * 1024
SC_ROWS_PER_STEP = 128


class Geo(NamedTuple):
    n_a: int
    len_a: int
    n_b: int
    len_b: int

    @property
    def rows_a(self):
        return self.n_a * self.len_a

    @property
    def total(self):
        return self.rows_a + self.n_b * self.len_b


def _seq_bounds(geo, r):
    in_a = r < geo.rows_a
    start_a = (r // geo.len_a) * geo.len_a
    start_b = geo.rows_a + ((r - geo.rows_a) // geo.len_b) * geo.len_b
    start = jnp.where(in_a, start_a, start_b)
    end = start + jnp.where(in_a, geo.len_a, geo.len_b)
    return start, end


def _params(*sem):
    return pltpu.CompilerParams(dimension_semantics=sem, vmem_limit_bytes=VMEM_LIMIT)


PACKED = D_MODEL // 2


def _pack_halves(x_bf16):
    hi = lax.bitcast_convert_type(x_bf16[:, :PACKED].astype(F32), I32)
    lo = lax.bitcast_convert_type(x_bf16[:, PACKED:].astype(F32), I32)
    return hi | lax.shift_right_logical(lo, 16)


def _unpack_halves(words, dtype):
    hi = lax.bitcast_convert_type(words & jnp.int32(-65536), F32)
    lo = lax.bitcast_convert_type(lax.shift_left(words, 16), F32)
    return jnp.concatenate([hi.astype(dtype), lo.astype(dtype)], axis=1)


def _two_group_specs(geo, tile, width):
    n_a = geo.rows_a // tile
    return n_a, [pl.BlockSpec((tile, width), lambda i: (jnp.minimum(i, n_a - 1), 0)),
                 pl.BlockSpec((tile, width), lambda i: (jnp.maximum(i - n_a, 0), 0))]


_C_Q = 0
_C_K = _C_Q + Q_COLS
_C_GA = _C_K + KV_DUP_COLS
_C_GG = _C_GA + D_MODEL
_C_GATE = _C_GG + D_MODEL
_N_IN = _C_GATE + 2 * D_MODEL
_PROJ_CHUNK = 512
N_SLAB = D_MODEL // LANES
_NT = (((1,), (1,)), ((), ()))
_LOG2E = math.log2(math.e)
_Q_SCALE = HEAD_DIM ** -0.5 * _LOG2E


def _in_proj_body(n_a, xa_ref, xb_ref, g_ref, w_ref, b_ref, wvt_ref, bvt_ref, cos_ref, sin_ref,
                  q_ref, kd_ref, vt_ref, glu_ref, gate_ref):
    x = jnp.where(pl.program_id(0) < n_a, xa_ref[...], xb_ref[...])
    ms = jnp.mean(x * x, axis=-1, keepdims=True)
    h = (x * lax.rsqrt(ms + NORM_EPS) * g_ref[...]).astype(BF16)
    cos = cos_ref[...]
    sin = sin_ref[...]

    def proj(c0):
        return (jnp.dot(h, w_ref[:, c0:c0 + _PROJ_CHUNK], preferred_element_type=F32)
                + b_ref[:, c0:c0 + _PROJ_CHUNK])

    def rope_store(z, out_ref, o0, scale):
        for c in range(_PROJ_CHUNK // LANES):
            zc = z[:, c * LANES:(c + 1) * LANES]
            r = zc * cos + pltpu.roll(zc, LANES // 2, 1) * sin
            if scale != 1.0:
                r = r * scale
            out_ref[:, o0 + c * LANES:o0 + (c + 1) * LANES] = r.astype(out_ref.dtype)

    for c in range(Q_COLS // _PROJ_CHUNK):
        rope_store(proj(_C_Q + c * _PROJ_CHUNK), q_ref, c * _PROJ_CHUNK, _Q_SCALE)
    for c in range(KV_DUP_COLS // _PROJ_CHUNK):
        rope_store(proj(_C_K + c * _PROJ_CHUNK), kd_ref, c * _PROJ_CHUNK, 1.0)
    vt = lax.dot_general(wvt_ref[...], h, _NT, preferred_element_type=F32) + bvt_ref[...]
    vt_ref[...] = vt.astype(BF16)
    per = _PROJ_CHUNK // LANES
    for c in range(D_MODEL // _PROJ_CHUNK):
        a = proj(_C_GA + c * _PROJ_CHUNK)
        g = proj(_C_GG + c * _PROJ_CHUNK)
        glu = a * jax.nn.sigmoid(g)
        for s in range(per):
            glu_ref[c * per + s] = glu[:, s * LANES:(s + 1) * LANES]
    for c in range(2 * D_MODEL // _PROJ_CHUNK):
        gate_ref[:, c * _PROJ_CHUNK:(c + 1) * _PROJ_CHUNK] = jax.nn.sigmoid(
            proj(_C_GATE + c * _PROJ_CHUNK)).astype(BF16)


def _in_proj(xa, xb, g_mix, w_perm, b_perm, wvt, bvt, cos_t, sin_t, geo, tm):
    T = geo.total

    def pos_map(i):
        r0 = i * tm
        start, _ = _seq_bounds(geo, r0)
        return ((r0 - start) // tm, 0)

    const = lambda i: (0, 0)
    row = lambda i: (i, 0)
    n_a, x_specs = _two_group_specs(geo, tm, D_MODEL)
    return pl.pallas_call(
        functools.partial(_in_proj_body, n_a),
        grid=(T // tm,),
        in_specs=x_specs + [
            pl.BlockSpec((1, D_MODEL), const),
            pl.BlockSpec((D_MODEL, _N_IN), const, pipeline_mode=pl.Buffered(1)),
            pl.BlockSpec((1, _N_IN), const),
            pl.BlockSpec((KV_DUP_COLS, D_MODEL), const),
            pl.BlockSpec((KV_DUP_COLS, 1), const),
            pl.BlockSpec((tm, LANES), pos_map),
            pl.BlockSpec((tm, LANES), pos_map),
        ],
        out_specs=[
            pl.BlockSpec((tm, Q_COLS), row),
            pl.BlockSpec((tm, KV_DUP_COLS), row),
            pl.BlockSpec((KV_DUP_COLS, tm), lambda i: (0, i)),
            pl.BlockSpec((N_SLAB, tm, LANES), lambda i: (0, i, 0)),
            pl.BlockSpec((tm, 2 * D_MODEL), row),
        ],
        out_shape=[
            jax.ShapeDtypeStruct((T, Q_COLS), BF16),
            jax.ShapeDtypeStruct((T, KV_DUP_COLS), BF16),
            jax.ShapeDtypeStruct((KV_DUP_COLS, T), BF16),
            jax.ShapeDtypeStruct((N_SLAB, T, LANES), F32),
            jax.ShapeDtypeStruct((T, 2 * D_MODEL), BF16),
        ],
        compiler_params=_params("parallel"),
        name="in_proj",
    )(xa, xb, g_mix, w_perm, b_perm, wvt, bvt, cos_t, sin_t)


_ONES_ROWS = 16
_ATTN_STEP_BLOCKS = 4


def _attn_body(geo, sink_ref, q_ref, kp_ref, kc_ref, kn_ref, vp_ref, vc_ref, vn_ref, o_ref):
    group = N_Q_HEADS // N_KV_HEADS
    nq = group * ATTN_BLOCK
    b = ATTN_BLOCK
    hd = HEAD_DIM
    nsub = _ATTN_STEP_BLOCKS

    key = lax.broadcasted_iota(I32, (b, nq), 0)
    qry = lax.broadcasted_iota(I32, (b, nq), 1) % b
    head_of_col = lax.broadcasted_iota(I32, (1, nq), 1) // b
    lane = lax.broadcasted_iota(I32, (b, LANES), 1)
    even_head = (lane % hd) < (hd // 2)
    ones = jnp.ones((_ONES_ROWS, 3 * b), BF16)

    for s in range(nsub):
        rows = slice(s * b, (s + 1) * b)
        r0 = (pl.program_id(0) * nsub + s) * b
        start, end = _seq_bounds(geo, r0)
        bias_prev = jnp.where((key >= qry) & (r0 > start), 0.0, NEG_INF)
        bias_next = jnp.where((key <= qry) & (r0 + b < end), 0.0, NEG_INF)

        for g in range(N_KV_HEADS):
            ls = slice(g * LANES, (g + 1) * LANES)
            k_prev = kp_ref[:, ls] if s == 0 else kc_ref[(s - 1) * b:s * b, ls]
            k_next = kn_ref[:, ls] if s == nsub - 1 else kc_ref[(s + 1) * b:(s + 2) * b, ls]
            v_prev = vp_ref[ls, :] if s == 0 else vc_ref[ls, (s - 1) * b:s * b]
            v_next = vn_ref[ls, :] if s == nsub - 1 else vc_ref[ls, (s + 1) * b:(s + 2) * b]

            qa = q_ref[rows, (2 * g) * LANES:(2 * g + 1) * LANES]
            qb = q_ref[rows, (2 * g + 1) * LANES:(2 * g + 2) * LANES]
            zero = jnp.zeros_like(qa)
            q4 = jnp.concatenate([jnp.where(even_head, qa, zero), jnp.where(even_head, zero, qa),
                                  jnp.where(even_head, qb, zero), jnp.where(even_head, zero, qb)],
                                 axis=0)
            k = jnp.concatenate([k_prev, kc_ref[rows, ls], k_next], axis=0)
            st = lax.dot_general(k, q4, _NT, preferred_element_type=F32)
            s_prev = st[0:b] + bias_prev
            s_cur = st[b:2 * b]
            s_next = st[2 * b:3 * b] + bias_next
            sink = jnp.full((1, nq), sink_ref[group * g] * _LOG2E, F32)
            for h in range(1, group):
                sink = jnp.where(head_of_col == h, sink_ref[group * g + h] * _LOG2E, sink)
            m = jnp.maximum(jnp.maximum(jnp.max(s_prev, axis=0, keepdims=True),
                                        jnp.max(s_cur, axis=0, keepdims=True)),
                            jnp.maximum(jnp.max(s_next, axis=0, keepdims=True), sink))
            p = jnp.concatenate([jnp.exp2(s_prev - m).astype(BF16), jnp.exp2(s_cur - m).astype(BF16),
                                 jnp.exp2(s_next - m).astype(BF16)], axis=0)
            vt = jnp.concatenate([v_prev, vc_ref[ls, rows], v_next], axis=1)
            ot = jnp.dot(jnp.concatenate([vt, ones], axis=0), p, preferred_element_type=F32)
            denom = ot[2 * hd:2 * hd + 1] + jnp.exp2(sink - m)
            on = ot[0:2 * hd] * (1.0 / denom)
            pair_a = jnp.concatenate([on[0:hd, 0:b], on[hd:2 * hd, b:2 * b]], axis=0)
            pair_b = jnp.concatenate([on[0:hd, 2 * b:3 * b], on[hd:2 * hd, 3 * b:4 * b]], axis=0)
            o_ref[rows, (2 * g) * LANES:(2 * g + 1) * LANES] = pair_a.T.astype(BF16)
            o_ref[rows, (2 * g + 1) * LANES:(2 * g + 2) * LANES] = pair_b.T.astype(BF16)


def _attention(q, kd, vt, sink, geo):
    T = q.shape[0]
    nsub = _ATTN_STEP_BLOCKS
    nb = T // ATTN_BLOCK
    step_rows = nsub * ATTN_BLOCK
    prev = lambda i: jnp.maximum(i * nsub - 1, 0)
    nxt = lambda i: jnp.minimum((i + 1) * nsub, nb - 1)
    k_edge = lambda m: pl.BlockSpec((ATTN_BLOCK, KV_DUP_COLS), lambda i: (m(i), 0))
    v_edge = lambda m: pl.BlockSpec((KV_DUP_COLS, ATTN_BLOCK), lambda i: (0, m(i)))
    return pl.pallas_call(
        functools.partial(_attn_body, geo),
        grid=(T // step_rows,),
        in_specs=[
            pl.BlockSpec(memory_space=pltpu.SMEM),
            pl.BlockSpec((step_rows, Q_COLS), lambda i: (i, 0)),
            k_edge(prev), pl.BlockSpec((step_rows, KV_DUP_COLS), lambda i: (i, 0)), k_edge(nxt),
            v_edge(prev), pl.BlockSpec((KV_DUP_COLS, step_rows), lambda i: (0, i)), v_edge(nxt),
        ],
        out_specs=pl.BlockSpec((step_rows, Q_COLS), lambda i: (i, 0)),
        out_shape=jax.ShapeDtypeStruct((T, Q_COLS), BF16),
        compiler_params=_params("parallel"),
        name="window_attn",
    )(sink, q, kd, kd, kd, vt, vt, vt)


_CONV_ROWS = 64
_LN_ROWS = 32


def _conv_body(geo, tc, z_ref, zp_ref, zn_ref, w_ref, dwb_ref, lng_ref, lnb_ref, o_ref,
               buf_ref, y_ref):
    i = pl.program_id(0)
    r0 = i * tc
    start, end = _seq_bounds(geo, r0)
    has_prev = (r0 > start).astype(F32)
    has_next = (r0 + tc < end).astype(F32)
    rows = tc + 2 * HALO
    buf_ref[:, 0:HALO, :] = zp_ref[...] * has_prev
    buf_ref[:, HALO:HALO + tc, :] = z_ref[...]
    buf_ref[:, HALO + tc:rows, :] = zn_ref[...] * has_next

    rep = _CONV_ROWS // SUBLANES

    def conv_chunk(j):
        s0 = pl.multiple_of(j * _CONV_ROWS, _CONV_ROWS)
        for c in range(N_SLAB):
            ls = slice(c * LANES, (c + 1) * LANES)
            acc = jnp.zeros((_CONV_ROWS, LANES), F32)
            for k in range(CONV_WIDTH):
                zt = buf_ref[c, pl.ds(s0 + (HALO - CONV_PAD + k), _CONV_ROWS, stride=1), :]
                wt = w_ref[k * SUBLANES:(k + 1) * SUBLANES, ls]
                acc = acc + zt * jnp.concatenate([wt] * rep, axis=0)
            y_ref[pl.ds(s0, _CONV_ROWS), ls] = acc

    def ln_chunk(j):
        for h in range(_CONV_ROWS // _LN_ROWS):
            s0 = pl.multiple_of(j * _CONV_ROWS + h * _LN_ROWS, _LN_ROWS)
            y = y_ref[pl.ds(s0, _LN_ROWS), :] + dwb_ref[...]
            mu = jnp.mean(y, axis=-1, keepdims=True)
            yc = y - mu
            var = jnp.mean(yc * yc, axis=-1, keepdims=True)
            yn = yc * lax.rsqrt(var + NORM_EPS) * lng_ref[...] + lnb_ref[...]
            o_ref[pl.ds(s0, _LN_ROWS), :] = (yn * jax.nn.sigmoid(yn)).astype(BF16)

    n = tc // _CONV_ROWS
    conv_chunk(0)

    def body(j, carry):
        ln_chunk(j - 1)
        conv_chunk(j)
        return carry

    lax.fori_loop(1, n, body, 0)
    ln_chunk(n - 1)


def _conv_branch(glu, w_rep, dw_b, ln_g, ln_b, geo, tc):
    T = glu.shape[1]
    nh = T // HALO
    per = tc // HALO
    const = lambda i: (0, 0)
    return pl.pallas_call(
        functools.partial(_conv_body, geo, tc),
        grid=(T // tc,),
        in_specs=[
            pl.BlockSpec((N_SLAB, tc, LANES), lambda i: (0, i, 0)),
            pl.BlockSpec((N_SLAB, HALO, LANES), lambda i: (0, jnp.maximum(i * per - 1, 0), 0)),
            pl.BlockSpec((N_SLAB, HALO, LANES), lambda i: (0, jnp.minimum((i + 1) * per, nh - 1), 0)),
            pl.BlockSpec((CONV_WIDTH * SUBLANES, D_MODEL), const),
            pl.BlockSpec((1, D_MODEL), const),
            pl.BlockSpec((1, D_MODEL), const),
            pl.BlockSpec((1, D_MODEL), const),
        ],
        out_specs=pl.BlockSpec((tc, D_MODEL), lambda i: (i, 0)),
        out_shape=jax.ShapeDtypeStruct((T, D_MODEL), BF16),
        scratch_shapes=[pltpu.VMEM((N_SLAB, tc + 2 * HALO, LANES), F32),
                        pltpu.VMEM((tc, D_MODEL), F32)],
        compiler_params=_params("parallel"),
        name="conv_branch",
    )(glu, glu, glu, w_rep, dw_b, ln_g, ln_b)


def _mix_body(n_a, xa_ref, xb_ref, o_ref, c_ref, gate_ref, wo_ref, wpw_ref, bpw_ref, wout_ref,
              gffn_ref, wrh_ref, wrl_ref, br_ref, tri_ref,
              x1_ref, h2_ref, idx_ref, gt_ref, rank_ref, cnt_ref, carry_ref):
    tm = xa_ref.shape[0]
    x = jnp.where(pl.program_id(0) < n_a, xa_ref[...], xb_ref[...])

    @pl.when(pl.program_id(0) == 0)
    def _():
        carry_ref[...] = jnp.zeros_like(carry_ref)

    attn = jnp.dot(o_ref[...], wo_ref[...], preferred_element_type=F32)
    conv = jnp.dot(c_ref[...], wpw_ref[...], preferred_element_type=F32) + bpw_ref[...]
    g_attn = gate_ref[:, 0:D_MODEL].astype(F32)
    g_conv = gate_ref[:, D_MODEL:2 * D_MODEL].astype(F32)
    mix = (g_attn * attn + g_conv * conv).astype(BF16)
    x1 = x + jnp.dot(mix, wout_ref[...], preferred_element_type=F32)
    x1_ref[...] = x1
    ms = jnp.mean(x1 * x1, axis=-1, keepdims=True)
    h2 = x1 * lax.rsqrt(ms + NORM_EPS) * gffn_ref[...]
    h_hi = h2.astype(BF16)
    h2_ref[...] = _pack_halves(h_hi)

    h_lo = (h2 - h_hi.astype(F32)).astype(BF16)
    logits = (lax.dot_general(wrh_ref[...], h_hi, _NT, preferred_element_type=F32)
              + lax.dot_general(wrh_ref[...], h_lo, _NT, preferred_element_type=F32)
              + lax.dot_general(wrl_ref[...], h_hi, _NT, preferred_element_type=F32)
              + br_ref[...])

    eidx = lax.broadcasted_iota(I32, (N_EXPERTS, tm), 0)
    vals = logits
    picked, top_vals, top_idx = [], [], []
    for _ in range(TOP_K):
        m = jnp.max(vals, axis=0, keepdims=True)
        idx = jnp.min(jnp.where(vals == m, eidx, N_EXPERTS), axis=0, keepdims=True)
        sel = eidx == idx
        vals = jnp.where(sel, -jnp.inf, vals)
        picked.append(sel)
        top_vals.append(m)
        top_idx.append(idx)

    exps = [jnp.exp(v - top_vals[0]) for v in top_vals]
    tot = exps[0] + exps[1] + exps[2] + exps[3]
    onehot = (picked[0] | picked[1] | picked[2] | picked[3])
    prefix = jnp.dot(onehot.astype(BF16), tri_ref[...], preferred_element_type=F32) + carry_ref[...]
    for j in range(TOP_K):
        idx_ref[j:j + 1, :] = top_idx[j]
        gt_ref[j:j + 1, :] = exps[j] / tot
        rank_ref[j:j + 1, :] = jnp.sum(jnp.where(picked[j], prefix, 0.0), axis=0,
                                       keepdims=True).astype(I32)
    carry_ref[...] = carry_ref[...] + jnp.sum(onehot.astype(F32), axis=1, keepdims=True)
    cnt_ref[...] = jnp.broadcast_to(carry_ref[...], cnt_ref.shape)


def _mix_route(xa, xb, o, c, gates, wo, wpw, bpw, wout, gffn, wr_hi, wr_lo, br, tri, geo, tm):
    T = geo.total
    row = lambda i: (i, 0)
    col = lambda i: (0, i)
    const = lambda i: (0, 0)
    wspec = lambda shape: pl.BlockSpec(shape, const, pipeline_mode=pl.Buffered(1))
    n_a, x_specs = _two_group_specs(geo, tm, D_MODEL)
    return pl.pallas_call(
        functools.partial(_mix_body, n_a),
        grid=(T // tm,),
        in_specs=x_specs + [
            pl.BlockSpec((tm, Q_COLS), row),
            pl.BlockSpec((tm, D_MODEL), row),
            pl.BlockSpec((tm, 2 * D_MODEL), row),
            wspec((Q_COLS, D_MODEL)),
            wspec((D_MODEL, D_MODEL)),
            pl.BlockSpec((1, D_MODEL), const),
            wspec((D_MODEL, D_MODEL)),
            pl.BlockSpec((1, D_MODEL), const),
            pl.BlockSpec((N_EXPERTS, D_MODEL), const),
            pl.BlockSpec((N_EXPERTS, D_MODEL), const),
            pl.BlockSpec((N_EXPERTS, 1), const),
            pl.BlockSpec((tm, tm), const),
        ],
        out_specs=[
            pl.BlockSpec((tm, D_MODEL), row),
            pl.BlockSpec((tm, PACKED), row),
            pl.BlockSpec((TOP_K, tm), col),
            pl.BlockSpec((TOP_K, tm), col),
            pl.BlockSpec((TOP_K, tm), col),
            pl.BlockSpec((N_EXPERTS, LANES), const),
        ],
        out_shape=[
            jax.ShapeDtypeStruct((T, D_MODEL), F32),
            jax.ShapeDtypeStruct((T, PACKED), I32),
            jax.ShapeDtypeStruct((TOP_K, T), I32),
            jax.ShapeDtypeStruct((TOP_K, T), F32),
            jax.ShapeDtypeStruct((TOP_K, T), I32),
            jax.ShapeDtypeStruct((N_EXPERTS, LANES), F32),
        ],
        scratch_shapes=[pltpu.VMEM((N_EXPERTS, 1), F32)],
        compiler_params=_params("arbitrary"),
        name="mix_route",
    )(xa, xb, o, c, gates, wo, wpw, bpw, wout, gffn, wr_hi, wr_lo, br, tri)


def _sc_workers():
    info = plsc.get_sparse_core_info()
    return info.num_cores, info.num_cores * info.num_subcores


def _sc_dispatch(h, dest, n_out):
    t_rows, width = h.shape
    nc, nw = _sc_workers()
    per_w = t_rows // nw
    step = SC_ROWS_PER_STEP
    assert per_w * nw == t_rows and per_w % step == 0
    mesh = plsc.VectorSubcoreMesh(core_axis_name="c", subcore_axis_name="s")

    @functools.partial(
        pl.kernel, mesh=mesh,
        out_type=jax.ShapeDtypeStruct((n_out, width), h.dtype),
        scratch_types=[pltpu.VMEM((step,), I32)] * TOP_K + [pltpu.VMEM((step, width), h.dtype)],
    )
    def scatter_rows(h_hbm, dest_hbm, out_hbm, i0, i1, i2, i3, rows_v):
        base = (lax.axis_index("s") * nc + lax.axis_index("c")) * per_w

        @pl.loop(0, per_w // step)
        def _(i):
            off = base + i * step
            pltpu.sync_copy(h_hbm.at[pl.ds(off, step)], rows_v)
            for j, idx_v in enumerate((i0, i1, i2, i3)):
                pltpu.sync_copy(dest_hbm.at[j, pl.ds(off, step)], idx_v)
                pltpu.sync_copy(rows_v, out_hbm.at[idx_v])

    return scatter_rows(h, dest)


def _sc_gather(table, dest, row0, rows):
    width = table.shape[1]
    nc, nw = _sc_workers()
    per_w = rows // nw
    step = SC_ROWS_PER_STEP
    assert per_w * nw == rows and per_w % step == 0
    mesh = plsc.VectorSubcoreMesh(core_axis_name="c", subcore_axis_name="s")

    @functools.partial(
        pl.kernel, mesh=mesh,
        out_type=jax.ShapeDtypeStruct((TOP_K * rows, width), table.dtype),
        scratch_types=[pltpu.VMEM((step,), I32), pltpu.VMEM((step, width), table.dtype),
                       pltpu.SemaphoreType.DMA],
    )
    def gather_rows(table_hbm, dest_hbm, out_hbm, idx_v, rows_v, sem):
        base = (lax.axis_index("s") * nc + lax.axis_index("c")) * per_w

        @pl.loop(0, per_w // step)
        def _(i):
            off = base + i * step
            for j in range(TOP_K):
                pltpu.sync_copy(dest_hbm.at[j, pl.ds(row0 + off, step)], idx_v)
                pltpu.async_copy(table_hbm.at[idx_v], rows_v, sem).wait()
                pltpu.sync_copy(rows_v, out_hbm.at[pl.ds(j * rows + off, step)])

    return gather_rows(table, dest)


def _expert_body(bexp_ref, nused_ref, valid_ref, xs_ref, wgu_ref, bgu_ref, wd_ref, bd_ref, o_ref,
                 wgu_bf, wd_bf):
    n = pl.program_id(0)
    active = n < nused_ref[0]
    new_expert = (n == 0) | (bexp_ref[n] != bexp_ref[jnp.maximum(n - 1, 0)])

    @pl.when(active & new_expert)
    def _():
        wgu_bf[...] = wgu_ref[...].astype(BF16)
        wd_bf[...] = wd_ref[...].astype(BF16)

    @pl.when(active)
    def _():
        row = lax.broadcasted_iota(I32, (xs_ref.shape[0], 1), 0)
        x = _unpack_halves(jnp.where(row < valid_ref[n], xs_ref[...], 0), BF16)
        gu = jnp.dot(x, wgu_bf[...], preferred_element_type=F32) + bgu_ref[...]
        g = jnp.minimum(gu[:, 0:D_FF], SWIGLU_LIMIT)
        u = jnp.clip(gu[:, D_FF:2 * D_FF], -SWIGLU_LIMIT, SWIGLU_LIMIT)
        act = g * jax.nn.sigmoid(SWIGLU_ALPHA * g) * (u + 1.0)
        out = jnp.dot(act.astype(BF16), wd_bf[...], preferred_element_type=F32) + bd_ref[...]
        o_ref[...] = _pack_halves(out.astype(BF16))

    @pl.when(jnp.logical_not(active))
    def _():
        o_ref[...] = jnp.zeros_like(o_ref)


def _experts(block_exp, n_used, block_valid, xs, wgu, bgu, wd, bd, bm):
    P = xs.shape[0]
    nblk = P // bm
    xmap = lambda n, be, nu, bv: (jnp.minimum(n, nu[0] - 1), 0)
    emap = lambda n, be, nu, bv: (be[n], 0, 0)
    grid_spec = pltpu.PrefetchScalarGridSpec(
        num_scalar_prefetch=3,
        grid=(nblk,),
        in_specs=[
            pl.BlockSpec((bm, PACKED), xmap),
            pl.BlockSpec((None, D_MODEL, 2 * D_FF), emap),
            pl.BlockSpec((None, 1, 2 * D_FF), emap),
            pl.BlockSpec((None, D_FF, D_MODEL), emap),
            pl.BlockSpec((None, 1, D_MODEL), emap),
        ],
        out_specs=pl.BlockSpec((bm, PACKED), lambda n, be, nu, bv: (n, 0)),
        scratch_shapes=[pltpu.VMEM((D_MODEL, 2 * D_FF), BF16), pltpu.VMEM((D_FF, D_MODEL), BF16)],
    )
    return pl.pallas_call(
        _expert_body,
        grid_spec=grid_spec,
        out_shape=jax.ShapeDtypeStruct((P, PACKED), I32),
        compiler_params=_params("arbitrary"),
        name="experts",
    )(block_exp, n_used, block_valid, xs, wgu, bgu, wd, bd)


def _combine_body(x1_ref, g0_ref, g1_ref, g2_ref, g3_ref, gate_ref, gfin_ref, y_ref):
    tf = x1_ref.shape[0]
    gate_rows = jnp.concatenate(
        [gate_ref[...], jnp.zeros((LANES - TOP_K, tf), F32)], axis=0)
    for c in range(tf // LANES):
        rs = slice(c * LANES, (c + 1) * LANES)
        gate = gate_rows[:, rs].T
        y = x1_ref[rs, :]
        for j, g_ref in enumerate((g0_ref, g1_ref, g2_ref, g3_ref)):
            y = y + gate[:, j:j + 1] * _unpack_halves(g_ref[rs, :], F32)
        ms = jnp.mean(y * y, axis=-1, keepdims=True)
        y_ref[rs, :] = y * lax.rsqrt(ms + NORM_EPS) * gfin_ref[...]


def _combine(x1, gathered, gate_t, gfin, row0, rows, tf):
    nt = rows // tf
    t0 = row0 // tf
    row = lambda i: (t0 + i, 0)
    choice = lambda j: pl.BlockSpec((tf, PACKED), lambda i: (j * nt + i, 0))
    return pl.pallas_call(
        _combine_body,
        grid=(nt,),
        in_specs=[pl.BlockSpec((tf, D_MODEL), row)] + [choice(j) for j in range(TOP_K)] + [
            pl.BlockSpec((TOP_K, tf), lambda i: (0, t0 + i)),
            pl.BlockSpec((1, D_MODEL), lambda i: (0, 0)),
        ],
        out_specs=pl.BlockSpec((tf, D_MODEL), lambda i: (i, 0)),
        out_shape=jax.ShapeDtypeStruct((rows, D_MODEL), F32),
        compiler_params=_params("parallel"),
        name="combine",
    )(x1, gathered, gathered, gathered, gathered, gate_t, gfin)


def _permute_in_proj(w):
    lead = w.shape[:-1]
    half = HEAD_DIM // 2
    q = w[..., :Q_COLS].reshape(*lead, N_Q_HEADS // 2, 2, 2, half)
    q = jnp.swapaxes(q, -3, -2).reshape(*lead, Q_COLS)
    k = w[..., Q_COLS:Q_COLS + KV_COLS].reshape(*lead, N_KV_HEADS, 2, 1, half)
    k = jnp.broadcast_to(k, (*lead, N_KV_HEADS, 2, 2, half)).reshape(*lead, KV_DUP_COLS)
    v = w[..., Q_COLS + KV_COLS:Q_COLS + 2 * KV_COLS].reshape(*lead, N_KV_HEADS, 1, HEAD_DIM)
    v = jnp.broadcast_to(v, (*lead, N_KV_HEADS, 2, HEAD_DIM)).reshape(*lead, KV_DUP_COLS)
    return jnp.concatenate([q, k, w[..., Q_COLS + 2 * KV_COLS:]], axis=-1), v


def _rope_tables(n_pos):
    half = HEAD_DIM // 2
    inv_freq = 1.0 / (ROPE_THETA ** (jnp.arange(half, dtype=F32) * (2.0 / HEAD_DIM)))
    ang = jnp.arange(n_pos, dtype=F32)[:, None] * inv_freq[None, :]
    cos = jnp.tile(jnp.cos(ang), (1, LANES // half))
    sin = jnp.tile(jnp.sin(ang), (1, LANES // half))
    sign = jnp.where(jnp.arange(LANES) < LANES // 2, -1.0, 1.0).astype(F32)
    return cos, sin * sign[None, :]


def _tiles(geo):
    unit = min(geo.len_a, geo.len_b)
    tile = min(512, unit)
    return dict(tm=tile, tc=tile, tf=tile, bm=512)


def kernel(x_prompt, x_sample, norm_mix_g, w_in, b_in, attn_sink, w_o_attn, conv_dw_w, conv_dw_b,
           conv_ln_g, conv_ln_b, w_pw2, b_pw2, w_out, norm_ffn_g, w_router, b_router, w_gu, b_gu,
           w_down, b_down, norm_final_g):
    assert w_in.shape[0] == 1, "single trunk layer"
    geo = Geo(x_prompt.shape[0], x_prompt.shape[1], x_sample.shape[0], x_sample.shape[1])
    T = geo.total
    ts = _tiles(geo)
    xa = x_prompt.reshape(-1, D_MODEL)
    xb = x_sample.reshape(-1, D_MODEL)

    w_perm, w_v = _permute_in_proj(w_in[0])
    b_perm, b_v = _permute_in_proj(b_in)
    cos_t, sin_t = _rope_tables(max(geo.len_a, geo.len_b))

    q, kd, vt, glu, gates = _in_proj(xa, xb, norm_mix_g, w_perm.astype(BF16), b_perm,
                                     w_v.T.astype(BF16), b_v.T, cos_t, sin_t, geo, ts["tm"])
    attn = _attention(q, kd, vt, attn_sink[0], geo)
    w_rep = jnp.repeat(conv_dw_w[0], SUBLANES, axis=0)
    conv = _conv_branch(glu, w_rep, conv_dw_b, conv_ln_g, conv_ln_b, geo, ts["tc"])

    wr_t = w_router[0].T
    wr_hi = wr_t.astype(BF16)
    wr_lo = (wr_t - wr_hi.astype(F32)).astype(BF16)
    tri = jnp.triu(jnp.ones((ts["tm"], ts["tm"]), BF16), 1)
    x1, h2, idx, gate_t, rank, counts = _mix_route(
        xa, xb, attn, conv, gates, w_o_attn[0].astype(BF16), w_pw2[0].astype(BF16), b_pw2,
        w_out[0].astype(BF16), norm_ffn_g, wr_hi, wr_lo, b_router[0][:, None], tri, geo, ts["tm"])

    bm = ts["bm"]
    n_blocks = (T * TOP_K) // bm + N_EXPERTS
    cnt = counts[:, 0].astype(I32)
    padded = ((cnt + bm - 1) // bm) * bm
    pad_end = jnp.cumsum(padded)
    pad_start = pad_end - padded
    expert_ids = jnp.arange(N_EXPERTS, dtype=I32)
    dest = rank + jnp.sum(jnp.where(idx[None] == expert_ids[:, None, None],
                                    pad_start[:, None, None], 0), axis=0)
    block_start = jnp.arange(n_blocks, dtype=I32) * bm
    block_exp = jnp.minimum(jnp.sum((pad_end[None, :] <= block_start[:, None]).astype(I32), axis=1),
                            N_EXPERTS - 1)
    n_used = (pad_end[-1:] // bm).astype(I32)
    seg_end = jnp.sum(jnp.where(block_exp[:, None] == expert_ids[None, :],
                                (pad_start + cnt)[None, :], 0), axis=1)
    block_valid = jnp.clip(seg_end - block_start, 0, bm).astype(I32)

    xs = _sc_dispatch(h2, dest, n_blocks * bm)
    ys = _experts(block_exp, n_used, block_valid, xs, w_gu[0], b_gu[0][:, None, :],
                  w_down[0], b_down[0][:, None, :], bm)
    outs = []
    for row0, rows in ((0, geo.rows_a), (geo.rows_a, T - geo.rows_a)):
        gathered = _sc_gather(ys, dest, row0, rows)
        outs.append(_combine(x1, gathered, gate_t, norm_final_g[None, :], row0, rows, ts["tf"]))
    return (outs[0].reshape(x_prompt.shape), outs[1].reshape(x_sample.shape))
```

```python
import functools
import math
from typing import NamedTuple

import jax
import jax.numpy as jnp
from jax import lax
from jax.experimental import pallas as pl
from jax.experimental.pallas import tpu as pltpu
from jax.experimental.pallas import tpu_sc as plsc

F32 = jnp.float32
BF16 = jnp.bfloat16
I32 = jnp.int32

D_MODEL = 1024
HEAD_DIM = 64
N_Q_HEADS = 16
N_KV_HEADS = 4
WINDOW = 128
ATTN_BLOCK = 128
ROPE_THETA = 10000.0
CONV_WIDTH = 31
CONV_PAD = CONV_WIDTH // 2
N_EXPERTS = 32
TOP_K = 4
D_FF = D_MODEL
SWIGLU_LIMIT = 7.0
SWIGLU_ALPHA = 1.702
NORM_EPS = 1e-5
NEG_INF = -1e30

Q_COLS = N_Q_HEADS * HEAD_DIM
KV_COLS = N_KV_HEADS * HEAD_DIM
LANES = 128
SUBLANES = 8
KV_DUP_COLS = N_KV_HEADS * LANES
HALO = 16
VMEM_LIMIT = 56 * 1024 * 1024
SC_ROWS_PER_STEP = 128


class Geo(NamedTuple):
    n_a: int
    len_a: int
    n_b: int
    len_b: int

    @property
    def rows_a(self):
        return self.n_a * self.len_a

    @property
    def total(self):
        return self.rows_a + self.n_b * self.len_b


def _seq_bounds(geo, r):
    in_a = r < geo.rows_a
    start_a = (r // geo.len_a) * geo.len_a
    start_b = geo.rows_a + ((r - geo.rows_a) // geo.len_b) * geo.len_b
    start = jnp.where(in_a, start_a, start_b)
    end = start + jnp.where(in_a, geo.len_a, geo.len_b)
    return start, end


def _params(*sem):
    return pltpu.CompilerParams(dimension_semantics=sem, vmem_limit_bytes=VMEM_LIMIT)


PACKED = D_MODEL // 2


def _pack_halves(x_bf16):
    hi = lax.bitcast_convert_type(x_bf16[:, :PACKED].astype(F32), I32)
    lo = lax.bitcast_convert_type(x_bf16[:, PACKED:].astype(F32), I32)
    return hi | lax.shift_right_logical(lo, 16)


def _unpack_halves(words, dtype):
    hi = lax.bitcast_convert_type(words & jnp.int32(-65536), F32)
    lo = lax.bitcast_convert_type(lax.shift_left(words, 16), F32)
    return jnp.concatenate([hi.astype(dtype), lo.astype(dtype)], axis=1)


def _two_group_specs(geo, tile, width):
    n_a = geo.rows_a // tile
    return n_a, [pl.BlockSpec((tile, width), lambda i: (jnp.minimum(i, n_a - 1), 0)),
                 pl.BlockSpec((tile, width), lambda i: (jnp.maximum(i - n_a, 0), 0))]


_C_Q = 0
_C_K = _C_Q + Q_COLS
_C_GA = _C_K + KV_DUP_COLS
_C_GG = _C_GA + D_MODEL
_C_GATE = _C_GG + D_MODEL
_N_IN = _C_GATE + 2 * D_MODEL
_PROJ_CHUNK = 512
N_SLAB = D_MODEL // LANES
_NT = (((1,), (1,)), ((), ()))
_LOG2E = math.log2(math.e)
_Q_SCALE = HEAD_DIM ** -0.5 * _LOG2E


def _in_proj_body(n_a, xa_ref, xb_ref, g_ref, w_ref, b_ref, wvt_ref, bvt_ref, cos_ref, sin_ref,
                  q_ref, kd_ref, vt_ref, glu_ref, gate_ref):
    x = jnp.where(pl.program_id(0) < n_a, xa_ref[...], xb_ref[...])
    ms = jnp.mean(x * x, axis=-1, keepdims=True)
    h = (x * lax.rsqrt(ms + NORM_EPS) * g_ref[...]).astype(BF16)
    cos = cos_ref[...]
    sin = sin_ref[...]

    def proj(c0):
        return (jnp.dot(h, w_ref[:, c0:c0 + _PROJ_CHUNK], preferred_element_type=F32)
                + b_ref[:, c0:c0 + _PROJ_CHUNK])

    def rope_store(z, out_ref, o0, scale):
        for c in range(_PROJ_CHUNK // LANES):
            zc = z[:, c * LANES:(c + 1) * LANES]
            r = zc * cos + pltpu.roll(zc, LANES // 2, 1) * sin
            if scale != 1.0:
                r = r * scale
            out_ref[:, o0 + c * LANES:o0 + (c + 1) * LANES] = r.astype(out_ref.dtype)

    for c in range(Q_COLS // _PROJ_CHUNK):
        rope_store(proj(_C_Q + c * _PROJ_CHUNK), q_ref, c * _PROJ_CHUNK, _Q_SCALE)
    for c in range(KV_DUP_COLS // _PROJ_CHUNK):
        rope_store(proj(_C_K + c * _PROJ_CHUNK), kd_ref, c * _PROJ_CHUNK, 1.0)
    vt = lax.dot_general(wvt_ref[...], h, _NT, preferred_element_type=F32) + bvt_ref[...]
    vt_ref[...] = vt.astype(BF16)
    per = _PROJ_CHUNK // LANES
    for c in range(D_MODEL // _PROJ_CHUNK):
        a = proj(_C_GA + c * _PROJ_CHUNK)
        g = proj(_C_GG + c * _PROJ_CHUNK)
        glu = a * jax.nn.sigmoid(g)
        for s in range(per):
            glu_ref[c * per + s] = glu[:, s * LANES:(s + 1) * LANES]
    for c in range(2 * D_MODEL // _PROJ_CHUNK):
        gate_ref[:, c * _PROJ_CHUNK:(c + 1) * _PROJ_CHUNK] = jax.nn.sigmoid(
            proj(_C_GATE + c * _PROJ_CHUNK)).astype(BF16)


def _in_proj(xa, xb, g_mix, w_perm, b_perm, wvt, bvt, cos_t, sin_t, geo, tm):
    T = geo.total

    def pos_map(i):
        r0 = i * tm
        start, _ = _seq_bounds(geo, r0)
        return ((r0 - start) // tm, 0)

    const = lambda i: (0, 0)
    row = lambda i: (i, 0)
    n_a, x_specs = _two_group_specs(geo, tm, D_MODEL)
    return pl.pallas_call(
        functools.partial(_in_proj_body, n_a),
        grid=(T // tm,),
        in_specs=x_specs + [
            pl.BlockSpec((1, D_MODEL), const),
            pl.BlockSpec((D_MODEL, _N_IN), const, pipeline_mode=pl.Buffered(1)),
            pl.BlockSpec((1, _N_IN), const),
            pl.BlockSpec((KV_DUP_COLS, D_MODEL), const),
            pl.BlockSpec((KV_DUP_COLS, 1), const),
            pl.BlockSpec((tm, LANES), pos_map),
            pl.BlockSpec((tm, LANES), pos_map),
        ],
        out_specs=[
            pl.BlockSpec((tm, Q_COLS), row),
            pl.BlockSpec((tm, KV_DUP_COLS), row),
            pl.BlockSpec((KV_DUP_COLS, tm), lambda i: (0, i)),
            pl.BlockSpec((N_SLAB, tm, LANES), lambda i: (0, i, 0)),
            pl.BlockSpec((tm, 2 * D_MODEL), row),
        ],
        out_shape=[
            jax.ShapeDtypeStruct((T, Q_COLS), BF16),
            jax.ShapeDtypeStruct((T, KV_DUP_COLS), BF16),
            jax.ShapeDtypeStruct((KV_DUP_COLS, T), BF16),
            jax.ShapeDtypeStruct((N_SLAB, T, LANES), F32),
            jax.ShapeDtypeStruct((T, 2 * D_MODEL), BF16),
        ],
        compiler_params=_params("parallel"),
        name="in_proj",
    )(xa, xb, g_mix, w_perm, b_perm, wvt, bvt, cos_t, sin_t)


_ONES_ROWS = 16
_ATTN_STEP_BLOCKS = 4


def _attn_body(geo, sink_ref, q_ref, kp_ref, kc_ref, kn_ref, vp_ref, vc_ref, vn_ref, o_ref):
    group = N_Q_HEADS // N_KV_HEADS
    nq = group * ATTN_BLOCK
    b = ATTN_BLOCK
    hd = HEAD_DIM
    nsub = _ATTN_STEP_BLOCKS

    key = lax.broadcasted_iota(I32, (b, nq), 0)
    qry = lax.broadcasted_iota(I32, (b, nq), 1) % b
    head_of_col = lax.broadcasted_iota(I32, (1, nq), 1) // b
    lane = lax.broadcasted_iota(I32, (b, LANES), 1)
    even_head = (lane % hd) < (hd // 2)
    ones = jnp.ones((_ONES_ROWS, 3 * b), BF16)

    for s in range(nsub):
        rows = slice(s * b, (s + 1) * b)
        r0 = (pl.program_id(0) * nsub + s) * b
        start, end = _seq_bounds(geo, r0)
        bias_prev = jnp.where((key >= qry) & (r0 > start), 0.0, NEG_INF)
        bias_next = jnp.where((key <= qry) & (r0 + b < end), 0.0, NEG_INF)

        for g in range(N_KV_HEADS):
            ls = slice(g * LANES, (g + 1) * LANES)
            k_prev = kp_ref[:, ls] if s == 0 else kc_ref[(s - 1) * b:s * b, ls]
            k_next = kn_ref[:, ls] if s == nsub - 1 else kc_ref[(s + 1) * b:(s + 2) * b, ls]
            v_prev = vp_ref[ls, :] if s == 0 else vc_ref[ls, (s - 1) * b:s * b]
            v_next = vn_ref[ls, :] if s == nsub - 1 else vc_ref[ls, (s + 1) * b:(s + 2) * b]

            qa = q_ref[rows, (2 * g) * LANES:(2 * g + 1) * LANES]
            qb = q_ref[rows, (2 * g + 1) * LANES:(2 * g + 2) * LANES]
            zero = jnp.zeros_like(qa)
            q4 = jnp.concatenate([jnp.where(even_head, qa, zero), jnp.where(even_head, zero, qa),
                                  jnp.where(even_head, qb, zero), jnp.where(even_head, zero, qb)],
                                 axis=0)
            k = jnp.concatenate([k_prev, kc_ref[rows, ls], k_next], axis=0)
            st = lax.dot_general(k, q4, _NT, preferred_element_type=F32)
            s_prev = st[0:b] + bias_prev
            s_cur = st[b:2 * b]
            s_next = st[2 * b:3 * b] + bias_next
            sink = jnp.full((1, nq), sink_ref[group * g] * _LOG2E, F32)
            for h in range(1, group):
                sink = jnp.where(head_of_col == h, sink_ref[group * g + h] * _LOG2E, sink)
            m = jnp.maximum(jnp.maximum(jnp.max(s_prev, axis=0, keepdims=True),
                                        jnp.max(s_cur, axis=0, keepdims=True)),
                            jnp.maximum(jnp.max(s_next, axis=0, keepdims=True), sink))
            p = jnp.concatenate([jnp.exp2(s_prev - m).astype(BF16), jnp.exp2(s_cur - m).astype(BF16),
                                 jnp.exp2(s_next - m).astype(BF16)], axis=0)
            vt = jnp.concatenate([v_prev, vc_ref[ls, rows], v_next], axis=1)
            ot = jnp.dot(jnp.concatenate([vt, ones], axis=0), p, preferred_element_type=F32)
            denom = ot[2 * hd:2 * hd + 1] + jnp.exp2(sink - m)
            on = ot[0:2 * hd] * (1.0 / denom)
            pair_a = jnp.concatenate([on[0:hd, 0:b], on[hd:2 * hd, b:2 * b]], axis=0)
            pair_b = jnp.concatenate([on[0:hd, 2 * b:3 * b], on[hd:2 * hd, 3 * b:4 * b]], axis=0)
            o_ref[rows, (2 * g) * LANES:(2 * g + 1) * LANES] = pair_a.T.astype(BF16)
            o_ref[rows, (2 * g + 1) * LANES:(2 * g + 2) * LANES] = pair_b.T.astype(BF16)


def _attention(q, kd, vt, sink, geo):
    T = q.shape[0]
    nsub = _ATTN_STEP_BLOCKS
    nb = T // ATTN_BLOCK
    step_rows = nsub * ATTN_BLOCK
    prev = lambda i: jnp.maximum(i * nsub - 1, 0)
    nxt = lambda i: jnp.minimum((i + 1) * nsub, nb - 1)
    k_edge = lambda m: pl.BlockSpec((ATTN_BLOCK, KV_DUP_COLS), lambda i: (m(i), 0))
    v_edge = lambda m: pl.BlockSpec((KV_DUP_COLS, ATTN_BLOCK), lambda i: (0, m(i)))
    return pl.pallas_call(
        functools.partial(_attn_body, geo),
        grid=(T // step_rows,),
        in_specs=[
            pl.BlockSpec(memory_space=pltpu.SMEM),
            pl.BlockSpec((step_rows, Q_COLS), lambda i: (i, 0)),
            k_edge(prev), pl.BlockSpec((step_rows, KV_DUP_COLS), lambda i: (i, 0)), k_edge(nxt),
            v_edge(prev), pl.BlockSpec((KV_DUP_COLS, step_rows), lambda i: (0, i)), v_edge(nxt),
        ],
        out_specs=pl.BlockSpec((step_rows, Q_COLS), lambda i: (i, 0)),
        out_shape=jax.ShapeDtypeStruct((T, Q_COLS), BF16),
        compiler_params=_params("parallel"),
        name="window_attn",
    )(sink, q, kd, kd, kd, vt, vt, vt)


_CONV_ROWS = 64
_LN_ROWS = 32


def _conv_body(geo, tc, z_ref, zp_ref, zn_ref, w_ref, dwb_ref, lng_ref, lnb_ref, o_ref,
               buf_ref, y_ref):
    i = pl.program_id(0)
    r0 = i * tc
    start, end = _seq_bounds(geo, r0)
    has_prev = (r0 > start).astype(F32)
    has_next = (r0 + tc < end).astype(F32)
    rows = tc + 2 * HALO
    buf_ref[:, 0:HALO, :] = zp_ref[...] * has_prev
    buf_ref[:, HALO:HALO + tc, :] = z_ref[...]
    buf_ref[:, HALO + tc:rows, :] = zn_ref[...] * has_next

    rep = _CONV_ROWS // SUBLANES

    def conv_chunk(j):
        s0 = pl.multiple_of(j * _CONV_ROWS, _CONV_ROWS)
        for c in range(N_SLAB):
            ls = slice(c * LANES, (c + 1) * LANES)
            acc = jnp.zeros((_CONV_ROWS, LANES), F32)
            for k in range(CONV_WIDTH):
                zt = buf_ref[c, pl.ds(s0 + (HALO - CONV_PAD + k), _CONV_ROWS, stride=1), :]
                wt = w_ref[k * SUBLANES:(k + 1) * SUBLANES, ls]
                acc = acc + zt * jnp.concatenate([wt] * rep, axis=0)
            y_ref[pl.ds(s0, _CONV_ROWS), ls] = acc

    def ln_chunk(j):
        for h in range(_CONV_ROWS // _LN_ROWS):
            s0 = pl.multiple_of(j * _CONV_ROWS + h * _LN_ROWS, _LN_ROWS)
            y = y_ref[pl.ds(s0, _LN_ROWS), :] + dwb_ref[...]
            mu = jnp.mean(y, axis=-1, keepdims=True)
            yc = y - mu
            var = jnp.mean(yc * yc, axis=-1, keepdims=True)
            yn = yc * lax.rsqrt(var + NORM_EPS) * lng_ref[...] + lnb_ref[...]
            o_ref[pl.ds(s0, _LN_ROWS), :] = (yn * jax.nn.sigmoid(yn)).astype(BF16)

    n = tc // _CONV_ROWS
    conv_chunk(0)

    def body(j, carry):
        ln_chunk(j - 1)
        conv_chunk(j)
        return carry

    lax.fori_loop(1, n, body, 0)
    ln_chunk(n - 1)


def _conv_branch(glu, w_rep, dw_b, ln_g, ln_b, geo, tc):
    T = glu.shape[1]
    nh = T // HALO
    per = tc // HALO
    const = lambda i: (0, 0)
    return pl.pallas_call(
        functools.partial(_conv_body, geo, tc),
        grid=(T // tc,),
        in_specs=[
            pl.BlockSpec((N_SLAB, tc, LANES), lambda i: (0, i, 0)),
            pl.BlockSpec((N_SLAB, HALO, LANES), lambda i: (0, jnp.maximum(i * per - 1, 0), 0)),
            pl.BlockSpec((N_SLAB, HALO, LANES), lambda i: (0, jnp.minimum((i + 1) * per, nh - 1), 0)),
            pl.BlockSpec((CONV_WIDTH * SUBLANES, D_MODEL), const),
            pl.BlockSpec((1, D_MODEL), const),
            pl.BlockSpec((1, D_MODEL), const),
            pl.BlockSpec((1, D_MODEL), const),
        ],
        out_specs=pl.BlockSpec((tc, D_MODEL), lambda i: (i, 0)),
        out_shape=jax.ShapeDtypeStruct((T, D_MODEL), BF16),
        scratch_shapes=[pltpu.VMEM((N_SLAB, tc + 2 * HALO, LANES), F32),
                        pltpu.VMEM((tc, D_MODEL), F32)],
        compiler_params=_params("parallel"),
        name="conv_branch",
    )(glu, glu, glu, w_rep, dw_b, ln_g, ln_b)


_MIX_COLS = 256


def _mix_body(n_a, xa_ref, xb_ref, o_ref, c_ref, gate_ref, wo_ref, wpw_ref, bpw_ref, wout_ref,
              gffn_ref, wrh_ref, wrl_ref, br_ref, tri_ref,
              x1_ref, h2_ref, idx_ref, gt_ref, rank_ref, cnt_ref, x1_scr, mix_ref, carry_ref):
    tm = xa_ref.shape[0]
    i = pl.program_id(0)
    last = pl.num_programs(0) - 2
    slot = i % 2

    @pl.when(i == 0)
    def _():
        carry_ref[...] = jnp.zeros_like(carry_ref)
        x1_scr[...] = jnp.zeros_like(x1_scr)

    x1_prev = x1_scr[1 - slot]
    x1_new = x1_scr.at[slot]
    in_a = jnp.minimum(i, last) < n_a
    n_pieces = D_MODEL // _MIX_COLS

    def mix_piece(p):
        cs = slice(p * _MIX_COLS, (p + 1) * _MIX_COLS)
        gs = slice(D_MODEL + p * _MIX_COLS, D_MODEL + (p + 1) * _MIX_COLS)
        attn = jnp.dot(o_ref[...], wo_ref[:, cs], preferred_element_type=F32)
        conv = jnp.dot(c_ref[...], wpw_ref[:, cs], preferred_element_type=F32) + bpw_ref[:, cs]
        mix_ref[:, cs] = (gate_ref[:, cs].astype(F32) * attn
                          + gate_ref[:, gs].astype(F32) * conv).astype(BF16)

    def out_piece(p):
        cs = slice(p * _MIX_COLS, (p + 1) * _MIX_COLS)
        x = jnp.where(in_a, xa_ref[:, cs], xb_ref[:, cs])
        x1 = x + jnp.dot(mix_ref[...], wout_ref[:, cs], preferred_element_type=F32)
        x1_ref[:, cs] = x1
        x1_new[:, cs] = x1

    mix_piece(0)
    ms = jnp.mean(x1_prev * x1_prev, axis=-1, keepdims=True)
    h2 = x1_prev * lax.rsqrt(ms + NORM_EPS) * gffn_ref[...]
    h_hi = h2.astype(BF16)
    h2_ref[...] = _pack_halves(h_hi)
    mix_piece(1)

    h_lo = (h2 - h_hi.astype(F32)).astype(BF16)
    logits = (lax.dot_general(wrh_ref[...], h_hi, _NT, preferred_element_type=F32)
              + lax.dot_general(wrh_ref[...], h_lo, _NT, preferred_element_type=F32)
              + lax.dot_general(wrl_ref[...], h_hi, _NT, preferred_element_type=F32)
              + br_ref[...])
    for p in range(2, n_pieces):
        mix_piece(p)
    out_piece(0)

    eidx = lax.broadcasted_iota(I32, (N_EXPERTS, tm), 0)
    vals = logits
    picked, top_vals, top_idx = [], [], []
    for _ in range(TOP_K):
        m = jnp.max(vals, axis=0, keepdims=True)
        idx = jnp.min(jnp.where(vals == m, eidx, N_EXPERTS), axis=0, keepdims=True)
        sel = eidx == idx
        vals = jnp.where(sel, -jnp.inf, vals)
        picked.append(sel)
        top_vals.append(m)
        top_idx.append(idx)
    out_piece(1)

    exps = [jnp.exp(v - top_vals[0]) for v in top_vals]
    tot = exps[0] + exps[1] + exps[2] + exps[3]
    onehot = (picked[0] | picked[1] | picked[2] | picked[3])
    prefix = jnp.dot(onehot.astype(BF16), tri_ref[...], preferred_element_type=F32) + carry_ref[...]
    for p in range(2, n_pieces):
        out_piece(p)
    for j in range(TOP_K):
        idx_ref[j:j + 1, :] = top_idx[j]
        gt_ref[j:j + 1, :] = exps[j] / tot
        rank_ref[j:j + 1, :] = jnp.sum(jnp.where(picked[j], prefix, 0.0), axis=0,
                                       keepdims=True).astype(I32)
    counted = jnp.where(i > 0, jnp.sum(onehot.astype(F32), axis=1, keepdims=True), 0.0)
    carry_ref[...] = carry_ref[...] + counted
    cnt_ref[...] = jnp.broadcast_to(carry_ref[...], cnt_ref.shape)


def _mix_route(xa, xb, o, c, gates, wo, wpw, bpw, wout, gffn, wr_hi, wr_lo, br, tri, geo, tm):
    T = geo.total
    n = T // tm
    n_a = geo.rows_a // tm
    head = lambda i: jnp.minimum(i, n - 1)
    tail = lambda i: jnp.maximum(i - 1, 0)
    row = lambda i: (head(i), 0)
    col = lambda i: (0, tail(i))
    const = lambda i: (0, 0)
    wspec = lambda shape: pl.BlockSpec(shape, const, pipeline_mode=pl.Buffered(1))
    x_specs = [pl.BlockSpec((tm, D_MODEL), lambda i: (jnp.minimum(head(i), n_a - 1), 0)),
               pl.BlockSpec((tm, D_MODEL), lambda i: (jnp.maximum(head(i) - n_a, 0), 0))]
    return pl.pallas_call(
        functools.partial(_mix_body, n_a),
        grid=(n + 1,),
        in_specs=x_specs + [
            pl.BlockSpec((tm, Q_COLS), row),
            pl.BlockSpec((tm, D_MODEL), row),
            pl.BlockSpec((tm, 2 * D_MODEL), row),
            wspec((Q_COLS, D_MODEL)),
            wspec((D_MODEL, D_MODEL)),
            pl.BlockSpec((1, D_MODEL), const),
            wspec((D_MODEL, D_MODEL)),
            pl.BlockSpec((1, D_MODEL), const),
            pl.BlockSpec((N_EXPERTS, D_MODEL), const),
            pl.BlockSpec((N_EXPERTS, D_MODEL), const),
            pl.BlockSpec((N_EXPERTS, 1), const),
            pl.BlockSpec((tm, tm), const),
        ],
        out_specs=[
            pl.BlockSpec((tm, D_MODEL), row),
            pl.BlockSpec((tm, PACKED), lambda i: (tail(i), 0)),
            pl.BlockSpec((TOP_K, tm), col),
            pl.BlockSpec((TOP_K, tm), col),
            pl.BlockSpec((TOP_K, tm), col),
            pl.BlockSpec((N_EXPERTS, LANES), const),
        ],
        out_shape=[
            jax.ShapeDtypeStruct((T, D_MODEL), F32),
            jax.ShapeDtypeStruct((T, PACKED), I32),
            jax.ShapeDtypeStruct((TOP_K, T), I32),
            jax.ShapeDtypeStruct((TOP_K, T), F32),
            jax.ShapeDtypeStruct((TOP_K, T), I32),
            jax.ShapeDtypeStruct((N_EXPERTS, LANES), F32),
        ],
        scratch_shapes=[pltpu.VMEM((2, tm, D_MODEL), F32), pltpu.VMEM((tm, D_MODEL), BF16),
                        pltpu.VMEM((N_EXPERTS, 1), F32)],
        compiler_params=_params("arbitrary"),
        name="mix_route",
    )(xa, xb, o, c, gates, wo, wpw, bpw, wout, gffn, wr_hi, wr_lo, br, tri)


def _sc_workers():
    info = plsc.get_sparse_core_info()
    return info.num_cores, info.num_cores * info.num_subcores


def _sc_dispatch(h, dest, n_out):
    t_rows, width = h.shape
    nc, nw = _sc_workers()
    per_w = t_rows // nw
    step = SC_ROWS_PER_STEP
    assert per_w * nw == t_rows and per_w % step == 0
    mesh = plsc.VectorSubcoreMesh(core_axis_name="c", subcore_axis_name="s")

    @functools.partial(
        pl.kernel, mesh=mesh,
        out_type=jax.ShapeDtypeStruct((n_out, width), h.dtype),
        scratch_types=[pltpu.VMEM((step,), I32)] * TOP_K + [pltpu.VMEM((step, width), h.dtype)],
    )
    def scatter_rows(h_hbm, dest_hbm, out_hbm, i0, i1, i2, i3, rows_v):
        base = (lax.axis_index("s") * nc + lax.axis_index("c")) * per_w

        @pl.loop(0, per_w // step)
        def _(i):
            off = base + i * step
            pltpu.sync_copy(h_hbm.at[pl.ds(off, step)], rows_v)
            for j, idx_v in enumerate((i0, i1, i2, i3)):
                pltpu.sync_copy(dest_hbm.at[j, pl.ds(off, step)], idx_v)
                pltpu.sync_copy(rows_v, out_hbm.at[idx_v])

    return scatter_rows(h, dest)


def _sc_gather(table, dest, row0, rows):
    width = table.shape[1]
    nc, nw = _sc_workers()
    per_w = rows // nw
    step = SC_ROWS_PER_STEP
    assert per_w * nw == rows and per_w % step == 0
    mesh = plsc.VectorSubcoreMesh(core_axis_name="c", subcore_axis_name="s")

    @functools.partial(
        pl.kernel, mesh=mesh,
        out_type=jax.ShapeDtypeStruct((TOP_K * rows, width), table.dtype),
        scratch_types=[pltpu.VMEM((step,), I32), pltpu.VMEM((step, width), table.dtype),
                       pltpu.SemaphoreType.DMA],
    )
    def gather_rows(table_hbm, dest_hbm, out_hbm, idx_v, rows_v, sem):
        base = (lax.axis_index("s") * nc + lax.axis_index("c")) * per_w

        @pl.loop(0, per_w // step)
        def _(i):
            off = base + i * step
            for j in range(TOP_K):
                pltpu.sync_copy(dest_hbm.at[j, pl.ds(row0 + off, step)], idx_v)
                pltpu.async_copy(table_hbm.at[idx_v], rows_v, sem).wait()
                pltpu.sync_copy(rows_v, out_hbm.at[pl.ds(j * rows + off, step)])

    return gather_rows(table, dest)


def _expert_body(bexp_ref, nused_ref, valid_ref, xs_ref, wgu_ref, bgu_ref, wd_ref, bd_ref, o_ref,
                 wgu_bf, wd_bf):
    n = pl.program_id(0)
    active = n < nused_ref[0]
    new_expert = (n == 0) | (bexp_ref[n] != bexp_ref[jnp.maximum(n - 1, 0)])

    @pl.when(active & new_expert)
    def _():
        wgu_bf[...] = wgu_ref[...].astype(BF16)
        wd_bf[...] = wd_ref[...].astype(BF16)

    @pl.when(active)
    def _():
        row = lax.broadcasted_iota(I32, (xs_ref.shape[0], 1), 0)
        x = _unpack_halves(jnp.where(row < valid_ref[n], xs_ref[...], 0), BF16)
        gu = jnp.dot(x, wgu_bf[...], preferred_element_type=F32) + bgu_ref[...]
        g = jnp.minimum(gu[:, 0:D_FF], SWIGLU_LIMIT)
        u = jnp.clip(gu[:, D_FF:2 * D_FF], -SWIGLU_LIMIT, SWIGLU_LIMIT)
        act = g * jax.nn.sigmoid(SWIGLU_ALPHA * g) * (u + 1.0)
        out = jnp.dot(act.astype(BF16), wd_bf[...], preferred_element_type=F32) + bd_ref[...]
        o_ref[...] = _pack_halves(out.astype(BF16))

    @pl.when(jnp.logical_not(active))
    def _():
        o_ref[...] = jnp.zeros_like(o_ref)


def _experts(block_exp, n_used, block_valid, xs, wgu, bgu, wd, bd, bm):
    P = xs.shape[0]
    nblk = P // bm
    xmap = lambda n, be, nu, bv: (jnp.minimum(n, nu[0] - 1), 0)
    emap = lambda n, be, nu, bv: (be[n], 0, 0)
    grid_spec = pltpu.PrefetchScalarGridSpec(
        num_scalar_prefetch=3,
        grid=(nblk,),
        in_specs=[
            pl.BlockSpec((bm, PACKED), xmap),
            pl.BlockSpec((None, D_MODEL, 2 * D_FF), emap),
            pl.BlockSpec((None, 1, 2 * D_FF), emap),
            pl.BlockSpec((None, D_FF, D_MODEL), emap),
            pl.BlockSpec((None, 1, D_MODEL), emap),
        ],
        out_specs=pl.BlockSpec((bm, PACKED), lambda n, be, nu, bv: (n, 0)),
        scratch_shapes=[pltpu.VMEM((D_MODEL, 2 * D_FF), BF16), pltpu.VMEM((D_FF, D_MODEL), BF16)],
    )
    return pl.pallas_call(
        _expert_body,
        grid_spec=grid_spec,
        out_shape=jax.ShapeDtypeStruct((P, PACKED), I32),
        compiler_params=_params("arbitrary"),
        name="experts",
    )(block_exp, n_used, block_valid, xs, wgu, bgu, wd, bd)


def _combine_body(x1_ref, g0_ref, g1_ref, g2_ref, g3_ref, gate_ref, gfin_ref, y_ref):
    tf = x1_ref.shape[0]
    gate_rows = jnp.concatenate(
        [gate_ref[...], jnp.zeros((LANES - TOP_K, tf), F32)], axis=0)
    for c in range(tf // LANES):
        rs = slice(c * LANES, (c + 1) * LANES)
        gate = gate_rows[:, rs].T
        y = x1_ref[rs, :]
        for j, g_ref in enumerate((g0_ref, g1_ref, g2_ref, g3_ref)):
            y = y + gate[:, j:j + 1] * _unpack_halves(g_ref[rs, :], F32)
        ms = jnp.mean(y * y, axis=-1, keepdims=True)
        y_ref[rs, :] = y * lax.rsqrt(ms + NORM_EPS) * gfin_ref[...]


def _combine(x1, gathered, gate_t, gfin, row0, rows, tf):
    nt = rows // tf
    t0 = row0 // tf
    row = lambda i: (t0 + i, 0)
    choice = lambda j: pl.BlockSpec((tf, PACKED), lambda i: (j * nt + i, 0))
    return pl.pallas_call(
        _combine_body,
        grid=(nt,),
        in_specs=[pl.BlockSpec((tf, D_MODEL), row)] + [choice(j) for j in range(TOP_K)] + [
            pl.BlockSpec((TOP_K, tf), lambda i: (0, t0 + i)),
            pl.BlockSpec((1, D_MODEL), lambda i: (0, 0)),
        ],
        out_specs=pl.BlockSpec((tf, D_MODEL), lambda i: (i, 0)),
        out_shape=jax.ShapeDtypeStruct((rows, D_MODEL), F32),
        compiler_params=_params("parallel"),
        name="combine",
    )(x1, gathered, gathered, gathered, gathered, gate_t, gfin)


def _permute_in_proj(w):
    lead = w.shape[:-1]
    half = HEAD_DIM // 2
    q = w[..., :Q_COLS].reshape(*lead, N_Q_HEADS // 2, 2, 2, half)
    q = jnp.swapaxes(q, -3, -2).reshape(*lead, Q_COLS)
    k = w[..., Q_COLS:Q_COLS + KV_COLS].reshape(*lead, N_KV_HEADS, 2, 1, half)
    k = jnp.broadcast_to(k, (*lead, N_KV_HEADS, 2, 2, half)).reshape(*lead, KV_DUP_COLS)
    v = w[..., Q_COLS + KV_COLS:Q_COLS + 2 * KV_COLS].reshape(*lead, N_KV_HEADS, 1, HEAD_DIM)
    v = jnp.broadcast_to(v, (*lead, N_KV_HEADS, 2, HEAD_DIM)).reshape(*lead, KV_DUP_COLS)
    return jnp.concatenate([q, k, w[..., Q_COLS + 2 * KV_COLS:]], axis=-1), v


def _rope_tables(n_pos):
    half = HEAD_DIM // 2
    inv_freq = 1.0 / (ROPE_THETA ** (jnp.arange(half, dtype=F32) * (2.0 / HEAD_DIM)))
    ang = jnp.arange(n_pos, dtype=F32)[:, None] * inv_freq[None, :]
    cos = jnp.tile(jnp.cos(ang), (1, LANES // half))
    sin = jnp.tile(jnp.sin(ang), (1, LANES // half))
    sign = jnp.where(jnp.arange(LANES) < LANES // 2, -1.0, 1.0).astype(F32)
    return cos, sin * sign[None, :]


def _tiles(geo):
    unit = min(geo.len_a, geo.len_b)
    tile = min(512, unit)
    return dict(tm=tile, tc=tile, tf=tile, bm=512)


def kernel(x_prompt, x_sample, norm_mix_g, w_in, b_in, attn_sink, w_o_attn, conv_dw_w, conv_dw_b,
           conv_ln_g, conv_ln_b, w_pw2, b_pw2, w_out, norm_ffn_g, w_router, b_router, w_gu, b_gu,
           w_down, b_down, norm_final_g):
    assert w_in.shape[0] == 1, "single trunk layer"
    geo = Geo(x_prompt.shape[0], x_prompt.shape[1], x_sample.shape[0], x_sample.shape[1])
    T = geo.total
    ts = _tiles(geo)
    xa = x_prompt.reshape(-1, D_MODEL)
    xb = x_sample.reshape(-1, D_MODEL)

    w_perm, w_v = _permute_in_proj(w_in[0])
    b_perm, b_v = _permute_in_proj(b_in)
    cos_t, sin_t = _rope_tables(max(geo.len_a, geo.len_b))

    q, kd, vt, glu, gates = _in_proj(xa, xb, norm_mix_g, w_perm.astype(BF16), b_perm,
                                     w_v.T.astype(BF16), b_v.T, cos_t, sin_t, geo, ts["tm"])
    attn = _attention(q, kd, vt, attn_sink[0], geo)
    w_rep = jnp.repeat(conv_dw_w[0], SUBLANES, axis=0)
    conv = _conv_branch(glu, w_rep, conv_dw_b, conv_ln_g, conv_ln_b, geo, ts["tc"])

    wr_t = w_router[0].T
    wr_hi = wr_t.astype(BF16)
    wr_lo = (wr_t - wr_hi.astype(F32)).astype(BF16)
    tri = jnp.triu(jnp.ones((ts["tm"], ts["tm"]), BF16), 1)
    x1, h2, idx, gate_t, rank, counts = _mix_route(
        xa, xb, attn, conv, gates, w_o_attn[0].astype(BF16), w_pw2[0].astype(BF16), b_pw2,
        w_out[0].astype(BF16), norm_ffn_g, wr_hi, wr_lo, b_router[0][:, None], tri, geo, ts["tm"])

    bm = ts["bm"]
    n_blocks = (T * TOP_K) // bm + N_EXPERTS
    cnt = counts[:, 0].astype(I32)
    padded = ((cnt + bm - 1) // bm) * bm
    pad_end = jnp.cumsum(padded)
    pad_start = pad_end - padded
    expert_ids = jnp.arange(N_EXPERTS, dtype=I32)
    dest = rank + jnp.sum(jnp.where(idx[None] == expert_ids[:, None, None],
                                    pad_start[:, None, None], 0), axis=0)
    block_start = jnp.arange(n_blocks, dtype=I32) * bm
    block_exp = jnp.minimum(jnp.sum((pad_end[None, :] <= block_start[:, None]).astype(I32), axis=1),
                            N_EXPERTS - 1)
    n_used = (pad_end[-1:] // bm).astype(I32)
    seg_end = jnp.sum(jnp.where(block_exp[:, None] == expert_ids[None, :],
                                (pad_start + cnt)[None, :], 0), axis=1)
    block_valid = jnp.clip(seg_end - block_start, 0, bm).astype(I32)

    xs = _sc_dispatch(h2, dest, n_blocks * bm)
    ys = _experts(block_exp, n_used, block_valid, xs, w_gu[0], b_gu[0][:, None, :],
                  w_down[0], b_down[0][:, None, :], bm)
    outs = []
    for row0, rows in ((0, geo.rows_a), (geo.rows_a, T - geo.rows_a)):
        gathered = _sc_gather(ys, dest, row0, rows)
        outs.append(_combine(x1, gathered, gate_t, norm_final_g[None, :], row0, rows, ts["tf"]))
    return (outs[0].reshape(x_prompt.shape), outs[1].reshape(x_sample.shape))
```

```python
import functools
import math
from typing import NamedTuple

import jax
import jax.numpy as jnp
from jax import lax
from jax.experimental import pallas as pl
from jax.experimental.pallas import tpu as pltpu
from jax.experimental.pallas import tpu_sc as plsc

F32 = jnp.float32
BF16 = jnp.bfloat16
I32 = jnp.int32

D_MODEL = 1024
HEAD_DIM = 64
N_Q_HEADS = 16
N_KV_HEADS = 4
WINDOW = 128
ATTN_BLOCK = 128
ROPE_THETA = 10000.0
CONV_WIDTH = 31
CONV_PAD = CONV_WIDTH // 2
N_EXPERTS = 32
TOP_K = 4
D_FF = D_MODEL
SWIGLU_LIMIT = 7.0
SWIGLU_ALPHA = 1.702
NORM_EPS = 1e-5
NEG_INF = -1e30

Q_COLS = N_Q_HEADS * HEAD_DIM
KV_COLS = N_KV_HEADS * HEAD_DIM
LANES = 128
SUBLANES = 8
KV_DUP_COLS = N_KV_HEADS * LANES
HALO = 16
VMEM_LIMIT = 56 * 1024 * 1024
SC_ROWS_PER_STEP = 128
EXPERT_SUB_ROWS = 512


class Geo(NamedTuple):
    n_a: int
    len_a: int
    n_b: int
    len_b: int

    @property
    def rows_a(self):
        return self.n_a * self.len_a

    @property
    def total(self):
        return self.rows_a + self.n_b * self.len_b


def _seq_bounds(geo, r):
    in_a = r < geo.rows_a
    start_a = (r // geo.len_a) * geo.len_a
    start_b = geo.rows_a + ((r - geo.rows_a) // geo.len_b) * geo.len_b
    start = jnp.where(in_a, start_a, start_b)
    end = start + jnp.where(in_a, geo.len_a, geo.len_b)
    return start, end


def _params(*sem):
    return pltpu.CompilerParams(dimension_semantics=sem, vmem_limit_bytes=VMEM_LIMIT)


PACKED = D_MODEL // 2


def _pack_halves(x_bf16):
    hi = lax.bitcast_convert_type(x_bf16[:, :PACKED].astype(F32), I32)
    lo = lax.bitcast_convert_type(x_bf16[:, PACKED:].astype(F32), I32)
    return hi | lax.shift_right_logical(lo, 16)


def _unpack_halves(words, dtype):
    hi = lax.bitcast_convert_type(words & jnp.int32(-65536), F32)
    lo = lax.bitcast_convert_type(lax.shift_left(words, 16), F32)
    return jnp.concatenate([hi.astype(dtype), lo.astype(dtype)], axis=1)


def _two_group_specs(geo, tile, width):
    n_a = geo.rows_a // tile
    return n_a, [pl.BlockSpec((tile, width), lambda i: (jnp.minimum(i, n_a - 1), 0)),
                 pl.BlockSpec((tile, width), lambda i: (jnp.maximum(i - n_a, 0), 0))]


_C_Q = 0
_C_K = _C_Q + Q_COLS
_C_GA = _C_K + KV_DUP_COLS
_C_GG = _C_GA + D_MODEL
_C_GATE = _C_GG + D_MODEL
_N_IN = _C_GATE + 2 * D_MODEL
_PROJ_CHUNK = 512
N_SLAB = D_MODEL // LANES
_NT = (((1,), (1,)), ((), ()))
_LOG2E = math.log2(math.e)
_Q_SCALE = HEAD_DIM ** -0.5 * _LOG2E


def _in_proj_body(n_a, xa_ref, xb_ref, g_ref, w_ref, b_ref, wvt_ref, bvt_ref, cos_ref, sin_ref,
                  q_ref, kd_ref, vt_ref, glu_ref, gate_ref):
    x = jnp.where(pl.program_id(0) < n_a, xa_ref[...], xb_ref[...])
    ms = jnp.mean(x * x, axis=-1, keepdims=True)
    h = (x * lax.rsqrt(ms + NORM_EPS) * g_ref[...]).astype(BF16)
    cos = cos_ref[...]
    sin = sin_ref[...]

    def proj(c0):
        return (jnp.dot(h, w_ref[:, c0:c0 + _PROJ_CHUNK], preferred_element_type=F32)
                + b_ref[:, c0:c0 + _PROJ_CHUNK])

    def rope_store(z, out_ref, o0, scale):
        for c in range(_PROJ_CHUNK // LANES):
            zc = z[:, c * LANES:(c + 1) * LANES]
            r = zc * cos + pltpu.roll(zc, LANES // 2, 1) * sin
            if scale != 1.0:
                r = r * scale
            out_ref[:, o0 + c * LANES:o0 + (c + 1) * LANES] = r.astype(out_ref.dtype)

    for c in range(Q_COLS // _PROJ_CHUNK):
        rope_store(proj(_C_Q + c * _PROJ_CHUNK), q_ref, c * _PROJ_CHUNK, _Q_SCALE)
    for c in range(KV_DUP_COLS // _PROJ_CHUNK):
        rope_store(proj(_C_K + c * _PROJ_CHUNK), kd_ref, c * _PROJ_CHUNK, 1.0)
    vt = lax.dot_general(wvt_ref[...], h, _NT, preferred_element_type=F32) + bvt_ref[...]
    vt_ref[...] = vt.astype(BF16)
    per = _PROJ_CHUNK // LANES
    for c in range(D_MODEL // _PROJ_CHUNK):
        a = proj(_C_GA + c * _PROJ_CHUNK)
        g = proj(_C_GG + c * _PROJ_CHUNK)
        glu = a * jax.nn.sigmoid(g)
        for s in range(per):
            glu_ref[c * per + s] = glu[:, s * LANES:(s + 1) * LANES]
    for c in range(2 * D_MODEL // _PROJ_CHUNK):
        gate_ref[:, c * _PROJ_CHUNK:(c + 1) * _PROJ_CHUNK] = jax.nn.sigmoid(
            proj(_C_GATE + c * _PROJ_CHUNK)).astype(BF16)


def _in_proj(xa, xb, g_mix, w_perm, b_perm, wvt, bvt, cos_t, sin_t, geo, tm):
    T = geo.total

    def pos_map(i):
        r0 = i * tm
        start, _ = _seq_bounds(geo, r0)
        return ((r0 - start) // tm, 0)

    const = lambda i: (0, 0)
    row = lambda i: (i, 0)
    n_a, x_specs = _two_group_specs(geo, tm, D_MODEL)
    return pl.pallas_call(
        functools.partial(_in_proj_body, n_a),
        grid=(T // tm,),
        in_specs=x_specs + [
            pl.BlockSpec((1, D_MODEL), const),
            pl.BlockSpec((D_MODEL, _N_IN), const, pipeline_mode=pl.Buffered(1)),
            pl.BlockSpec((1, _N_IN), const),
            pl.BlockSpec((KV_DUP_COLS, D_MODEL), const),
            pl.BlockSpec((KV_DUP_COLS, 1), const),
            pl.BlockSpec((tm, LANES), pos_map),
            pl.BlockSpec((tm, LANES), pos_map),
        ],
        out_specs=[
            pl.BlockSpec((tm, Q_COLS), row),
            pl.BlockSpec((tm, KV_DUP_COLS), row),
            pl.BlockSpec((KV_DUP_COLS, tm), lambda i: (0, i)),
            pl.BlockSpec((N_SLAB, tm, LANES), lambda i: (0, i, 0)),
            pl.BlockSpec((tm, 2 * D_MODEL), row),
        ],
        out_shape=[
            jax.ShapeDtypeStruct((T, Q_COLS), BF16),
            jax.ShapeDtypeStruct((T, KV_DUP_COLS), BF16),
            jax.ShapeDtypeStruct((KV_DUP_COLS, T), BF16),
            jax.ShapeDtypeStruct((N_SLAB, T, LANES), F32),
            jax.ShapeDtypeStruct((T, 2 * D_MODEL), BF16),
        ],
        compiler_params=_params("parallel"),
        name="in_proj",
    )(xa, xb, g_mix, w_perm, b_perm, wvt, bvt, cos_t, sin_t)


_ONES_ROWS = 16
_ATTN_STEP_BLOCKS = 4


def _attn_body(geo, sink_ref, q_ref, kp_ref, kc_ref, kn_ref, vp_ref, vc_ref, vn_ref, o_ref):
    group = N_Q_HEADS // N_KV_HEADS
    nq = group * ATTN_BLOCK
    b = ATTN_BLOCK
    hd = HEAD_DIM
    nsub = _ATTN_STEP_BLOCKS

    key = lax.broadcasted_iota(I32, (b, nq), 0)
    qry = lax.broadcasted_iota(I32, (b, nq), 1) % b
    head_of_col = lax.broadcasted_iota(I32, (1, nq), 1) // b
    lane = lax.broadcasted_iota(I32, (b, LANES), 1)
    even_head = (lane % hd) < (hd // 2)
    ones = jnp.ones((_ONES_ROWS, 3 * b), BF16)

    for s in range(nsub):
        rows = slice(s * b, (s + 1) * b)
        r0 = (pl.program_id(0) * nsub + s) * b
        start, end = _seq_bounds(geo, r0)
        bias_prev = jnp.where((key >= qry) & (r0 > start), 0.0, NEG_INF)
        bias_next = jnp.where((key <= qry) & (r0 + b < end), 0.0, NEG_INF)

        for g in range(N_KV_HEADS):
            ls = slice(g * LANES, (g + 1) * LANES)
            k_prev = kp_ref[:, ls] if s == 0 else kc_ref[(s - 1) * b:s * b, ls]
            k_next = kn_ref[:, ls] if s == nsub - 1 else kc_ref[(s + 1) * b:(s + 2) * b, ls]
            v_prev = vp_ref[ls, :] if s == 0 else vc_ref[ls, (s - 1) * b:s * b]
            v_next = vn_ref[ls, :] if s == nsub - 1 else vc_ref[ls, (s + 1) * b:(s + 2) * b]

            qa = q_ref[rows, (2 * g) * LANES:(2 * g + 1) * LANES]
            qb = q_ref[rows, (2 * g + 1) * LANES:(2 * g + 2) * LANES]
            zero = jnp.zeros_like(qa)
            q4 = jnp.concatenate([jnp.where(even_head, qa, zero), jnp.where(even_head, zero, qa),
                                  jnp.where(even_head, qb, zero), jnp.where(even_head, zero, qb)],
                                 axis=0)
            k = jnp.concatenate([k_prev, kc_ref[rows, ls], k_next], axis=0)
            st = lax.dot_general(k, q4, _NT, preferred_element_type=F32)
            s_prev = st[0:b] + bias_prev
            s_cur = st[b:2 * b]
            s_next = st[2 * b:3 * b] + bias_next
            sink = jnp.full((1, nq), sink_ref[group * g] * _LOG2E, F32)
            for h in range(1, group):
                sink = jnp.where(head_of_col == h, sink_ref[group * g + h] * _LOG2E, sink)
            m = jnp.maximum(jnp.maximum(jnp.max(s_prev, axis=0, keepdims=True),
                                        jnp.max(s_cur, axis=0, keepdims=True)),
                            jnp.maximum(jnp.max(s_next, axis=0, keepdims=True), sink))
            p = jnp.concatenate([jnp.exp2(s_prev - m).astype(BF16), jnp.exp2(s_cur - m).astype(BF16),
                                 jnp.exp2(s_next - m).astype(BF16)], axis=0)
            vt = jnp.concatenate([v_prev, vc_ref[ls, rows], v_next], axis=1)
            ot = jnp.dot(jnp.concatenate([vt, ones], axis=0), p, preferred_element_type=F32)
            denom = ot[2 * hd:2 * hd + 1] + jnp.exp2(sink - m)
            on = ot[0:2 * hd] * (1.0 / denom)
            pair_a = jnp.concatenate([on[0:hd, 0:b], on[hd:2 * hd, b:2 * b]], axis=0)
            pair_b = jnp.concatenate([on[0:hd, 2 * b:3 * b], on[hd:2 * hd, 3 * b:4 * b]], axis=0)
            o_ref[rows, (2 * g) * LANES:(2 * g + 1) * LANES] = pair_a.T.astype(BF16)
            o_ref[rows, (2 * g + 1) * LANES:(2 * g + 2) * LANES] = pair_b.T.astype(BF16)


def _attention(q, kd, vt, sink, geo):
    T = q.shape[0]
    nsub = _ATTN_STEP_BLOCKS
    nb = T // ATTN_BLOCK
    step_rows = nsub * ATTN_BLOCK
    prev = lambda i: jnp.maximum(i * nsub - 1, 0)
    nxt = lambda i: jnp.minimum((i + 1) * nsub, nb - 1)
    k_edge = lambda m: pl.BlockSpec((ATTN_BLOCK, KV_DUP_COLS), lambda i: (m(i), 0))
    v_edge = lambda m: pl.BlockSpec((KV_DUP_COLS, ATTN_BLOCK), lambda i: (0, m(i)))
    return pl.pallas_call(
        functools.partial(_attn_body, geo),
        grid=(T // step_rows,),
        in_specs=[
            pl.BlockSpec(memory_space=pltpu.SMEM),
            pl.BlockSpec((step_rows, Q_COLS), lambda i: (i, 0)),
            k_edge(prev), pl.BlockSpec((step_rows, KV_DUP_COLS), lambda i: (i, 0)), k_edge(nxt),
            v_edge(prev), pl.BlockSpec((KV_DUP_COLS, step_rows), lambda i: (0, i)), v_edge(nxt),
        ],
        out_specs=pl.BlockSpec((step_rows, Q_COLS), lambda i: (i, 0)),
        out_shape=jax.ShapeDtypeStruct((T, Q_COLS), BF16),
        compiler_params=_params("parallel"),
        name="window_attn",
    )(sink, q, kd, kd, kd, vt, vt, vt)


_CONV_ROWS = 64
_LN_ROWS = 32


def _conv_body(geo, tc, z_ref, zp_ref, zn_ref, w_ref, dwb_ref, lng_ref, lnb_ref, o_ref,
               buf_ref, y_ref):
    i = pl.program_id(0)
    r0 = i * tc
    start, end = _seq_bounds(geo, r0)
    has_prev = (r0 > start).astype(F32)
    has_next = (r0 + tc < end).astype(F32)
    rows = tc + 2 * HALO
    buf_ref[:, 0:HALO, :] = zp_ref[...] * has_prev
    buf_ref[:, HALO:HALO + tc, :] = z_ref[...]
    buf_ref[:, HALO + tc:rows, :] = zn_ref[...] * has_next

    rep = _CONV_ROWS // SUBLANES

    def conv_chunk(j):
        s0 = pl.multiple_of(j * _CONV_ROWS, _CONV_ROWS)
        for c in range(N_SLAB):
            ls = slice(c * LANES, (c + 1) * LANES)
            acc = jnp.zeros((_CONV_ROWS, LANES), F32)
            for k in range(CONV_WIDTH):
                zt = buf_ref[c, pl.ds(s0 + (HALO - CONV_PAD + k), _CONV_ROWS, stride=1), :]
                wt = w_ref[k * SUBLANES:(k + 1) * SUBLANES, ls]
                acc = acc + zt * jnp.concatenate([wt] * rep, axis=0)
            y_ref[pl.ds(s0, _CONV_ROWS), ls] = acc

    def ln_chunk(j):
        for h in range(_CONV_ROWS // _LN_ROWS):
            s0 = pl.multiple_of(j * _CONV_ROWS + h * _LN_ROWS, _LN_ROWS)
            y = y_ref[pl.ds(s0, _LN_ROWS), :] + dwb_ref[...]
            mu = jnp.mean(y, axis=-1, keepdims=True)
            yc = y - mu
            var = jnp.mean(yc * yc, axis=-1, keepdims=True)
            yn = yc * lax.rsqrt(var + NORM_EPS) * lng_ref[...] + lnb_ref[...]
            o_ref[pl.ds(s0, _LN_ROWS), :] = (yn * jax.nn.sigmoid(yn)).astype(BF16)

    n = tc // _CONV_ROWS
    conv_chunk(0)

    def body(j, carry):
        ln_chunk(j - 1)
        conv_chunk(j)
        return carry

    lax.fori_loop(1, n, body, 0)
    ln_chunk(n - 1)


def _conv_branch(glu, w_rep, dw_b, ln_g, ln_b, geo, tc):
    T = glu.shape[1]
    nh = T // HALO
    per = tc // HALO
    const = lambda i: (0, 0)
    return pl.pallas_call(
        functools.partial(_conv_body, geo, tc),
        grid=(T // tc,),
        in_specs=[
            pl.BlockSpec((N_SLAB, tc, LANES), lambda i: (0, i, 0)),
            pl.BlockSpec((N_SLAB, HALO, LANES), lambda i: (0, jnp.maximum(i * per - 1, 0), 0)),
            pl.BlockSpec((N_SLAB, HALO, LANES), lambda i: (0, jnp.minimum((i + 1) * per, nh - 1), 0)),
            pl.BlockSpec((CONV_WIDTH * SUBLANES, D_MODEL), const),
            pl.BlockSpec((1, D_MODEL), const),
            pl.BlockSpec((1, D_MODEL), const),
            pl.BlockSpec((1, D_MODEL), const),
        ],
        out_specs=pl.BlockSpec((tc, D_MODEL), lambda i: (i, 0)),
        out_shape=jax.ShapeDtypeStruct((T, D_MODEL), BF16),
        scratch_shapes=[pltpu.VMEM((N_SLAB, tc + 2 * HALO, LANES), F32),
                        pltpu.VMEM((tc, D_MODEL), F32)],
        compiler_params=_params("parallel"),
        name="conv_branch",
    )(glu, glu, glu, w_rep, dw_b, ln_g, ln_b)


_MIX_COLS = 256


def _mix_body(n_a, xa_ref, xb_ref, o_ref, c_ref, gate_ref, wo_ref, wpw_ref, bpw_ref, wout_ref,
              gffn_ref, wrh_ref, wrl_ref, br_ref, tri_ref,
              x1_ref, h2_ref, idx_ref, gt_ref, rank_ref, cnt_ref, x1_scr, mix_ref, carry_ref):
    tm = xa_ref.shape[0]
    i = pl.program_id(0)
    last = pl.num_programs(0) - 2
    slot = i % 2

    @pl.when(i == 0)
    def _():
        carry_ref[...] = jnp.zeros_like(carry_ref)
        x1_scr[...] = jnp.zeros_like(x1_scr)

    x1_prev = x1_scr[1 - slot]
    x1_new = x1_scr.at[slot]
    in_a = jnp.minimum(i, last) < n_a
    n_pieces = D_MODEL // _MIX_COLS

    def mix_piece(p):
        cs = slice(p * _MIX_COLS, (p + 1) * _MIX_COLS)
        gs = slice(D_MODEL + p * _MIX_COLS, D_MODEL + (p + 1) * _MIX_COLS)
        attn = jnp.dot(o_ref[...], wo_ref[:, cs], preferred_element_type=F32)
        conv = jnp.dot(c_ref[...], wpw_ref[:, cs], preferred_element_type=F32) + bpw_ref[:, cs]
        mix_ref[:, cs] = (gate_ref[:, cs].astype(F32) * attn
                          + gate_ref[:, gs].astype(F32) * conv).astype(BF16)

    def out_piece(p):
        cs = slice(p * _MIX_COLS, (p + 1) * _MIX_COLS)
        x = jnp.where(in_a, xa_ref[:, cs], xb_ref[:, cs])
        x1 = x + jnp.dot(mix_ref[...], wout_ref[:, cs], preferred_element_type=F32)
        x1_ref[:, cs] = x1
        x1_new[:, cs] = x1

    mix_piece(0)
    ms = jnp.mean(x1_prev * x1_prev, axis=-1, keepdims=True)
    h2 = x1_prev * lax.rsqrt(ms + NORM_EPS) * gffn_ref[...]
    h_hi = h2.astype(BF16)
    h2_ref[...] = _pack_halves(h_hi)
    mix_piece(1)

    h_lo = (h2 - h_hi.astype(F32)).astype(BF16)
    logits = (lax.dot_general(wrh_ref[...], h_hi, _NT, preferred_element_type=F32)
              + lax.dot_general(wrh_ref[...], h_lo, _NT, preferred_element_type=F32)
              + lax.dot_general(wrl_ref[...], h_hi, _NT, preferred_element_type=F32)
              + br_ref[...])
    for p in range(2, n_pieces):
        mix_piece(p)
    out_piece(0)

    eidx = lax.broadcasted_iota(I32, (N_EXPERTS, tm), 0)
    vals = logits
    picked, top_vals, top_idx = [], [], []
    for _ in range(TOP_K):
        m = jnp.max(vals, axis=0, keepdims=True)
        idx = jnp.min(jnp.where(vals == m, eidx, N_EXPERTS), axis=0, keepdims=True)
        sel = eidx == idx
        vals = jnp.where(sel, -jnp.inf, vals)
        picked.append(sel)
        top_vals.append(m)
        top_idx.append(idx)
    out_piece(1)

    exps = [jnp.exp(v - top_vals[0]) for v in top_vals]
    tot = exps[0] + exps[1] + exps[2] + exps[3]
    onehot = (picked[0] | picked[1] | picked[2] | picked[3])
    prefix = jnp.dot(onehot.astype(BF16), tri_ref[...], preferred_element_type=F32) + carry_ref[...]
    for p in range(2, n_pieces):
        out_piece(p)
    for j in range(TOP_K):
        idx_ref[j:j + 1, :] = top_idx[j]
        gt_ref[j:j + 1, :] = exps[j] / tot
        rank_ref[j:j + 1, :] = jnp.sum(jnp.where(picked[j], prefix, 0.0), axis=0,
                                       keepdims=True).astype(I32)
    counted = jnp.where(i > 0, jnp.sum(onehot.astype(F32), axis=1, keepdims=True), 0.0)
    carry_ref[...] = carry_ref[...] + counted
    cnt_ref[...] = jnp.broadcast_to(carry_ref[...], cnt_ref.shape)


def _mix_route(xa, xb, o, c, gates, wo, wpw, bpw, wout, gffn, wr_hi, wr_lo, br, tri, geo, tm):
    T = geo.total
    n = T // tm
    n_a = geo.rows_a // tm
    head = lambda i: jnp.minimum(i, n - 1)
    tail = lambda i: jnp.maximum(i - 1, 0)
    row = lambda i: (head(i), 0)
    col = lambda i: (0, tail(i))
    const = lambda i: (0, 0)
    wspec = lambda shape: pl.BlockSpec(shape, const, pipeline_mode=pl.Buffered(1))
    x_specs = [pl.BlockSpec((tm, D_MODEL), lambda i: (jnp.minimum(head(i), n_a - 1), 0)),
               pl.BlockSpec((tm, D_MODEL), lambda i: (jnp.maximum(head(i) - n_a, 0), 0))]
    return pl.pallas_call(
        functools.partial(_mix_body, n_a),
        grid=(n + 1,),
        in_specs=x_specs + [
            pl.BlockSpec((tm, Q_COLS), row),
            pl.BlockSpec((tm, D_MODEL), row),
            pl.BlockSpec((tm, 2 * D_MODEL), row),
            wspec((Q_COLS, D_MODEL)),
            wspec((D_MODEL, D_MODEL)),
            pl.BlockSpec((1, D_MODEL), const),
            wspec((D_MODEL, D_MODEL)),
            pl.BlockSpec((1, D_MODEL), const),
            pl.BlockSpec((N_EXPERTS, D_MODEL), const),
            pl.BlockSpec((N_EXPERTS, D_MODEL), const),
            pl.BlockSpec((N_EXPERTS, 1), const),
            pl.BlockSpec((tm, tm), const),
        ],
        out_specs=[
            pl.BlockSpec((tm, D_MODEL), row),
            pl.BlockSpec((tm, PACKED), lambda i: (tail(i), 0)),
            pl.BlockSpec((TOP_K, tm), col),
            pl.BlockSpec((TOP_K, tm), col),
            pl.BlockSpec((TOP_K, tm), col),
            pl.BlockSpec((N_EXPERTS, LANES), const),
        ],
        out_shape=[
            jax.ShapeDtypeStruct((T, D_MODEL), F32),
            jax.ShapeDtypeStruct((T, PACKED), I32),
            jax.ShapeDtypeStruct((TOP_K, T), I32),
            jax.ShapeDtypeStruct((TOP_K, T), F32),
            jax.ShapeDtypeStruct((TOP_K, T), I32),
            jax.ShapeDtypeStruct((N_EXPERTS, LANES), F32),
        ],
        scratch_shapes=[pltpu.VMEM((2, tm, D_MODEL), F32), pltpu.VMEM((tm, D_MODEL), BF16),
                        pltpu.VMEM((N_EXPERTS, 1), F32)],
        compiler_params=_params("arbitrary"),
        name="mix_route",
    )(xa, xb, o, c, gates, wo, wpw, bpw, wout, gffn, wr_hi, wr_lo, br, tri)


def _sc_workers():
    info = plsc.get_sparse_core_info()
    return info.num_cores, info.num_cores * info.num_subcores


def _sc_dispatch(h, dest, n_out):
    t_rows, width = h.shape
    nc, nw = _sc_workers()
    per_w = t_rows // nw
    step = SC_ROWS_PER_STEP
    assert per_w * nw == t_rows and per_w % step == 0
    mesh = plsc.VectorSubcoreMesh(core_axis_name="c", subcore_axis_name="s")

    @functools.partial(
        pl.kernel, mesh=mesh,
        out_type=jax.ShapeDtypeStruct((n_out, width), h.dtype),
        scratch_types=[pltpu.VMEM((step,), I32)] * TOP_K + [pltpu.VMEM((step, width), h.dtype)],
    )
    def scatter_rows(h_hbm, dest_hbm, out_hbm, i0, i1, i2, i3, rows_v):
        base = (lax.axis_index("s") * nc + lax.axis_index("c")) * per_w

        @pl.loop(0, per_w // step)
        def _(i):
            off = base + i * step
            pltpu.sync_copy(h_hbm.at[pl.ds(off, step)], rows_v)
            for j, idx_v in enumerate((i0, i1, i2, i3)):
                pltpu.sync_copy(dest_hbm.at[j, pl.ds(off, step)], idx_v)
                pltpu.sync_copy(rows_v, out_hbm.at[idx_v])

    return scatter_rows(h, dest)


def _sc_gather(table, dest, row0, rows):
    width = table.shape[1]
    nc, nw = _sc_workers()
    per_w = rows // nw
    step = SC_ROWS_PER_STEP
    assert per_w * nw == rows and per_w % step == 0
    mesh = plsc.VectorSubcoreMesh(core_axis_name="c", subcore_axis_name="s")

    @functools.partial(
        pl.kernel, mesh=mesh,
        out_type=jax.ShapeDtypeStruct((TOP_K * rows, width), table.dtype),
        scratch_types=[pltpu.VMEM((step,), I32), pltpu.VMEM((step, width), table.dtype),
                       pltpu.SemaphoreType.DMA],
    )
    def gather_rows(table_hbm, dest_hbm, out_hbm, idx_v, rows_v, sem):
        base = (lax.axis_index("s") * nc + lax.axis_index("c")) * per_w

        @pl.loop(0, per_w // step)
        def _(i):
            off = base + i * step
            for j in range(TOP_K):
                pltpu.sync_copy(dest_hbm.at[j, pl.ds(row0 + off, step)], idx_v)
                pltpu.async_copy(table_hbm.at[idx_v], rows_v, sem).wait()
                pltpu.sync_copy(rows_v, out_hbm.at[pl.ds(j * rows + off, step)])

    return gather_rows(table, dest)


def _expert_body(bexp_ref, nused_ref, valid_ref, xs_ref, wgu_ref, bgu_ref, wd_ref, bd_ref, o_ref,
                 wgu_bf, wd_bf):
    n = pl.program_id(0)
    active = n < nused_ref[0]
    new_expert = (n == 0) | (bexp_ref[n] != bexp_ref[jnp.maximum(n - 1, 0)])

    @pl.when(active & new_expert)
    def _():
        wgu_bf[...] = wgu_ref[...].astype(BF16)
        wd_bf[...] = wd_ref[...].astype(BF16)

    sub = EXPERT_SUB_ROWS
    for r0 in range(0, xs_ref.shape[0], sub):
        rs = slice(r0, r0 + sub)
        live = active & (valid_ref[n] > r0)

        @pl.when(live)
        def _():
            row = r0 + lax.broadcasted_iota(I32, (sub, 1), 0)
            x = _unpack_halves(jnp.where(row < valid_ref[n], xs_ref[rs, :], 0), BF16)
            gu = jnp.dot(x, wgu_bf[...], preferred_element_type=F32) + bgu_ref[...]
            g = jnp.minimum(gu[:, 0:D_FF], SWIGLU_LIMIT)
            u = jnp.clip(gu[:, D_FF:2 * D_FF], -SWIGLU_LIMIT, SWIGLU_LIMIT)
            act = g * jax.nn.sigmoid(SWIGLU_ALPHA * g) * (u + 1.0)
            out = jnp.dot(act.astype(BF16), wd_bf[...], preferred_element_type=F32) + bd_ref[...]
            o_ref[rs, :] = _pack_halves(out.astype(BF16))

        @pl.when(jnp.logical_not(live))
        def _():
            o_ref[rs, :] = jnp.zeros((sub, o_ref.shape[1]), o_ref.dtype)


def _experts(block_exp, n_used, block_valid, xs, wgu, bgu, wd, bd, bm):
    P = xs.shape[0]
    nblk = P // bm
    xmap = lambda n, be, nu, bv: (jnp.minimum(n, nu[0] - 1), 0)
    emap = lambda n, be, nu, bv: (be[n], 0, 0)
    grid_spec = pltpu.PrefetchScalarGridSpec(
        num_scalar_prefetch=3,
        grid=(nblk,),
        in_specs=[
            pl.BlockSpec((bm, PACKED), xmap),
            pl.BlockSpec((None, D_MODEL, 2 * D_FF), emap),
            pl.BlockSpec((None, 1, 2 * D_FF), emap),
            pl.BlockSpec((None, D_FF, D_MODEL), emap),
            pl.BlockSpec((None, 1, D_MODEL), emap),
        ],
        out_specs=pl.BlockSpec((bm, PACKED), lambda n, be, nu, bv: (n, 0)),
        scratch_shapes=[pltpu.VMEM((D_MODEL, 2 * D_FF), BF16), pltpu.VMEM((D_FF, D_MODEL), BF16)],
    )
    return pl.pallas_call(
        _expert_body,
        grid_spec=grid_spec,
        out_shape=jax.ShapeDtypeStruct((P, PACKED), I32),
        compiler_params=_params("arbitrary"),
        name="experts",
    )(block_exp, n_used, block_valid, xs, wgu, bgu, wd, bd)


def _combine_body(x1_ref, g0_ref, g1_ref, g2_ref, g3_ref, gate_ref, gfin_ref, y_ref):
    tf = x1_ref.shape[0]
    gate_rows = jnp.concatenate(
        [gate_ref[...], jnp.zeros((LANES - TOP_K, tf), F32)], axis=0)
    for c in range(tf // LANES):
        rs = slice(c * LANES, (c + 1) * LANES)
        gate = gate_rows[:, rs].T
        y = x1_ref[rs, :]
        for j, g_ref in enumerate((g0_ref, g1_ref, g2_ref, g3_ref)):
            y = y + gate[:, j:j + 1] * _unpack_halves(g_ref[rs, :], F32)
        ms = jnp.mean(y * y, axis=-1, keepdims=True)
        y_ref[rs, :] = y * lax.rsqrt(ms + NORM_EPS) * gfin_ref[...]


def _combine(x1, gathered, gate_t, gfin, row0, rows, tf):
    nt = rows // tf
    t0 = row0 // tf
    row = lambda i: (t0 + i, 0)
    choice = lambda j: pl.BlockSpec((tf, PACKED), lambda i: (j * nt + i, 0))
    return pl.pallas_call(
        _combine_body,
        grid=(nt,),
        in_specs=[pl.BlockSpec((tf, D_MODEL), row)] + [choice(j) for j in range(TOP_K)] + [
            pl.BlockSpec((TOP_K, tf), lambda i: (0, t0 + i)),
            pl.BlockSpec((1, D_MODEL), lambda i: (0, 0)),
        ],
        out_specs=pl.BlockSpec((tf, D_MODEL), lambda i: (i, 0)),
        out_shape=jax.ShapeDtypeStruct((rows, D_MODEL), F32),
        compiler_params=_params("parallel"),
        name="combine",
    )(x1, gathered, gathered, gathered, gathered, gate_t, gfin)


def _permute_in_proj(w):
    lead = w.shape[:-1]
    half = HEAD_DIM // 2
    q = w[..., :Q_COLS].reshape(*lead, N_Q_HEADS // 2, 2, 2, half)
    q = jnp.swapaxes(q, -3, -2).reshape(*lead, Q_COLS)
    k = w[..., Q_COLS:Q_COLS + KV_COLS].reshape(*lead, N_KV_HEADS, 2, 1, half)
    k = jnp.broadcast_to(k, (*lead, N_KV_HEADS, 2, 2, half)).reshape(*lead, KV_DUP_COLS)
    v = w[..., Q_COLS + KV_COLS:Q_COLS + 2 * KV_COLS].reshape(*lead, N_KV_HEADS, 1, HEAD_DIM)
    v = jnp.broadcast_to(v, (*lead, N_KV_HEADS, 2, HEAD_DIM)).reshape(*lead, KV_DUP_COLS)
    return jnp.concatenate([q, k, w[..., Q_COLS + 2 * KV_COLS:]], axis=-1), v


def _rope_tables(n_pos):
    half = HEAD_DIM // 2
    inv_freq = 1.0 / (ROPE_THETA ** (jnp.arange(half, dtype=F32) * (2.0 / HEAD_DIM)))
    ang = jnp.arange(n_pos, dtype=F32)[:, None] * inv_freq[None, :]
    cos = jnp.tile(jnp.cos(ang), (1, LANES // half))
    sin = jnp.tile(jnp.sin(ang), (1, LANES // half))
    sign = jnp.where(jnp.arange(LANES) < LANES // 2, -1.0, 1.0).astype(F32)
    return cos, sin * sign[None, :]


def _tiles(geo):
    unit = min(geo.len_a, geo.len_b)
    tile = min(512, unit)
    return dict(tm=tile, tc=tile, tf=tile, bm=2 * EXPERT_SUB_ROWS)


def kernel(x_prompt, x_sample, norm_mix_g, w_in, b_in, attn_sink, w_o_attn, conv_dw_w, conv_dw_b,
           conv_ln_g, conv_ln_b, w_pw2, b_pw2, w_out, norm_ffn_g, w_router, b_router, w_gu, b_gu,
           w_down, b_down, norm_final_g):
    assert w_in.shape[0] == 1, "single trunk layer"
    geo = Geo(x_prompt.shape[0], x_prompt.shape[1], x_sample.shape[0], x_sample.shape[1])
    T = geo.total
    ts = _tiles(geo)
    xa = x_prompt.reshape(-1, D_MODEL)
    xb = x_sample.reshape(-1, D_MODEL)

    w_perm, w_v = _permute_in_proj(w_in[0])
    b_perm, b_v = _permute_in_proj(b_in)
    cos_t, sin_t = _rope_tables(max(geo.len_a, geo.len_b))

    q, kd, vt, glu, gates = _in_proj(xa, xb, norm_mix_g, w_perm.astype(BF16), b_perm,
                                     w_v.T.astype(BF16), b_v.T, cos_t, sin_t, geo, ts["tm"])
    attn = _attention(q, kd, vt, attn_sink[0], geo)
    w_rep = jnp.repeat(conv_dw_w[0], SUBLANES, axis=0)
    conv = _conv_branch(glu, w_rep, conv_dw_b, conv_ln_g, conv_ln_b, geo, ts["tc"])

    wr_t = w_router[0].T
    wr_hi = wr_t.astype(BF16)
    wr_lo = (wr_t - wr_hi.astype(F32)).astype(BF16)
    tri = jnp.triu(jnp.ones((ts["tm"], ts["tm"]), BF16), 1)
    x1, h2, idx, gate_t, rank, counts = _mix_route(
        xa, xb, attn, conv, gates, w_o_attn[0].astype(BF16), w_pw2[0].astype(BF16), b_pw2,
        w_out[0].astype(BF16), norm_ffn_g, wr_hi, wr_lo, b_router[0][:, None], tri, geo, ts["tm"])

    bm = ts["bm"]
    n_blocks = (T * TOP_K) // bm + N_EXPERTS
    cnt = counts[:, 0].astype(I32)
    padded = ((cnt + bm - 1) // bm) * bm
    pad_end = jnp.cumsum(padded)
    pad_start = pad_end - padded
    expert_ids = jnp.arange(N_EXPERTS, dtype=I32)
    dest = rank + jnp.sum(jnp.where(idx[None] == expert_ids[:, None, None],
                                    pad_start[:, None, None], 0), axis=0)
    block_start = jnp.arange(n_blocks, dtype=I32) * bm
    block_exp = jnp.minimum(jnp.sum((pad_end[None, :] <= block_start[:, None]).astype(I32), axis=1),
                            N_EXPERTS - 1)
    n_used = (pad_end[-1:] // bm).astype(I32)
    seg_end = jnp.sum(jnp.where(block_exp[:, None] == expert_ids[None, :],
                                (pad_start + cnt)[None, :], 0), axis=1)
    block_valid = jnp.clip(seg_end - block_start, 0, bm).astype(I32)

    xs = _sc_dispatch(h2, dest, n_blocks * bm)
    ys = _experts(block_exp, n_used, block_valid, xs, w_gu[0], b_gu[0][:, None, :],
                  w_down[0], b_down[0][:, None, :], bm)
    outs = []
    for row0, rows in ((0, geo.rows_a), (geo.rows_a, T - geo.rows_a)):
        gathered = _sc_gather(ys, dest, row0, rows)
        outs.append(_combine(x1, gathered, gate_t, norm_final_g[None, :], row0, rows, ts["tf"]))
    return (outs[0].reshape(x_prompt.shape), outs[1].reshape(x_sample.shape))
```

```python
import functools
import math
from typing import NamedTuple

import jax
import jax.numpy as jnp
from jax import lax
from jax.experimental import pallas as pl
from jax.experimental.pallas import tpu as pltpu
from jax.experimental.pallas import tpu_sc as plsc

F32 = jnp.float32
BF16 = jnp.bfloat16
I32 = jnp.int32

D_MODEL = 1024
HEAD_DIM = 64
N_Q_HEADS = 16
N_KV_HEADS = 4
WINDOW = 128
ATTN_BLOCK = 128
ROPE_THETA = 10000.0
CONV_WIDTH = 31
CONV_PAD = CONV_WIDTH // 2
N_EXPERTS = 32
TOP_K = 4
D_FF = D_MODEL
SWIGLU_LIMIT = 7.0
SWIGLU_ALPHA = 1.702
NORM_EPS = 1e-5
NEG_INF = -1e30

Q_COLS = N_Q_HEADS * HEAD_DIM
KV_COLS = N_KV_HEADS * HEAD_DIM
LANES = 128
SUBLANES = 8
KV_DUP_COLS = N_KV_HEADS * LANES
HALO = 16
VMEM_LIMIT = 56 * 1024 * 1024
SC_ROWS_PER_STEP = 128
EXPERT_SUB_ROWS = 512


class Geo(NamedTuple):
    n_a: int
    len_a: int
    n_b: int
    len_b: int

    @property
    def rows_a(self):
        return self.n_a * self.len_a

    @property
    def total(self):
        return self.rows_a + self.n_b * self.len_b


def _seq_bounds(geo, r):
    in_a = r < geo.rows_a
    start_a = (r // geo.len_a) * geo.len_a
    start_b = geo.rows_a + ((r - geo.rows_a) // geo.len_b) * geo.len_b
    start = jnp.where(in_a, start_a, start_b)
    end = start + jnp.where(in_a, geo.len_a, geo.len_b)
    return start, end


def _params(*sem):
    return pltpu.CompilerParams(dimension_semantics=sem, vmem_limit_bytes=VMEM_LIMIT)


PACKED = D_MODEL // 2


def _pack_halves(x_bf16):
    hi = lax.bitcast_convert_type(x_bf16[:, :PACKED].astype(F32), I32)
    lo = lax.bitcast_convert_type(x_bf16[:, PACKED:].astype(F32), I32)
    return hi | lax.shift_right_logical(lo, 16)


def _unpack_halves(words, dtype):
    hi = lax.bitcast_convert_type(words & jnp.int32(-65536), F32)
    lo = lax.bitcast_convert_type(lax.shift_left(words, 16), F32)
    return jnp.concatenate([hi.astype(dtype), lo.astype(dtype)], axis=1)


def _two_group_specs(geo, tile, width):
    n_a = geo.rows_a // tile
    return n_a, [pl.BlockSpec((tile, width), lambda i: (jnp.minimum(i, n_a - 1), 0)),
                 pl.BlockSpec((tile, width), lambda i: (jnp.maximum(i - n_a, 0), 0))]


_C_Q = 0
_C_K = _C_Q + Q_COLS
_C_GA = _C_K + KV_DUP_COLS
_C_GG = _C_GA + D_MODEL
_C_GATE = _C_GG + D_MODEL
_N_IN = _C_GATE + 2 * D_MODEL
_PROJ_CHUNK = 512
N_SLAB = D_MODEL // LANES
_NT = (((1,), (1,)), ((), ()))
_LOG2E = math.log2(math.e)
_Q_SCALE = HEAD_DIM ** -0.5 * _LOG2E


def _in_proj_body(n_a, xa_ref, xb_ref, g_ref, w_ref, b_ref, wvt_ref, bvt_ref, cos_ref, sin_ref,
                  q_ref, kd_ref, vt_ref, glu_ref, gate_ref):
    x = jnp.where(pl.program_id(0) < n_a, xa_ref[...], xb_ref[...])
    ms = jnp.mean(x * x, axis=-1, keepdims=True)
    h = (x * lax.rsqrt(ms + NORM_EPS) * g_ref[...]).astype(BF16)
    cos = cos_ref[...]
    sin = sin_ref[...]

    def proj(c0):
        return (jnp.dot(h, w_ref[:, c0:c0 + _PROJ_CHUNK], preferred_element_type=F32)
                + b_ref[:, c0:c0 + _PROJ_CHUNK])

    def rope_store(z, out_ref, o0, scale):
        for c in range(_PROJ_CHUNK // LANES):
            zc = z[:, c * LANES:(c + 1) * LANES]
            r = zc * cos + pltpu.roll(zc, LANES // 2, 1) * sin
            if scale != 1.0:
                r = r * scale
            out_ref[:, o0 + c * LANES:o0 + (c + 1) * LANES] = r.astype(out_ref.dtype)

    for c in range(Q_COLS // _PROJ_CHUNK):
        rope_store(proj(_C_Q + c * _PROJ_CHUNK), q_ref, c * _PROJ_CHUNK, _Q_SCALE)
    for c in range(KV_DUP_COLS // _PROJ_CHUNK):
        rope_store(proj(_C_K + c * _PROJ_CHUNK), kd_ref, c * _PROJ_CHUNK, 1.0)
    vt = lax.dot_general(wvt_ref[...], h, _NT, preferred_element_type=F32) + bvt_ref[...]
    vt_ref[...] = vt.astype(BF16)
    per = _PROJ_CHUNK // LANES
    for c in range(D_MODEL // _PROJ_CHUNK):
        a = proj(_C_GA + c * _PROJ_CHUNK)
        g = proj(_C_GG + c * _PROJ_CHUNK)
        glu = a * jax.nn.sigmoid(g)
        for s in range(per):
            glu_ref[c * per + s] = glu[:, s * LANES:(s + 1) * LANES]
    for c in range(2 * D_MODEL // _PROJ_CHUNK):
        gate_ref[:, c * _PROJ_CHUNK:(c + 1) * _PROJ_CHUNK] = jax.nn.sigmoid(
            proj(_C_GATE + c * _PROJ_CHUNK)).astype(BF16)


def _in_proj(xa, xb, g_mix, w_perm, b_perm, wvt, bvt, cos_t, sin_t, geo, tm):
    T = geo.total

    def pos_map(i):
        r0 = i * tm
        start, _ = _seq_bounds(geo, r0)
        return ((r0 - start) // tm, 0)

    const = lambda i: (0, 0)
    row = lambda i: (i, 0)
    n_a, x_specs = _two_group_specs(geo, tm, D_MODEL)
    return pl.pallas_call(
        functools.partial(_in_proj_body, n_a),
        grid=(T // tm,),
        in_specs=x_specs + [
            pl.BlockSpec((1, D_MODEL), const),
            pl.BlockSpec((D_MODEL, _N_IN), const, pipeline_mode=pl.Buffered(1)),
            pl.BlockSpec((1, _N_IN), const),
            pl.BlockSpec((KV_DUP_COLS, D_MODEL), const),
            pl.BlockSpec((KV_DUP_COLS, 1), const),
            pl.BlockSpec((tm, LANES), pos_map),
            pl.BlockSpec((tm, LANES), pos_map),
        ],
        out_specs=[
            pl.BlockSpec((tm, Q_COLS), row),
            pl.BlockSpec((tm, KV_DUP_COLS), row),
            pl.BlockSpec((KV_DUP_COLS, tm), lambda i: (0, i)),
            pl.BlockSpec((N_SLAB, tm, LANES), lambda i: (0, i, 0)),
            pl.BlockSpec((tm, 2 * D_MODEL), row),
        ],
        out_shape=[
            jax.ShapeDtypeStruct((T, Q_COLS), BF16),
            jax.ShapeDtypeStruct((T, KV_DUP_COLS), BF16),
            jax.ShapeDtypeStruct((KV_DUP_COLS, T), BF16),
            jax.ShapeDtypeStruct((N_SLAB, T, LANES), F32),
            jax.ShapeDtypeStruct((T, 2 * D_MODEL), BF16),
        ],
        compiler_params=_params("parallel"),
        name="in_proj",
    )(xa, xb, g_mix, w_perm, b_perm, wvt, bvt, cos_t, sin_t)


_ONES_ROWS = 16
_ATTN_STEP_BLOCKS = 8


def _attn_body(geo, sink_ref, q_ref, kp_ref, kc_ref, kn_ref, vp_ref, vc_ref, vn_ref, o_ref):
    group = N_Q_HEADS // N_KV_HEADS
    nq = group * ATTN_BLOCK
    b = ATTN_BLOCK
    hd = HEAD_DIM
    nsub = _ATTN_STEP_BLOCKS

    key = lax.broadcasted_iota(I32, (b, nq), 0)
    qry = lax.broadcasted_iota(I32, (b, nq), 1) % b
    head_of_col = lax.broadcasted_iota(I32, (1, nq), 1) // b
    lane = lax.broadcasted_iota(I32, (b, LANES), 1)
    even_head = (lane % hd) < (hd // 2)
    ones = jnp.ones((_ONES_ROWS, 3 * b), BF16)

    for s in range(nsub):
        rows = slice(s * b, (s + 1) * b)
        r0 = (pl.program_id(0) * nsub + s) * b
        start, end = _seq_bounds(geo, r0)
        bias_prev = jnp.where((key >= qry) & (r0 > start), 0.0, NEG_INF)
        bias_next = jnp.where((key <= qry) & (r0 + b < end), 0.0, NEG_INF)

        for g in range(N_KV_HEADS):
            ls = slice(g * LANES, (g + 1) * LANES)
            k_prev = kp_ref[:, ls] if s == 0 else kc_ref[(s - 1) * b:s * b, ls]
            k_next = kn_ref[:, ls] if s == nsub - 1 else kc_ref[(s + 1) * b:(s + 2) * b, ls]
            v_prev = vp_ref[ls, :] if s == 0 else vc_ref[ls, (s - 1) * b:s * b]
            v_next = vn_ref[ls, :] if s == nsub - 1 else vc_ref[ls, (s + 1) * b:(s + 2) * b]

            qa = q_ref[rows, (2 * g) * LANES:(2 * g + 1) * LANES]
            qb = q_ref[rows, (2 * g + 1) * LANES:(2 * g + 2) * LANES]
            zero = jnp.zeros_like(qa)
            q4 = jnp.concatenate([jnp.where(even_head, qa, zero), jnp.where(even_head, zero, qa),
                                  jnp.where(even_head, qb, zero), jnp.where(even_head, zero, qb)],
                                 axis=0)
            k = jnp.concatenate([k_prev, kc_ref[rows, ls], k_next], axis=0)
            st = lax.dot_general(k, q4, _NT, preferred_element_type=F32)
            s_prev = st[0:b] + bias_prev
            s_cur = st[b:2 * b]
            s_next = st[2 * b:3 * b] + bias_next
            sink = jnp.full((1, nq), sink_ref[group * g] * _LOG2E, F32)
            for h in range(1, group):
                sink = jnp.where(head_of_col == h, sink_ref[group * g + h] * _LOG2E, sink)
            m = jnp.maximum(jnp.maximum(jnp.max(s_prev, axis=0, keepdims=True),
                                        jnp.max(s_cur, axis=0, keepdims=True)),
                            jnp.maximum(jnp.max(s_next, axis=0, keepdims=True), sink))
            p = jnp.concatenate([jnp.exp2(s_prev - m).astype(BF16), jnp.exp2(s_cur - m).astype(BF16),
                                 jnp.exp2(s_next - m).astype(BF16)], axis=0)
            vt = jnp.concatenate([v_prev, vc_ref[ls, rows], v_next], axis=1)
            ot = jnp.dot(jnp.concatenate([vt, ones], axis=0), p, preferred_element_type=F32)
            denom = ot[2 * hd:2 * hd + 1] + jnp.exp2(sink - m)
            on = ot[0:2 * hd] * (1.0 / denom)
            pair_a = jnp.concatenate([on[0:hd, 0:b], on[hd:2 * hd, b:2 * b]], axis=0)
            pair_b = jnp.concatenate([on[0:hd, 2 * b:3 * b], on[hd:2 * hd, 3 * b:4 * b]], axis=0)
            o_ref[rows, (2 * g) * LANES:(2 * g + 1) * LANES] = pair_a.T.astype(BF16)
            o_ref[rows, (2 * g + 1) * LANES:(2 * g + 2) * LANES] = pair_b.T.astype(BF16)


def _attention(q, kd, vt, sink, geo):
    T = q.shape[0]
    nsub = _ATTN_STEP_BLOCKS
    nb = T // ATTN_BLOCK
    step_rows = nsub * ATTN_BLOCK
    prev = lambda i: jnp.maximum(i * nsub - 1, 0)
    nxt = lambda i: jnp.minimum((i + 1) * nsub, nb - 1)
    k_edge = lambda m: pl.BlockSpec((ATTN_BLOCK, KV_DUP_COLS), lambda i: (m(i), 0))
    v_edge = lambda m: pl.BlockSpec((KV_DUP_COLS, ATTN_BLOCK), lambda i: (0, m(i)))
    return pl.pallas_call(
        functools.partial(_attn_body, geo),
        grid=(T // step_rows,),
        in_specs=[
            pl.BlockSpec(memory_space=pltpu.SMEM),
            pl.BlockSpec((step_rows, Q_COLS), lambda i: (i, 0)),
            k_edge(prev), pl.BlockSpec((step_rows, KV_DUP_COLS), lambda i: (i, 0)), k_edge(nxt),
            v_edge(prev), pl.BlockSpec((KV_DUP_COLS, step_rows), lambda i: (0, i)), v_edge(nxt),
        ],
        out_specs=pl.BlockSpec((step_rows, Q_COLS), lambda i: (i, 0)),
        out_shape=jax.ShapeDtypeStruct((T, Q_COLS), BF16),
        compiler_params=_params("parallel"),
        name="window_attn",
    )(sink, q, kd, kd, kd, vt, vt, vt)


_CONV_ROWS = 64
_LN_ROWS = 32


def _conv_body(geo, tc, z_ref, zp_ref, zn_ref, w_ref, dwb_ref, lng_ref, lnb_ref, o_ref,
               buf_ref, y_ref):
    i = pl.program_id(0)
    r0 = i * tc
    start, end = _seq_bounds(geo, r0)
    has_prev = (r0 > start).astype(F32)
    has_next = (r0 + tc < end).astype(F32)
    rows = tc + 2 * HALO
    buf_ref[:, 0:HALO, :] = zp_ref[...] * has_prev
    buf_ref[:, HALO:HALO + tc, :] = z_ref[...]
    buf_ref[:, HALO + tc:rows, :] = zn_ref[...] * has_next

    rep = _CONV_ROWS // SUBLANES

    def conv_chunk(j):
        s0 = pl.multiple_of(j * _CONV_ROWS, _CONV_ROWS)
        for c in range(N_SLAB):
            ls = slice(c * LANES, (c + 1) * LANES)
            acc = jnp.zeros((_CONV_ROWS, LANES), F32)
            for k in range(CONV_WIDTH):
                zt = buf_ref[c, pl.ds(s0 + (HALO - CONV_PAD + k), _CONV_ROWS, stride=1), :]
                wt = w_ref[k * SUBLANES:(k + 1) * SUBLANES, ls]
                acc = acc + zt * jnp.concatenate([wt] * rep, axis=0)
            y_ref[pl.ds(s0, _CONV_ROWS), ls] = acc

    def ln_chunk(j):
        for h in range(_CONV_ROWS // _LN_ROWS):
            s0 = pl.multiple_of(j * _CONV_ROWS + h * _LN_ROWS, _LN_ROWS)
            y = y_ref[pl.ds(s0, _LN_ROWS), :] + dwb_ref[...]
            mu = jnp.mean(y, axis=-1, keepdims=True)
            yc = y - mu
            var = jnp.mean(yc * yc, axis=-1, keepdims=True)
            yn = yc * lax.rsqrt(var + NORM_EPS) * lng_ref[...] + lnb_ref[...]
            o_ref[pl.ds(s0, _LN_ROWS), :] = (yn * jax.nn.sigmoid(yn)).astype(BF16)

    n = tc // _CONV_ROWS
    conv_chunk(0)

    def body(j, carry):
        ln_chunk(j - 1)
        conv_chunk(j)
        return carry

    lax.fori_loop(1, n, body, 0)
    ln_chunk(n - 1)


def _conv_branch(glu, w_rep, dw_b, ln_g, ln_b, geo, tc):
    T = glu.shape[1]
    nh = T // HALO
    per = tc // HALO
    const = lambda i: (0, 0)
    return pl.pallas_call(
        functools.partial(_conv_body, geo, tc),
        grid=(T // tc,),
        in_specs=[
            pl.BlockSpec((N_SLAB, tc, LANES), lambda i: (0, i, 0)),
            pl.BlockSpec((N_SLAB, HALO, LANES), lambda i: (0, jnp.maximum(i * per - 1, 0), 0)),
            pl.BlockSpec((N_SLAB, HALO, LANES), lambda i: (0, jnp.minimum((i + 1) * per, nh - 1), 0)),
            pl.BlockSpec((CONV_WIDTH * SUBLANES, D_MODEL), const),
            pl.BlockSpec((1, D_MODEL), const),
            pl.BlockSpec((1, D_MODEL), const),
            pl.BlockSpec((1, D_MODEL), const),
        ],
        out_specs=pl.BlockSpec((tc, D_MODEL), lambda i: (i, 0)),
        out_shape=jax.ShapeDtypeStruct((T, D_MODEL), BF16),
        scratch_shapes=[pltpu.VMEM((N_SLAB, tc + 2 * HALO, LANES), F32),
                        pltpu.VMEM((tc, D_MODEL), F32)],
        compiler_params=_params("parallel"),
        name="conv_branch",
    )(glu, glu, glu, w_rep, dw_b, ln_g, ln_b)


_MIX_COLS = 256


def _mix_body(n_a, xa_ref, xb_ref, o_ref, c_ref, gate_ref, wo_ref, wpw_ref, bpw_ref, wout_ref,
              gffn_ref, wrh_ref, wrl_ref, br_ref, tri_ref,
              x1_ref, h2_ref, idx_ref, gt_ref, rank_ref, cnt_ref, x1_scr, mix_ref, carry_ref):
    tm = xa_ref.shape[0]
    i = pl.program_id(0)
    last = pl.num_programs(0) - 2
    slot = i % 2

    @pl.when(i == 0)
    def _():
        carry_ref[...] = jnp.zeros_like(carry_ref)
        x1_scr[...] = jnp.zeros_like(x1_scr)

    x1_prev = x1_scr[1 - slot]
    x1_new = x1_scr.at[slot]
    in_a = jnp.minimum(i, last) < n_a
    n_pieces = D_MODEL // _MIX_COLS

    def mix_piece(p):
        cs = slice(p * _MIX_COLS, (p + 1) * _MIX_COLS)
        gs = slice(D_MODEL + p * _MIX_COLS, D_MODEL + (p + 1) * _MIX_COLS)
        attn = jnp.dot(o_ref[...], wo_ref[:, cs], preferred_element_type=F32)
        conv = jnp.dot(c_ref[...], wpw_ref[:, cs], preferred_element_type=F32) + bpw_ref[:, cs]
        mix_ref[:, cs] = (gate_ref[:, cs].astype(F32) * attn
                          + gate_ref[:, gs].astype(F32) * conv).astype(BF16)

    def out_piece(p):
        cs = slice(p * _MIX_COLS, (p + 1) * _MIX_COLS)
        x = jnp.where(in_a, xa_ref[:, cs], xb_ref[:, cs])
        x1 = x + jnp.dot(mix_ref[...], wout_ref[:, cs], preferred_element_type=F32)
        x1_ref[:, cs] = x1
        x1_new[:, cs] = x1

    mix_piece(0)
    ms = jnp.mean(x1_prev * x1_prev, axis=-1, keepdims=True)
    h2 = x1_prev * lax.rsqrt(ms + NORM_EPS) * gffn_ref[...]
    h_hi = h2.astype(BF16)
    h2_ref[...] = _pack_halves(h_hi)
    mix_piece(1)

    h_lo = (h2 - h_hi.astype(F32)).astype(BF16)
    logits = (lax.dot_general(wrh_ref[...], h_hi, _NT, preferred_element_type=F32)
              + lax.dot_general(wrh_ref[...], h_lo, _NT, preferred_element_type=F32)
              + lax.dot_general(wrl_ref[...], h_hi, _NT, preferred_element_type=F32)
              + br_ref[...])
    for p in range(2, n_pieces):
        mix_piece(p)
    out_piece(0)

    eidx = lax.broadcasted_iota(I32, (N_EXPERTS, tm), 0)
    vals = logits
    picked, top_vals, top_idx = [], [], []
    for _ in range(TOP_K):
        m = jnp.max(vals, axis=0, keepdims=True)
        idx = jnp.min(jnp.where(vals == m, eidx, N_EXPERTS), axis=0, keepdims=True)
        sel = eidx == idx
        vals = jnp.where(sel, -jnp.inf, vals)
        picked.append(sel)
        top_vals.append(m)
        top_idx.append(idx)
    out_piece(1)

    exps = [jnp.exp(v - top_vals[0]) for v in top_vals]
    tot = exps[0] + exps[1] + exps[2] + exps[3]
    onehot = (picked[0] | picked[1] | picked[2] | picked[3])
    prefix = jnp.dot(onehot.astype(BF16), tri_ref[...], preferred_element_type=F32) + carry_ref[...]
    for p in range(2, n_pieces):
        out_piece(p)
    for j in range(TOP_K):
        idx_ref[j:j + 1, :] = top_idx[j]
        gt_ref[j:j + 1, :] = exps[j] / tot
        rank_ref[j:j + 1, :] = jnp.sum(jnp.where(picked[j], prefix, 0.0), axis=0,
                                       keepdims=True).astype(I32)
    counted = jnp.where(i > 0, jnp.sum(onehot.astype(F32), axis=1, keepdims=True), 0.0)
    carry_ref[...] = carry_ref[...] + counted
    cnt_ref[...] = jnp.broadcast_to(carry_ref[...], cnt_ref.shape)


def _mix_route(xa, xb, o, c, gates, wo, wpw, bpw, wout, gffn, wr_hi, wr_lo, br, tri, geo, tm):
    T = geo.total
    n = T // tm
    n_a = geo.rows_a // tm
    head = lambda i: jnp.minimum(i, n - 1)
    tail = lambda i: jnp.maximum(i - 1, 0)
    row = lambda i: (head(i), 0)
    col = lambda i: (0, tail(i))
    const = lambda i: (0, 0)
    wspec = lambda shape: pl.BlockSpec(shape, const, pipeline_mode=pl.Buffered(1))
    x_specs = [pl.BlockSpec((tm, D_MODEL), lambda i: (jnp.minimum(head(i), n_a - 1), 0)),
               pl.BlockSpec((tm, D_MODEL), lambda i: (jnp.maximum(head(i) - n_a, 0), 0))]
    return pl.pallas_call(
        functools.partial(_mix_body, n_a),
        grid=(n + 1,),
        in_specs=x_specs + [
            pl.BlockSpec((tm, Q_COLS), row),
            pl.BlockSpec((tm, D_MODEL), row),
            pl.BlockSpec((tm, 2 * D_MODEL), row),
            wspec((Q_COLS, D_MODEL)),
            wspec((D_MODEL, D_MODEL)),
            pl.BlockSpec((1, D_MODEL), const),
            wspec((D_MODEL, D_MODEL)),
            pl.BlockSpec((1, D_MODEL), const),
            pl.BlockSpec((N_EXPERTS, D_MODEL), const),
            pl.BlockSpec((N_EXPERTS, D_MODEL), const),
            pl.BlockSpec((N_EXPERTS, 1), const),
            pl.BlockSpec((tm, tm), const),
        ],
        out_specs=[
            pl.BlockSpec((tm, D_MODEL), row),
            pl.BlockSpec((tm, PACKED), lambda i: (tail(i), 0)),
            pl.BlockSpec((TOP_K, tm), col),
            pl.BlockSpec((TOP_K, tm), col),
            pl.BlockSpec((TOP_K, tm), col),
            pl.BlockSpec((N_EXPERTS, LANES), const),
        ],
        out_shape=[
            jax.ShapeDtypeStruct((T, D_MODEL), F32),
            jax.ShapeDtypeStruct((T, PACKED), I32),
            jax.ShapeDtypeStruct((TOP_K, T), I32),
            jax.ShapeDtypeStruct((TOP_K, T), F32),
            jax.ShapeDtypeStruct((TOP_K, T), I32),
            jax.ShapeDtypeStruct((N_EXPERTS, LANES), F32),
        ],
        scratch_shapes=[pltpu.VMEM((2, tm, D_MODEL), F32), pltpu.VMEM((tm, D_MODEL), BF16),
                        pltpu.VMEM((N_EXPERTS, 1), F32)],
        compiler_params=_params("arbitrary"),
        name="mix_route",
    )(xa, xb, o, c, gates, wo, wpw, bpw, wout, gffn, wr_hi, wr_lo, br, tri)


def _sc_workers():
    info = plsc.get_sparse_core_info()
    return info.num_cores, info.num_cores * info.num_subcores


def _sc_dispatch(h, dest, n_out):
    t_rows, width = h.shape
    nc, nw = _sc_workers()
    per_w = t_rows // nw
    step = SC_ROWS_PER_STEP
    assert per_w * nw == t_rows and per_w % step == 0
    mesh = plsc.VectorSubcoreMesh(core_axis_name="c", subcore_axis_name="s")

    @functools.partial(
        pl.kernel, mesh=mesh,
        out_type=jax.ShapeDtypeStruct((n_out, width), h.dtype),
        scratch_types=[pltpu.VMEM((step,), I32)] * TOP_K + [pltpu.VMEM((step, width), h.dtype)],
    )
    def scatter_rows(h_hbm, dest_hbm, out_hbm, i0, i1, i2, i3, rows_v):
        base = (lax.axis_index("s") * nc + lax.axis_index("c")) * per_w

        @pl.loop(0, per_w // step)
        def _(i):
            off = base + i * step
            pltpu.sync_copy(h_hbm.at[pl.ds(off, step)], rows_v)
            for j, idx_v in enumerate((i0, i1, i2, i3)):
                pltpu.sync_copy(dest_hbm.at[j, pl.ds(off, step)], idx_v)
                pltpu.sync_copy(rows_v, out_hbm.at[idx_v])

    return scatter_rows(h, dest)


def _sc_gather(table, dest, row0, rows):
    width = table.shape[1]
    nc, nw = _sc_workers()
    per_w = rows // nw
    step = SC_ROWS_PER_STEP
    assert per_w * nw == rows and per_w % step == 0
    mesh = plsc.VectorSubcoreMesh(core_axis_name="c", subcore_axis_name="s")

    @functools.partial(
        pl.kernel, mesh=mesh,
        out_type=jax.ShapeDtypeStruct((TOP_K * rows, width), table.dtype),
        scratch_types=[pltpu.VMEM((step,), I32), pltpu.VMEM((step, width), table.dtype),
                       pltpu.SemaphoreType.DMA],
    )
    def gather_rows(table_hbm, dest_hbm, out_hbm, idx_v, rows_v, sem):
        base = (lax.axis_index("s") * nc + lax.axis_index("c")) * per_w

        @pl.loop(0, per_w // step)
        def _(i):
            off = base + i * step
            for j in range(TOP_K):
                pltpu.sync_copy(dest_hbm.at[j, pl.ds(row0 + off, step)], idx_v)
                pltpu.async_copy(table_hbm.at[idx_v], rows_v, sem).wait()
                pltpu.sync_copy(rows_v, out_hbm.at[pl.ds(j * rows + off, step)])

    return gather_rows(table, dest)


def _expert_body(bexp_ref, nused_ref, valid_ref, xs_ref, wgu_ref, bgu_ref, wd_ref, bd_ref, o_ref,
                 wgu_bf, wd_bf):
    n = pl.program_id(0)
    active = n < nused_ref[0]
    new_expert = (n == 0) | (bexp_ref[n] != bexp_ref[jnp.maximum(n - 1, 0)])

    @pl.when(active & new_expert)
    def _():
        wgu_bf[...] = wgu_ref[...].astype(BF16)
        wd_bf[...] = wd_ref[...].astype(BF16)

    sub = EXPERT_SUB_ROWS
    for r0 in range(0, xs_ref.shape[0], sub):
        rs = slice(r0, r0 + sub)
        live = active & (valid_ref[n] > r0)

        @pl.when(live)
        def _():
            row = r0 + lax.broadcasted_iota(I32, (sub, 1), 0)
            x = _unpack_halves(jnp.where(row < valid_ref[n], xs_ref[rs, :], 0), BF16)
            gu = jnp.dot(x, wgu_bf[...], preferred_element_type=F32) + bgu_ref[...]
            g = jnp.minimum(gu[:, 0:D_FF], SWIGLU_LIMIT)
            u = jnp.clip(gu[:, D_FF:2 * D_FF], -SWIGLU_LIMIT, SWIGLU_LIMIT)
            act = g * jax.nn.sigmoid(SWIGLU_ALPHA * g) * (u + 1.0)
            out = jnp.dot(act.astype(BF16), wd_bf[...], preferred_element_type=F32) + bd_ref[...]
            o_ref[rs, :] = _pack_halves(out.astype(BF16))

        @pl.when(jnp.logical_not(live))
        def _():
            o_ref[rs, :] = jnp.zeros((sub, o_ref.shape[1]), o_ref.dtype)


def _experts(block_exp, n_used, block_valid, xs, wgu, bgu, wd, bd, bm):
    P = xs.shape[0]
    nblk = P // bm
    xmap = lambda n, be, nu, bv: (jnp.minimum(n, nu[0] - 1), 0)
    emap = lambda n, be, nu, bv: (be[n], 0, 0)
    grid_spec = pltpu.PrefetchScalarGridSpec(
        num_scalar_prefetch=3,
        grid=(nblk,),
        in_specs=[
            pl.BlockSpec((bm, PACKED), xmap),
            pl.BlockSpec((None, D_MODEL, 2 * D_FF), emap),
            pl.BlockSpec((None, 1, 2 * D_FF), emap),
            pl.BlockSpec((None, D_FF, D_MODEL), emap),
            pl.BlockSpec((None, 1, D_MODEL), emap),
        ],
        out_specs=pl.BlockSpec((bm, PACKED), lambda n, be, nu, bv: (n, 0)),
        scratch_shapes=[pltpu.VMEM((D_MODEL, 2 * D_FF), BF16), pltpu.VMEM((D_FF, D_MODEL), BF16)],
    )
    return pl.pallas_call(
        _expert_body,
        grid_spec=grid_spec,
        out_shape=jax.ShapeDtypeStruct((P, PACKED), I32),
        compiler_params=_params("arbitrary"),
        name="experts",
    )(block_exp, n_used, block_valid, xs, wgu, bgu, wd, bd)


def _combine_body(x1_ref, g0_ref, g1_ref, g2_ref, g3_ref, gate_ref, gfin_ref, y_ref):
    tf = x1_ref.shape[0]
    gate_rows = jnp.concatenate(
        [gate_ref[...], jnp.zeros((LANES - TOP_K, tf), F32)], axis=0)
    for c in range(tf // LANES):
        rs = slice(c * LANES, (c + 1) * LANES)
        gate = gate_rows[:, rs].T
        y = x1_ref[rs, :]
        for j, g_ref in enumerate((g0_ref, g1_ref, g2_ref, g3_ref)):
            y = y + gate[:, j:j + 1] * _unpack_halves(g_ref[rs, :], F32)
        ms = jnp.mean(y * y, axis=-1, keepdims=True)
        y_ref[rs, :] = y * lax.rsqrt(ms + NORM_EPS) * gfin_ref[...]


def _combine(x1, gathered, gate_t, gfin, row0, rows, tf):
    nt = rows // tf
    t0 = row0 // tf
    row = lambda i: (t0 + i, 0)
    choice = lambda j: pl.BlockSpec((tf, PACKED), lambda i: (j * nt + i, 0))
    return pl.pallas_call(
        _combine_body,
        grid=(nt,),
        in_specs=[pl.BlockSpec((tf, D_MODEL), row)] + [choice(j) for j in range(TOP_K)] + [
            pl.BlockSpec((TOP_K, tf), lambda i: (0, t0 + i)),
            pl.BlockSpec((1, D_MODEL), lambda i: (0, 0)),
        ],
        out_specs=pl.BlockSpec((tf, D_MODEL), lambda i: (i, 0)),
        out_shape=jax.ShapeDtypeStruct((rows, D_MODEL), F32),
        compiler_params=_params("parallel"),
        name="combine",
    )(x1, gathered, gathered, gathered, gathered, gate_t, gfin)


def _permute_in_proj(w):
    lead = w.shape[:-1]
    half = HEAD_DIM // 2
    q = w[..., :Q_COLS].reshape(*lead, N_Q_HEADS // 2, 2, 2, half)
    q = jnp.swapaxes(q, -3, -2).reshape(*lead, Q_COLS)
    k = w[..., Q_COLS:Q_COLS + KV_COLS].reshape(*lead, N_KV_HEADS, 2, 1, half)
    k = jnp.broadcast_to(k, (*lead, N_KV_HEADS, 2, 2, half)).reshape(*lead, KV_DUP_COLS)
    v = w[..., Q_COLS + KV_COLS:Q_COLS + 2 * KV_COLS].reshape(*lead, N_KV_HEADS, 1, HEAD_DIM)
    v = jnp.broadcast_to(v, (*lead, N_KV_HEADS, 2, HEAD_DIM)).reshape(*lead, KV_DUP_COLS)
    return jnp.concatenate([q, k, w[..., Q_COLS + 2 * KV_COLS:]], axis=-1), v


def _rope_tables(n_pos):
    half = HEAD_DIM // 2
    inv_freq = 1.0 / (ROPE_THETA ** (jnp.arange(half, dtype=F32) * (2.0 / HEAD_DIM)))
    ang = jnp.arange(n_pos, dtype=F32)[:, None] * inv_freq[None, :]
    cos = jnp.tile(jnp.cos(ang), (1, LANES // half))
    sin = jnp.tile(jnp.sin(ang), (1, LANES // half))
    sign = jnp.where(jnp.arange(LANES) < LANES // 2, -1.0, 1.0).astype(F32)
    return cos, sin * sign[None, :]


def _tiles(geo):
    unit = min(geo.len_a, geo.len_b)
    tile = min(512, unit)
    wide = min(1024, unit)
    return dict(tm=tile, tc=wide, tf=wide, bm=2 * EXPERT_SUB_ROWS)


def kernel(x_prompt, x_sample, norm_mix_g, w_in, b_in, attn_sink, w_o_attn, conv_dw_w, conv_dw_b,
           conv_ln_g, conv_ln_b, w_pw2, b_pw2, w_out, norm_ffn_g, w_router, b_router, w_gu, b_gu,
           w_down, b_down, norm_final_g):
    assert w_in.shape[0] == 1, "single trunk layer"
    geo = Geo(x_prompt.shape[0], x_prompt.shape[1], x_sample.shape[0], x_sample.shape[1])
    T = geo.total
    ts = _tiles(geo)
    xa = x_prompt.reshape(-1, D_MODEL)
    xb = x_sample.reshape(-1, D_MODEL)

    w_perm, w_v = _permute_in_proj(w_in[0])
    b_perm, b_v = _permute_in_proj(b_in)
    cos_t, sin_t = _rope_tables(max(geo.len_a, geo.len_b))

    q, kd, vt, glu, gates = _in_proj(xa, xb, norm_mix_g, w_perm.astype(BF16), b_perm,
                                     w_v.T.astype(BF16), b_v.T, cos_t, sin_t, geo, ts["tm"])
    attn = _attention(q, kd, vt, attn_sink[0], geo)
    w_rep = jnp.repeat(conv_dw_w[0], SUBLANES, axis=0)
    conv = _conv_branch(glu, w_rep, conv_dw_b, conv_ln_g, conv_ln_b, geo, ts["tc"])

    wr_t = w_router[0].T
    wr_hi = wr_t.astype(BF16)
    wr_lo = (wr_t - wr_hi.astype(F32)).astype(BF16)
    tri = jnp.triu(jnp.ones((ts["tm"], ts["tm"]), BF16), 1)
    x1, h2, idx, gate_t, rank, counts = _mix_route(
        xa, xb, attn, conv, gates, w_o_attn[0].astype(BF16), w_pw2[0].astype(BF16), b_pw2,
        w_out[0].astype(BF16), norm_ffn_g, wr_hi, wr_lo, b_router[0][:, None], tri, geo, ts["tm"])

    bm = ts["bm"]
    n_blocks = (T * TOP_K) // bm + N_EXPERTS
    cnt = counts[:, 0].astype(I32)
    padded = ((cnt + bm - 1) // bm) * bm
    pad_end = jnp.cumsum(padded)
    pad_start = pad_end - padded
    expert_ids = jnp.arange(N_EXPERTS, dtype=I32)
    dest = rank + jnp.sum(jnp.where(idx[None] == expert_ids[:, None, None],
                                    pad_start[:, None, None], 0), axis=0)
    block_start = jnp.arange(n_blocks, dtype=I32) * bm
    block_exp = jnp.minimum(jnp.sum((pad_end[None, :] <= block_start[:, None]).astype(I32), axis=1),
                            N_EXPERTS - 1)
    n_used = (pad_end[-1:] // bm).astype(I32)
    seg_end = jnp.sum(jnp.where(block_exp[:, None] == expert_ids[None, :],
                                (pad_start + cnt)[None, :], 0), axis=1)
    block_valid = jnp.clip(seg_end - block_start, 0, bm).astype(I32)

    xs = _sc_dispatch(h2, dest, n_blocks * bm)
    ys = _experts(block_exp, n_used, block_valid, xs, w_gu[0], b_gu[0][:, None, :],
                  w_down[0], b_down[0][:, None, :], bm)
    outs = []
    for row0, rows in ((0, geo.rows_a), (geo.rows_a, T - geo.rows_a)):
        gathered = _sc_gather(ys, dest, row0, rows)
        outs.append(_combine(x1, gathered, gate_t, norm_final_g[None, :], row0, rows, ts["tf"]))
    return (outs[0].reshape(x_prompt.shape), outs[1].reshape(x_sample.shape))
```

```python
import functools
import math
from typing import NamedTuple

import jax
import jax.numpy as jnp
from jax import lax
from jax.experimental import pallas as pl
from jax.experimental.pallas import tpu as pltpu
from jax.experimental.pallas import tpu_sc as plsc

F32 = jnp.float32
BF16 = jnp.bfloat16
I32 = jnp.int32

D_MODEL = 1024
HEAD_DIM = 64
N_Q_HEADS = 16
N_KV_HEADS = 4
WINDOW = 128
ATTN_BLOCK = 128
ROPE_THETA = 10000.0
CONV_WIDTH = 31
CONV_PAD = CONV_WIDTH // 2
N_EXPERTS = 32
TOP_K = 4
D_FF = D_MODEL
SWIGLU_LIMIT = 7.0
SWIGLU_ALPHA = 1.702
NORM_EPS = 1e-5
NEG_INF = -1e30

Q_COLS = N_Q_HEADS * HEAD_DIM
KV_COLS = N_KV_HEADS * HEAD_DIM
LANES = 128
SUBLANES = 8
KV_DUP_COLS = N_KV_HEADS * LANES
HALO = 16
VMEM_LIMIT = 56 * 1024 * 1024
SC_ROWS_PER_STEP = 128
EXPERT_SUB_ROWS = 512


class Geo(NamedTuple):
    n_a: int
    len_a: int
    n_b: int
    len_b: int

    @property
    def rows_a(self):
        return self.n_a * self.len_a

    @property
    def total(self):
        return self.rows_a + self.n_b * self.len_b


def _seq_bounds(geo, r):
    in_a = r < geo.rows_a
    start_a = (r // geo.len_a) * geo.len_a
    start_b = geo.rows_a + ((r - geo.rows_a) // geo.len_b) * geo.len_b
    start = jnp.where(in_a, start_a, start_b)
    end = start + jnp.where(in_a, geo.len_a, geo.len_b)
    return start, end


def _params(*sem):
    return pltpu.CompilerParams(dimension_semantics=sem, vmem_limit_bytes=VMEM_LIMIT)


PACKED = D_MODEL // 2


def _pack_halves(x_bf16):
    hi = lax.bitcast_convert_type(x_bf16[:, :PACKED].astype(F32), I32)
    lo = lax.bitcast_convert_type(x_bf16[:, PACKED:].astype(F32), I32)
    return hi | lax.shift_right_logical(lo, 16)


def _unpack_halves(words, dtype):
    hi = lax.bitcast_convert_type(words & jnp.int32(-65536), F32)
    lo = lax.bitcast_convert_type(lax.shift_left(words, 16), F32)
    return jnp.concatenate([hi.astype(dtype), lo.astype(dtype)], axis=1)


def _two_group_specs(geo, tile, width):
    n_a = geo.rows_a // tile
    return n_a, [pl.BlockSpec((tile, width), lambda i: (jnp.minimum(i, n_a - 1), 0)),
                 pl.BlockSpec((tile, width), lambda i: (jnp.maximum(i - n_a, 0), 0))]


_C_Q = 0
_C_K = _C_Q + Q_COLS
_C_GA = _C_K + KV_DUP_COLS
_C_GG = _C_GA + D_MODEL
_C_GATE = _C_GG + D_MODEL
_N_IN = _C_GATE + 2 * D_MODEL
_PROJ_CHUNK = 512
N_SLAB = D_MODEL // LANES
_NT = (((1,), (1,)), ((), ()))
_LOG2E = math.log2(math.e)
_Q_SCALE = HEAD_DIM ** -0.5 * _LOG2E


def _in_proj_body(n_a, xa_ref, xb_ref, g_ref, w_ref, b_ref, wvt_ref, bvt_ref, cos_ref, sin_ref,
                  q_ref, kd_ref, vt_ref, glu_ref, gate_ref):
    x = jnp.where(pl.program_id(0) < n_a, xa_ref[...], xb_ref[...])
    ms = jnp.mean(x * x, axis=-1, keepdims=True)
    h = (x * lax.rsqrt(ms + NORM_EPS) * g_ref[...]).astype(BF16)
    cos = cos_ref[...]
    sin = sin_ref[...]

    def proj(c0):
        return (jnp.dot(h, w_ref[:, c0:c0 + _PROJ_CHUNK], preferred_element_type=F32)
                + b_ref[:, c0:c0 + _PROJ_CHUNK])

    def rope_store(z, out_ref, o0, scale):
        for c in range(_PROJ_CHUNK // LANES):
            zc = z[:, c * LANES:(c + 1) * LANES]
            r = zc * cos + pltpu.roll(zc, LANES // 2, 1) * sin
            if scale != 1.0:
                r = r * scale
            out_ref[:, o0 + c * LANES:o0 + (c + 1) * LANES] = r.astype(out_ref.dtype)

    for c in range(Q_COLS // _PROJ_CHUNK):
        rope_store(proj(_C_Q + c * _PROJ_CHUNK), q_ref, c * _PROJ_CHUNK, _Q_SCALE)
    for c in range(KV_DUP_COLS // _PROJ_CHUNK):
        rope_store(proj(_C_K + c * _PROJ_CHUNK), kd_ref, c * _PROJ_CHUNK, 1.0)
    vt = lax.dot_general(wvt_ref[...], h, _NT, preferred_element_type=F32) + bvt_ref[...]
    vt_ref[...] = vt.astype(BF16)
    per = _PROJ_CHUNK // LANES
    for c in range(D_MODEL // _PROJ_CHUNK):
        a = proj(_C_GA + c * _PROJ_CHUNK)
        g = proj(_C_GG + c * _PROJ_CHUNK)
        glu = a * jax.nn.sigmoid(g)
        for s in range(per):
            glu_ref[c * per + s] = glu[:, s * LANES:(s + 1) * LANES]
    for c in range(2 * D_MODEL // _PROJ_CHUNK):
        gate_ref[:, c * _PROJ_CHUNK:(c + 1) * _PROJ_CHUNK] = jax.nn.sigmoid(
            proj(_C_GATE + c * _PROJ_CHUNK)).astype(BF16)


def _in_proj(xa, xb, g_mix, w_perm, b_perm, wvt, bvt, cos_t, sin_t, geo, tm):
    T = geo.total

    def pos_map(i):
        r0 = i * tm
        start, _ = _seq_bounds(geo, r0)
        return ((r0 - start) // tm, 0)

    const = lambda i: (0, 0)
    row = lambda i: (i, 0)
    n_a, x_specs = _two_group_specs(geo, tm, D_MODEL)
    return pl.pallas_call(
        functools.partial(_in_proj_body, n_a),
        grid=(T // tm,),
        in_specs=x_specs + [
            pl.BlockSpec((1, D_MODEL), const),
            pl.BlockSpec((D_MODEL, _N_IN), const, pipeline_mode=pl.Buffered(1)),
            pl.BlockSpec((1, _N_IN), const),
            pl.BlockSpec((KV_DUP_COLS, D_MODEL), const),
            pl.BlockSpec((KV_DUP_COLS, 1), const),
            pl.BlockSpec((tm, LANES), pos_map),
            pl.BlockSpec((tm, LANES), pos_map),
        ],
        out_specs=[
            pl.BlockSpec((tm, Q_COLS), row),
            pl.BlockSpec((tm, KV_DUP_COLS), row),
            pl.BlockSpec((KV_DUP_COLS, tm), lambda i: (0, i)),
            pl.BlockSpec((N_SLAB, tm, LANES), lambda i: (0, i, 0)),
            pl.BlockSpec((tm, 2 * D_MODEL), row),
        ],
        out_shape=[
            jax.ShapeDtypeStruct((T, Q_COLS), BF16),
            jax.ShapeDtypeStruct((T, KV_DUP_COLS), BF16),
            jax.ShapeDtypeStruct((KV_DUP_COLS, T), BF16),
            jax.ShapeDtypeStruct((N_SLAB, T, LANES), F32),
            jax.ShapeDtypeStruct((T, 2 * D_MODEL), BF16),
        ],
        compiler_params=_params("parallel"),
        name="in_proj",
    )(xa, xb, g_mix, w_perm, b_perm, wvt, bvt, cos_t, sin_t)


_ONES_ROWS = 16
_ATTN_STEP_BLOCKS = 4


def _attn_body(geo, sink_ref, q_ref, kp_ref, kc_ref, kn_ref, vp_ref, vc_ref, vn_ref, o_ref):
    group = N_Q_HEADS // N_KV_HEADS
    nq = group * ATTN_BLOCK
    b = ATTN_BLOCK
    hd = HEAD_DIM
    nsub = _ATTN_STEP_BLOCKS

    key = lax.broadcasted_iota(I32, (b, nq), 0)
    qry = lax.broadcasted_iota(I32, (b, nq), 1) % b
    head_of_col = lax.broadcasted_iota(I32, (1, nq), 1) // b
    lane = lax.broadcasted_iota(I32, (b, LANES), 1)
    even_head = (lane % hd) < (hd // 2)
    ones = jnp.ones((_ONES_ROWS, 3 * b), BF16)

    for s in range(nsub):
        rows = slice(s * b, (s + 1) * b)
        r0 = (pl.program_id(0) * nsub + s) * b
        start, end = _seq_bounds(geo, r0)
        bias_prev = jnp.where((key >= qry) & (r0 > start), 0.0, NEG_INF)
        bias_next = jnp.where((key <= qry) & (r0 + b < end), 0.0, NEG_INF)

        for g in range(N_KV_HEADS):
            ls = slice(g * LANES, (g + 1) * LANES)
            k_prev = kp_ref[:, ls] if s == 0 else kc_ref[(s - 1) * b:s * b, ls]
            k_next = kn_ref[:, ls] if s == nsub - 1 else kc_ref[(s + 1) * b:(s + 2) * b, ls]
            v_prev = vp_ref[ls, :] if s == 0 else vc_ref[ls, (s - 1) * b:s * b]
            v_next = vn_ref[ls, :] if s == nsub - 1 else vc_ref[ls, (s + 1) * b:(s + 2) * b]

            qa = q_ref[rows, (2 * g) * LANES:(2 * g + 1) * LANES]
            qb = q_ref[rows, (2 * g + 1) * LANES:(2 * g + 2) * LANES]
            zero = jnp.zeros_like(qa)
            q4 = jnp.concatenate([jnp.where(even_head, qa, zero), jnp.where(even_head, zero, qa),
                                  jnp.where(even_head, qb, zero), jnp.where(even_head, zero, qb)],
                                 axis=0)
            k = jnp.concatenate([k_prev, kc_ref[rows, ls], k_next], axis=0)
            st = lax.dot_general(k, q4, _NT, preferred_element_type=F32)
            s_prev = st[0:b] + bias_prev
            s_cur = st[b:2 * b]
            s_next = st[2 * b:3 * b] + bias_next
            sink = jnp.full((1, nq), sink_ref[group * g] * _LOG2E, F32)
            for h in range(1, group):
                sink = jnp.where(head_of_col == h, sink_ref[group * g + h] * _LOG2E, sink)
            m = jnp.maximum(jnp.maximum(jnp.max(s_prev, axis=0, keepdims=True),
                                        jnp.max(s_cur, axis=0, keepdims=True)),
                            jnp.maximum(jnp.max(s_next, axis=0, keepdims=True), sink))
            p = jnp.concatenate([jnp.exp2(s_prev - m).astype(BF16), jnp.exp2(s_cur - m).astype(BF16),
                                 jnp.exp2(s_next - m).astype(BF16)], axis=0)
            vt = jnp.concatenate([v_prev, vc_ref[ls, rows], v_next], axis=1)
            ot = jnp.dot(jnp.concatenate([vt, ones], axis=0), p, preferred_element_type=F32)
            denom = ot[2 * hd:2 * hd + 1] + jnp.exp2(sink - m)
            on = ot[0:2 * hd] * (1.0 / denom)
            pair_a = jnp.concatenate([on[0:hd, 0:b], on[hd:2 * hd, b:2 * b]], axis=0)
            pair_b = jnp.concatenate([on[0:hd, 2 * b:3 * b], on[hd:2 * hd, 3 * b:4 * b]], axis=0)
            o_ref[rows, (2 * g) * LANES:(2 * g + 1) * LANES] = pair_a.T.astype(BF16)
            o_ref[rows, (2 * g + 1) * LANES:(2 * g + 2) * LANES] = pair_b.T.astype(BF16)


def _attention(q, kd, vt, sink, geo):
    T = q.shape[0]
    nsub = _ATTN_STEP_BLOCKS
    nb = T // ATTN_BLOCK
    step_rows = nsub * ATTN_BLOCK
    prev = lambda i: jnp.maximum(i * nsub - 1, 0)
    nxt = lambda i: jnp.minimum((i + 1) * nsub, nb - 1)
    k_edge = lambda m: pl.BlockSpec((ATTN_BLOCK, KV_DUP_COLS), lambda i: (m(i), 0))
    v_edge = lambda m: pl.BlockSpec((KV_DUP_COLS, ATTN_BLOCK), lambda i: (0, m(i)))
    return pl.pallas_call(
        functools.partial(_attn_body, geo),
        grid=(T // step_rows,),
        in_specs=[
            pl.BlockSpec(memory_space=pltpu.SMEM),
            pl.BlockSpec((step_rows, Q_COLS), lambda i: (i, 0)),
            k_edge(prev), pl.BlockSpec((step_rows, KV_DUP_COLS), lambda i: (i, 0)), k_edge(nxt),
            v_edge(prev), pl.BlockSpec((KV_DUP_COLS, step_rows), lambda i: (0, i)), v_edge(nxt),
        ],
        out_specs=pl.BlockSpec((step_rows, Q_COLS), lambda i: (i, 0)),
        out_shape=jax.ShapeDtypeStruct((T, Q_COLS), BF16),
        compiler_params=_params("parallel"),
        name="window_attn",
    )(sink, q, kd, kd, kd, vt, vt, vt)


_CONV_ROWS = 64
_LN_ROWS = 32


def _conv_body(geo, tc, z_ref, zp_ref, zn_ref, w_ref, dwb_ref, lng_ref, lnb_ref, o_ref,
               buf_ref, y_ref):
    i = pl.program_id(0)
    r0 = i * tc
    start, end = _seq_bounds(geo, r0)
    has_prev = (r0 > start).astype(F32)
    has_next = (r0 + tc < end).astype(F32)
    rows = tc + 2 * HALO
    buf_ref[:, 0:HALO, :] = zp_ref[...] * has_prev
    buf_ref[:, HALO:HALO + tc, :] = z_ref[...]
    buf_ref[:, HALO + tc:rows, :] = zn_ref[...] * has_next

    rep = _CONV_ROWS // SUBLANES

    def conv_chunk(j):
        s0 = pl.multiple_of(j * _CONV_ROWS, _CONV_ROWS)
        for c in range(N_SLAB):
            ls = slice(c * LANES, (c + 1) * LANES)
            acc = jnp.zeros((_CONV_ROWS, LANES), F32)
            for k in range(CONV_WIDTH):
                zt = buf_ref[c, pl.ds(s0 + (HALO - CONV_PAD + k), _CONV_ROWS, stride=1), :]
                wt = w_ref[k * SUBLANES:(k + 1) * SUBLANES, ls]
                acc = acc + zt * jnp.concatenate([wt] * rep, axis=0)
            y_ref[pl.ds(s0, _CONV_ROWS), ls] = acc

    def ln_chunk(j):
        for h in range(_CONV_ROWS // _LN_ROWS):
            s0 = pl.multiple_of(j * _CONV_ROWS + h * _LN_ROWS, _LN_ROWS)
            y = y_ref[pl.ds(s0, _LN_ROWS), :] + dwb_ref[...]
            mu = jnp.mean(y, axis=-1, keepdims=True)
            yc = y - mu
            var = jnp.mean(yc * yc, axis=-1, keepdims=True)
            yn = yc * lax.rsqrt(var + NORM_EPS) * lng_ref[...] + lnb_ref[...]
            o_ref[pl.ds(s0, _LN_ROWS), :] = (yn * jax.nn.sigmoid(yn)).astype(BF16)

    n = tc // _CONV_ROWS
    conv_chunk(0)

    def body(j, carry):
        ln_chunk(j - 1)
        conv_chunk(j)
        return carry

    lax.fori_loop(1, n, body, 0)
    ln_chunk(n - 1)


def _conv_branch(glu, w_rep, dw_b, ln_g, ln_b, geo, tc):
    T = glu.shape[1]
    nh = T // HALO
    per = tc // HALO
    const = lambda i: (0, 0)
    return pl.pallas_call(
        functools.partial(_conv_body, geo, tc),
        grid=(T // tc,),
        in_specs=[
            pl.BlockSpec((N_SLAB, tc, LANES), lambda i: (0, i, 0)),
            pl.BlockSpec((N_SLAB, HALO, LANES), lambda i: (0, jnp.maximum(i * per - 1, 0), 0)),
            pl.BlockSpec((N_SLAB, HALO, LANES), lambda i: (0, jnp.minimum((i + 1) * per, nh - 1), 0)),
            pl.BlockSpec((CONV_WIDTH * SUBLANES, D_MODEL), const),
            pl.BlockSpec((1, D_MODEL), const),
            pl.BlockSpec((1, D_MODEL), const),
            pl.BlockSpec((1, D_MODEL), const),
        ],
        out_specs=pl.BlockSpec((tc, D_MODEL), lambda i: (i, 0)),
        out_shape=jax.ShapeDtypeStruct((T, D_MODEL), BF16),
        scratch_shapes=[pltpu.VMEM((N_SLAB, tc + 2 * HALO, LANES), F32),
                        pltpu.VMEM((tc, D_MODEL), F32)],
        compiler_params=_params("parallel"),
        name="conv_branch",
    )(glu, glu, glu, w_rep, dw_b, ln_g, ln_b)


_MIX_COLS = 256


def _mix_body(n_a, xa_ref, xb_ref, o_ref, c_ref, gate_ref, wo32_ref, wpw32_ref, bpw_ref, wout32_ref,
              gffn_ref, wrh_ref, wrl_ref, br_ref, tri_ref,
              x1_ref, h2_ref, idx_ref, gt_ref, rank_ref, cnt_ref,
              x1_scr, mix_ref, wo_ref, wpw_ref, wout_ref, carry_ref):
    tm = xa_ref.shape[0]
    i = pl.program_id(0)
    last = pl.num_programs(0) - 2
    slot = i % 2

    @pl.when(i == 0)
    def _():
        carry_ref[...] = jnp.zeros_like(carry_ref)
        x1_scr[...] = jnp.zeros_like(x1_scr)
        wo_ref[...] = wo32_ref[...].astype(BF16)
        wpw_ref[...] = wpw32_ref[...].astype(BF16)
        wout_ref[...] = wout32_ref[...].astype(BF16)

    x1_prev = x1_scr[1 - slot]
    x1_new = x1_scr.at[slot]
    in_a = jnp.minimum(i, last) < n_a
    n_pieces = D_MODEL // _MIX_COLS

    def mix_piece(p):
        cs = slice(p * _MIX_COLS, (p + 1) * _MIX_COLS)
        gs = slice(D_MODEL + p * _MIX_COLS, D_MODEL + (p + 1) * _MIX_COLS)
        attn = jnp.dot(o_ref[...], wo_ref[:, cs], preferred_element_type=F32)
        conv = jnp.dot(c_ref[...], wpw_ref[:, cs], preferred_element_type=F32) + bpw_ref[:, cs]
        mix_ref[:, cs] = (gate_ref[:, cs].astype(F32) * attn
                          + gate_ref[:, gs].astype(F32) * conv).astype(BF16)

    def out_piece(p):
        cs = slice(p * _MIX_COLS, (p + 1) * _MIX_COLS)
        x = jnp.where(in_a, xa_ref[:, cs], xb_ref[:, cs])
        x1 = x + jnp.dot(mix_ref[...], wout_ref[:, cs], preferred_element_type=F32)
        x1_ref[:, cs] = x1
        x1_new[:, cs] = x1

    mix_piece(0)
    ms = jnp.mean(x1_prev * x1_prev, axis=-1, keepdims=True)
    h2 = x1_prev * lax.rsqrt(ms + NORM_EPS) * gffn_ref[...]
    h_hi = h2.astype(BF16)
    h2_ref[...] = _pack_halves(h_hi)
    mix_piece(1)

    h_lo = (h2 - h_hi.astype(F32)).astype(BF16)
    logits = (lax.dot_general(wrh_ref[...], h_hi, _NT, preferred_element_type=F32)
              + lax.dot_general(wrh_ref[...], h_lo, _NT, preferred_element_type=F32)
              + lax.dot_general(wrl_ref[...], h_hi, _NT, preferred_element_type=F32)
              + br_ref[...])
    for p in range(2, n_pieces):
        mix_piece(p)
    out_piece(0)

    eidx = lax.broadcasted_iota(I32, (N_EXPERTS, tm), 0)
    vals = logits
    picked, top_vals, top_idx = [], [], []
    for _ in range(TOP_K):
        m = jnp.max(vals, axis=0, keepdims=True)
        idx = jnp.min(jnp.where(vals == m, eidx, N_EXPERTS), axis=0, keepdims=True)
        sel = eidx == idx
        vals = jnp.where(sel, -jnp.inf, vals)
        picked.append(sel)
        top_vals.append(m)
        top_idx.append(idx)
    out_piece(1)

    exps = [jnp.exp(v - top_vals[0]) for v in top_vals]
    tot = exps[0] + exps[1] + exps[2] + exps[3]
    onehot = (picked[0] | picked[1] | picked[2] | picked[3])
    prefix = jnp.dot(onehot.astype(BF16), tri_ref[...], preferred_element_type=F32) + carry_ref[...]
    for p in range(2, n_pieces):
        out_piece(p)
    for j in range(TOP_K):
        idx_ref[j:j + 1, :] = top_idx[j]
        gt_ref[j:j + 1, :] = exps[j] / tot
        rank_ref[j:j + 1, :] = jnp.sum(jnp.where(picked[j], prefix, 0.0), axis=0,
                                       keepdims=True).astype(I32)
    counted = jnp.where(i > 0, jnp.sum(onehot.astype(F32), axis=1, keepdims=True), 0.0)
    carry_ref[...] = carry_ref[...] + counted
    cnt_ref[...] = jnp.broadcast_to(carry_ref[...], cnt_ref.shape)


def _mix_route(xa, xb, o, c, gates, wo, wpw, bpw, wout, gffn, wr_hi, wr_lo, br, tri, geo, tm):
    T = geo.total
    n = T // tm
    n_a = geo.rows_a // tm
    head = lambda i: jnp.minimum(i, n - 1)
    tail = lambda i: jnp.maximum(i - 1, 0)
    row = lambda i: (head(i), 0)
    col = lambda i: (0, tail(i))
    const = lambda i: (0, 0)
    wspec = lambda shape: pl.BlockSpec(shape, const, pipeline_mode=pl.Buffered(1))
    x_specs = [pl.BlockSpec((tm, D_MODEL), lambda i: (jnp.minimum(head(i), n_a - 1), 0)),
               pl.BlockSpec((tm, D_MODEL), lambda i: (jnp.maximum(head(i) - n_a, 0), 0))]
    return pl.pallas_call(
        functools.partial(_mix_body, n_a),
        grid=(n + 1,),
        in_specs=x_specs + [
            pl.BlockSpec((tm, Q_COLS), row),
            pl.BlockSpec((tm, D_MODEL), row),
            pl.BlockSpec((tm, 2 * D_MODEL), row),
            wspec((Q_COLS, D_MODEL)),
            wspec((D_MODEL, D_MODEL)),
            pl.BlockSpec((1, D_MODEL), const),
            wspec((D_MODEL, D_MODEL)),
            pl.BlockSpec((1, D_MODEL), const),
            pl.BlockSpec((N_EXPERTS, D_MODEL), const),
            pl.BlockSpec((N_EXPERTS, D_MODEL), const),
            pl.BlockSpec((N_EXPERTS, 1), const),
            pl.BlockSpec((tm, tm), const),
        ],
        out_specs=[
            pl.BlockSpec((tm, D_MODEL), row),
            pl.BlockSpec((tm, PACKED), lambda i: (tail(i), 0)),
            pl.BlockSpec((TOP_K, tm), col),
            pl.BlockSpec((TOP_K, tm), col),
            pl.BlockSpec((TOP_K, tm), col),
            pl.BlockSpec((N_EXPERTS, LANES), const),
        ],
        out_shape=[
            jax.ShapeDtypeStruct((T, D_MODEL), F32),
            jax.ShapeDtypeStruct((T, PACKED), I32),
            jax.ShapeDtypeStruct((TOP_K, T), I32),
            jax.ShapeDtypeStruct((TOP_K, T), F32),
            jax.ShapeDtypeStruct((TOP_K, T), I32),
            jax.ShapeDtypeStruct((N_EXPERTS, LANES), F32),
        ],
        scratch_shapes=[pltpu.VMEM((2, tm, D_MODEL), F32), pltpu.VMEM((tm, D_MODEL), BF16),
                        pltpu.VMEM((Q_COLS, D_MODEL), BF16), pltpu.VMEM((D_MODEL, D_MODEL), BF16),
                        pltpu.VMEM((D_MODEL, D_MODEL), BF16), pltpu.VMEM((N_EXPERTS, 1), F32)],
        compiler_params=_params("arbitrary"),
        name="mix_route",
    )(xa, xb, o, c, gates, wo, wpw, bpw, wout, gffn, wr_hi, wr_lo, br, tri)


def _sc_workers():
    info = plsc.get_sparse_core_info()
    return info.num_cores, info.num_cores * info.num_subcores


def _sc_dispatch(h, dest, n_out):
    t_rows, width = h.shape
    nc, nw = _sc_workers()
    per_w = t_rows // nw
    step = SC_ROWS_PER_STEP
    assert per_w * nw == t_rows and per_w % step == 0
    mesh = plsc.VectorSubcoreMesh(core_axis_name="c", subcore_axis_name="s")

    @functools.partial(
        pl.kernel, mesh=mesh,
        out_type=jax.ShapeDtypeStruct((n_out, width), h.dtype),
        scratch_types=[pltpu.VMEM((step,), I32)] * TOP_K + [pltpu.VMEM((step, width), h.dtype)],
    )
    def scatter_rows(h_hbm, dest_hbm, out_hbm, i0, i1, i2, i3, rows_v):
        base = (lax.axis_index("s") * nc + lax.axis_index("c")) * per_w

        @pl.loop(0, per_w // step)
        def _(i):
            off = base + i * step
            pltpu.sync_copy(h_hbm.at[pl.ds(off, step)], rows_v)
            for j, idx_v in enumerate((i0, i1, i2, i3)):
                pltpu.sync_copy(dest_hbm.at[j, pl.ds(off, step)], idx_v)
                pltpu.sync_copy(rows_v, out_hbm.at[idx_v])

    return scatter_rows(h, dest)


def _sc_gather(table, dest, row0, rows):
    width = table.shape[1]
    nc, nw = _sc_workers()
    per_w = rows // nw
    step = SC_ROWS_PER_STEP
    assert per_w * nw == rows and per_w % step == 0
    mesh = plsc.VectorSubcoreMesh(core_axis_name="c", subcore_axis_name="s")

    @functools.partial(
        pl.kernel, mesh=mesh,
        out_type=jax.ShapeDtypeStruct((TOP_K * rows, width), table.dtype),
        scratch_types=[pltpu.VMEM((step,), I32), pltpu.VMEM((step, width), table.dtype),
                       pltpu.SemaphoreType.DMA],
    )
    def gather_rows(table_hbm, dest_hbm, out_hbm, idx_v, rows_v, sem):
        base = (lax.axis_index("s") * nc + lax.axis_index("c")) * per_w

        @pl.loop(0, per_w // step)
        def _(i):
            off = base + i * step
            for j in range(TOP_K):
                pltpu.sync_copy(dest_hbm.at[j, pl.ds(row0 + off, step)], idx_v)
                pltpu.async_copy(table_hbm.at[idx_v], rows_v, sem).wait()
                pltpu.sync_copy(rows_v, out_hbm.at[pl.ds(j * rows + off, step)])

    return gather_rows(table, dest)


def _expert_body(bexp_ref, nused_ref, valid_ref, xs_ref, wgu_ref, bgu_ref, wd_ref, bd_ref, o_ref,
                 wgu_bf, wd_bf):
    n = pl.program_id(0)
    active = n < nused_ref[0]
    new_expert = (n == 0) | (bexp_ref[n] != bexp_ref[jnp.maximum(n - 1, 0)])

    @pl.when(active & new_expert)
    def _():
        wgu_bf[...] = wgu_ref[...].astype(BF16)
        wd_bf[...] = wd_ref[...].astype(BF16)

    sub = EXPERT_SUB_ROWS
    for r0 in range(0, xs_ref.shape[0], sub):
        rs = slice(r0, r0 + sub)
        live = active & (valid_ref[n] > r0)

        @pl.when(live)
        def _():
            row = r0 + lax.broadcasted_iota(I32, (sub, 1), 0)
            x = _unpack_halves(jnp.where(row < valid_ref[n], xs_ref[rs, :], 0), BF16)
            gu = jnp.dot(x, wgu_bf[...], preferred_element_type=F32) + bgu_ref[...]
            g = jnp.minimum(gu[:, 0:D_FF], SWIGLU_LIMIT)
            u = jnp.clip(gu[:, D_FF:2 * D_FF], -SWIGLU_LIMIT, SWIGLU_LIMIT)
            act = g * jax.nn.sigmoid(SWIGLU_ALPHA * g) * (u + 1.0)
            out = jnp.dot(act.astype(BF16), wd_bf[...], preferred_element_type=F32) + bd_ref[...]
            o_ref[rs, :] = _pack_halves(out.astype(BF16))

        @pl.when(jnp.logical_not(live))
        def _():
            o_ref[rs, :] = jnp.zeros((sub, o_ref.shape[1]), o_ref.dtype)


def _experts(block_exp, n_used, block_valid, xs, wgu, bgu, wd, bd, bm):
    P = xs.shape[0]
    nblk = P // bm
    xmap = lambda n, be, nu, bv: (jnp.minimum(n, nu[0] - 1), 0)
    emap = lambda n, be, nu, bv: (be[n], 0, 0)
    grid_spec = pltpu.PrefetchScalarGridSpec(
        num_scalar_prefetch=3,
        grid=(nblk,),
        in_specs=[
            pl.BlockSpec((bm, PACKED), xmap),
            pl.BlockSpec((None, D_MODEL, 2 * D_FF), emap),
            pl.BlockSpec((None, 1, 2 * D_FF), emap),
            pl.BlockSpec((None, D_FF, D_MODEL), emap),
            pl.BlockSpec((None, 1, D_MODEL), emap),
        ],
        out_specs=pl.BlockSpec((bm, PACKED), lambda n, be, nu, bv: (n, 0)),
        scratch_shapes=[pltpu.VMEM((D_MODEL, 2 * D_FF), BF16), pltpu.VMEM((D_FF, D_MODEL), BF16)],
    )
    return pl.pallas_call(
        _expert_body,
        grid_spec=grid_spec,
        out_shape=jax.ShapeDtypeStruct((P, PACKED), I32),
        compiler_params=_params("arbitrary"),
        name="experts",
    )(block_exp, n_used, block_valid, xs, wgu, bgu, wd, bd)


def _combine_body(x1_ref, g0_ref, g1_ref, g2_ref, g3_ref, gate_ref, gfin_ref, y_ref):
    tf = x1_ref.shape[0]
    gate_rows = jnp.concatenate(
        [gate_ref[...], jnp.zeros((LANES - TOP_K, tf), F32)], axis=0)
    for c in range(tf // LANES):
        rs = slice(c * LANES, (c + 1) * LANES)
        gate = gate_rows[:, rs].T
        y = x1_ref[rs, :]
        for j, g_ref in enumerate((g0_ref, g1_ref, g2_ref, g3_ref)):
            y = y + gate[:, j:j + 1] * _unpack_halves(g_ref[rs, :], F32)
        ms = jnp.mean(y * y, axis=-1, keepdims=True)
        y_ref[rs, :] = y * lax.rsqrt(ms + NORM_EPS) * gfin_ref[...]


def _combine(x1, gathered, gate_t, gfin, row0, rows, tf):
    nt = rows // tf
    t0 = row0 // tf
    row = lambda i: (t0 + i, 0)
    choice = lambda j: pl.BlockSpec((tf, PACKED), lambda i: (j * nt + i, 0))
    return pl.pallas_call(
        _combine_body,
        grid=(nt,),
        in_specs=[pl.BlockSpec((tf, D_MODEL), row)] + [choice(j) for j in range(TOP_K)] + [
            pl.BlockSpec((TOP_K, tf), lambda i: (0, t0 + i)),
            pl.BlockSpec((1, D_MODEL), lambda i: (0, 0)),
        ],
        out_specs=pl.BlockSpec((tf, D_MODEL), lambda i: (i, 0)),
        out_shape=jax.ShapeDtypeStruct((rows, D_MODEL), F32),
        compiler_params=_params("parallel"),
        name="combine",
    )(x1, gathered, gathered, gathered, gathered, gate_t, gfin)


def _permute_in_proj(w):
    lead = w.shape[:-1]
    half = HEAD_DIM // 2
    q = w[..., :Q_COLS].reshape(*lead, N_Q_HEADS // 2, 2, 2, half)
    q = jnp.swapaxes(q, -3, -2).reshape(*lead, Q_COLS)
    k = w[..., Q_COLS:Q_COLS + KV_COLS].reshape(*lead, N_KV_HEADS, 2, 1, half)
    k = jnp.broadcast_to(k, (*lead, N_KV_HEADS, 2, 2, half)).reshape(*lead, KV_DUP_COLS)
    v = w[..., Q_COLS + KV_COLS:Q_COLS + 2 * KV_COLS].reshape(*lead, N_KV_HEADS, 1, HEAD_DIM)
    v = jnp.broadcast_to(v, (*lead, N_KV_HEADS, 2, HEAD_DIM)).reshape(*lead, KV_DUP_COLS)
    return jnp.concatenate([q, k, w[..., Q_COLS + 2 * KV_COLS:]], axis=-1), v


def _rope_tables(n_pos):
    half = HEAD_DIM // 2
    inv_freq = 1.0 / (ROPE_THETA ** (jnp.arange(half, dtype=F32) * (2.0 / HEAD_DIM)))
    ang = jnp.arange(n_pos, dtype=F32)[:, None] * inv_freq[None, :]
    cos = jnp.tile(jnp.cos(ang), (1, LANES // half))
    sin = jnp.tile(jnp.sin(ang), (1, LANES // half))
    sign = jnp.where(jnp.arange(LANES) < LANES // 2, -1.0, 1.0).astype(F32)
    return cos, sin * sign[None, :]


def _tiles(geo):
    unit = min(geo.len_a, geo.len_b)
    tile = min(512, unit)
    return dict(tm=tile, tc=tile, tf=tile, bm=2 * EXPERT_SUB_ROWS)


def kernel(x_prompt, x_sample, norm_mix_g, w_in, b_in, attn_sink, w_o_attn, conv_dw_w, conv_dw_b,
           conv_ln_g, conv_ln_b, w_pw2, b_pw2, w_out, norm_ffn_g, w_router, b_router, w_gu, b_gu,
           w_down, b_down, norm_final_g):
    assert w_in.shape[0] == 1, "single trunk layer"
    geo = Geo(x_prompt.shape[0], x_prompt.shape[1], x_sample.shape[0], x_sample.shape[1])
    T = geo.total
    ts = _tiles(geo)
    xa = x_prompt.reshape(-1, D_MODEL)
    xb = x_sample.reshape(-1, D_MODEL)

    w_perm, w_v = _permute_in_proj(w_in[0])
    b_perm, b_v = _permute_in_proj(b_in)
    cos_t, sin_t = _rope_tables(max(geo.len_a, geo.len_b))

    q, kd, vt, glu, gates = _in_proj(xa, xb, norm_mix_g, w_perm.astype(BF16), b_perm,
                                     w_v.T.astype(BF16), b_v.T, cos_t, sin_t, geo, ts["tm"])
    attn = _attention(q, kd, vt, attn_sink[0], geo)
    w_rep = jnp.repeat(conv_dw_w[0], SUBLANES, axis=0)
    conv = _conv_branch(glu, w_rep, conv_dw_b, conv_ln_g, conv_ln_b, geo, ts["tc"])

    wr_t = w_router[0].T
    wr_hi = wr_t.astype(BF16)
    wr_lo = (wr_t - wr_hi.astype(F32)).astype(BF16)
    tri = jnp.triu(jnp.ones((ts["tm"], ts["tm"]), BF16), 1)
    x1, h2, idx, gate_t, rank, counts = _mix_route(
        xa, xb, attn, conv, gates, w_o_attn[0], w_pw2[0], b_pw2, w_out[0], norm_ffn_g,
        wr_hi, wr_lo, b_router[0][:, None], tri, geo, ts["tm"])

    bm = ts["bm"]
    n_blocks = (T * TOP_K) // bm + N_EXPERTS
    cnt = counts[:, 0].astype(I32)
    padded = ((cnt + bm - 1) // bm) * bm
    pad_end = jnp.cumsum(padded)
    pad_start = pad_end - padded
    expert_ids = jnp.arange(N_EXPERTS, dtype=I32)
    dest = rank + jnp.sum(jnp.where(idx[None] == expert_ids[:, None, None],
                                    pad_start[:, None, None], 0), axis=0)
    block_start = jnp.arange(n_blocks, dtype=I32) * bm
    block_exp = jnp.minimum(jnp.sum((pad_end[None, :] <= block_start[:, None]).astype(I32), axis=1),
                            N_EXPERTS - 1)
    n_used = (pad_end[-1:] // bm).astype(I32)
    seg_end = jnp.sum(jnp.where(block_exp[:, None] == expert_ids[None, :],
                                (pad_start + cnt)[None, :], 0), axis=1)
    block_valid = jnp.clip(seg_end - block_start, 0, bm).astype(I32)

    xs = _sc_dispatch(h2, dest, n_blocks * bm)
    ys = _experts(block_exp, n_used, block_valid, xs, w_gu[0], b_gu[0][:, None, :],
                  w_down[0], b_down[0][:, None, :], bm)
    outs = []
    for row0, rows in ((0, geo.rows_a), (geo.rows_a, T - geo.rows_a)):
        gathered = _sc_gather(ys, dest, row0, rows)
        outs.append(_combine(x1, gathered, gate_t, norm_final_g[None, :], row0, rows, ts["tf"]))
    return (outs[0].reshape(x_prompt.shape), outs[1].reshape(x_sample.shape))
```

```python
import functools
import math
from typing import NamedTuple

import jax
import jax.numpy as jnp
from jax import lax
from jax.experimental import pallas as pl
from jax.experimental.pallas import tpu as pltpu
from jax.experimental.pallas import tpu_sc as plsc

F32 = jnp.float32
BF16 = jnp.bfloat16
I32 = jnp.int32

D_MODEL = 1024
HEAD_DIM = 64
N_Q_HEADS = 16
N_KV_HEADS = 4
WINDOW = 128
ATTN_BLOCK = 128
ROPE_THETA = 10000.0
CONV_WIDTH = 31
CONV_PAD = CONV_WIDTH // 2
N_EXPERTS = 32
TOP_K = 4
D_FF = D_MODEL
SWIGLU_LIMIT = 7.0
SWIGLU_ALPHA = 1.702
NORM_EPS = 1e-5
NEG_INF = -1e30

Q_COLS = N_Q_HEADS * HEAD_DIM
KV_COLS = N_KV_HEADS * HEAD_DIM
LANES = 128
SUBLANES = 8
KV_DUP_COLS = N_KV_HEADS * LANES
HALO = 16
VMEM_LIMIT = 56 * 1024 * 1024
SC_ROWS_PER_STEP = 128
EXPERT_SUB_ROWS = 512


class Geo(NamedTuple):
    n_a: int
    len_a: int
    n_b: int
    len_b: int

    @property
    def rows_a(self):
        return self.n_a * self.len_a

    @property
    def total(self):
        return self.rows_a + self.n_b * self.len_b


def _seq_bounds(geo, r):
    in_a = r < geo.rows_a
    start_a = (r // geo.len_a) * geo.len_a
    start_b = geo.rows_a + ((r - geo.rows_a) // geo.len_b) * geo.len_b
    start = jnp.where(in_a, start_a, start_b)
    end = start + jnp.where(in_a, geo.len_a, geo.len_b)
    return start, end


def _params(*sem):
    return pltpu.CompilerParams(dimension_semantics=sem, vmem_limit_bytes=VMEM_LIMIT)


PACKED = D_MODEL // 2


def _pack_halves(x_bf16):
    hi = lax.bitcast_convert_type(x_bf16[:, :PACKED].astype(F32), I32)
    lo = lax.bitcast_convert_type(x_bf16[:, PACKED:].astype(F32), I32)
    return hi | lax.shift_right_logical(lo, 16)


def _unpack_halves(words, dtype):
    hi = lax.bitcast_convert_type(words & jnp.int32(-65536), F32)
    lo = lax.bitcast_convert_type(lax.shift_left(words, 16), F32)
    return jnp.concatenate([hi.astype(dtype), lo.astype(dtype)], axis=1)


def _two_group_specs(geo, tile, width):
    n_a = geo.rows_a // tile
    return n_a, [pl.BlockSpec((tile, width), lambda i: (jnp.minimum(i, n_a - 1), 0)),
                 pl.BlockSpec((tile, width), lambda i: (jnp.maximum(i - n_a, 0), 0))]


_C_Q = 0
_C_K = _C_Q + Q_COLS
_C_GA = _C_K + KV_DUP_COLS
_C_GG = _C_GA + D_MODEL
_C_GATE = _C_GG + D_MODEL
_N_IN = _C_GATE + 2 * D_MODEL
_PROJ_CHUNK = 512
N_SLAB = D_MODEL // LANES
_NT = (((1,), (1,)), ((), ()))
_LOG2E = math.log2(math.e)
_Q_SCALE = HEAD_DIM ** -0.5 * _LOG2E


def _in_proj_body(n_a, xa_ref, xb_ref, g_ref, w_ref, b_ref, wvt_ref, bvt_ref, cos_ref, sin_ref,
                  q_ref, kd_ref, vt_ref, glu_ref, gate_ref):
    x = jnp.where(pl.program_id(0) < n_a, xa_ref[...], xb_ref[...])
    ms = jnp.mean(x * x, axis=-1, keepdims=True)
    h = (x * lax.rsqrt(ms + NORM_EPS) * g_ref[...]).astype(BF16)
    cos = cos_ref[...]
    sin = sin_ref[...]

    def proj(c0):
        return (jnp.dot(h, w_ref[:, c0:c0 + _PROJ_CHUNK], preferred_element_type=F32)
                + b_ref[:, c0:c0 + _PROJ_CHUNK])

    def rope_store(z, out_ref, o0, scale):
        for c in range(_PROJ_CHUNK // LANES):
            zc = z[:, c * LANES:(c + 1) * LANES]
            r = zc * cos + pltpu.roll(zc, LANES // 2, 1) * sin
            if scale != 1.0:
                r = r * scale
            out_ref[:, o0 + c * LANES:o0 + (c + 1) * LANES] = r.astype(out_ref.dtype)

    for c in range(Q_COLS // _PROJ_CHUNK):
        rope_store(proj(_C_Q + c * _PROJ_CHUNK), q_ref, c * _PROJ_CHUNK, _Q_SCALE)
    for c in range(KV_DUP_COLS // _PROJ_CHUNK):
        rope_store(proj(_C_K + c * _PROJ_CHUNK), kd_ref, c * _PROJ_CHUNK, 1.0)
    vt = lax.dot_general(wvt_ref[...], h, _NT, preferred_element_type=F32) + bvt_ref[...]
    vt_ref[...] = vt.astype(BF16)
    per = _PROJ_CHUNK // LANES
    for c in range(D_MODEL // _PROJ_CHUNK):
        a = proj(_C_GA + c * _PROJ_CHUNK)
        g = proj(_C_GG + c * _PROJ_CHUNK)
        glu = a * jax.nn.sigmoid(g)
        for s in range(per):
            glu_ref[c * per + s] = glu[:, s * LANES:(s + 1) * LANES]
    for c in range(2 * D_MODEL // _PROJ_CHUNK):
        gate_ref[:, c * _PROJ_CHUNK:(c + 1) * _PROJ_CHUNK] = jax.nn.sigmoid(
            proj(_C_GATE + c * _PROJ_CHUNK)).astype(BF16)


def _in_proj(xa, xb, g_mix, w_perm, b_perm, wvt, bvt, cos_t, sin_t, geo, tm):
    T = geo.total

    def pos_map(i):
        r0 = i * tm
        start, _ = _seq_bounds(geo, r0)
        return ((r0 - start) // tm, 0)

    const = lambda i: (0, 0)
    row = lambda i: (i, 0)
    n_a, x_specs = _two_group_specs(geo, tm, D_MODEL)
    return pl.pallas_call(
        functools.partial(_in_proj_body, n_a),
        grid=(T // tm,),
        in_specs=x_specs + [
            pl.BlockSpec((1, D_MODEL), const),
            pl.BlockSpec((D_MODEL, _N_IN), const, pipeline_mode=pl.Buffered(1)),
            pl.BlockSpec((1, _N_IN), const),
            pl.BlockSpec((KV_DUP_COLS, D_MODEL), const),
            pl.BlockSpec((KV_DUP_COLS, 1), const),
            pl.BlockSpec((tm, LANES), pos_map),
            pl.BlockSpec((tm, LANES), pos_map),
        ],
        out_specs=[
            pl.BlockSpec((tm, Q_COLS), row),
            pl.BlockSpec((tm, KV_DUP_COLS), row),
            pl.BlockSpec((KV_DUP_COLS, tm), lambda i: (0, i)),
            pl.BlockSpec((N_SLAB, tm, LANES), lambda i: (0, i, 0)),
            pl.BlockSpec((tm, 2 * D_MODEL), row),
        ],
        out_shape=[
            jax.ShapeDtypeStruct((T, Q_COLS), BF16),
            jax.ShapeDtypeStruct((T, KV_DUP_COLS), BF16),
            jax.ShapeDtypeStruct((KV_DUP_COLS, T), BF16),
            jax.ShapeDtypeStruct((N_SLAB, T, LANES), F32),
            jax.ShapeDtypeStruct((T, 2 * D_MODEL), BF16),
        ],
        compiler_params=_params("parallel"),
        name="in_proj",
    )(xa, xb, g_mix, w_perm, b_perm, wvt, bvt, cos_t, sin_t)


_ONES_ROWS = 16
_ATTN_STEP_BLOCKS = 4


def _attn_body(geo, sink_ref, q_ref, kp_ref, kc_ref, kn_ref, vp_ref, vc_ref, vn_ref, o_ref):
    group = N_Q_HEADS // N_KV_HEADS
    nq = group * ATTN_BLOCK
    b = ATTN_BLOCK
    hd = HEAD_DIM
    nsub = _ATTN_STEP_BLOCKS

    key = lax.broadcasted_iota(I32, (b, nq), 0)
    qry = lax.broadcasted_iota(I32, (b, nq), 1) % b
    head_of_col = lax.broadcasted_iota(I32, (1, nq), 1) // b
    lane = lax.broadcasted_iota(I32, (b, LANES), 1)
    even_head = (lane % hd) < (hd // 2)
    ones = jnp.ones((_ONES_ROWS, 3 * b), BF16)

    for s in range(nsub):
        rows = slice(s * b, (s + 1) * b)
        r0 = (pl.program_id(0) * nsub + s) * b
        start, end = _seq_bounds(geo, r0)
        bias_prev = jnp.where((key >= qry) & (r0 > start), 0.0, NEG_INF)
        bias_next = jnp.where((key <= qry) & (r0 + b < end), 0.0, NEG_INF)

        for g in range(N_KV_HEADS):
            ls = slice(g * LANES, (g + 1) * LANES)
            k_prev = kp_ref[:, ls] if s == 0 else kc_ref[(s - 1) * b:s * b, ls]
            k_next = kn_ref[:, ls] if s == nsub - 1 else kc_ref[(s + 1) * b:(s + 2) * b, ls]
            v_prev = vp_ref[ls, :] if s == 0 else vc_ref[ls, (s - 1) * b:s * b]
            v_next = vn_ref[ls, :] if s == nsub - 1 else vc_ref[ls, (s + 1) * b:(s + 2) * b]

            qa = q_ref[rows, (2 * g) * LANES:(2 * g + 1) * LANES]
            qb = q_ref[rows, (2 * g + 1) * LANES:(2 * g + 2) * LANES]
            zero = jnp.zeros_like(qa)
            q4 = jnp.concatenate([jnp.where(even_head, qa, zero), jnp.where(even_head, zero, qa),
                                  jnp.where(even_head, qb, zero), jnp.where(even_head, zero, qb)],
                                 axis=0)
            k = jnp.concatenate([k_prev, kc_ref[rows, ls], k_next], axis=0)
            st = lax.dot_general(k, q4, _NT, preferred_element_type=F32)
            s_prev = st[0:b] + bias_prev
            s_cur = st[b:2 * b]
            s_next = st[2 * b:3 * b] + bias_next
            sink = jnp.full((1, nq), sink_ref[group * g] * _LOG2E, F32)
            for h in range(1, group):
                sink = jnp.where(head_of_col == h, sink_ref[group * g + h] * _LOG2E, sink)
            m = jnp.maximum(jnp.maximum(jnp.max(s_prev, axis=0, keepdims=True),
                                        jnp.max(s_cur, axis=0, keepdims=True)),
                            jnp.maximum(jnp.max(s_next, axis=0, keepdims=True), sink))
            p = jnp.concatenate([jnp.exp2(s_prev - m).astype(BF16), jnp.exp2(s_cur - m).astype(BF16),
                                 jnp.exp2(s_next - m).astype(BF16)], axis=0)
            vt = jnp.concatenate([v_prev, vc_ref[ls, rows], v_next], axis=1)
            ot = jnp.dot(jnp.concatenate([vt, ones], axis=0), p, preferred_element_type=F32)
            denom = ot[2 * hd:2 * hd + 1] + jnp.exp2(sink - m)
            on = ot[0:2 * hd] * (1.0 / denom)
            pair_a = jnp.concatenate([on[0:hd, 0:b], on[hd:2 * hd, b:2 * b]], axis=0)
            pair_b = jnp.concatenate([on[0:hd, 2 * b:3 * b], on[hd:2 * hd, 3 * b:4 * b]], axis=0)
            o_ref[rows, (2 * g) * LANES:(2 * g + 1) * LANES] = pair_a.T.astype(BF16)
            o_ref[rows, (2 * g + 1) * LANES:(2 * g + 2) * LANES] = pair_b.T.astype(BF16)


def _attention(q, kd, vt, sink, geo):
    T = q.shape[0]
    nsub = _ATTN_STEP_BLOCKS
    nb = T // ATTN_BLOCK
    step_rows = nsub * ATTN_BLOCK
    prev = lambda i: jnp.maximum(i * nsub - 1, 0)
    nxt = lambda i: jnp.minimum((i + 1) * nsub, nb - 1)
    k_edge = lambda m: pl.BlockSpec((ATTN_BLOCK, KV_DUP_COLS), lambda i: (m(i), 0))
    v_edge = lambda m: pl.BlockSpec((KV_DUP_COLS, ATTN_BLOCK), lambda i: (0, m(i)))
    return pl.pallas_call(
        functools.partial(_attn_body, geo),
        grid=(T // step_rows,),
        in_specs=[
            pl.BlockSpec(memory_space=pltpu.SMEM),
            pl.BlockSpec((step_rows, Q_COLS), lambda i: (i, 0)),
            k_edge(prev), pl.BlockSpec((step_rows, KV_DUP_COLS), lambda i: (i, 0)), k_edge(nxt),
            v_edge(prev), pl.BlockSpec((KV_DUP_COLS, step_rows), lambda i: (0, i)), v_edge(nxt),
        ],
        out_specs=pl.BlockSpec((step_rows, Q_COLS), lambda i: (i, 0)),
        out_shape=jax.ShapeDtypeStruct((T, Q_COLS), BF16),
        compiler_params=_params("parallel"),
        name="window_attn",
    )(sink, q, kd, kd, kd, vt, vt, vt)


_CONV_ROWS = 128
_LN_ROWS = 32


def _conv_body(geo, tc, z_ref, zp_ref, zn_ref, w_ref, dwb_ref, lng_ref, lnb_ref, o_ref,
               buf_ref, y_ref):
    i = pl.program_id(0)
    r0 = i * tc
    start, end = _seq_bounds(geo, r0)
    has_prev = (r0 > start).astype(F32)
    has_next = (r0 + tc < end).astype(F32)
    rows = tc + 2 * HALO
    buf_ref[:, 0:HALO, :] = zp_ref[...] * has_prev
    buf_ref[:, HALO:HALO + tc, :] = z_ref[...]
    buf_ref[:, HALO + tc:rows, :] = zn_ref[...] * has_next

    rep = _CONV_ROWS // SUBLANES

    def conv_chunk(j):
        s0 = pl.multiple_of(j * _CONV_ROWS, _CONV_ROWS)
        for c in range(N_SLAB):
            ls = slice(c * LANES, (c + 1) * LANES)
            acc = jnp.zeros((_CONV_ROWS, LANES), F32)
            for k in range(CONV_WIDTH):
                zt = buf_ref[c, pl.ds(s0 + (HALO - CONV_PAD + k), _CONV_ROWS, stride=1), :]
                wt = w_ref[k * SUBLANES:(k + 1) * SUBLANES, ls]
                acc = acc + zt * jnp.concatenate([wt] * rep, axis=0)
            y_ref[pl.ds(s0, _CONV_ROWS), ls] = acc

    def ln_chunk(j):
        for h in range(_CONV_ROWS // _LN_ROWS):
            s0 = pl.multiple_of(j * _CONV_ROWS + h * _LN_ROWS, _LN_ROWS)
            y = y_ref[pl.ds(s0, _LN_ROWS), :] + dwb_ref[...]
            mu = jnp.mean(y, axis=-1, keepdims=True)
            yc = y - mu
            var = jnp.mean(yc * yc, axis=-1, keepdims=True)
            yn = yc * lax.rsqrt(var + NORM_EPS) * lng_ref[...] + lnb_ref[...]
            o_ref[pl.ds(s0, _LN_ROWS), :] = (yn * jax.nn.sigmoid(yn)).astype(BF16)

    n = tc // _CONV_ROWS
    conv_chunk(0)

    def body(j, carry):
        ln_chunk(j - 1)
        conv_chunk(j)
        return carry

    lax.fori_loop(1, n, body, 0)
    ln_chunk(n - 1)


def _conv_branch(glu, w_rep, dw_b, ln_g, ln_b, geo, tc):
    T = glu.shape[1]
    nh = T // HALO
    per = tc // HALO
    const = lambda i: (0, 0)
    return pl.pallas_call(
        functools.partial(_conv_body, geo, tc),
        grid=(T // tc,),
        in_specs=[
            pl.BlockSpec((N_SLAB, tc, LANES), lambda i: (0, i, 0)),
            pl.BlockSpec((N_SLAB, HALO, LANES), lambda i: (0, jnp.maximum(i * per - 1, 0), 0)),
            pl.BlockSpec((N_SLAB, HALO, LANES), lambda i: (0, jnp.minimum((i + 1) * per, nh - 1), 0)),
            pl.BlockSpec((CONV_WIDTH * SUBLANES, D_MODEL), const),
            pl.BlockSpec((1, D_MODEL), const),
            pl.BlockSpec((1, D_MODEL), const),
            pl.BlockSpec((1, D_MODEL), const),
        ],
        out_specs=pl.BlockSpec((tc, D_MODEL), lambda i: (i, 0)),
        out_shape=jax.ShapeDtypeStruct((T, D_MODEL), BF16),
        scratch_shapes=[pltpu.VMEM((N_SLAB, tc + 2 * HALO, LANES), F32),
                        pltpu.VMEM((tc, D_MODEL), F32)],
        compiler_params=_params("parallel"),
        name="conv_branch",
    )(glu, glu, glu, w_rep, dw_b, ln_g, ln_b)


_MIX_COLS = 256


def _mix_body(n_a, xa_ref, xb_ref, o_ref, c_ref, gate_ref, wo_ref, wpw_ref, bpw_ref, wout_ref,
              gffn_ref, wrh_ref, wrl_ref, br_ref, tri_ref,
              x1_ref, h2_ref, idx_ref, gt_ref, rank_ref, cnt_ref, x1_scr, mix_ref, carry_ref):
    tm = xa_ref.shape[0]
    i = pl.program_id(0)
    last = pl.num_programs(0) - 2
    slot = i % 2

    @pl.when(i == 0)
    def _():
        carry_ref[...] = jnp.zeros_like(carry_ref)
        x1_scr[...] = jnp.zeros_like(x1_scr)

    x1_prev = x1_scr[1 - slot]
    x1_new = x1_scr.at[slot]
    in_a = jnp.minimum(i, last) < n_a
    n_pieces = D_MODEL // _MIX_COLS

    def mix_piece(p):
        cs = slice(p * _MIX_COLS, (p + 1) * _MIX_COLS)
        gs = slice(D_MODEL + p * _MIX_COLS, D_MODEL + (p + 1) * _MIX_COLS)
        attn = jnp.dot(o_ref[...], wo_ref[:, cs], preferred_element_type=F32)
        conv = jnp.dot(c_ref[...], wpw_ref[:, cs], preferred_element_type=F32) + bpw_ref[:, cs]
        mix_ref[:, cs] = (gate_ref[:, cs].astype(F32) * attn
                          + gate_ref[:, gs].astype(F32) * conv).astype(BF16)

    def out_piece(p):
        cs = slice(p * _MIX_COLS, (p + 1) * _MIX_COLS)
        x = jnp.where(in_a, xa_ref[:, cs], xb_ref[:, cs])
        x1 = x + jnp.dot(mix_ref[...], wout_ref[:, cs], preferred_element_type=F32)
        x1_ref[:, cs] = x1
        x1_new[:, cs] = x1

    mix_piece(0)
    ms = jnp.mean(x1_prev * x1_prev, axis=-1, keepdims=True)
    h2 = x1_prev * lax.rsqrt(ms + NORM_EPS) * gffn_ref[...]
    h_hi = h2.astype(BF16)
    h2_ref[...] = _pack_halves(h_hi)
    mix_piece(1)

    h_lo = (h2 - h_hi.astype(F32)).astype(BF16)
    logits = (lax.dot_general(wrh_ref[...], h_hi, _NT, preferred_element_type=F32)
              + lax.dot_general(wrh_ref[...], h_lo, _NT, preferred_element_type=F32)
              + lax.dot_general(wrl_ref[...], h_hi, _NT, preferred_element_type=F32)
              + br_ref[...])
    for p in range(2, n_pieces):
        mix_piece(p)
    out_piece(0)

    eidx = lax.broadcasted_iota(I32, (N_EXPERTS, tm), 0)
    vals = logits
    picked, top_vals, top_idx = [], [], []
    for _ in range(TOP_K):
        m = jnp.max(vals, axis=0, keepdims=True)
        idx = jnp.min(jnp.where(vals == m, eidx, N_EXPERTS), axis=0, keepdims=True)
        sel = eidx == idx
        vals = jnp.where(sel, -jnp.inf, vals)
        picked.append(sel)
        top_vals.append(m)
        top_idx.append(idx)
    out_piece(1)

    exps = [jnp.exp(v - top_vals[0]) for v in top_vals]
    tot = exps[0] + exps[1] + exps[2] + exps[3]
    onehot = (picked[0] | picked[1] | picked[2] | picked[3])
    prefix = jnp.dot(onehot.astype(BF16), tri_ref[...], preferred_element_type=F32) + carry_ref[...]
    for p in range(2, n_pieces):
        out_piece(p)
    for j in range(TOP_K):
        idx_ref[j:j + 1, :] = top_idx[j]
        gt_ref[j:j + 1, :] = exps[j] / tot
        rank_ref[j:j + 1, :] = jnp.sum(jnp.where(picked[j], prefix, 0.0), axis=0,
                                       keepdims=True).astype(I32)
    counted = jnp.where(i > 0, jnp.sum(onehot.astype(F32), axis=1, keepdims=True), 0.0)
    carry_ref[...] = carry_ref[...] + counted
    cnt_ref[...] = jnp.broadcast_to(carry_ref[...], cnt_ref.shape)


def _mix_route(xa, xb, o, c, gates, wo, wpw, bpw, wout, gffn, wr_hi, wr_lo, br, tri, geo, tm):
    T = geo.total
    n = T // tm
    n_a = geo.rows_a // tm
    head = lambda i: jnp.minimum(i, n - 1)
    tail = lambda i: jnp.maximum(i - 1, 0)
    row = lambda i: (head(i), 0)
    col = lambda i: (0, tail(i))
    const = lambda i: (0, 0)
    wspec = lambda shape: pl.BlockSpec(shape, const, pipeline_mode=pl.Buffered(1))
    x_specs = [pl.BlockSpec((tm, D_MODEL), lambda i: (jnp.minimum(head(i), n_a - 1), 0)),
               pl.BlockSpec((tm, D_MODEL), lambda i: (jnp.maximum(head(i) - n_a, 0), 0))]
    return pl.pallas_call(
        functools.partial(_mix_body, n_a),
        grid=(n + 1,),
        in_specs=x_specs + [
            pl.BlockSpec((tm, Q_COLS), row),
            pl.BlockSpec((tm, D_MODEL), row),
            pl.BlockSpec((tm, 2 * D_MODEL), row),
            wspec((Q_COLS, D_MODEL)),
            wspec((D_MODEL, D_MODEL)),
            pl.BlockSpec((1, D_MODEL), const),
            wspec((D_MODEL, D_MODEL)),
            pl.BlockSpec((1, D_MODEL), const),
            pl.BlockSpec((N_EXPERTS, D_MODEL), const),
            pl.BlockSpec((N_EXPERTS, D_MODEL), const),
            pl.BlockSpec((N_EXPERTS, 1), const),
            pl.BlockSpec((tm, tm), const),
        ],
        out_specs=[
            pl.BlockSpec((tm, D_MODEL), row),
            pl.BlockSpec((tm, PACKED), lambda i: (tail(i), 0)),
            pl.BlockSpec((TOP_K, tm), col),
            pl.BlockSpec((TOP_K, tm), col),
            pl.BlockSpec((TOP_K, tm), col),
            pl.BlockSpec((N_EXPERTS, LANES), const),
        ],
        out_shape=[
            jax.ShapeDtypeStruct((T, D_MODEL), F32),
            jax.ShapeDtypeStruct((T, PACKED), I32),
            jax.ShapeDtypeStruct((TOP_K, T), I32),
            jax.ShapeDtypeStruct((TOP_K, T), F32),
            jax.ShapeDtypeStruct((TOP_K, T), I32),
            jax.ShapeDtypeStruct((N_EXPERTS, LANES), F32),
        ],
        scratch_shapes=[pltpu.VMEM((2, tm, D_MODEL), F32), pltpu.VMEM((tm, D_MODEL), BF16),
                        pltpu.VMEM((N_EXPERTS, 1), F32)],
        compiler_params=_params("arbitrary"),
        name="mix_route",
    )(xa, xb, o, c, gates, wo, wpw, bpw, wout, gffn, wr_hi, wr_lo, br, tri)


def _sc_workers():
    info = plsc.get_sparse_core_info()
    return info.num_cores, info.num_cores * info.num_subcores


def _sc_dispatch(h, dest, n_out):
    t_rows, width = h.shape
    nc, nw = _sc_workers()
    per_w = t_rows // nw
    step = SC_ROWS_PER_STEP
    assert per_w * nw == t_rows and per_w % step == 0
    mesh = plsc.VectorSubcoreMesh(core_axis_name="c", subcore_axis_name="s")

    @functools.partial(
        pl.kernel, mesh=mesh,
        out_type=jax.ShapeDtypeStruct((n_out, width), h.dtype),
        scratch_types=[pltpu.VMEM((step,), I32)] * TOP_K + [pltpu.VMEM((step, width), h.dtype)],
    )
    def scatter_rows(h_hbm, dest_hbm, out_hbm, i0, i1, i2, i3, rows_v):
        base = (lax.axis_index("s") * nc + lax.axis_index("c")) * per_w

        @pl.loop(0, per_w // step)
        def _(i):
            off = base + i * step
            pltpu.sync_copy(h_hbm.at[pl.ds(off, step)], rows_v)
            for j, idx_v in enumerate((i0, i1, i2, i3)):
                pltpu.sync_copy(dest_hbm.at[j, pl.ds(off, step)], idx_v)
                pltpu.sync_copy(rows_v, out_hbm.at[idx_v])

    return scatter_rows(h, dest)


def _sc_gather(table, dest, row0, rows):
    width = table.shape[1]
    nc, nw = _sc_workers()
    per_w = rows // nw
    step = SC_ROWS_PER_STEP
    assert per_w * nw == rows and per_w % step == 0
    mesh = plsc.VectorSubcoreMesh(core_axis_name="c", subcore_axis_name="s")

    @functools.partial(
        pl.kernel, mesh=mesh,
        out_type=jax.ShapeDtypeStruct((TOP_K * rows, width), table.dtype),
        scratch_types=[pltpu.VMEM((step,), I32), pltpu.VMEM((step, width), table.dtype),
                       pltpu.SemaphoreType.DMA],
    )
    def gather_rows(table_hbm, dest_hbm, out_hbm, idx_v, rows_v, sem):
        base = (lax.axis_index("s") * nc + lax.axis_index("c")) * per_w

        @pl.loop(0, per_w // step)
        def _(i):
            off = base + i * step
            for j in range(TOP_K):
                pltpu.sync_copy(dest_hbm.at[j, pl.ds(row0 + off, step)], idx_v)
                pltpu.async_copy(table_hbm.at[idx_v], rows_v, sem).wait()
                pltpu.sync_copy(rows_v, out_hbm.at[pl.ds(j * rows + off, step)])

    return gather_rows(table, dest)


def _expert_body(bexp_ref, nused_ref, valid_ref, xs_ref, wgu_ref, bgu_ref, wd_ref, bd_ref, o_ref,
                 wgu_bf, wd_bf):
    n = pl.program_id(0)
    active = n < nused_ref[0]
    new_expert = (n == 0) | (bexp_ref[n] != bexp_ref[jnp.maximum(n - 1, 0)])

    @pl.when(active & new_expert)
    def _():
        wgu_bf[...] = wgu_ref[...].astype(BF16)
        wd_bf[...] = wd_ref[...].astype(BF16)

    sub = EXPERT_SUB_ROWS
    for r0 in range(0, xs_ref.shape[0], sub):
        rs = slice(r0, r0 + sub)
        live = active & (valid_ref[n] > r0)

        @pl.when(live)
        def _():
            row = r0 + lax.broadcasted_iota(I32, (sub, 1), 0)
            x = _unpack_halves(jnp.where(row < valid_ref[n], xs_ref[rs, :], 0), BF16)
            gu = jnp.dot(x, wgu_bf[...], preferred_element_type=F32) + bgu_ref[...]
            g = jnp.minimum(gu[:, 0:D_FF], SWIGLU_LIMIT)
            u = jnp.clip(gu[:, D_FF:2 * D_FF], -SWIGLU_LIMIT, SWIGLU_LIMIT)
            act = g * jax.nn.sigmoid(SWIGLU_ALPHA * g) * (u + 1.0)
            out = jnp.dot(act.astype(BF16), wd_bf[...], preferred_element_type=F32) + bd_ref[...]
            o_ref[rs, :] = _pack_halves(out.astype(BF16))

        @pl.when(jnp.logical_not(live))
        def _():
            o_ref[rs, :] = jnp.zeros((sub, o_ref.shape[1]), o_ref.dtype)


def _experts(block_exp, n_used, block_valid, xs, wgu, bgu, wd, bd, bm):
    P = xs.shape[0]
    nblk = P // bm
    xmap = lambda n, be, nu, bv: (jnp.minimum(n, nu[0] - 1), 0)
    emap = lambda n, be, nu, bv: (be[n], 0, 0)
    grid_spec = pltpu.PrefetchScalarGridSpec(
        num_scalar_prefetch=3,
        grid=(nblk,),
        in_specs=[
            pl.BlockSpec((bm, PACKED), xmap),
            pl.BlockSpec((None, D_MODEL, 2 * D_FF), emap),
            pl.BlockSpec((None, 1, 2 * D_FF), emap),
            pl.BlockSpec((None, D_FF, D_MODEL), emap),
            pl.BlockSpec((None, 1, D_MODEL), emap),
        ],
        out_specs=pl.BlockSpec((bm, PACKED), lambda n, be, nu, bv: (n, 0)),
        scratch_shapes=[pltpu.VMEM((D_MODEL, 2 * D_FF), BF16), pltpu.VMEM((D_FF, D_MODEL), BF16)],
    )
    return pl.pallas_call(
        _expert_body,
        grid_spec=grid_spec,
        out_shape=jax.ShapeDtypeStruct((P, PACKED), I32),
        compiler_params=_params("arbitrary"),
        name="experts",
    )(block_exp, n_used, block_valid, xs, wgu, bgu, wd, bd)


def _combine_body(x1_ref, g0_ref, g1_ref, g2_ref, g3_ref, gate_ref, gfin_ref, y_ref):
    tf = x1_ref.shape[0]
    gate_rows = jnp.concatenate(
        [gate_ref[...], jnp.zeros((LANES - TOP_K, tf), F32)], axis=0)
    for c in range(tf // LANES):
        rs = slice(c * LANES, (c + 1) * LANES)
        gate = gate_rows[:, rs].T
        y = x1_ref[rs, :]
        for j, g_ref in enumerate((g0_ref, g1_ref, g2_ref, g3_ref)):
            y = y + gate[:, j:j + 1] * _unpack_halves(g_ref[rs, :], F32)
        ms = jnp.mean(y * y, axis=-1, keepdims=True)
        y_ref[rs, :] = y * lax.rsqrt(ms + NORM_EPS) * gfin_ref[...]


def _combine(x1, gathered, gate_t, gfin, row0, rows, tf):
    nt = rows // tf
    t0 = row0 // tf
    row = lambda i: (t0 + i, 0)
    choice = lambda j: pl.BlockSpec((tf, PACKED), lambda i: (j * nt + i, 0))
    return pl.pallas_call(
        _combine_body,
        grid=(nt,),
        in_specs=[pl.BlockSpec((tf, D_MODEL), row)] + [choice(j) for j in range(TOP_K)] + [
            pl.BlockSpec((TOP_K, tf), lambda i: (0, t0 + i)),
            pl.BlockSpec((1, D_MODEL), lambda i: (0, 0)),
        ],
        out_specs=pl.BlockSpec((tf, D_MODEL), lambda i: (i, 0)),
        out_shape=jax.ShapeDtypeStruct((rows, D_MODEL), F32),
        compiler_params=_params("parallel"),
        name="combine",
    )(x1, gathered, gathered, gathered, gathered, gate_t, gfin)


def _permute_in_proj(w):
    lead = w.shape[:-1]
    half = HEAD_DIM // 2
    q = w[..., :Q_COLS].reshape(*lead, N_Q_HEADS // 2, 2, 2, half)
    q = jnp.swapaxes(q, -3, -2).reshape(*lead, Q_COLS)
    k = w[..., Q_COLS:Q_COLS + KV_COLS].reshape(*lead, N_KV_HEADS, 2, 1, half)
    k = jnp.broadcast_to(k, (*lead, N_KV_HEADS, 2, 2, half)).reshape(*lead, KV_DUP_COLS)
    v = w[..., Q_COLS + KV_COLS:Q_COLS + 2 * KV_COLS].reshape(*lead, N_KV_HEADS, 1, HEAD_DIM)
    v = jnp.broadcast_to(v, (*lead, N_KV_HEADS, 2, HEAD_DIM)).reshape(*lead, KV_DUP_COLS)
    return jnp.concatenate([q, k, w[..., Q_COLS + 2 * KV_COLS:]], axis=-1), v


def _rope_tables(n_pos):
    half = HEAD_DIM // 2
    inv_freq = 1.0 / (ROPE_THETA ** (jnp.arange(half, dtype=F32) * (2.0 / HEAD_DIM)))
    ang = jnp.arange(n_pos, dtype=F32)[:, None] * inv_freq[None, :]
    cos = jnp.tile(jnp.cos(ang), (1, LANES // half))
    sin = jnp.tile(jnp.sin(ang), (1, LANES // half))
    sign = jnp.where(jnp.arange(LANES) < LANES // 2, -1.0, 1.0).astype(F32)
    return cos, sin * sign[None, :]


def _tiles(geo):
    unit = min(geo.len_a, geo.len_b)
    tile = min(512, unit)
    return dict(tm=tile, tc=tile, tf=tile, bm=2 * EXPERT_SUB_ROWS)


def kernel(x_prompt, x_sample, norm_mix_g, w_in, b_in, attn_sink, w_o_attn, conv_dw_w, conv_dw_b,
           conv_ln_g, conv_ln_b, w_pw2, b_pw2, w_out, norm_ffn_g, w_router, b_router, w_gu, b_gu,
           w_down, b_down, norm_final_g):
    assert w_in.shape[0] == 1, "single trunk layer"
    geo = Geo(x_prompt.shape[0], x_prompt.shape[1], x_sample.shape[0], x_sample.shape[1])
    T = geo.total
    ts = _tiles(geo)
    xa = x_prompt.reshape(-1, D_MODEL)
    xb = x_sample.reshape(-1, D_MODEL)

    w_perm, w_v = _permute_in_proj(w_in[0])
    b_perm, b_v = _permute_in_proj(b_in)
    cos_t, sin_t = _rope_tables(max(geo.len_a, geo.len_b))

    q, kd, vt, glu, gates = _in_proj(xa, xb, norm_mix_g, w_perm.astype(BF16), b_perm,
                                     w_v.T.astype(BF16), b_v.T, cos_t, sin_t, geo, ts["tm"])
    attn = _attention(q, kd, vt, attn_sink[0], geo)
    w_rep = jnp.repeat(conv_dw_w[0], SUBLANES, axis=0)
    conv = _conv_branch(glu, w_rep, conv_dw_b, conv_ln_g, conv_ln_b, geo, ts["tc"])

    wr_t = w_router[0].T
    wr_hi = wr_t.astype(BF16)
    wr_lo = (wr_t - wr_hi.astype(F32)).astype(BF16)
    tri = jnp.triu(jnp.ones((ts["tm"], ts["tm"]), BF16), 1)
    x1, h2, idx, gate_t, rank, counts = _mix_route(
        xa, xb, attn, conv, gates, w_o_attn[0].astype(BF16), w_pw2[0].astype(BF16), b_pw2,
        w_out[0].astype(BF16), norm_ffn_g, wr_hi, wr_lo, b_router[0][:, None], tri, geo, ts["tm"])

    bm = ts["bm"]
    n_blocks = (T * TOP_K) // bm + N_EXPERTS
    cnt = counts[:, 0].astype(I32)
    padded = ((cnt + bm - 1) // bm) * bm
    pad_end = jnp.cumsum(padded)
    pad_start = pad_end - padded
    expert_ids = jnp.arange(N_EXPERTS, dtype=I32)
    dest = rank + jnp.sum(jnp.where(idx[None] == expert_ids[:, None, None],
                                    pad_start[:, None, None], 0), axis=0)
    block_start = jnp.arange(n_blocks, dtype=I32) * bm
    block_exp = jnp.minimum(jnp.sum((pad_end[None, :] <= block_start[:, None]).astype(I32), axis=1),
                            N_EXPERTS - 1)
    n_used = (pad_end[-1:] // bm).astype(I32)
    seg_end = jnp.sum(jnp.where(block_exp[:, None] == expert_ids[None, :],
                                (pad_start + cnt)[None, :], 0), axis=1)
    block_valid = jnp.clip(seg_end - block_start, 0, bm).astype(I32)

    xs = _sc_dispatch(h2, dest, n_blocks * bm)
    ys = _experts(block_exp, n_used, block_valid, xs, w_gu[0], b_gu[0][:, None, :],
                  w_down[0], b_down[0][:, None, :], bm)
    outs = []
    for row0, rows in ((0, geo.rows_a), (geo.rows_a, T - geo.rows_a)):
        gathered = _sc_gather(ys, dest, row0, rows)
        outs.append(_combine(x1, gathered, gate_t, norm_final_g[None, :], row0, rows, ts["tf"]))
    return (outs[0].reshape(x_prompt.shape), outs[1].reshape(x_sample.shape))
```

```python
import functools
import math
from typing import NamedTuple

import jax
import jax.numpy as jnp
from jax import lax
from jax.experimental import pallas as pl
from jax.experimental.pallas import tpu as pltpu
from jax.experimental.pallas import tpu_sc as plsc

F32 = jnp.float32
BF16 = jnp.bfloat16
I32 = jnp.int32

D_MODEL = 1024
HEAD_DIM = 64
N_Q_HEADS = 16
N_KV_HEADS = 4
WINDOW = 128
ATTN_BLOCK = 128
ROPE_THETA = 10000.0
CONV_WIDTH = 31
CONV_PAD = CONV_WIDTH // 2
N_EXPERTS = 32
TOP_K = 4
D_FF = D_MODEL
SWIGLU_LIMIT = 7.0
SWIGLU_ALPHA = 1.702
NORM_EPS = 1e-5
NEG_INF = -1e30

Q_COLS = N_Q_HEADS * HEAD_DIM
KV_COLS = N_KV_HEADS * HEAD_DIM
LANES = 128
SUBLANES = 8
KV_DUP_COLS = N_KV_HEADS * LANES
HALO = 16
VMEM_LIMIT = 56 * 1024 * 1024
SC_ROWS_PER_STEP = 128
EXPERT_SUB_ROWS = 512


class Geo(NamedTuple):
    n_a: int
    len_a: int
    n_b: int
    len_b: int

    @property
    def rows_a(self):
        return self.n_a * self.len_a

    @property
    def total(self):
        return self.rows_a + self.n_b * self.len_b


def _seq_bounds(geo, r):
    in_a = r < geo.rows_a
    start_a = (r // geo.len_a) * geo.len_a
    start_b = geo.rows_a + ((r - geo.rows_a) // geo.len_b) * geo.len_b
    start = jnp.where(in_a, start_a, start_b)
    end = start + jnp.where(in_a, geo.len_a, geo.len_b)
    return start, end


def _params(*sem):
    return pltpu.CompilerParams(dimension_semantics=sem, vmem_limit_bytes=VMEM_LIMIT)


PACKED = D_MODEL // 2


def _pack_halves(x_bf16):
    hi = lax.bitcast_convert_type(x_bf16[:, :PACKED].astype(F32), I32)
    lo = lax.bitcast_convert_type(x_bf16[:, PACKED:].astype(F32), I32)
    return hi | lax.shift_right_logical(lo, 16)


def _unpack_halves(words, dtype):
    hi = lax.bitcast_convert_type(words & jnp.int32(-65536), F32)
    lo = lax.bitcast_convert_type(lax.shift_left(words, 16), F32)
    return jnp.concatenate([hi.astype(dtype), lo.astype(dtype)], axis=1)


def _two_group_specs(geo, tile, width):
    n_a = geo.rows_a // tile
    return n_a, [pl.BlockSpec((tile, width), lambda i: (jnp.minimum(i, n_a - 1), 0)),
                 pl.BlockSpec((tile, width), lambda i: (jnp.maximum(i - n_a, 0), 0))]


_C_Q = 0
_C_K = _C_Q + Q_COLS
_C_GA = _C_K + KV_DUP_COLS
_C_GG = _C_GA + D_MODEL
_C_GATE = _C_GG + D_MODEL
_N_IN = _C_GATE + 2 * D_MODEL
_PROJ_CHUNK = 512
N_SLAB = D_MODEL // LANES
_NT = (((1,), (1,)), ((), ()))
_LOG2E = math.log2(math.e)
_Q_SCALE = HEAD_DIM ** -0.5 * _LOG2E


def _in_proj_body(n_a, xa_ref, xb_ref, g_ref, w_ref, b_ref, wvt_ref, bvt_ref, cos_ref, sin_ref,
                  q_ref, kd_ref, vt_ref, glu_ref, gate_ref):
    x = jnp.where(pl.program_id(0) < n_a, xa_ref[...], xb_ref[...])
    ms = jnp.mean(x * x, axis=-1, keepdims=True)
    h = (x * lax.rsqrt(ms + NORM_EPS) * g_ref[...]).astype(BF16)
    cos = cos_ref[...]
    sin = sin_ref[...]

    def proj(c0):
        return (jnp.dot(h, w_ref[:, c0:c0 + _PROJ_CHUNK], preferred_element_type=F32)
                + b_ref[:, c0:c0 + _PROJ_CHUNK])

    def rope_store(z, out_ref, o0, scale):
        for c in range(_PROJ_CHUNK // LANES):
            zc = z[:, c * LANES:(c + 1) * LANES]
            r = zc * cos + pltpu.roll(zc, LANES // 2, 1) * sin
            if scale != 1.0:
                r = r * scale
            out_ref[:, o0 + c * LANES:o0 + (c + 1) * LANES] = r.astype(out_ref.dtype)

    for c in range(Q_COLS // _PROJ_CHUNK):
        rope_store(proj(_C_Q + c * _PROJ_CHUNK), q_ref, c * _PROJ_CHUNK, _Q_SCALE)
    for c in range(KV_DUP_COLS // _PROJ_CHUNK):
        rope_store(proj(_C_K + c * _PROJ_CHUNK), kd_ref, c * _PROJ_CHUNK, 1.0)
    vt = lax.dot_general(wvt_ref[...], h, _NT, preferred_element_type=F32) + bvt_ref[...]
    vt_ref[...] = vt.astype(BF16)
    per = _PROJ_CHUNK // LANES
    for c in range(D_MODEL // _PROJ_CHUNK):
        a = proj(_C_GA + c * _PROJ_CHUNK)
        g = proj(_C_GG + c * _PROJ_CHUNK)
        glu = a * jax.nn.sigmoid(g)
        for s in range(per):
            glu_ref[c * per + s] = glu[:, s * LANES:(s + 1) * LANES]
    for c in range(2 * D_MODEL // _PROJ_CHUNK):
        gate_ref[:, c * _PROJ_CHUNK:(c + 1) * _PROJ_CHUNK] = jax.nn.sigmoid(
            proj(_C_GATE + c * _PROJ_CHUNK)).astype(BF16)


def _in_proj(xa, xb, g_mix, w_perm, b_perm, wvt, bvt, cos_t, sin_t, geo, tm):
    T = geo.total

    def pos_map(i):
        r0 = i * tm
        start, _ = _seq_bounds(geo, r0)
        return ((r0 - start) // tm, 0)

    const = lambda i: (0, 0)
    row = lambda i: (i, 0)
    n_a, x_specs = _two_group_specs(geo, tm, D_MODEL)
    return pl.pallas_call(
        functools.partial(_in_proj_body, n_a),
        grid=(T // tm,),
        in_specs=x_specs + [
            pl.BlockSpec((1, D_MODEL), const),
            pl.BlockSpec((D_MODEL, _N_IN), const, pipeline_mode=pl.Buffered(1)),
            pl.BlockSpec((1, _N_IN), const),
            pl.BlockSpec((KV_COLS, D_MODEL), const),
            pl.BlockSpec((KV_COLS, 1), const),
            pl.BlockSpec((tm, LANES), pos_map),
            pl.BlockSpec((tm, LANES), pos_map),
        ],
        out_specs=[
            pl.BlockSpec((tm, Q_COLS), row),
            pl.BlockSpec((tm, KV_DUP_COLS), row),
            pl.BlockSpec((KV_COLS, tm), lambda i: (0, i)),
            pl.BlockSpec((N_SLAB, tm, LANES), lambda i: (0, i, 0)),
            pl.BlockSpec((tm, 2 * D_MODEL), row),
        ],
        out_shape=[
            jax.ShapeDtypeStruct((T, Q_COLS), BF16),
            jax.ShapeDtypeStruct((T, KV_DUP_COLS), BF16),
            jax.ShapeDtypeStruct((KV_COLS, T), BF16),
            jax.ShapeDtypeStruct((N_SLAB, T, LANES), F32),
            jax.ShapeDtypeStruct((T, 2 * D_MODEL), BF16),
        ],
        compiler_params=_params("parallel"),
        name="in_proj",
    )(xa, xb, g_mix, w_perm, b_perm, wvt, bvt, cos_t, sin_t)


_ONES_ROWS = 16
_ATTN_STEP_BLOCKS = 4


def _attn_body(geo, sink_ref, q_ref, kp_ref, kc_ref, kn_ref, vp_ref, vc_ref, vn_ref, o_ref):
    group = N_Q_HEADS // N_KV_HEADS
    nq = group * ATTN_BLOCK
    b = ATTN_BLOCK
    hd = HEAD_DIM
    nsub = _ATTN_STEP_BLOCKS

    key = lax.broadcasted_iota(I32, (b, nq), 0)
    qry = lax.broadcasted_iota(I32, (b, nq), 1) % b
    head_of_col = lax.broadcasted_iota(I32, (1, nq), 1) // b
    lane = lax.broadcasted_iota(I32, (b, LANES), 1)
    even_head = (lane % hd) < (hd // 2)
    ones = jnp.ones((_ONES_ROWS, 3 * b), BF16)

    for s in range(nsub):
        rows = slice(s * b, (s + 1) * b)
        r0 = (pl.program_id(0) * nsub + s) * b
        start, end = _seq_bounds(geo, r0)
        bias_prev = jnp.where((key >= qry) & (r0 > start), 0.0, NEG_INF)
        bias_next = jnp.where((key <= qry) & (r0 + b < end), 0.0, NEG_INF)

        for g in range(N_KV_HEADS):
            ls = slice(g * LANES, (g + 1) * LANES)
            k_prev = kp_ref[:, ls] if s == 0 else kc_ref[(s - 1) * b:s * b, ls]
            k_next = kn_ref[:, ls] if s == nsub - 1 else kc_ref[(s + 1) * b:(s + 2) * b, ls]
            vs = slice(g * hd, (g + 1) * hd)
            v_prev = vp_ref[vs, :] if s == 0 else vc_ref[vs, (s - 1) * b:s * b]
            v_next = vn_ref[vs, :] if s == nsub - 1 else vc_ref[vs, (s + 1) * b:(s + 2) * b]

            qa = q_ref[rows, (2 * g) * LANES:(2 * g + 1) * LANES]
            qb = q_ref[rows, (2 * g + 1) * LANES:(2 * g + 2) * LANES]
            zero = jnp.zeros_like(qa)
            q4 = jnp.concatenate([jnp.where(even_head, qa, zero), jnp.where(even_head, zero, qa),
                                  jnp.where(even_head, qb, zero), jnp.where(even_head, zero, qb)],
                                 axis=0)
            k = jnp.concatenate([k_prev, kc_ref[rows, ls], k_next], axis=0)
            st = lax.dot_general(k, q4, _NT, preferred_element_type=F32)
            s_prev = st[0:b] + bias_prev
            s_cur = st[b:2 * b]
            s_next = st[2 * b:3 * b] + bias_next
            sink = jnp.full((1, nq), sink_ref[group * g] * _LOG2E, F32)
            for h in range(1, group):
                sink = jnp.where(head_of_col == h, sink_ref[group * g + h] * _LOG2E, sink)
            m = jnp.maximum(jnp.maximum(jnp.max(s_prev, axis=0, keepdims=True),
                                        jnp.max(s_cur, axis=0, keepdims=True)),
                            jnp.maximum(jnp.max(s_next, axis=0, keepdims=True), sink))
            p = jnp.concatenate([jnp.exp2(s_prev - m).astype(BF16), jnp.exp2(s_cur - m).astype(BF16),
                                 jnp.exp2(s_next - m).astype(BF16)], axis=0)
            vt = jnp.concatenate([v_prev, vc_ref[vs, rows], v_next], axis=1)
            ot = jnp.dot(jnp.concatenate([vt, ones], axis=0), p, preferred_element_type=F32)
            denom = ot[hd:hd + 1] + jnp.exp2(sink - m)
            on = ot[0:hd] * (1.0 / denom)
            pair_a = jnp.concatenate([on[:, 0:b], on[:, b:2 * b]], axis=0)
            pair_b = jnp.concatenate([on[:, 2 * b:3 * b], on[:, 3 * b:4 * b]], axis=0)
            o_ref[rows, (2 * g) * LANES:(2 * g + 1) * LANES] = pair_a.T.astype(BF16)
            o_ref[rows, (2 * g + 1) * LANES:(2 * g + 2) * LANES] = pair_b.T.astype(BF16)


def _attention(q, kd, vt, sink, geo):
    T = q.shape[0]
    nsub = _ATTN_STEP_BLOCKS
    nb = T // ATTN_BLOCK
    step_rows = nsub * ATTN_BLOCK
    prev = lambda i: jnp.maximum(i * nsub - 1, 0)
    nxt = lambda i: jnp.minimum((i + 1) * nsub, nb - 1)
    k_edge = lambda m: pl.BlockSpec((ATTN_BLOCK, KV_DUP_COLS), lambda i: (m(i), 0))
    v_edge = lambda m: pl.BlockSpec((KV_COLS, ATTN_BLOCK), lambda i: (0, m(i)))
    return pl.pallas_call(
        functools.partial(_attn_body, geo),
        grid=(T // step_rows,),
        in_specs=[
            pl.BlockSpec(memory_space=pltpu.SMEM),
            pl.BlockSpec((step_rows, Q_COLS), lambda i: (i, 0)),
            k_edge(prev), pl.BlockSpec((step_rows, KV_DUP_COLS), lambda i: (i, 0)), k_edge(nxt),
            v_edge(prev), pl.BlockSpec((KV_COLS, step_rows), lambda i: (0, i)), v_edge(nxt),
        ],
        out_specs=pl.BlockSpec((step_rows, Q_COLS), lambda i: (i, 0)),
        out_shape=jax.ShapeDtypeStruct((T, Q_COLS), BF16),
        compiler_params=_params("parallel"),
        name="window_attn",
    )(sink, q, kd, kd, kd, vt, vt, vt)


_CONV_ROWS = 128
_LN_ROWS = 32


def _conv_body(geo, tc, z_ref, zp_ref, zn_ref, w_ref, dwb_ref, lng_ref, lnb_ref, o_ref,
               buf_ref, y_ref):
    i = pl.program_id(0)
    r0 = i * tc
    start, end = _seq_bounds(geo, r0)
    has_prev = (r0 > start).astype(F32)
    has_next = (r0 + tc < end).astype(F32)
    rows = tc + 2 * HALO
    buf_ref[:, 0:HALO, :] = zp_ref[...] * has_prev
    buf_ref[:, HALO:HALO + tc, :] = z_ref[...]
    buf_ref[:, HALO + tc:rows, :] = zn_ref[...] * has_next

    rep = _CONV_ROWS // SUBLANES

    def conv_chunk(j):
        s0 = pl.multiple_of(j * _CONV_ROWS, _CONV_ROWS)
        for c in range(N_SLAB):
            ls = slice(c * LANES, (c + 1) * LANES)
            acc = jnp.zeros((_CONV_ROWS, LANES), F32)
            for k in range(CONV_WIDTH):
                zt = buf_ref[c, pl.ds(s0 + (HALO - CONV_PAD + k), _CONV_ROWS, stride=1), :]
                wt = w_ref[k * SUBLANES:(k + 1) * SUBLANES, ls]
                acc = acc + zt * jnp.concatenate([wt] * rep, axis=0)
            y_ref[pl.ds(s0, _CONV_ROWS), ls] = acc

    def ln_chunk(j):
        for h in range(_CONV_ROWS // _LN_ROWS):
            s0 = pl.multiple_of(j * _CONV_ROWS + h * _LN_ROWS, _LN_ROWS)
            y = y_ref[pl.ds(s0, _LN_ROWS), :] + dwb_ref[...]
            mu = jnp.mean(y, axis=-1, keepdims=True)
            yc = y - mu
            var = jnp.mean(yc * yc, axis=-1, keepdims=True)
            yn = yc * lax.rsqrt(var + NORM_EPS) * lng_ref[...] + lnb_ref[...]
            o_ref[pl.ds(s0, _LN_ROWS), :] = (yn * jax.nn.sigmoid(yn)).astype(BF16)

    n = tc // _CONV_ROWS
    conv_chunk(0)

    def body(j, carry):
        ln_chunk(j - 1)
        conv_chunk(j)
        return carry

    lax.fori_loop(1, n, body, 0)
    ln_chunk(n - 1)


def _conv_branch(glu, w_rep, dw_b, ln_g, ln_b, geo, tc):
    T = glu.shape[1]
    nh = T // HALO
    per = tc // HALO
    const = lambda i: (0, 0)
    return pl.pallas_call(
        functools.partial(_conv_body, geo, tc),
        grid=(T // tc,),
        in_specs=[
            pl.BlockSpec((N_SLAB, tc, LANES), lambda i: (0, i, 0)),
            pl.BlockSpec((N_SLAB, HALO, LANES), lambda i: (0, jnp.maximum(i * per - 1, 0), 0)),
            pl.BlockSpec((N_SLAB, HALO, LANES), lambda i: (0, jnp.minimum((i + 1) * per, nh - 1), 0)),
            pl.BlockSpec((CONV_WIDTH * SUBLANES, D_MODEL), const),
            pl.BlockSpec((1, D_MODEL), const),
            pl.BlockSpec((1, D_MODEL), const),
            pl.BlockSpec((1, D_MODEL), const),
        ],
        out_specs=pl.BlockSpec((tc, D_MODEL), lambda i: (i, 0)),
        out_shape=jax.ShapeDtypeStruct((T, D_MODEL), BF16),
        scratch_shapes=[pltpu.VMEM((N_SLAB, tc + 2 * HALO, LANES), F32),
                        pltpu.VMEM((tc, D_MODEL), F32)],
        compiler_params=_params("parallel"),
        name="conv_branch",
    )(glu, glu, glu, w_rep, dw_b, ln_g, ln_b)


_MIX_COLS = 256


def _mix_body(n_a, xa_ref, xb_ref, o_ref, c_ref, gate_ref, wo_ref, wpw_ref, bpw_ref, wout_ref,
              gffn_ref, wrh_ref, wrl_ref, br_ref, tri_ref,
              x1_ref, h2_ref, idx_ref, gt_ref, rank_ref, cnt_ref, x1_scr, mix_ref, carry_ref):
    tm = xa_ref.shape[0]
    i = pl.program_id(0)
    last = pl.num_programs(0) - 2
    slot = i % 2

    @pl.when(i == 0)
    def _():
        carry_ref[...] = jnp.zeros_like(carry_ref)
        x1_scr[...] = jnp.zeros_like(x1_scr)

    x1_prev = x1_scr[1 - slot]
    x1_new = x1_scr.at[slot]
    in_a = jnp.minimum(i, last) < n_a
    n_pieces = D_MODEL // _MIX_COLS

    def mix_piece(p):
        cs = slice(p * _MIX_COLS, (p + 1) * _MIX_COLS)
        gs = slice(D_MODEL + p * _MIX_COLS, D_MODEL + (p + 1) * _MIX_COLS)
        attn = jnp.dot(o_ref[...], wo_ref[:, cs], preferred_element_type=F32)
        conv = jnp.dot(c_ref[...], wpw_ref[:, cs], preferred_element_type=F32) + bpw_ref[:, cs]
        mix_ref[:, cs] = (gate_ref[:, cs].astype(F32) * attn
                          + gate_ref[:, gs].astype(F32) * conv).astype(BF16)

    def out_piece(p):
        cs = slice(p * _MIX_COLS, (p + 1) * _MIX_COLS)
        x = jnp.where(in_a, xa_ref[:, cs], xb_ref[:, cs])
        x1 = x + jnp.dot(mix_ref[...], wout_ref[:, cs], preferred_element_type=F32)
        x1_ref[:, cs] = x1
        x1_new[:, cs] = x1

    mix_piece(0)
    ms = jnp.mean(x1_prev * x1_prev, axis=-1, keepdims=True)
    h2 = x1_prev * lax.rsqrt(ms + NORM_EPS) * gffn_ref[...]
    h_hi = h2.astype(BF16)
    h2_ref[...] = _pack_halves(h_hi)
    mix_piece(1)

    h_lo = (h2 - h_hi.astype(F32)).astype(BF16)
    logits = (lax.dot_general(wrh_ref[...], h_hi, _NT, preferred_element_type=F32)
              + lax.dot_general(wrh_ref[...], h_lo, _NT, preferred_element_type=F32)
              + lax.dot_general(wrl_ref[...], h_hi, _NT, preferred_element_type=F32)
              + br_ref[...])
    for p in range(2, n_pieces):
        mix_piece(p)
    out_piece(0)

    eidx = lax.broadcasted_iota(I32, (N_EXPERTS, tm), 0)
    vals = logits
    picked, top_vals, top_idx = [], [], []
    for _ in range(TOP_K):
        m = jnp.max(vals, axis=0, keepdims=True)
        idx = jnp.min(jnp.where(vals == m, eidx, N_EXPERTS), axis=0, keepdims=True)
        sel = eidx == idx
        vals = jnp.where(sel, -jnp.inf, vals)
        picked.append(sel)
        top_vals.append(m)
        top_idx.append(idx)
    out_piece(1)

    exps = [jnp.exp(v - top_vals[0]) for v in top_vals]
    tot = exps[0] + exps[1] + exps[2] + exps[3]
    onehot = (picked[0] | picked[1] | picked[2] | picked[3])
    prefix = jnp.dot(onehot.astype(BF16), tri_ref[...], preferred_element_type=F32) + carry_ref[...]
    for p in range(2, n_pieces):
        out_piece(p)
    for j in range(TOP_K):
        idx_ref[j:j + 1, :] = top_idx[j]
        gt_ref[j:j + 1, :] = exps[j] / tot
        rank_ref[j:j + 1, :] = jnp.sum(jnp.where(picked[j], prefix, 0.0), axis=0,
                                       keepdims=True).astype(I32)
    counted = jnp.where(i > 0, jnp.sum(onehot.astype(F32), axis=1, keepdims=True), 0.0)
    carry_ref[...] = carry_ref[...] + counted
    cnt_ref[...] = jnp.broadcast_to(carry_ref[...], cnt_ref.shape)


def _mix_route(xa, xb, o, c, gates, wo, wpw, bpw, wout, gffn, wr_hi, wr_lo, br, tri, geo, tm):
    T = geo.total
    n = T // tm
    n_a = geo.rows_a // tm
    head = lambda i: jnp.minimum(i, n - 1)
    tail = lambda i: jnp.maximum(i - 1, 0)
    row = lambda i: (head(i), 0)
    col = lambda i: (0, tail(i))
    const = lambda i: (0, 0)
    wspec = lambda shape: pl.BlockSpec(shape, const, pipeline_mode=pl.Buffered(1))
    x_specs = [pl.BlockSpec((tm, D_MODEL), lambda i: (jnp.minimum(head(i), n_a - 1), 0)),
               pl.BlockSpec((tm, D_MODEL), lambda i: (jnp.maximum(head(i) - n_a, 0), 0))]
    return pl.pallas_call(
        functools.partial(_mix_body, n_a),
        grid=(n + 1,),
        in_specs=x_specs + [
            pl.BlockSpec((tm, Q_COLS), row),
            pl.BlockSpec((tm, D_MODEL), row),
            pl.BlockSpec((tm, 2 * D_MODEL), row),
            wspec((Q_COLS, D_MODEL)),
            wspec((D_MODEL, D_MODEL)),
            pl.BlockSpec((1, D_MODEL), const),
            wspec((D_MODEL, D_MODEL)),
            pl.BlockSpec((1, D_MODEL), const),
            pl.BlockSpec((N_EXPERTS, D_MODEL), const),
            pl.BlockSpec((N_EXPERTS, D_MODEL), const),
            pl.BlockSpec((N_EXPERTS, 1), const),
            pl.BlockSpec((tm, tm), const),
        ],
        out_specs=[
            pl.BlockSpec((tm, D_MODEL), row),
            pl.BlockSpec((tm, PACKED), lambda i: (tail(i), 0)),
            pl.BlockSpec((TOP_K, tm), col),
            pl.BlockSpec((TOP_K, tm), col),
            pl.BlockSpec((TOP_K, tm), col),
            pl.BlockSpec((N_EXPERTS, LANES), const),
        ],
        out_shape=[
            jax.ShapeDtypeStruct((T, D_MODEL), F32),
            jax.ShapeDtypeStruct((T, PACKED), I32),
            jax.ShapeDtypeStruct((TOP_K, T), I32),
            jax.ShapeDtypeStruct((TOP_K, T), F32),
            jax.ShapeDtypeStruct((TOP_K, T), I32),
            jax.ShapeDtypeStruct((N_EXPERTS, LANES), F32),
        ],
        scratch_shapes=[pltpu.VMEM((2, tm, D_MODEL), F32), pltpu.VMEM((tm, D_MODEL), BF16),
                        pltpu.VMEM((N_EXPERTS, 1), F32)],
        compiler_params=_params("arbitrary"),
        name="mix_route",
    )(xa, xb, o, c, gates, wo, wpw, bpw, wout, gffn, wr_hi, wr_lo, br, tri)


def _sc_workers():
    info = plsc.get_sparse_core_info()
    return info.num_cores, info.num_cores * info.num_subcores


def _sc_dispatch(h, dest, n_out):
    t_rows, width = h.shape
    nc, nw = _sc_workers()
    per_w = t_rows // nw
    step = SC_ROWS_PER_STEP
    assert per_w * nw == t_rows and per_w % step == 0
    mesh = plsc.VectorSubcoreMesh(core_axis_name="c", subcore_axis_name="s")

    @functools.partial(
        pl.kernel, mesh=mesh,
        out_type=jax.ShapeDtypeStruct((n_out, width), h.dtype),
        scratch_types=[pltpu.VMEM((step,), I32)] * TOP_K + [pltpu.VMEM((step, width), h.dtype)],
    )
    def scatter_rows(h_hbm, dest_hbm, out_hbm, i0, i1, i2, i3, rows_v):
        base = (lax.axis_index("s") * nc + lax.axis_index("c")) * per_w

        @pl.loop(0, per_w // step)
        def _(i):
            off = base + i * step
            pltpu.sync_copy(h_hbm.at[pl.ds(off, step)], rows_v)
            for j, idx_v in enumerate((i0, i1, i2, i3)):
                pltpu.sync_copy(dest_hbm.at[j, pl.ds(off, step)], idx_v)
                pltpu.sync_copy(rows_v, out_hbm.at[idx_v])

    return scatter_rows(h, dest)


def _sc_gather(table, dest, row0, rows):
    width = table.shape[1]
    nc, nw = _sc_workers()
    per_w = rows // nw
    step = SC_ROWS_PER_STEP
    assert per_w * nw == rows and per_w % step == 0
    mesh = plsc.VectorSubcoreMesh(core_axis_name="c", subcore_axis_name="s")

    @functools.partial(
        pl.kernel, mesh=mesh,
        out_type=jax.ShapeDtypeStruct((TOP_K * rows, width), table.dtype),
        scratch_types=[pltpu.VMEM((step,), I32), pltpu.VMEM((step, width), table.dtype),
                       pltpu.SemaphoreType.DMA],
    )
    def gather_rows(table_hbm, dest_hbm, out_hbm, idx_v, rows_v, sem):
        base = (lax.axis_index("s") * nc + lax.axis_index("c")) * per_w

        @pl.loop(0, per_w // step)
        def _(i):
            off = base + i * step
            for j in range(TOP_K):
                pltpu.sync_copy(dest_hbm.at[j, pl.ds(row0 + off, step)], idx_v)
                pltpu.async_copy(table_hbm.at[idx_v], rows_v, sem).wait()
                pltpu.sync_copy(rows_v, out_hbm.at[pl.ds(j * rows + off, step)])

    return gather_rows(table, dest)


def _expert_body(bexp_ref, nused_ref, valid_ref, xs_ref, wgu_ref, bgu_ref, wd_ref, bd_ref, o_ref,
                 wgu_bf, wd_bf):
    n = pl.program_id(0)
    active = n < nused_ref[0]
    new_expert = (n == 0) | (bexp_ref[n] != bexp_ref[jnp.maximum(n - 1, 0)])

    @pl.when(active & new_expert)
    def _():
        wgu_bf[...] = wgu_ref[...].astype(BF16)
        wd_bf[...] = wd_ref[...].astype(BF16)

    sub = EXPERT_SUB_ROWS
    for r0 in range(0, xs_ref.shape[0], sub):
        rs = slice(r0, r0 + sub)
        live = active & (valid_ref[n] > r0)

        @pl.when(live)
        def _():
            row = r0 + lax.broadcasted_iota(I32, (sub, 1), 0)
            x = _unpack_halves(jnp.where(row < valid_ref[n], xs_ref[rs, :], 0), BF16)
            gu = jnp.dot(x, wgu_bf[...], preferred_element_type=F32) + bgu_ref[...]
            g = jnp.minimum(gu[:, 0:D_FF], SWIGLU_LIMIT)
            u = jnp.clip(gu[:, D_FF:2 * D_FF], -SWIGLU_LIMIT, SWIGLU_LIMIT)
            act = g * jax.nn.sigmoid(SWIGLU_ALPHA * g) * (u + 1.0)
            out = jnp.dot(act.astype(BF16), wd_bf[...], preferred_element_type=F32) + bd_ref[...]
            o_ref[rs, :] = _pack_halves(out.astype(BF16))

        @pl.when(jnp.logical_not(live))
        def _():
            o_ref[rs, :] = jnp.zeros((sub, o_ref.shape[1]), o_ref.dtype)


def _experts(block_exp, n_used, block_valid, xs, wgu, bgu, wd, bd, bm):
    P = xs.shape[0]
    nblk = P // bm
    xmap = lambda n, be, nu, bv: (jnp.minimum(n, nu[0] - 1), 0)
    emap = lambda n, be, nu, bv: (be[n], 0, 0)
    grid_spec = pltpu.PrefetchScalarGridSpec(
        num_scalar_prefetch=3,
        grid=(nblk,),
        in_specs=[
            pl.BlockSpec((bm, PACKED), xmap),
            pl.BlockSpec((None, D_MODEL, 2 * D_FF), emap),
            pl.BlockSpec((None, 1, 2 * D_FF), emap),
            pl.BlockSpec((None, D_FF, D_MODEL), emap),
            pl.BlockSpec((None, 1, D_MODEL), emap),
        ],
        out_specs=pl.BlockSpec((bm, PACKED), lambda n, be, nu, bv: (n, 0)),
        scratch_shapes=[pltpu.VMEM((D_MODEL, 2 * D_FF), BF16), pltpu.VMEM((D_FF, D_MODEL), BF16)],
    )
    return pl.pallas_call(
        _expert_body,
        grid_spec=grid_spec,
        out_shape=jax.ShapeDtypeStruct((P, PACKED), I32),
        compiler_params=_params("arbitrary"),
        name="experts",
    )(block_exp, n_used, block_valid, xs, wgu, bgu, wd, bd)


def _combine_body(x1_ref, g0_ref, g1_ref, g2_ref, g3_ref, gate_ref, gfin_ref, y_ref):
    tf = x1_ref.shape[0]
    gate_rows = jnp.concatenate(
        [gate_ref[...], jnp.zeros((LANES - TOP_K, tf), F32)], axis=0)
    for c in range(tf // LANES):
        rs = slice(c * LANES, (c + 1) * LANES)
        gate = gate_rows[:, rs].T
        y = x1_ref[rs, :]
        for j, g_ref in enumerate((g0_ref, g1_ref, g2_ref, g3_ref)):
            y = y + gate[:, j:j + 1] * _unpack_halves(g_ref[rs, :], F32)
        ms = jnp.mean(y * y, axis=-1, keepdims=True)
        y_ref[rs, :] = y * lax.rsqrt(ms + NORM_EPS) * gfin_ref[...]


def _combine(x1, gathered, gate_t, gfin, row0, rows, tf):
    nt = rows // tf
    t0 = row0 // tf
    row = lambda i: (t0 + i, 0)
    choice = lambda j: pl.BlockSpec((tf, PACKED), lambda i: (j * nt + i, 0))
    return pl.pallas_call(
        _combine_body,
        grid=(nt,),
        in_specs=[pl.BlockSpec((tf, D_MODEL), row)] + [choice(j) for j in range(TOP_K)] + [
            pl.BlockSpec((TOP_K, tf), lambda i: (0, t0 + i)),
            pl.BlockSpec((1, D_MODEL), lambda i: (0, 0)),
        ],
        out_specs=pl.BlockSpec((tf, D_MODEL), lambda i: (i, 0)),
        out_shape=jax.ShapeDtypeStruct((rows, D_MODEL), F32),
        compiler_params=_params("parallel"),
        name="combine",
    )(x1, gathered, gathered, gathered, gathered, gate_t, gfin)


def _permute_in_proj(w):
    lead = w.shape[:-1]
    half = HEAD_DIM // 2
    q = w[..., :Q_COLS].reshape(*lead, N_Q_HEADS // 2, 2, 2, half)
    q = jnp.swapaxes(q, -3, -2).reshape(*lead, Q_COLS)
    k = w[..., Q_COLS:Q_COLS + KV_COLS].reshape(*lead, N_KV_HEADS, 2, 1, half)
    k = jnp.broadcast_to(k, (*lead, N_KV_HEADS, 2, 2, half)).reshape(*lead, KV_DUP_COLS)
    v = w[..., Q_COLS + KV_COLS:Q_COLS + 2 * KV_COLS]
    return jnp.concatenate([q, k, w[..., Q_COLS + 2 * KV_COLS:]], axis=-1), v


def _rope_tables(n_pos):
    half = HEAD_DIM // 2
    inv_freq = 1.0 / (ROPE_THETA ** (jnp.arange(half, dtype=F32) * (2.0 / HEAD_DIM)))
    ang = jnp.arange(n_pos, dtype=F32)[:, None] * inv_freq[None, :]
    cos = jnp.tile(jnp.cos(ang), (1, LANES // half))
    sin = jnp.tile(jnp.sin(ang), (1, LANES // half))
    sign = jnp.where(jnp.arange(LANES) < LANES // 2, -1.0, 1.0).astype(F32)
    return cos, sin * sign[None, :]


def _tiles(geo):
    unit = min(geo.len_a, geo.len_b)
    tile = min(512, unit)
    return dict(tm=tile, tc=tile, tf=tile, bm=2 * EXPERT_SUB_ROWS)


def kernel(x_prompt, x_sample, norm_mix_g, w_in, b_in, attn_sink, w_o_attn, conv_dw_w, conv_dw_b,
           conv_ln_g, conv_ln_b, w_pw2, b_pw2, w_out, norm_ffn_g, w_router, b_router, w_gu, b_gu,
           w_down, b_down, norm_final_g):
    assert w_in.shape[0] == 1, "single trunk layer"
    geo = Geo(x_prompt.shape[0], x_prompt.shape[1], x_sample.shape[0], x_sample.shape[1])
    T = geo.total
    ts = _tiles(geo)
    xa = x_prompt.reshape(-1, D_MODEL)
    xb = x_sample.reshape(-1, D_MODEL)

    w_perm, w_v = _permute_in_proj(w_in[0])
    b_perm, b_v = _permute_in_proj(b_in)
    cos_t, sin_t = _rope_tables(max(geo.len_a, geo.len_b))

    q, kd, vt, glu, gates = _in_proj(xa, xb, norm_mix_g, w_perm.astype(BF16), b_perm,
                                     w_v.T.astype(BF16), b_v.T, cos_t, sin_t, geo, ts["tm"])
    attn = _attention(q, kd, vt, attn_sink[0], geo)
    w_rep = jnp.repeat(conv_dw_w[0], SUBLANES, axis=0)
    conv = _conv_branch(glu, w_rep, conv_dw_b, conv_ln_g, conv_ln_b, geo, ts["tc"])

    wr_t = w_router[0].T
    wr_hi = wr_t.astype(BF16)
    wr_lo = (wr_t - wr_hi.astype(F32)).astype(BF16)
    tri = jnp.triu(jnp.ones((ts["tm"], ts["tm"]), BF16), 1)
    x1, h2, idx, gate_t, rank, counts = _mix_route(
        xa, xb, attn, conv, gates, w_o_attn[0].astype(BF16), w_pw2[0].astype(BF16), b_pw2,
        w_out[0].astype(BF16), norm_ffn_g, wr_hi, wr_lo, b_router[0][:, None], tri, geo, ts["tm"])

    bm = ts["bm"]
    n_blocks = (T * TOP_K) // bm + N_EXPERTS
    cnt = counts[:, 0].astype(I32)
    padded = ((cnt + bm - 1) // bm) * bm
    pad_end = jnp.cumsum(padded)
    pad_start = pad_end - padded
    expert_ids = jnp.arange(N_EXPERTS, dtype=I32)
    dest = rank + jnp.sum(jnp.where(idx[None] == expert_ids[:, None, None],
                                    pad_start[:, None, None], 0), axis=0)
    block_start = jnp.arange(n_blocks, dtype=I32) * bm
    block_exp = jnp.minimum(jnp.sum((pad_end[None, :] <= block_start[:, None]).astype(I32), axis=1),
                            N_EXPERTS - 1)
    n_used = (pad_end[-1:] // bm).astype(I32)
    seg_end = jnp.sum(jnp.where(block_exp[:, None] == expert_ids[None, :],
                                (pad_start + cnt)[None, :], 0), axis=1)
    block_valid = jnp.clip(seg_end - block_start, 0, bm).astype(I32)

    xs = _sc_dispatch(h2, dest, n_blocks * bm)
    ys = _experts(block_exp, n_used, block_valid, xs, w_gu[0], b_gu[0][:, None, :],
                  w_down[0], b_down[0][:, None, :], bm)
    outs = []
    for row0, rows in ((0, geo.rows_a), (geo.rows_a, T - geo.rows_a)):
        gathered = _sc_gather(ys, dest, row0, rows)
        outs.append(_combine(x1, gathered, gate_t, norm_final_g[None, :], row0, rows, ts["tf"]))
    return (outs[0].reshape(x_prompt.shape), outs[1].reshape(x_sample.shape))
```

```python
import functools
import math
from typing import NamedTuple

import jax
import jax.numpy as jnp
from jax import lax
from jax.experimental import pallas as pl
from jax.experimental.pallas import tpu as pltpu
from jax.experimental.pallas import tpu_sc as plsc

F32 = jnp.float32
BF16 = jnp.bfloat16
I32 = jnp.int32

D_MODEL = 1024
HEAD_DIM = 64
N_Q_HEADS = 16
N_KV_HEADS = 4
WINDOW = 128
ATTN_BLOCK = 128
ROPE_THETA = 10000.0
CONV_WIDTH = 31
CONV_PAD = CONV_WIDTH // 2
N_EXPERTS = 32
TOP_K = 4
D_FF = D_MODEL
SWIGLU_LIMIT = 7.0
SWIGLU_ALPHA = 1.702
NORM_EPS = 1e-5
NEG_INF = -1e30

Q_COLS = N_Q_HEADS * HEAD_DIM
KV_COLS = N_KV_HEADS * HEAD_DIM
LANES = 128
SUBLANES = 8
KV_DUP_COLS = N_KV_HEADS * LANES
HALO = 16
VMEM_LIMIT = 56 * 1024 * 1024
SC_ROWS_PER_STEP = 128
EXPERT_SUB_ROWS = 512


class Geo(NamedTuple):
    n_a: int
    len_a: int
    n_b: int
    len_b: int

    @property
    def rows_a(self):
        return self.n_a * self.len_a

    @property
    def total(self):
        return self.rows_a + self.n_b * self.len_b


def _seq_bounds(geo, r):
    in_a = r < geo.rows_a
    start_a = (r // geo.len_a) * geo.len_a
    start_b = geo.rows_a + ((r - geo.rows_a) // geo.len_b) * geo.len_b
    start = jnp.where(in_a, start_a, start_b)
    end = start + jnp.where(in_a, geo.len_a, geo.len_b)
    return start, end


def _params(*sem):
    return pltpu.CompilerParams(dimension_semantics=sem, vmem_limit_bytes=VMEM_LIMIT)


PACKED = D_MODEL // 2


def _pack_halves(x_bf16):
    hi = lax.bitcast_convert_type(x_bf16[:, :PACKED].astype(F32), I32)
    lo = lax.bitcast_convert_type(x_bf16[:, PACKED:].astype(F32), I32)
    return hi | lax.shift_right_logical(lo, 16)


def _unpack_halves(words, dtype):
    hi = lax.bitcast_convert_type(words & jnp.int32(-65536), F32)
    lo = lax.bitcast_convert_type(lax.shift_left(words, 16), F32)
    return jnp.concatenate([hi.astype(dtype), lo.astype(dtype)], axis=1)


def _two_group_specs(geo, tile, width):
    n_a = geo.rows_a // tile
    return n_a, [pl.BlockSpec((tile, width), lambda i: (jnp.minimum(i, n_a - 1), 0)),
                 pl.BlockSpec((tile, width), lambda i: (jnp.maximum(i - n_a, 0), 0))]


_C_Q = 0
_C_K = _C_Q + Q_COLS
_C_GA = _C_K + KV_DUP_COLS
_C_GG = _C_GA + D_MODEL
_C_GATE = _C_GG + D_MODEL
_N_IN = _C_GATE + 2 * D_MODEL
_PROJ_CHUNK = 512
N_SLAB = D_MODEL // LANES
_NT = (((1,), (1,)), ((), ()))
_LOG2E = math.log2(math.e)
_Q_SCALE = HEAD_DIM ** -0.5 * _LOG2E


def _in_proj_body(n_a, xa_ref, xb_ref, g_ref, w_ref, b_ref, wvt_ref, bvt_ref, cos_ref, sin_ref,
                  q_ref, kd_ref, vt_ref, glu_ref, gate_ref):
    x = jnp.where(pl.program_id(0) < n_a, xa_ref[...], xb_ref[...])
    ms = jnp.mean(x * x, axis=-1, keepdims=True)
    h = (x * lax.rsqrt(ms + NORM_EPS) * g_ref[...]).astype(BF16)
    cos = cos_ref[...]
    sin = sin_ref[...]

    def proj(c0):
        return (jnp.dot(h, w_ref[:, c0:c0 + _PROJ_CHUNK], preferred_element_type=F32)
                + b_ref[:, c0:c0 + _PROJ_CHUNK])

    def rope_store(z, out_ref, o0, scale):
        for c in range(_PROJ_CHUNK // LANES):
            zc = z[:, c * LANES:(c + 1) * LANES]
            r = zc * cos + pltpu.roll(zc, LANES // 2, 1) * sin
            if scale != 1.0:
                r = r * scale
            out_ref[:, o0 + c * LANES:o0 + (c + 1) * LANES] = r.astype(out_ref.dtype)

    for c in range(Q_COLS // _PROJ_CHUNK):
        rope_store(proj(_C_Q + c * _PROJ_CHUNK), q_ref, c * _PROJ_CHUNK, _Q_SCALE)
    for c in range(KV_DUP_COLS // _PROJ_CHUNK):
        rope_store(proj(_C_K + c * _PROJ_CHUNK), kd_ref, c * _PROJ_CHUNK, 1.0)
    vt = lax.dot_general(wvt_ref[...], h, _NT, preferred_element_type=F32) + bvt_ref[...]
    vt_ref[...] = vt.astype(BF16)
    per = _PROJ_CHUNK // LANES
    for c in range(D_MODEL // _PROJ_CHUNK):
        a = proj(_C_GA + c * _PROJ_CHUNK)
        g = proj(_C_GG + c * _PROJ_CHUNK)
        glu = a * jax.nn.sigmoid(g)
        for s in range(per):
            glu_ref[c * per + s] = glu[:, s * LANES:(s + 1) * LANES]
    for c in range(2 * D_MODEL // _PROJ_CHUNK):
        gate_ref[:, c * _PROJ_CHUNK:(c + 1) * _PROJ_CHUNK] = jax.nn.sigmoid(
            proj(_C_GATE + c * _PROJ_CHUNK)).astype(BF16)


def _in_proj(xa, xb, g_mix, w_perm, b_perm, wvt, bvt, cos_t, sin_t, geo, tm):
    T = geo.total

    def pos_map(i):
        r0 = i * tm
        start, _ = _seq_bounds(geo, r0)
        return ((r0 - start) // tm, 0)

    const = lambda i: (0, 0)
    row = lambda i: (i, 0)
    n_a, x_specs = _two_group_specs(geo, tm, D_MODEL)
    return pl.pallas_call(
        functools.partial(_in_proj_body, n_a),
        grid=(T // tm,),
        in_specs=x_specs + [
            pl.BlockSpec((1, D_MODEL), const),
            pl.BlockSpec((D_MODEL, _N_IN), const, pipeline_mode=pl.Buffered(1)),
            pl.BlockSpec((1, _N_IN), const),
            pl.BlockSpec((KV_COLS, D_MODEL), const),
            pl.BlockSpec((KV_COLS, 1), const),
            pl.BlockSpec((tm, LANES), pos_map),
            pl.BlockSpec((tm, LANES), pos_map),
        ],
        out_specs=[
            pl.BlockSpec((tm, Q_COLS), row),
            pl.BlockSpec((tm, KV_DUP_COLS), row),
            pl.BlockSpec((KV_COLS, tm), lambda i: (0, i)),
            pl.BlockSpec((N_SLAB, tm, LANES), lambda i: (0, i, 0)),
            pl.BlockSpec((tm, 2 * D_MODEL), row),
        ],
        out_shape=[
            jax.ShapeDtypeStruct((T, Q_COLS), BF16),
            jax.ShapeDtypeStruct((T, KV_DUP_COLS), BF16),
            jax.ShapeDtypeStruct((KV_COLS, T), BF16),
            jax.ShapeDtypeStruct((N_SLAB, T, LANES), F32),
            jax.ShapeDtypeStruct((T, 2 * D_MODEL), BF16),
        ],
        compiler_params=_params("parallel"),
        name="in_proj",
    )(xa, xb, g_mix, w_perm, b_perm, wvt, bvt, cos_t, sin_t)


_ATTN_STEP_BLOCKS = 4


def _attn_body(geo, sink_ref, q_ref, kp_ref, kc_ref, kn_ref, vp_ref, vc_ref, vn_ref, o_ref):
    group = N_Q_HEADS // N_KV_HEADS
    nq = group * ATTN_BLOCK
    b = ATTN_BLOCK
    hd = HEAD_DIM
    nsub = _ATTN_STEP_BLOCKS

    key = lax.broadcasted_iota(I32, (b, nq), 0)
    qry = lax.broadcasted_iota(I32, (b, nq), 1) % b
    head_of_col = lax.broadcasted_iota(I32, (1, nq), 1) // b
    lane = lax.broadcasted_iota(I32, (b, LANES), 1)
    even_head = (lane % hd) < (hd // 2)

    for s in range(nsub):
        rows = slice(s * b, (s + 1) * b)
        r0 = (pl.program_id(0) * nsub + s) * b
        start, end = _seq_bounds(geo, r0)
        bias_prev = jnp.where((key >= qry) & (r0 > start), 0.0, NEG_INF)
        bias_next = jnp.where((key <= qry) & (r0 + b < end), 0.0, NEG_INF)

        for g in range(N_KV_HEADS):
            ls = slice(g * LANES, (g + 1) * LANES)
            k_prev = kp_ref[:, ls] if s == 0 else kc_ref[(s - 1) * b:s * b, ls]
            k_next = kn_ref[:, ls] if s == nsub - 1 else kc_ref[(s + 1) * b:(s + 2) * b, ls]
            vs = slice(g * hd, (g + 1) * hd)
            v_prev = vp_ref[vs, :] if s == 0 else vc_ref[vs, (s - 1) * b:s * b]
            v_next = vn_ref[vs, :] if s == nsub - 1 else vc_ref[vs, (s + 1) * b:(s + 2) * b]

            qa = q_ref[rows, (2 * g) * LANES:(2 * g + 1) * LANES]
            qb = q_ref[rows, (2 * g + 1) * LANES:(2 * g + 2) * LANES]
            zero = jnp.zeros_like(qa)
            q4 = jnp.concatenate([jnp.where(even_head, qa, zero), jnp.where(even_head, zero, qa),
                                  jnp.where(even_head, qb, zero), jnp.where(even_head, zero, qb)],
                                 axis=0)
            k = jnp.concatenate([k_prev, kc_ref[rows, ls], k_next], axis=0)
            st = lax.dot_general(k, q4, _NT, preferred_element_type=F32)
            s_prev = st[0:b] + bias_prev
            s_cur = st[b:2 * b]
            s_next = st[2 * b:3 * b] + bias_next
            sink = jnp.full((1, nq), sink_ref[group * g] * _LOG2E, F32)
            for h in range(1, group):
                sink = jnp.where(head_of_col == h, sink_ref[group * g + h] * _LOG2E, sink)
            m = jnp.maximum(jnp.maximum(jnp.max(s_prev, axis=0, keepdims=True),
                                        jnp.max(s_cur, axis=0, keepdims=True)),
                            jnp.maximum(jnp.max(s_next, axis=0, keepdims=True), sink))
            e_prev = jnp.exp2(s_prev - m)
            e_cur = jnp.exp2(s_cur - m)
            e_next = jnp.exp2(s_next - m)
            p = jnp.concatenate([e_prev.astype(BF16), e_cur.astype(BF16), e_next.astype(BF16)],
                                axis=0)
            vt = jnp.concatenate([v_prev, vc_ref[vs, rows], v_next], axis=1)
            ot = jnp.dot(vt, p, preferred_element_type=F32)
            denom = (jnp.sum(e_prev + e_cur + e_next, axis=0, keepdims=True)
                     + jnp.exp2(sink - m))
            on = ot * (1.0 / denom)
            pair_a = jnp.concatenate([on[:, 0:b], on[:, b:2 * b]], axis=0)
            pair_b = jnp.concatenate([on[:, 2 * b:3 * b], on[:, 3 * b:4 * b]], axis=0)
            o_ref[rows, (2 * g) * LANES:(2 * g + 1) * LANES] = pair_a.T.astype(BF16)
            o_ref[rows, (2 * g + 1) * LANES:(2 * g + 2) * LANES] = pair_b.T.astype(BF16)


def _attention(q, kd, vt, sink, geo):
    T = q.shape[0]
    nsub = _ATTN_STEP_BLOCKS
    nb = T // ATTN_BLOCK
    step_rows = nsub * ATTN_BLOCK
    prev = lambda i: jnp.maximum(i * nsub - 1, 0)
    nxt = lambda i: jnp.minimum((i + 1) * nsub, nb - 1)
    k_edge = lambda m: pl.BlockSpec((ATTN_BLOCK, KV_DUP_COLS), lambda i: (m(i), 0))
    v_edge = lambda m: pl.BlockSpec((KV_COLS, ATTN_BLOCK), lambda i: (0, m(i)))
    return pl.pallas_call(
        functools.partial(_attn_body, geo),
        grid=(T // step_rows,),
        in_specs=[
            pl.BlockSpec(memory_space=pltpu.SMEM),
            pl.BlockSpec((step_rows, Q_COLS), lambda i: (i, 0)),
            k_edge(prev), pl.BlockSpec((step_rows, KV_DUP_COLS), lambda i: (i, 0)), k_edge(nxt),
            v_edge(prev), pl.BlockSpec((KV_COLS, step_rows), lambda i: (0, i)), v_edge(nxt),
        ],
        out_specs=pl.BlockSpec((step_rows, Q_COLS), lambda i: (i, 0)),
        out_shape=jax.ShapeDtypeStruct((T, Q_COLS), BF16),
        compiler_params=_params("parallel"),
        name="window_attn",
    )(sink, q, kd, kd, kd, vt, vt, vt)


_CONV_ROWS = 128
_LN_ROWS = 32


def _conv_body(geo, tc, z_ref, zp_ref, zn_ref, w_ref, dwb_ref, lng_ref, lnb_ref, o_ref,
               buf_ref, y_ref):
    i = pl.program_id(0)
    r0 = i * tc
    start, end = _seq_bounds(geo, r0)
    has_prev = (r0 > start).astype(F32)
    has_next = (r0 + tc < end).astype(F32)
    rows = tc + 2 * HALO
    buf_ref[:, 0:HALO, :] = zp_ref[...] * has_prev
    buf_ref[:, HALO:HALO + tc, :] = z_ref[...]
    buf_ref[:, HALO + tc:rows, :] = zn_ref[...] * has_next

    rep = _CONV_ROWS // SUBLANES

    def conv_chunk(j):
        s0 = pl.multiple_of(j * _CONV_ROWS, _CONV_ROWS)
        for c in range(N_SLAB):
            ls = slice(c * LANES, (c + 1) * LANES)
            acc = jnp.zeros((_CONV_ROWS, LANES), F32)
            for k in range(CONV_WIDTH):
                zt = buf_ref[c, pl.ds(s0 + (HALO - CONV_PAD + k), _CONV_ROWS, stride=1), :]
                wt = w_ref[k * SUBLANES:(k + 1) * SUBLANES, ls]
                acc = acc + zt * jnp.concatenate([wt] * rep, axis=0)
            y_ref[pl.ds(s0, _CONV_ROWS), ls] = acc

    def ln_chunk(j):
        for h in range(_CONV_ROWS // _LN_ROWS):
            s0 = pl.multiple_of(j * _CONV_ROWS + h * _LN_ROWS, _LN_ROWS)
            y = y_ref[pl.ds(s0, _LN_ROWS), :] + dwb_ref[...]
            mu = jnp.mean(y, axis=-1, keepdims=True)
            yc = y - mu
            var = jnp.mean(yc * yc, axis=-1, keepdims=True)
            yn = yc * lax.rsqrt(var + NORM_EPS) * lng_ref[...] + lnb_ref[...]
            o_ref[pl.ds(s0, _LN_ROWS), :] = (yn * jax.nn.sigmoid(yn)).astype(BF16)

    n = tc // _CONV_ROWS
    conv_chunk(0)

    def body(j, carry):
        ln_chunk(j - 1)
        conv_chunk(j)
        return carry

    lax.fori_loop(1, n, body, 0)
    ln_chunk(n - 1)


def _conv_branch(glu, w_rep, dw_b, ln_g, ln_b, geo, tc):
    T = glu.shape[1]
    nh = T // HALO
    per = tc // HALO
    const = lambda i: (0, 0)
    return pl.pallas_call(
        functools.partial(_conv_body, geo, tc),
        grid=(T // tc,),
        in_specs=[
            pl.BlockSpec((N_SLAB, tc, LANES), lambda i: (0, i, 0)),
            pl.BlockSpec((N_SLAB, HALO, LANES), lambda i: (0, jnp.maximum(i * per - 1, 0), 0)),
            pl.BlockSpec((N_SLAB, HALO, LANES), lambda i: (0, jnp.minimum((i + 1) * per, nh - 1), 0)),
            pl.BlockSpec((CONV_WIDTH * SUBLANES, D_MODEL), const),
            pl.BlockSpec((1, D_MODEL), const),
            pl.BlockSpec((1, D_MODEL), const),
            pl.BlockSpec((1, D_MODEL), const),
        ],
        out_specs=pl.BlockSpec((tc, D_MODEL), lambda i: (i, 0)),
        out_shape=jax.ShapeDtypeStruct((T, D_MODEL), BF16),
        scratch_shapes=[pltpu.VMEM((N_SLAB, tc + 2 * HALO, LANES), F32),
                        pltpu.VMEM((tc, D_MODEL), F32)],
        compiler_params=_params("parallel"),
        name="conv_branch",
    )(glu, glu, glu, w_rep, dw_b, ln_g, ln_b)


_MIX_COLS = 256


def _mix_body(n_a, xa_ref, xb_ref, o_ref, c_ref, gate_ref, wo_ref, wpw_ref, bpw_ref, wout_ref,
              gffn_ref, wrh_ref, wrl_ref, br_ref, tri_ref,
              x1_ref, h2_ref, idx_ref, gt_ref, rank_ref, cnt_ref, x1_scr, mix_ref, carry_ref):
    tm = xa_ref.shape[0]
    i = pl.program_id(0)
    last = pl.num_programs(0) - 2
    slot = i % 2

    @pl.when(i == 0)
    def _():
        carry_ref[...] = jnp.zeros_like(carry_ref)
        x1_scr[...] = jnp.zeros_like(x1_scr)

    x1_prev = x1_scr[1 - slot]
    x1_new = x1_scr.at[slot]
    in_a = jnp.minimum(i, last) < n_a
    n_pieces = D_MODEL // _MIX_COLS

    def mix_piece(p):
        cs = slice(p * _MIX_COLS, (p + 1) * _MIX_COLS)
        gs = slice(D_MODEL + p * _MIX_COLS, D_MODEL + (p + 1) * _MIX_COLS)
        attn = jnp.dot(o_ref[...], wo_ref[:, cs], preferred_element_type=F32)
        conv = jnp.dot(c_ref[...], wpw_ref[:, cs], preferred_element_type=F32) + bpw_ref[:, cs]
        mix_ref[:, cs] = (gate_ref[:, cs].astype(F32) * attn
                          + gate_ref[:, gs].astype(F32) * conv).astype(BF16)

    def out_piece(p):
        cs = slice(p * _MIX_COLS, (p + 1) * _MIX_COLS)
        x = jnp.where(in_a, xa_ref[:, cs], xb_ref[:, cs])
        x1 = x + jnp.dot(mix_ref[...], wout_ref[:, cs], preferred_element_type=F32)
        x1_ref[:, cs] = x1
        x1_new[:, cs] = x1

    mix_piece(0)
    ms = jnp.mean(x1_prev * x1_prev, axis=-1, keepdims=True)
    h2 = x1_prev * lax.rsqrt(ms + NORM_EPS) * gffn_ref[...]
    h_hi = h2.astype(BF16)
    h2_ref[...] = _pack_halves(h_hi)
    mix_piece(1)

    h_lo = (h2 - h_hi.astype(F32)).astype(BF16)
    logits = (lax.dot_general(wrh_ref[...], h_hi, _NT, preferred_element_type=F32)
              + lax.dot_general(wrh_ref[...], h_lo, _NT, preferred_element_type=F32)
              + lax.dot_general(wrl_ref[...], h_hi, _NT, preferred_element_type=F32)
              + br_ref[...])
    for p in range(2, n_pieces):
        mix_piece(p)
    out_piece(0)

    eidx = lax.broadcasted_iota(I32, (N_EXPERTS, tm), 0)
    vals = logits
    picked, top_vals, top_idx = [], [], []
    for _ in range(TOP_K):
        m = jnp.max(vals, axis=0, keepdims=True)
        idx = jnp.min(jnp.where(vals == m, eidx, N_EXPERTS), axis=0, keepdims=True)
        sel = eidx == idx
        vals = jnp.where(sel, -jnp.inf, vals)
        picked.append(sel)
        top_vals.append(m)
        top_idx.append(idx)
    out_piece(1)

    exps = [jnp.exp(v - top_vals[0]) for v in top_vals]
    tot = exps[0] + exps[1] + exps[2] + exps[3]
    onehot = (picked[0] | picked[1] | picked[2] | picked[3])
    prefix = jnp.dot(onehot.astype(BF16), tri_ref[...], preferred_element_type=F32) + carry_ref[...]
    for p in range(2, n_pieces):
        out_piece(p)
    for j in range(TOP_K):
        idx_ref[j:j + 1, :] = top_idx[j]
        gt_ref[j:j + 1, :] = exps[j] / tot
        rank_ref[j:j + 1, :] = jnp.sum(jnp.where(picked[j], prefix, 0.0), axis=0,
                                       keepdims=True).astype(I32)
    counted = jnp.where(i > 0, jnp.sum(onehot.astype(F32), axis=1, keepdims=True), 0.0)
    carry_ref[...] = carry_ref[...] + counted
    cnt_ref[...] = jnp.broadcast_to(carry_ref[...], cnt_ref.shape)


def _mix_route(xa, xb, o, c, gates, wo, wpw, bpw, wout, gffn, wr_hi, wr_lo, br, tri, geo, tm):
    T = geo.total
    n = T // tm
    n_a = geo.rows_a // tm
    head = lambda i: jnp.minimum(i, n - 1)
    tail = lambda i: jnp.maximum(i - 1, 0)
    row = lambda i: (head(i), 0)
    col = lambda i: (0, tail(i))
    const = lambda i: (0, 0)
    wspec = lambda shape: pl.BlockSpec(shape, const, pipeline_mode=pl.Buffered(1))
    x_specs = [pl.BlockSpec((tm, D_MODEL), lambda i: (jnp.minimum(head(i), n_a - 1), 0)),
               pl.BlockSpec((tm, D_MODEL), lambda i: (jnp.maximum(head(i) - n_a, 0), 0))]
    return pl.pallas_call(
        functools.partial(_mix_body, n_a),
        grid=(n + 1,),
        in_specs=x_specs + [
            pl.BlockSpec((tm, Q_COLS), row),
            pl.BlockSpec((tm, D_MODEL), row),
            pl.BlockSpec((tm, 2 * D_MODEL), row),
            wspec((Q_COLS, D_MODEL)),
            wspec((D_MODEL, D_MODEL)),
            pl.BlockSpec((1, D_MODEL), const),
            wspec((D_MODEL, D_MODEL)),
            pl.BlockSpec((1, D_MODEL), const),
            pl.BlockSpec((N_EXPERTS, D_MODEL), const),
            pl.BlockSpec((N_EXPERTS, D_MODEL), const),
            pl.BlockSpec((N_EXPERTS, 1), const),
            pl.BlockSpec((tm, tm), const),
        ],
        out_specs=[
            pl.BlockSpec((tm, D_MODEL), row),
            pl.BlockSpec((tm, PACKED), lambda i: (tail(i), 0)),
            pl.BlockSpec((TOP_K, tm), col),
            pl.BlockSpec((TOP_K, tm), col),
            pl.BlockSpec((TOP_K, tm), col),
            pl.BlockSpec((N_EXPERTS, LANES), const),
        ],
        out_shape=[
            jax.ShapeDtypeStruct((T, D_MODEL), F32),
            jax.ShapeDtypeStruct((T, PACKED), I32),
            jax.ShapeDtypeStruct((TOP_K, T), I32),
            jax.ShapeDtypeStruct((TOP_K, T), F32),
            jax.ShapeDtypeStruct((TOP_K, T), I32),
            jax.ShapeDtypeStruct((N_EXPERTS, LANES), F32),
        ],
        scratch_shapes=[pltpu.VMEM((2, tm, D_MODEL), F32), pltpu.VMEM((tm, D_MODEL), BF16),
                        pltpu.VMEM((N_EXPERTS, 1), F32)],
        compiler_params=_params("arbitrary"),
        name="mix_route",
    )(xa, xb, o, c, gates, wo, wpw, bpw, wout, gffn, wr_hi, wr_lo, br, tri)


def _sc_workers():
    info = plsc.get_sparse_core_info()
    return info.num_cores, info.num_cores * info.num_subcores


def _sc_dispatch(h, dest, n_out):
    t_rows, width = h.shape
    nc, nw = _sc_workers()
    per_w = t_rows // nw
    step = SC_ROWS_PER_STEP
    assert per_w * nw == t_rows and per_w % step == 0
    mesh = plsc.VectorSubcoreMesh(core_axis_name="c", subcore_axis_name="s")

    @functools.partial(
        pl.kernel, mesh=mesh,
        out_type=jax.ShapeDtypeStruct((n_out, width), h.dtype),
        scratch_types=[pltpu.VMEM((step,), I32)] * TOP_K + [pltpu.VMEM((step, width), h.dtype)],
    )
    def scatter_rows(h_hbm, dest_hbm, out_hbm, i0, i1, i2, i3, rows_v):
        base = (lax.axis_index("s") * nc + lax.axis_index("c")) * per_w

        @pl.loop(0, per_w // step)
        def _(i):
            off = base + i * step
            pltpu.sync_copy(h_hbm.at[pl.ds(off, step)], rows_v)
            for j, idx_v in enumerate((i0, i1, i2, i3)):
                pltpu.sync_copy(dest_hbm.at[j, pl.ds(off, step)], idx_v)
                pltpu.sync_copy(rows_v, out_hbm.at[idx_v])

    return scatter_rows(h, dest)


def _sc_gather(table, dest, row0, rows):
    width = table.shape[1]
    nc, nw = _sc_workers()
    per_w = rows // nw
    step = SC_ROWS_PER_STEP
    assert per_w * nw == rows and per_w % step == 0
    mesh = plsc.VectorSubcoreMesh(core_axis_name="c", subcore_axis_name="s")

    @functools.partial(
        pl.kernel, mesh=mesh,
        out_type=jax.ShapeDtypeStruct((TOP_K * rows, width), table.dtype),
        scratch_types=[pltpu.VMEM((step,), I32), pltpu.VMEM((step, width), table.dtype),
                       pltpu.SemaphoreType.DMA],
    )
    def gather_rows(table_hbm, dest_hbm, out_hbm, idx_v, rows_v, sem):
        base = (lax.axis_index("s") * nc + lax.axis_index("c")) * per_w

        @pl.loop(0, per_w // step)
        def _(i):
            off = base + i * step
            for j in range(TOP_K):
                pltpu.sync_copy(dest_hbm.at[j, pl.ds(row0 + off, step)], idx_v)
                pltpu.async_copy(table_hbm.at[idx_v], rows_v, sem).wait()
                pltpu.sync_copy(rows_v, out_hbm.at[pl.ds(j * rows + off, step)])

    return gather_rows(table, dest)


def _expert_body(bexp_ref, nused_ref, valid_ref, xs_ref, wgu_ref, bgu_ref, wd_ref, bd_ref, o_ref,
                 wgu_bf, wd_bf):
    n = pl.program_id(0)
    active = n < nused_ref[0]
    new_expert = (n == 0) | (bexp_ref[n] != bexp_ref[jnp.maximum(n - 1, 0)])

    @pl.when(active & new_expert)
    def _():
        wgu_bf[...] = wgu_ref[...].astype(BF16)
        wd_bf[...] = wd_ref[...].astype(BF16)

    sub = EXPERT_SUB_ROWS
    for r0 in range(0, xs_ref.shape[0], sub):
        rs = slice(r0, r0 + sub)
        live = active & (valid_ref[n] > r0)

        @pl.when(live)
        def _():
            row = r0 + lax.broadcasted_iota(I32, (sub, 1), 0)
            x = _unpack_halves(jnp.where(row < valid_ref[n], xs_ref[rs, :], 0), BF16)
            gu = jnp.dot(x, wgu_bf[...], preferred_element_type=F32) + bgu_ref[...]
            g = jnp.minimum(gu[:, 0:D_FF], SWIGLU_LIMIT)
            u = jnp.clip(gu[:, D_FF:2 * D_FF], -SWIGLU_LIMIT, SWIGLU_LIMIT)
            act = g * jax.nn.sigmoid(SWIGLU_ALPHA * g) * (u + 1.0)
            out = jnp.dot(act.astype(BF16), wd_bf[...], preferred_element_type=F32) + bd_ref[...]
            o_ref[rs, :] = _pack_halves(out.astype(BF16))

        @pl.when(jnp.logical_not(live))
        def _():
            o_ref[rs, :] = jnp.zeros((sub, o_ref.shape[1]), o_ref.dtype)


def _experts(block_exp, n_used, block_valid, xs, wgu, bgu, wd, bd, bm):
    P = xs.shape[0]
    nblk = P // bm
    xmap = lambda n, be, nu, bv: (jnp.minimum(n, nu[0] - 1), 0)
    emap = lambda n, be, nu, bv: (be[n], 0, 0)
    grid_spec = pltpu.PrefetchScalarGridSpec(
        num_scalar_prefetch=3,
        grid=(nblk,),
        in_specs=[
            pl.BlockSpec((bm, PACKED), xmap),
            pl.BlockSpec((None, D_MODEL, 2 * D_FF), emap),
            pl.BlockSpec((None, 1, 2 * D_FF), emap),
            pl.BlockSpec((None, D_FF, D_MODEL), emap),
            pl.BlockSpec((None, 1, D_MODEL), emap),
        ],
        out_specs=pl.BlockSpec((bm, PACKED), lambda n, be, nu, bv: (n, 0)),
        scratch_shapes=[pltpu.VMEM((D_MODEL, 2 * D_FF), BF16), pltpu.VMEM((D_FF, D_MODEL), BF16)],
    )
    return pl.pallas_call(
        _expert_body,
        grid_spec=grid_spec,
        out_shape=jax.ShapeDtypeStruct((P, PACKED), I32),
        compiler_params=_params("arbitrary"),
        name="experts",
    )(block_exp, n_used, block_valid, xs, wgu, bgu, wd, bd)


def _combine_body(x1_ref, g0_ref, g1_ref, g2_ref, g3_ref, gate_ref, gfin_ref, y_ref):
    tf = x1_ref.shape[0]
    gate_rows = jnp.concatenate(
        [gate_ref[...], jnp.zeros((LANES - TOP_K, tf), F32)], axis=0)
    for c in range(tf // LANES):
        rs = slice(c * LANES, (c + 1) * LANES)
        gate = gate_rows[:, rs].T
        y = x1_ref[rs, :]
        for j, g_ref in enumerate((g0_ref, g1_ref, g2_ref, g3_ref)):
            y = y + gate[:, j:j + 1] * _unpack_halves(g_ref[rs, :], F32)
        ms = jnp.mean(y * y, axis=-1, keepdims=True)
        y_ref[rs, :] = y * lax.rsqrt(ms + NORM_EPS) * gfin_ref[...]


def _combine(x1, gathered, gate_t, gfin, row0, rows, tf):
    nt = rows // tf
    t0 = row0 // tf
    row = lambda i: (t0 + i, 0)
    choice = lambda j: pl.BlockSpec((tf, PACKED), lambda i: (j * nt + i, 0))
    return pl.pallas_call(
        _combine_body,
        grid=(nt,),
        in_specs=[pl.BlockSpec((tf, D_MODEL), row)] + [choice(j) for j in range(TOP_K)] + [
            pl.BlockSpec((TOP_K, tf), lambda i: (0, t0 + i)),
            pl.BlockSpec((1, D_MODEL), lambda i: (0, 0)),
        ],
        out_specs=pl.BlockSpec((tf, D_MODEL), lambda i: (i, 0)),
        out_shape=jax.ShapeDtypeStruct((rows, D_MODEL), F32),
        compiler_params=_params("parallel"),
        name="combine",
    )(x1, gathered, gathered, gathered, gathered, gate_t, gfin)


def _permute_in_proj(w):
    lead = w.shape[:-1]
    half = HEAD_DIM // 2
    q = w[..., :Q_COLS].reshape(*lead, N_Q_HEADS // 2, 2, 2, half)
    q = jnp.swapaxes(q, -3, -2).reshape(*lead, Q_COLS)
    k = w[..., Q_COLS:Q_COLS + KV_COLS].reshape(*lead, N_KV_HEADS, 2, 1, half)
    k = jnp.broadcast_to(k, (*lead, N_KV_HEADS, 2, 2, half)).reshape(*lead, KV_DUP_COLS)
    v = w[..., Q_COLS + KV_COLS:Q_COLS + 2 * KV_COLS]
    return jnp.concatenate([q, k, w[..., Q_COLS + 2 * KV_COLS:]], axis=-1), v


def _rope_tables(n_pos):
    half = HEAD_DIM // 2
    inv_freq = 1.0 / (ROPE_THETA ** (jnp.arange(half, dtype=F32) * (2.0 / HEAD_DIM)))
    ang = jnp.arange(n_pos, dtype=F32)[:, None] * inv_freq[None, :]
    cos = jnp.tile(jnp.cos(ang), (1, LANES // half))
    sin = jnp.tile(jnp.sin(ang), (1, LANES // half))
    sign = jnp.where(jnp.arange(LANES) < LANES // 2, -1.0, 1.0).astype(F32)
    return cos, sin * sign[None, :]


def _tiles(geo):
    unit = min(geo.len_a, geo.len_b)
    tile = min(512, unit)
    return dict(tm=tile, tc=tile, tf=tile, bm=2 * EXPERT_SUB_ROWS)


def kernel(x_prompt, x_sample, norm_mix_g, w_in, b_in, attn_sink, w_o_attn, conv_dw_w, conv_dw_b,
           conv_ln_g, conv_ln_b, w_pw2, b_pw2, w_out, norm_ffn_g, w_router, b_router, w_gu, b_gu,
           w_down, b_down, norm_final_g):
    assert w_in.shape[0] == 1, "single trunk layer"
    geo = Geo(x_prompt.shape[0], x_prompt.shape[1], x_sample.shape[0], x_sample.shape[1])
    T = geo.total
    ts = _tiles(geo)
    xa = x_prompt.reshape(-1, D_MODEL)
    xb = x_sample.reshape(-1, D_MODEL)

    w_perm, w_v = _permute_in_proj(w_in[0])
    b_perm, b_v = _permute_in_proj(b_in)
    cos_t, sin_t = _rope_tables(max(geo.len_a, geo.len_b))

    q, kd, vt, glu, gates = _in_proj(xa, xb, norm_mix_g, w_perm.astype(BF16), b_perm,
                                     w_v.T.astype(BF16), b_v.T, cos_t, sin_t, geo, ts["tm"])
    attn = _attention(q, kd, vt, attn_sink[0], geo)
    w_rep = jnp.repeat(conv_dw_w[0], SUBLANES, axis=0)
    conv = _conv_branch(glu, w_rep, conv_dw_b, conv_ln_g, conv_ln_b, geo, ts["tc"])

    wr_t = w_router[0].T
    wr_hi = wr_t.astype(BF16)
    wr_lo = (wr_t - wr_hi.astype(F32)).astype(BF16)
    tri = jnp.triu(jnp.ones((ts["tm"], ts["tm"]), BF16), 1)
    x1, h2, idx, gate_t, rank, counts = _mix_route(
        xa, xb, attn, conv, gates, w_o_attn[0].astype(BF16), w_pw2[0].astype(BF16), b_pw2,
        w_out[0].astype(BF16), norm_ffn_g, wr_hi, wr_lo, b_router[0][:, None], tri, geo, ts["tm"])

    bm = ts["bm"]
    n_blocks = (T * TOP_K) // bm + N_EXPERTS
    cnt = counts[:, 0].astype(I32)
    padded = ((cnt + bm - 1) // bm) * bm
    pad_end = jnp.cumsum(padded)
    pad_start = pad_end - padded
    expert_ids = jnp.arange(N_EXPERTS, dtype=I32)
    dest = rank + jnp.sum(jnp.where(idx[None] == expert_ids[:, None, None],
                                    pad_start[:, None, None], 0), axis=0)
    block_start = jnp.arange(n_blocks, dtype=I32) * bm
    block_exp = jnp.minimum(jnp.sum((pad_end[None, :] <= block_start[:, None]).astype(I32), axis=1),
                            N_EXPERTS - 1)
    n_used = (pad_end[-1:] // bm).astype(I32)
    seg_end = jnp.sum(jnp.where(block_exp[:, None] == expert_ids[None, :],
                                (pad_start + cnt)[None, :], 0), axis=1)
    block_valid = jnp.clip(seg_end - block_start, 0, bm).astype(I32)

    xs = _sc_dispatch(h2, dest, n_blocks * bm)
    ys = _experts(block_exp, n_used, block_valid, xs, w_gu[0], b_gu[0][:, None, :],
                  w_down[0], b_down[0][:, None, :], bm)
    outs = []
    for row0, rows in ((0, geo.rows_a), (geo.rows_a, T - geo.rows_a)):
        gathered = _sc_gather(ys, dest, row0, rows)
        outs.append(_combine(x1, gathered, gate_t, norm_final_g[None, :], row0, rows, ts["tf"]))
    return (outs[0].reshape(x_prompt.shape), outs[1].reshape(x_sample.shape))
```

```python
import functools
import math
from typing import NamedTuple

import jax
import jax.numpy as jnp
from jax import lax
from jax.experimental import pallas as pl
from jax.experimental.pallas import tpu as pltpu
from jax.experimental.pallas import tpu_sc as plsc

F32 = jnp.float32
BF16 = jnp.bfloat16
I32 = jnp.int32

D_MODEL = 1024
HEAD_DIM = 64
N_Q_HEADS = 16
N_KV_HEADS = 4
WINDOW = 128
ATTN_BLOCK = 128
ROPE_THETA = 10000.0
CONV_WIDTH = 31
CONV_PAD = CONV_WIDTH // 2
N_EXPERTS = 32
TOP_K = 4
D_FF = D_MODEL
SWIGLU_LIMIT = 7.0
SWIGLU_ALPHA = 1.702
NORM_EPS = 1e-5
NEG_INF = -1e30

Q_COLS = N_Q_HEADS * HEAD_DIM
KV_COLS = N_KV_HEADS * HEAD_DIM
LANES = 128
SUBLANES = 8
KV_DUP_COLS = N_KV_HEADS * LANES
HALO = 16
VMEM_LIMIT = 56 * 1024 * 1024
SC_ROWS_PER_STEP = 128
EXPERT_SUB_ROWS = 512


class Geo(NamedTuple):
    n_a: int
    len_a: int
    n_b: int
    len_b: int

    @property
    def rows_a(self):
        return self.n_a * self.len_a

    @property
    def total(self):
        return self.rows_a + self.n_b * self.len_b


def _seq_bounds(geo, r):
    in_a = r < geo.rows_a
    start_a = (r // geo.len_a) * geo.len_a
    start_b = geo.rows_a + ((r - geo.rows_a) // geo.len_b) * geo.len_b
    start = jnp.where(in_a, start_a, start_b)
    end = start + jnp.where(in_a, geo.len_a, geo.len_b)
    return start, end


def _params(*sem):
    return pltpu.CompilerParams(dimension_semantics=sem, vmem_limit_bytes=VMEM_LIMIT)


PACKED = D_MODEL // 2


def _pack_halves(x_bf16):
    hi = lax.bitcast_convert_type(x_bf16[:, :PACKED].astype(F32), I32)
    lo = lax.bitcast_convert_type(x_bf16[:, PACKED:].astype(F32), I32)
    return hi | lax.shift_right_logical(lo, 16)


def _unpack_halves(words, dtype):
    hi = lax.bitcast_convert_type(words & jnp.int32(-65536), F32)
    lo = lax.bitcast_convert_type(lax.shift_left(words, 16), F32)
    return jnp.concatenate([hi.astype(dtype), lo.astype(dtype)], axis=1)


def _two_group_specs(geo, tile, width):
    n_a = geo.rows_a // tile
    return n_a, [pl.BlockSpec((tile, width), lambda i: (jnp.minimum(i, n_a - 1), 0)),
                 pl.BlockSpec((tile, width), lambda i: (jnp.maximum(i - n_a, 0), 0))]


_C_Q = 0
_C_K = _C_Q + Q_COLS
_C_GA = _C_K + KV_DUP_COLS
_C_GG = _C_GA + D_MODEL
_C_GATE = _C_GG + D_MODEL
_N_IN = _C_GATE + 2 * D_MODEL
_PROJ_CHUNK = 512
N_SLAB = D_MODEL // LANES
_NT = (((1,), (1,)), ((), ()))
_LOG2E = math.log2(math.e)
_Q_SCALE = HEAD_DIM ** -0.5 * _LOG2E


def _in_proj_body(n_a, xa_ref, xb_ref, g_ref, w_ref, b_ref, wvt_ref, bvt_ref, cos_ref, sin_ref,
                  q_ref, kd_ref, vt_ref, glu_ref, gate_ref):
    x = jnp.where(pl.program_id(0) < n_a, xa_ref[...], xb_ref[...])
    ms = jnp.mean(x * x, axis=-1, keepdims=True)
    h = (x * lax.rsqrt(ms + NORM_EPS) * g_ref[...]).astype(BF16)
    cos = cos_ref[...]
    sin = sin_ref[...]

    def proj(c0):
        return (jnp.dot(h, w_ref[:, c0:c0 + _PROJ_CHUNK], preferred_element_type=F32)
                + b_ref[:, c0:c0 + _PROJ_CHUNK])

    def rope_store(z, out_ref, o0, scale):
        for c in range(_PROJ_CHUNK // LANES):
            zc = z[:, c * LANES:(c + 1) * LANES]
            r = zc * cos + pltpu.roll(zc, LANES // 2, 1) * sin
            if scale != 1.0:
                r = r * scale
            out_ref[:, o0 + c * LANES:o0 + (c + 1) * LANES] = r.astype(out_ref.dtype)

    for c in range(Q_COLS // _PROJ_CHUNK):
        rope_store(proj(_C_Q + c * _PROJ_CHUNK), q_ref, c * _PROJ_CHUNK, _Q_SCALE)
    for c in range(KV_DUP_COLS // _PROJ_CHUNK):
        rope_store(proj(_C_K + c * _PROJ_CHUNK), kd_ref, c * _PROJ_CHUNK, 1.0)
    vt = lax.dot_general(wvt_ref[...], h, _NT, preferred_element_type=F32) + bvt_ref[...]
    vt_ref[...] = vt.astype(BF16)
    per = _PROJ_CHUNK // LANES
    for c in range(D_MODEL // _PROJ_CHUNK):
        a = proj(_C_GA + c * _PROJ_CHUNK)
        g = proj(_C_GG + c * _PROJ_CHUNK)
        glu = a * jax.nn.sigmoid(g)
        for s in range(per):
            glu_ref[c * per + s] = glu[:, s * LANES:(s + 1) * LANES]
    for c in range(2 * D_MODEL // _PROJ_CHUNK):
        gate_ref[:, c * _PROJ_CHUNK:(c + 1) * _PROJ_CHUNK] = jax.nn.sigmoid(
            proj(_C_GATE + c * _PROJ_CHUNK)).astype(BF16)


def _in_proj(xa, xb, g_mix, w_perm, b_perm, wvt, bvt, cos_t, sin_t, geo, tm):
    T = geo.total

    def pos_map(i):
        r0 = i * tm
        start, _ = _seq_bounds(geo, r0)
        return ((r0 - start) // tm, 0)

    const = lambda i: (0, 0)
    row = lambda i: (i, 0)
    n_a, x_specs = _two_group_specs(geo, tm, D_MODEL)
    return pl.pallas_call(
        functools.partial(_in_proj_body, n_a),
        grid=(T // tm,),
        in_specs=x_specs + [
            pl.BlockSpec((1, D_MODEL), const),
            pl.BlockSpec((D_MODEL, _N_IN), const, pipeline_mode=pl.Buffered(1)),
            pl.BlockSpec((1, _N_IN), const),
            pl.BlockSpec((KV_COLS, D_MODEL), const),
            pl.BlockSpec((KV_COLS, 1), const),
            pl.BlockSpec((tm, LANES), pos_map),
            pl.BlockSpec((tm, LANES), pos_map),
        ],
        out_specs=[
            pl.BlockSpec((tm, Q_COLS), row),
            pl.BlockSpec((tm, KV_DUP_COLS), row),
            pl.BlockSpec((KV_COLS, tm), lambda i: (0, i)),
            pl.BlockSpec((N_SLAB, tm, LANES), lambda i: (0, i, 0)),
            pl.BlockSpec((tm, 2 * D_MODEL), row),
        ],
        out_shape=[
            jax.ShapeDtypeStruct((T, Q_COLS), BF16),
            jax.ShapeDtypeStruct((T, KV_DUP_COLS), BF16),
            jax.ShapeDtypeStruct((KV_COLS, T), BF16),
            jax.ShapeDtypeStruct((N_SLAB, T, LANES), F32),
            jax.ShapeDtypeStruct((T, 2 * D_MODEL), BF16),
        ],
        compiler_params=_params("parallel"),
        name="in_proj",
    )(xa, xb, g_mix, w_perm, b_perm, wvt, bvt, cos_t, sin_t)


_ONES_ROWS = 16
_ATTN_STEP_BLOCKS = 4


def _attn_body(geo, sink_ref, q_ref, kp_ref, kc_ref, kn_ref, vp_ref, vc_ref, vn_ref, o_ref):
    group = N_Q_HEADS // N_KV_HEADS
    nq = group * ATTN_BLOCK
    b = ATTN_BLOCK
    hd = HEAD_DIM
    nsub = _ATTN_STEP_BLOCKS

    key = lax.broadcasted_iota(I32, (b, nq), 0)
    qry = lax.broadcasted_iota(I32, (b, nq), 1) % b
    head_of_col = lax.broadcasted_iota(I32, (1, nq), 1) // b
    lane = lax.broadcasted_iota(I32, (b, LANES), 1)
    even_head = (lane % hd) < (hd // 2)
    ones = jnp.ones((_ONES_ROWS, 3 * b), BF16)

    for s in range(nsub):
        rows = slice(s * b, (s + 1) * b)
        r0 = (pl.program_id(0) * nsub + s) * b
        start, end = _seq_bounds(geo, r0)
        bias_prev = jnp.where((key >= qry) & (r0 > start), 0.0, NEG_INF)
        bias_next = jnp.where((key <= qry) & (r0 + b < end), 0.0, NEG_INF)

        for g in range(N_KV_HEADS):
            ls = slice(g * LANES, (g + 1) * LANES)
            k_prev = kp_ref[:, ls] if s == 0 else kc_ref[(s - 1) * b:s * b, ls]
            k_next = kn_ref[:, ls] if s == nsub - 1 else kc_ref[(s + 1) * b:(s + 2) * b, ls]
            vs = slice(g * hd, (g + 1) * hd)
            v_prev = vp_ref[vs, :] if s == 0 else vc_ref[vs, (s - 1) * b:s * b]
            v_next = vn_ref[vs, :] if s == nsub - 1 else vc_ref[vs, (s + 1) * b:(s + 2) * b]

            qa = q_ref[rows, (2 * g) * LANES:(2 * g + 1) * LANES]
            qb = q_ref[rows, (2 * g + 1) * LANES:(2 * g + 2) * LANES]
            zero = jnp.zeros_like(qa)
            q4 = jnp.concatenate([jnp.where(even_head, qa, zero), jnp.where(even_head, zero, qa),
                                  jnp.where(even_head, qb, zero), jnp.where(even_head, zero, qb)],
                                 axis=0)
            k = jnp.concatenate([k_prev, kc_ref[rows, ls], k_next], axis=0)
            st = lax.dot_general(k, q4, _NT, preferred_element_type=F32)
            s_prev = st[0:b] + bias_prev
            s_cur = st[b:2 * b]
            s_next = st[2 * b:3 * b] + bias_next
            sink = jnp.full((1, nq), sink_ref[group * g] * _LOG2E, F32)
            for h in range(1, group):
                sink = jnp.where(head_of_col == h, sink_ref[group * g + h] * _LOG2E, sink)
            m = jnp.maximum(jnp.maximum(jnp.max(s_prev, axis=0, keepdims=True),
                                        jnp.max(s_cur, axis=0, keepdims=True)),
                            jnp.maximum(jnp.max(s_next, axis=0, keepdims=True), sink))
            p = jnp.concatenate([jnp.exp2(s_prev - m).astype(BF16), jnp.exp2(s_cur - m).astype(BF16),
                                 jnp.exp2(s_next - m).astype(BF16)], axis=0)
            vt = jnp.concatenate([v_prev, vc_ref[vs, rows], v_next], axis=1)
            ot = jnp.dot(jnp.concatenate([vt, ones], axis=0), p, preferred_element_type=F32)
            denom = ot[hd:hd + 1] + jnp.exp2(sink - m)
            on = ot[0:hd] * (1.0 / denom)
            pair_a = jnp.concatenate([on[:, 0:b], on[:, b:2 * b]], axis=0)
            pair_b = jnp.concatenate([on[:, 2 * b:3 * b], on[:, 3 * b:4 * b]], axis=0)
            o_ref[rows, (2 * g) * LANES:(2 * g + 1) * LANES] = pair_a.T.astype(BF16)
            o_ref[rows, (2 * g + 1) * LANES:(2 * g + 2) * LANES] = pair_b.T.astype(BF16)


def _attention(q, kd, vt, sink, geo):
    T = q.shape[0]
    nsub = _ATTN_STEP_BLOCKS
    nb = T // ATTN_BLOCK
    step_rows = nsub * ATTN_BLOCK
    prev = lambda i: jnp.maximum(i * nsub - 1, 0)
    nxt = lambda i: jnp.minimum((i + 1) * nsub, nb - 1)
    k_edge = lambda m: pl.BlockSpec((ATTN_BLOCK, KV_DUP_COLS), lambda i: (m(i), 0))
    v_edge = lambda m: pl.BlockSpec((KV_COLS, ATTN_BLOCK), lambda i: (0, m(i)))
    return pl.pallas_call(
        functools.partial(_attn_body, geo),
        grid=(T // step_rows,),
        in_specs=[
            pl.BlockSpec(memory_space=pltpu.SMEM),
            pl.BlockSpec((step_rows, Q_COLS), lambda i: (i, 0)),
            k_edge(prev), pl.BlockSpec((step_rows, KV_DUP_COLS), lambda i: (i, 0)), k_edge(nxt),
            v_edge(prev), pl.BlockSpec((KV_COLS, step_rows), lambda i: (0, i)), v_edge(nxt),
        ],
        out_specs=pl.BlockSpec((step_rows, Q_COLS), lambda i: (i, 0)),
        out_shape=jax.ShapeDtypeStruct((T, Q_COLS), BF16),
        compiler_params=_params("parallel"),
        name="window_attn",
    )(sink, q, kd, kd, kd, vt, vt, vt)


_CONV_ROWS = 128
_LN_ROWS = 32


def _conv_body(geo, tc, z_ref, zp_ref, zn_ref, w_ref, dwb_ref, lng_ref, lnb_ref, o_ref,
               buf_ref, y_ref):
    i = pl.program_id(0)
    r0 = i * tc
    start, end = _seq_bounds(geo, r0)
    has_prev = (r0 > start).astype(F32)
    has_next = (r0 + tc < end).astype(F32)
    rows = tc + 2 * HALO
    buf_ref[:, 0:HALO, :] = zp_ref[...] * has_prev
    buf_ref[:, HALO:HALO + tc, :] = z_ref[...]
    buf_ref[:, HALO + tc:rows, :] = zn_ref[...] * has_next

    rep = _CONV_ROWS // SUBLANES

    def conv_chunk(j):
        s0 = pl.multiple_of(j * _CONV_ROWS, _CONV_ROWS)
        for c in range(N_SLAB):
            ls = slice(c * LANES, (c + 1) * LANES)
            acc = jnp.zeros((_CONV_ROWS, LANES), F32)
            for k in range(CONV_WIDTH):
                zt = buf_ref[c, pl.ds(s0 + (HALO - CONV_PAD + k), _CONV_ROWS, stride=1), :]
                wt = w_ref[k * SUBLANES:(k + 1) * SUBLANES, ls]
                acc = acc + zt * jnp.concatenate([wt] * rep, axis=0)
            y_ref[pl.ds(s0, _CONV_ROWS), ls] = acc

    def ln_chunk(j):
        for h in range(_CONV_ROWS // _LN_ROWS):
            s0 = pl.multiple_of(j * _CONV_ROWS + h * _LN_ROWS, _LN_ROWS)
            y = y_ref[pl.ds(s0, _LN_ROWS), :] + dwb_ref[...]
            mu = jnp.mean(y, axis=-1, keepdims=True)
            yc = y - mu
            var = jnp.mean(yc * yc, axis=-1, keepdims=True)
            yn = yc * lax.rsqrt(var + NORM_EPS) * lng_ref[...] + lnb_ref[...]
            o_ref[pl.ds(s0, _LN_ROWS), :] = (yn * jax.nn.sigmoid(yn)).astype(BF16)

    n = tc // _CONV_ROWS
    conv_chunk(0)

    def body(j, carry):
        ln_chunk(j - 1)
        conv_chunk(j)
        return carry

    lax.fori_loop(1, n, body, 0)
    ln_chunk(n - 1)


def _conv_branch(glu, w_rep, dw_b, ln_g, ln_b, geo, tc):
    T = glu.shape[1]
    nh = T // HALO
    per = tc // HALO
    const = lambda i: (0, 0)
    return pl.pallas_call(
        functools.partial(_conv_body, geo, tc),
        grid=(T // tc,),
        in_specs=[
            pl.BlockSpec((N_SLAB, tc, LANES), lambda i: (0, i, 0)),
            pl.BlockSpec((N_SLAB, HALO, LANES), lambda i: (0, jnp.maximum(i * per - 1, 0), 0)),
            pl.BlockSpec((N_SLAB, HALO, LANES), lambda i: (0, jnp.minimum((i + 1) * per, nh - 1), 0)),
            pl.BlockSpec((CONV_WIDTH * SUBLANES, D_MODEL), const),
            pl.BlockSpec((1, D_MODEL), const),
            pl.BlockSpec((1, D_MODEL), const),
            pl.BlockSpec((1, D_MODEL), const),
        ],
        out_specs=pl.BlockSpec((tc, D_MODEL), lambda i: (i, 0)),
        out_shape=jax.ShapeDtypeStruct((T, D_MODEL), BF16),
        scratch_shapes=[pltpu.VMEM((N_SLAB, tc + 2 * HALO, LANES), F32),
                        pltpu.VMEM((tc, D_MODEL), F32)],
        compiler_params=_params("parallel"),
        name="conv_branch",
    )(glu, glu, glu, w_rep, dw_b, ln_g, ln_b)


_MIX_COLS = 256


def _mix_body(n_a, xa_ref, xb_ref, o_ref, c_ref, gate_ref, wo_ref, wpw_ref, bpw_ref, wout_ref,
              gffn_ref, wrh_ref, wrl_ref, br_ref, tri_ref,
              x1_ref, h2_ref, idx_ref, gt_ref, rank_ref, cnt_ref, x1_scr, mix_ref, carry_ref):
    tm = xa_ref.shape[0]
    i = pl.program_id(0)
    last = pl.num_programs(0) - 2
    slot = i % 2

    @pl.when(i == 0)
    def _():
        carry_ref[...] = jnp.zeros_like(carry_ref)
        x1_scr[...] = jnp.zeros_like(x1_scr)

    x1_prev = x1_scr[1 - slot]
    x1_new = x1_scr.at[slot]
    in_a = jnp.minimum(i, last) < n_a
    n_pieces = D_MODEL // _MIX_COLS

    def mix_piece(p):
        cs = slice(p * _MIX_COLS, (p + 1) * _MIX_COLS)
        gs = slice(D_MODEL + p * _MIX_COLS, D_MODEL + (p + 1) * _MIX_COLS)
        attn = jnp.dot(o_ref[...], wo_ref[:, cs], preferred_element_type=F32)
        conv = jnp.dot(c_ref[...], wpw_ref[:, cs], preferred_element_type=F32) + bpw_ref[:, cs]
        mix_ref[:, cs] = (gate_ref[:, cs].astype(F32) * attn
                          + gate_ref[:, gs].astype(F32) * conv).astype(BF16)

    def out_piece(p):
        cs = slice(p * _MIX_COLS, (p + 1) * _MIX_COLS)
        x = jnp.where(in_a, xa_ref[:, cs], xb_ref[:, cs])
        x1 = x + jnp.dot(mix_ref[...], wout_ref[:, cs], preferred_element_type=F32)
        x1_ref[:, cs] = x1
        x1_new[:, cs] = x1

    mix_piece(0)
    ms = jnp.mean(x1_prev * x1_prev, axis=-1, keepdims=True)
    h2 = x1_prev * lax.rsqrt(ms + NORM_EPS) * gffn_ref[...]
    h_hi = h2.astype(BF16)
    h2_ref[...] = _pack_halves(h_hi)
    mix_piece(1)

    h_lo = (h2 - h_hi.astype(F32)).astype(BF16)
    logits = (lax.dot_general(wrh_ref[...], h_hi, _NT, preferred_element_type=F32)
              + lax.dot_general(wrh_ref[...], h_lo, _NT, preferred_element_type=F32)
              + lax.dot_general(wrl_ref[...], h_hi, _NT, preferred_element_type=F32)
              + br_ref[...])
    for p in range(2, n_pieces):
        mix_piece(p)
    out_piece(0)

    eidx = lax.broadcasted_iota(I32, (N_EXPERTS, tm), 0)
    vals = logits
    picked, top_vals, top_idx = [], [], []
    for _ in range(TOP_K):
        m = jnp.max(vals, axis=0, keepdims=True)
        idx = jnp.min(jnp.where(vals == m, eidx, N_EXPERTS), axis=0, keepdims=True)
        sel = eidx == idx
        vals = jnp.where(sel, -jnp.inf, vals)
        picked.append(sel)
        top_vals.append(m)
        top_idx.append(idx)
    out_piece(1)

    exps = [jnp.exp(v - top_vals[0]) for v in top_vals]
    tot = exps[0] + exps[1] + exps[2] + exps[3]
    onehot = (picked[0] | picked[1] | picked[2] | picked[3])
    prefix = jnp.dot(onehot.astype(BF16), tri_ref[...], preferred_element_type=F32) + carry_ref[...]
    for p in range(2, n_pieces):
        out_piece(p)
    for j in range(TOP_K):
        idx_ref[j:j + 1, :] = top_idx[j]
        gt_ref[j:j + 1, :] = exps[j] / tot
        rank_ref[j:j + 1, :] = jnp.sum(jnp.where(picked[j], prefix, 0.0), axis=0,
                                       keepdims=True).astype(I32)
    counted = jnp.where(i > 0, jnp.sum(onehot.astype(F32), axis=1, keepdims=True), 0.0)
    carry_ref[...] = carry_ref[...] + counted
    cnt_ref[...] = jnp.broadcast_to(carry_ref[...], cnt_ref.shape)


def _mix_route(xa, xb, o, c, gates, wo, wpw, bpw, wout, gffn, wr_hi, wr_lo, br, tri, geo, tm):
    T = geo.total
    n = T // tm
    n_a = geo.rows_a // tm
    head = lambda i: jnp.minimum(i, n - 1)
    tail = lambda i: jnp.maximum(i - 1, 0)
    row = lambda i: (head(i), 0)
    col = lambda i: (0, tail(i))
    const = lambda i: (0, 0)
    wspec = lambda shape: pl.BlockSpec(shape, const, pipeline_mode=pl.Buffered(1))
    x_specs = [pl.BlockSpec((tm, D_MODEL), lambda i: (jnp.minimum(head(i), n_a - 1), 0)),
               pl.BlockSpec((tm, D_MODEL), lambda i: (jnp.maximum(head(i) - n_a, 0), 0))]
    return pl.pallas_call(
        functools.partial(_mix_body, n_a),
        grid=(n + 1,),
        in_specs=x_specs + [
            pl.BlockSpec((tm, Q_COLS), row),
            pl.BlockSpec((tm, D_MODEL), row),
            pl.BlockSpec((tm, 2 * D_MODEL), row),
            wspec((Q_COLS, D_MODEL)),
            wspec((D_MODEL, D_MODEL)),
            pl.BlockSpec((1, D_MODEL), const),
            wspec((D_MODEL, D_MODEL)),
            pl.BlockSpec((1, D_MODEL), const),
            pl.BlockSpec((N_EXPERTS, D_MODEL), const),
            pl.BlockSpec((N_EXPERTS, D_MODEL), const),
            pl.BlockSpec((N_EXPERTS, 1), const),
            pl.BlockSpec((tm, tm), const),
        ],
        out_specs=[
            pl.BlockSpec((tm, D_MODEL), row),
            pl.BlockSpec((tm, PACKED), lambda i: (tail(i), 0)),
            pl.BlockSpec((TOP_K, tm), col),
            pl.BlockSpec((TOP_K, tm), col),
            pl.BlockSpec((TOP_K, tm), col),
            pl.BlockSpec((N_EXPERTS, LANES), const),
        ],
        out_shape=[
            jax.ShapeDtypeStruct((T, D_MODEL), F32),
            jax.ShapeDtypeStruct((T, PACKED), I32),
            jax.ShapeDtypeStruct((TOP_K, T), I32),
            jax.ShapeDtypeStruct((TOP_K, T), F32),
            jax.ShapeDtypeStruct((TOP_K, T), I32),
            jax.ShapeDtypeStruct((N_EXPERTS, LANES), F32),
        ],
        scratch_shapes=[pltpu.VMEM((2, tm, D_MODEL), F32), pltpu.VMEM((tm, D_MODEL), BF16),
                        pltpu.VMEM((N_EXPERTS, 1), F32)],
        compiler_params=_params("arbitrary"),
        name="mix_route",
    )(xa, xb, o, c, gates, wo, wpw, bpw, wout, gffn, wr_hi, wr_lo, br, tri)


def _sc_workers():
    info = plsc.get_sparse_core_info()
    return info.num_cores, info.num_cores * info.num_subcores


def _sc_dispatch(h, dest, n_out):
    t_rows, width = h.shape
    nc, nw = _sc_workers()
    per_w = t_rows // nw
    step = SC_ROWS_PER_STEP
    assert per_w * nw == t_rows and per_w % step == 0
    mesh = plsc.VectorSubcoreMesh(core_axis_name="c", subcore_axis_name="s")

    @functools.partial(
        pl.kernel, mesh=mesh,
        out_type=jax.ShapeDtypeStruct((n_out, width), h.dtype),
        scratch_types=[pltpu.VMEM((step,), I32)] * TOP_K + [pltpu.VMEM((step, width), h.dtype)],
    )
    def scatter_rows(h_hbm, dest_hbm, out_hbm, i0, i1, i2, i3, rows_v):
        base = (lax.axis_index("s") * nc + lax.axis_index("c")) * per_w

        @pl.loop(0, per_w // step)
        def _(i):
            off = base + i * step
            pltpu.sync_copy(h_hbm.at[pl.ds(off, step)], rows_v)
            for j, idx_v in enumerate((i0, i1, i2, i3)):
                pltpu.sync_copy(dest_hbm.at[j, pl.ds(off, step)], idx_v)
                pltpu.sync_copy(rows_v, out_hbm.at[idx_v])

    return scatter_rows(h, dest)


def _sc_gather(table, dest, row0, rows):
    width = table.shape[1]
    nc, nw = _sc_workers()
    per_w = rows // nw
    step = SC_ROWS_PER_STEP
    assert per_w * nw == rows and per_w % step == 0
    mesh = plsc.VectorSubcoreMesh(core_axis_name="c", subcore_axis_name="s")

    @functools.partial(
        pl.kernel, mesh=mesh,
        out_type=jax.ShapeDtypeStruct((TOP_K * rows, width), table.dtype),
        scratch_types=[pltpu.VMEM((step,), I32), pltpu.VMEM((step, width), table.dtype),
                       pltpu.SemaphoreType.DMA],
    )
    def gather_rows(table_hbm, dest_hbm, out_hbm, idx_v, rows_v, sem):
        base = (lax.axis_index("s") * nc + lax.axis_index("c")) * per_w

        @pl.loop(0, per_w // step)
        def _(i):
            off = base + i * step
            for j in range(TOP_K):
                pltpu.sync_copy(dest_hbm.at[j, pl.ds(row0 + off, step)], idx_v)
                pltpu.async_copy(table_hbm.at[idx_v], rows_v, sem).wait()
                pltpu.sync_copy(rows_v, out_hbm.at[pl.ds(j * rows + off, step)])

    return gather_rows(table, dest)


def _expert_body(bexp_ref, nused_ref, valid_ref, xs_ref, wgu_ref, bgu_ref, wd_ref, bd_ref, o_ref,
                 wgu_bf, wd_bf):
    n = pl.program_id(0)
    active = n < nused_ref[0]
    new_expert = (n == 0) | (bexp_ref[n] != bexp_ref[jnp.maximum(n - 1, 0)])

    @pl.when(active & new_expert)
    def _():
        wgu_bf[...] = wgu_ref[...].astype(BF16)
        wd_bf[...] = wd_ref[...].astype(BF16)

    sub = EXPERT_SUB_ROWS
    for r0 in range(0, xs_ref.shape[0], sub):
        rs = slice(r0, r0 + sub)
        live = active & (valid_ref[n] > r0)

        @pl.when(live)
        def _():
            row = r0 + lax.broadcasted_iota(I32, (sub, 1), 0)
            x = _unpack_halves(jnp.where(row < valid_ref[n], xs_ref[rs, :], 0), BF16)
            gu = jnp.dot(x, wgu_bf[...], preferred_element_type=F32) + bgu_ref[...]
            g = jnp.minimum(gu[:, 0:D_FF], SWIGLU_LIMIT)
            u = jnp.clip(gu[:, D_FF:2 * D_FF], -SWIGLU_LIMIT, SWIGLU_LIMIT)
            act = g * jax.nn.sigmoid(SWIGLU_ALPHA * g) * (u + 1.0)
            out = jnp.dot(act.astype(BF16), wd_bf[...], preferred_element_type=F32) + bd_ref[...]
            o_ref[rs, :] = _pack_halves(out.astype(BF16))

        @pl.when(jnp.logical_not(live))
        def _():
            o_ref[rs, :] = jnp.zeros((sub, o_ref.shape[1]), o_ref.dtype)


def _experts(block_exp, n_used, block_valid, xs, wgu, bgu, wd, bd, bm):
    P = xs.shape[0]
    nblk = P // bm
    xmap = lambda n, be, nu, bv: (jnp.minimum(n, nu[0] - 1), 0)
    emap = lambda n, be, nu, bv: (be[n], 0, 0)
    grid_spec = pltpu.PrefetchScalarGridSpec(
        num_scalar_prefetch=3,
        grid=(nblk,),
        in_specs=[
            pl.BlockSpec((bm, PACKED), xmap),
            pl.BlockSpec((None, D_MODEL, 2 * D_FF), emap),
            pl.BlockSpec((None, 1, 2 * D_FF), emap),
            pl.BlockSpec((None, D_FF, D_MODEL), emap),
            pl.BlockSpec((None, 1, D_MODEL), emap),
        ],
        out_specs=pl.BlockSpec((bm, PACKED), lambda n, be, nu, bv: (n, 0)),
        scratch_shapes=[pltpu.VMEM((D_MODEL, 2 * D_FF), BF16), pltpu.VMEM((D_FF, D_MODEL), BF16)],
    )
    return pl.pallas_call(
        _expert_body,
        grid_spec=grid_spec,
        out_shape=jax.ShapeDtypeStruct((P, PACKED), I32),
        compiler_params=_params("arbitrary"),
        name="experts",
    )(block_exp, n_used, block_valid, xs, wgu, bgu, wd, bd)


def _combine_body(x1_ref, g0_ref, g1_ref, g2_ref, g3_ref, gate_ref, gfin_ref, y_ref):
    tf = x1_ref.shape[0]
    gate_rows = jnp.concatenate(
        [gate_ref[...], jnp.zeros((LANES - TOP_K, tf), F32)], axis=0)
    for c in range(tf // LANES):
        rs = slice(c * LANES, (c + 1) * LANES)
        gate = gate_rows[:, rs].T
        y = x1_ref[rs, :]
        for j, g_ref in enumerate((g0_ref, g1_ref, g2_ref, g3_ref)):
            y = y + gate[:, j:j + 1] * _unpack_halves(g_ref[rs, :], F32)
        ms = jnp.mean(y * y, axis=-1, keepdims=True)
        y_ref[rs, :] = y * lax.rsqrt(ms + NORM_EPS) * gfin_ref[...]


def _combine(x1, gathered, gate_t, gfin, row0, rows, tf):
    nt = rows // tf
    t0 = row0 // tf
    row = lambda i: (t0 + i, 0)
    choice = lambda j: pl.BlockSpec((tf, PACKED), lambda i: (j * nt + i, 0))
    return pl.pallas_call(
        _combine_body,
        grid=(nt,),
        in_specs=[pl.BlockSpec((tf, D_MODEL), row)] + [choice(j) for j in range(TOP_K)] + [
            pl.BlockSpec((TOP_K, tf), lambda i: (0, t0 + i)),
            pl.BlockSpec((1, D_MODEL), lambda i: (0, 0)),
        ],
        out_specs=pl.BlockSpec((tf, D_MODEL), lambda i: (i, 0)),
        out_shape=jax.ShapeDtypeStruct((rows, D_MODEL), F32),
        compiler_params=_params("parallel"),
        name="combine",
    )(x1, gathered, gathered, gathered, gathered, gate_t, gfin)


def _permute_in_proj(w):
    lead = w.shape[:-1]
    half = HEAD_DIM // 2
    q = w[..., :Q_COLS].reshape(*lead, N_Q_HEADS // 2, 2, 2, half)
    q = jnp.swapaxes(q, -3, -2).reshape(*lead, Q_COLS)
    k = w[..., Q_COLS:Q_COLS + KV_COLS].reshape(*lead, N_KV_HEADS, 2, 1, half)
    k = jnp.broadcast_to(k, (*lead, N_KV_HEADS, 2, 2, half)).reshape(*lead, KV_DUP_COLS)
    v = w[..., Q_COLS + KV_COLS:Q_COLS + 2 * KV_COLS]
    return jnp.concatenate([q, k, w[..., Q_COLS + 2 * KV_COLS:]], axis=-1), v


def _rope_tables(n_pos):
    half = HEAD_DIM // 2
    inv_freq = 1.0 / (ROPE_THETA ** (jnp.arange(half, dtype=F32) * (2.0 / HEAD_DIM)))
    ang = jnp.arange(n_pos, dtype=F32)[:, None] * inv_freq[None, :]
    cos = jnp.tile(jnp.cos(ang), (1, LANES // half))
    sin = jnp.tile(jnp.sin(ang), (1, LANES // half))
    sign = jnp.where(jnp.arange(LANES) < LANES // 2, -1.0, 1.0).astype(F32)
    return cos, sin * sign[None, :]


def _tiles(geo):
    unit = min(geo.len_a, geo.len_b)
    tile = min(512, unit)
    wide = min(1024, unit)
    return dict(tm=tile, tc=wide, tf=wide, bm=2 * EXPERT_SUB_ROWS)


def kernel(x_prompt, x_sample, norm_mix_g, w_in, b_in, attn_sink, w_o_attn, conv_dw_w, conv_dw_b,
           conv_ln_g, conv_ln_b, w_pw2, b_pw2, w_out, norm_ffn_g, w_router, b_router, w_gu, b_gu,
           w_down, b_down, norm_final_g):
    assert w_in.shape[0] == 1, "single trunk layer"
    geo = Geo(x_prompt.shape[0], x_prompt.shape[1], x_sample.shape[0], x_sample.shape[1])
    T = geo.total
    ts = _tiles(geo)
    xa = x_prompt.reshape(-1, D_MODEL)
    xb = x_sample.reshape(-1, D_MODEL)

    w_perm, w_v = _permute_in_proj(w_in[0])
    b_perm, b_v = _permute_in_proj(b_in)
    cos_t, sin_t = _rope_tables(max(geo.len_a, geo.len_b))

    q, kd, vt, glu, gates = _in_proj(xa, xb, norm_mix_g, w_perm.astype(BF16), b_perm,
                                     w_v.T.astype(BF16), b_v.T, cos_t, sin_t, geo, ts["tm"])
    attn = _attention(q, kd, vt, attn_sink[0], geo)
    w_rep = jnp.repeat(conv_dw_w[0], SUBLANES, axis=0)
    conv = _conv_branch(glu, w_rep, conv_dw_b, conv_ln_g, conv_ln_b, geo, ts["tc"])

    wr_t = w_router[0].T
    wr_hi = wr_t.astype(BF16)
    wr_lo = (wr_t - wr_hi.astype(F32)).astype(BF16)
    tri = jnp.triu(jnp.ones((ts["tm"], ts["tm"]), BF16), 1)
    x1, h2, idx, gate_t, rank, counts = _mix_route(
        xa, xb, attn, conv, gates, w_o_attn[0].astype(BF16), w_pw2[0].astype(BF16), b_pw2,
        w_out[0].astype(BF16), norm_ffn_g, wr_hi, wr_lo, b_router[0][:, None], tri, geo, ts["tm"])

    bm = ts["bm"]
    n_blocks = (T * TOP_K) // bm + N_EXPERTS
    cnt = counts[:, 0].astype(I32)
    padded = ((cnt + bm - 1) // bm) * bm
    pad_end = jnp.cumsum(padded)
    pad_start = pad_end - padded
    expert_ids = jnp.arange(N_EXPERTS, dtype=I32)
    dest = rank + jnp.sum(jnp.where(idx[None] == expert_ids[:, None, None],
                                    pad_start[:, None, None], 0), axis=0)
    block_start = jnp.arange(n_blocks, dtype=I32) * bm
    block_exp = jnp.minimum(jnp.sum((pad_end[None, :] <= block_start[:, None]).astype(I32), axis=1),
                            N_EXPERTS - 1)
    n_used = (pad_end[-1:] // bm).astype(I32)
    seg_end = jnp.sum(jnp.where(block_exp[:, None] == expert_ids[None, :],
                                (pad_start + cnt)[None, :], 0), axis=1)
    block_valid = jnp.clip(seg_end - block_start, 0, bm).astype(I32)

    xs = _sc_dispatch(h2, dest, n_blocks * bm)
    ys = _experts(block_exp, n_used, block_valid, xs, w_gu[0], b_gu[0][:, None, :],
                  w_down[0], b_down[0][:, None, :], bm)
    outs = []
    for row0, rows in ((0, geo.rows_a), (geo.rows_a, T - geo.rows_a)):
        gathered = _sc_gather(ys, dest, row0, rows)
        outs.append(_combine(x1, gathered, gate_t, norm_final_g[None, :], row0, rows, ts["tf"]))
    return (outs[0].reshape(x_prompt.shape), outs[1].reshape(x_sample.shape))
```

```python
import functools
import math
from typing import NamedTuple

import jax
import jax.numpy as jnp
from jax import lax
from jax.experimental import pallas as pl
from jax.experimental.pallas import tpu as pltpu
from jax.experimental.pallas import tpu_sc as plsc

F32 = jnp.float32
BF16 = jnp.bfloat16
I32 = jnp.int32

D_MODEL = 1024
HEAD_DIM = 64
N_Q_HEADS = 16
N_KV_HEADS = 4
WINDOW = 128
ATTN_BLOCK = 128
ROPE_THETA = 10000.0
CONV_WIDTH = 31
CONV_PAD = CONV_WIDTH // 2
N_EXPERTS = 32
TOP_K = 4
D_FF = D_MODEL
SWIGLU_LIMIT = 7.0
SWIGLU_ALPHA = 1.702
NORM_EPS = 1e-5
NEG_INF = -1e30

Q_COLS = N_Q_HEADS * HEAD_DIM
KV_COLS = N_KV_HEADS * HEAD_DIM
LANES = 128
SUBLANES = 8
KV_DUP_COLS = N_KV_HEADS * LANES
HALO = 16
VMEM_LIMIT = 56 * 1024 * 1024
SC_ROWS_PER_STEP = 128
EXPERT_SUB_ROWS = 512


class Geo(NamedTuple):
    n_a: int
    len_a: int
    n_b: int
    len_b: int

    @property
    def rows_a(self):
        return self.n_a * self.len_a

    @property
    def total(self):
        return self.rows_a + self.n_b * self.len_b


def _seq_bounds(geo, r):
    in_a = r < geo.rows_a
    start_a = (r // geo.len_a) * geo.len_a
    start_b = geo.rows_a + ((r - geo.rows_a) // geo.len_b) * geo.len_b
    start = jnp.where(in_a, start_a, start_b)
    end = start + jnp.where(in_a, geo.len_a, geo.len_b)
    return start, end


def _params(*sem):
    return pltpu.CompilerParams(dimension_semantics=sem, vmem_limit_bytes=VMEM_LIMIT)


PACKED = D_MODEL // 2


def _pack_halves(x_bf16):
    hi = lax.bitcast_convert_type(x_bf16[:, :PACKED].astype(F32), I32)
    lo = lax.bitcast_convert_type(x_bf16[:, PACKED:].astype(F32), I32)
    return hi | lax.shift_right_logical(lo, 16)


def _unpack_halves(words, dtype):
    hi = lax.bitcast_convert_type(words & jnp.int32(-65536), F32)
    lo = lax.bitcast_convert_type(lax.shift_left(words, 16), F32)
    return jnp.concatenate([hi.astype(dtype), lo.astype(dtype)], axis=1)


def _two_group_specs(geo, tile, width):
    n_a = geo.rows_a // tile
    return n_a, [pl.BlockSpec((tile, width), lambda i: (jnp.minimum(i, n_a - 1), 0)),
                 pl.BlockSpec((tile, width), lambda i: (jnp.maximum(i - n_a, 0), 0))]


_C_Q = 0
_C_K = _C_Q + Q_COLS
_C_GA = _C_K + KV_DUP_COLS
_C_GG = _C_GA + D_MODEL
_C_GATE = _C_GG + D_MODEL
_N_IN = _C_GATE + 2 * D_MODEL
_PROJ_CHUNK = 512
N_SLAB = D_MODEL // LANES
_NT = (((1,), (1,)), ((), ()))
_LOG2E = math.log2(math.e)
_Q_SCALE = HEAD_DIM ** -0.5 * _LOG2E


def _in_proj_body(n_a, xa_ref, xb_ref, g_ref, w_ref, b_ref, wvt_ref, bvt_ref, cos_ref, sin_ref,
                  q_ref, kd_ref, vt_ref, glu_ref, gate_ref):
    x = jnp.where(pl.program_id(0) < n_a, xa_ref[...], xb_ref[...])
    ms = jnp.mean(x * x, axis=-1, keepdims=True)
    h = (x * lax.rsqrt(ms + NORM_EPS) * g_ref[...]).astype(BF16)
    cos = cos_ref[...]
    sin = sin_ref[...]

    def proj(c0):
        return (jnp.dot(h, w_ref[:, c0:c0 + _PROJ_CHUNK], preferred_element_type=F32)
                + b_ref[:, c0:c0 + _PROJ_CHUNK])

    def rope_store(z, out_ref, o0, scale):
        for c in range(_PROJ_CHUNK // LANES):
            zc = z[:, c * LANES:(c + 1) * LANES]
            r = zc * cos + pltpu.roll(zc, LANES // 2, 1) * sin
            if scale != 1.0:
                r = r * scale
            out_ref[:, o0 + c * LANES:o0 + (c + 1) * LANES] = r.astype(out_ref.dtype)

    for c in range(Q_COLS // _PROJ_CHUNK):
        rope_store(proj(_C_Q + c * _PROJ_CHUNK), q_ref, c * _PROJ_CHUNK, _Q_SCALE)
    for c in range(KV_DUP_COLS // _PROJ_CHUNK):
        rope_store(proj(_C_K + c * _PROJ_CHUNK), kd_ref, c * _PROJ_CHUNK, 1.0)
    vt = lax.dot_general(wvt_ref[...], h, _NT, preferred_element_type=F32) + bvt_ref[...]
    vt_ref[...] = vt.astype(BF16)
    per = _PROJ_CHUNK // LANES
    for c in range(D_MODEL // _PROJ_CHUNK):
        a = proj(_C_GA + c * _PROJ_CHUNK)
        g = proj(_C_GG + c * _PROJ_CHUNK)
        glu = a * jax.nn.sigmoid(g)
        for s in range(per):
            glu_ref[c * per + s] = glu[:, s * LANES:(s + 1) * LANES]
    for c in range(2 * D_MODEL // _PROJ_CHUNK):
        gate_ref[:, c * _PROJ_CHUNK:(c + 1) * _PROJ_CHUNK] = jax.nn.sigmoid(
            proj(_C_GATE + c * _PROJ_CHUNK)).astype(BF16)


def _in_proj(xa, xb, g_mix, w_perm, b_perm, wvt, bvt, cos_t, sin_t, geo, tm):
    T = geo.total

    def pos_map(i):
        r0 = i * tm
        start, _ = _seq_bounds(geo, r0)
        return ((r0 - start) // tm, 0)

    const = lambda i: (0, 0)
    row = lambda i: (i, 0)
    n_a, x_specs = _two_group_specs(geo, tm, D_MODEL)
    return pl.pallas_call(
        functools.partial(_in_proj_body, n_a),
        grid=(T // tm,),
        in_specs=x_specs + [
            pl.BlockSpec((1, D_MODEL), const),
            pl.BlockSpec((D_MODEL, _N_IN), const, pipeline_mode=pl.Buffered(1)),
            pl.BlockSpec((1, _N_IN), const),
            pl.BlockSpec((KV_COLS, D_MODEL), const),
            pl.BlockSpec((KV_COLS, 1), const),
            pl.BlockSpec((tm, LANES), pos_map),
            pl.BlockSpec((tm, LANES), pos_map),
        ],
        out_specs=[
            pl.BlockSpec((tm, Q_COLS), row),
            pl.BlockSpec((tm, KV_DUP_COLS), row),
            pl.BlockSpec((KV_COLS, tm), lambda i: (0, i)),
            pl.BlockSpec((N_SLAB, tm, LANES), lambda i: (0, i, 0)),
            pl.BlockSpec((tm, 2 * D_MODEL), row),
        ],
        out_shape=[
            jax.ShapeDtypeStruct((T, Q_COLS), BF16),
            jax.ShapeDtypeStruct((T, KV_DUP_COLS), BF16),
            jax.ShapeDtypeStruct((KV_COLS, T), BF16),
            jax.ShapeDtypeStruct((N_SLAB, T, LANES), F32),
            jax.ShapeDtypeStruct((T, 2 * D_MODEL), BF16),
        ],
        compiler_params=_params("parallel"),
        name="in_proj",
    )(xa, xb, g_mix, w_perm, b_perm, wvt, bvt, cos_t, sin_t)


_ONES_ROWS = 16
_ATTN_STEP_BLOCKS = 4


def _attn_body(geo, sink_ref, q_ref, kp_ref, kc_ref, kn_ref, vp_ref, vc_ref, vn_ref, o_ref):
    group = N_Q_HEADS // N_KV_HEADS
    nq = group * ATTN_BLOCK
    b = ATTN_BLOCK
    hd = HEAD_DIM
    nsub = _ATTN_STEP_BLOCKS

    key = lax.broadcasted_iota(I32, (b, nq), 0)
    qry = lax.broadcasted_iota(I32, (b, nq), 1) % b
    head_of_col = lax.broadcasted_iota(I32, (1, nq), 1) // b
    lane = lax.broadcasted_iota(I32, (b, LANES), 1)
    even_head = (lane % hd) < (hd // 2)
    ones = jnp.ones((_ONES_ROWS, 3 * b), BF16)

    for s in range(nsub):
        rows = slice(s * b, (s + 1) * b)
        r0 = (pl.program_id(0) * nsub + s) * b
        start, end = _seq_bounds(geo, r0)
        bias_prev = jnp.where((key >= qry) & (r0 > start), 0.0, NEG_INF)
        bias_next = jnp.where((key <= qry) & (r0 + b < end), 0.0, NEG_INF)

        sts, vts = [], []
        for g in range(N_KV_HEADS):
            ls = slice(g * LANES, (g + 1) * LANES)
            k_prev = kp_ref[:, ls] if s == 0 else kc_ref[(s - 1) * b:s * b, ls]
            k_next = kn_ref[:, ls] if s == nsub - 1 else kc_ref[(s + 1) * b:(s + 2) * b, ls]
            vs = slice(g * hd, (g + 1) * hd)
            v_prev = vp_ref[vs, :] if s == 0 else vc_ref[vs, (s - 1) * b:s * b]
            v_next = vn_ref[vs, :] if s == nsub - 1 else vc_ref[vs, (s + 1) * b:(s + 2) * b]

            qa = q_ref[rows, (2 * g) * LANES:(2 * g + 1) * LANES]
            qb = q_ref[rows, (2 * g + 1) * LANES:(2 * g + 2) * LANES]
            zero = jnp.zeros_like(qa)
            q4 = jnp.concatenate([jnp.where(even_head, qa, zero), jnp.where(even_head, zero, qa),
                                  jnp.where(even_head, qb, zero), jnp.where(even_head, zero, qb)],
                                 axis=0)
            k = jnp.concatenate([k_prev, kc_ref[rows, ls], k_next], axis=0)
            sts.append(lax.dot_general(k, q4, _NT, preferred_element_type=F32))
            vts.append(jnp.concatenate([v_prev, vc_ref[vs, rows], v_next], axis=1))

        ps, ms, sinks = [], [], []
        for g in range(N_KV_HEADS):
            st = sts[g]
            s_prev = st[0:b] + bias_prev
            s_cur = st[b:2 * b]
            s_next = st[2 * b:3 * b] + bias_next
            sink = jnp.full((1, nq), sink_ref[group * g] * _LOG2E, F32)
            for h in range(1, group):
                sink = jnp.where(head_of_col == h, sink_ref[group * g + h] * _LOG2E, sink)
            m = jnp.maximum(jnp.maximum(jnp.max(s_prev, axis=0, keepdims=True),
                                        jnp.max(s_cur, axis=0, keepdims=True)),
                            jnp.maximum(jnp.max(s_next, axis=0, keepdims=True), sink))
            ps.append(jnp.concatenate([jnp.exp2(s_prev - m).astype(BF16),
                                       jnp.exp2(s_cur - m).astype(BF16),
                                       jnp.exp2(s_next - m).astype(BF16)], axis=0))
            ms.append(m)
            sinks.append(sink)

        for g in range(N_KV_HEADS):
            ot = jnp.dot(jnp.concatenate([vts[g], ones], axis=0), ps[g], preferred_element_type=F32)
            denom = ot[hd:hd + 1] + jnp.exp2(sinks[g] - ms[g])
            on = ot[0:hd] * (1.0 / denom)
            pair_a = jnp.concatenate([on[:, 0:b], on[:, b:2 * b]], axis=0)
            pair_b = jnp.concatenate([on[:, 2 * b:3 * b], on[:, 3 * b:4 * b]], axis=0)
            o_ref[rows, (2 * g) * LANES:(2 * g + 1) * LANES] = pair_a.T.astype(BF16)
            o_ref[rows, (2 * g + 1) * LANES:(2 * g + 2) * LANES] = pair_b.T.astype(BF16)


def _attention(q, kd, vt, sink, geo):
    T = q.shape[0]
    nsub = _ATTN_STEP_BLOCKS
    nb = T // ATTN_BLOCK
    step_rows = nsub * ATTN_BLOCK
    prev = lambda i: jnp.maximum(i * nsub - 1, 0)
    nxt = lambda i: jnp.minimum((i + 1) * nsub, nb - 1)
    k_edge = lambda m: pl.BlockSpec((ATTN_BLOCK, KV_DUP_COLS), lambda i: (m(i), 0))
    v_edge = lambda m: pl.BlockSpec((KV_COLS, ATTN_BLOCK), lambda i: (0, m(i)))
    return pl.pallas_call(
        functools.partial(_attn_body, geo),
        grid=(T // step_rows,),
        in_specs=[
            pl.BlockSpec(memory_space=pltpu.SMEM),
            pl.BlockSpec((step_rows, Q_COLS), lambda i: (i, 0)),
            k_edge(prev), pl.BlockSpec((step_rows, KV_DUP_COLS), lambda i: (i, 0)), k_edge(nxt),
            v_edge(prev), pl.BlockSpec((KV_COLS, step_rows), lambda i: (0, i)), v_edge(nxt),
        ],
        out_specs=pl.BlockSpec((step_rows, Q_COLS), lambda i: (i, 0)),
        out_shape=jax.ShapeDtypeStruct((T, Q_COLS), BF16),
        compiler_params=_params("parallel"),
        name="window_attn",
    )(sink, q, kd, kd, kd, vt, vt, vt)


_CONV_ROWS = 128
_LN_ROWS = 32


def _conv_body(geo, tc, z_ref, zp_ref, zn_ref, w_ref, dwb_ref, lng_ref, lnb_ref, o_ref,
               buf_ref, y_ref):
    i = pl.program_id(0)
    r0 = i * tc
    start, end = _seq_bounds(geo, r0)
    has_prev = (r0 > start).astype(F32)
    has_next = (r0 + tc < end).astype(F32)
    rows = tc + 2 * HALO
    buf_ref[:, 0:HALO, :] = zp_ref[...] * has_prev
    buf_ref[:, HALO:HALO + tc, :] = z_ref[...]
    buf_ref[:, HALO + tc:rows, :] = zn_ref[...] * has_next

    rep = _CONV_ROWS // SUBLANES

    def conv_chunk(j):
        s0 = pl.multiple_of(j * _CONV_ROWS, _CONV_ROWS)
        for c in range(N_SLAB):
            ls = slice(c * LANES, (c + 1) * LANES)
            acc = jnp.zeros((_CONV_ROWS, LANES), F32)
            for k in range(CONV_WIDTH):
                zt = buf_ref[c, pl.ds(s0 + (HALO - CONV_PAD + k), _CONV_ROWS, stride=1), :]
                wt = w_ref[k * SUBLANES:(k + 1) * SUBLANES, ls]
                acc = acc + zt * jnp.concatenate([wt] * rep, axis=0)
            y_ref[pl.ds(s0, _CONV_ROWS), ls] = acc

    def ln_chunk(j):
        for h in range(_CONV_ROWS // _LN_ROWS):
            s0 = pl.multiple_of(j * _CONV_ROWS + h * _LN_ROWS, _LN_ROWS)
            y = y_ref[pl.ds(s0, _LN_ROWS), :] + dwb_ref[...]
            mu = jnp.mean(y, axis=-1, keepdims=True)
            yc = y - mu
            var = jnp.mean(yc * yc, axis=-1, keepdims=True)
            yn = yc * lax.rsqrt(var + NORM_EPS) * lng_ref[...] + lnb_ref[...]
            o_ref[pl.ds(s0, _LN_ROWS), :] = (yn * jax.nn.sigmoid(yn)).astype(BF16)

    n = tc // _CONV_ROWS
    conv_chunk(0)

    def body(j, carry):
        ln_chunk(j - 1)
        conv_chunk(j)
        return carry

    lax.fori_loop(1, n, body, 0)
    ln_chunk(n - 1)


def _conv_branch(glu, w_rep, dw_b, ln_g, ln_b, geo, tc):
    T = glu.shape[1]
    nh = T // HALO
    per = tc // HALO
    const = lambda i: (0, 0)
    return pl.pallas_call(
        functools.partial(_conv_body, geo, tc),
        grid=(T // tc,),
        in_specs=[
            pl.BlockSpec((N_SLAB, tc, LANES), lambda i: (0, i, 0)),
            pl.BlockSpec((N_SLAB, HALO, LANES), lambda i: (0, jnp.maximum(i * per - 1, 0), 0)),
            pl.BlockSpec((N_SLAB, HALO, LANES), lambda i: (0, jnp.minimum((i + 1) * per, nh - 1), 0)),
            pl.BlockSpec((CONV_WIDTH * SUBLANES, D_MODEL), const),
            pl.BlockSpec((1, D_MODEL), const),
            pl.BlockSpec((1, D_MODEL), const),
            pl.BlockSpec((1, D_MODEL), const),
        ],
        out_specs=pl.BlockSpec((tc, D_MODEL), lambda i: (i, 0)),
        out_shape=jax.ShapeDtypeStruct((T, D_MODEL), BF16),
        scratch_shapes=[pltpu.VMEM((N_SLAB, tc + 2 * HALO, LANES), F32),
                        pltpu.VMEM((tc, D_MODEL), F32)],
        compiler_params=_params("parallel"),
        name="conv_branch",
    )(glu, glu, glu, w_rep, dw_b, ln_g, ln_b)


_MIX_COLS = 256


def _mix_body(n_a, xa_ref, xb_ref, o_ref, c_ref, gate_ref, wo_ref, wpw_ref, bpw_ref, wout_ref,
              gffn_ref, wrh_ref, wrl_ref, br_ref, tri_ref,
              x1_ref, h2_ref, idx_ref, gt_ref, rank_ref, cnt_ref, x1_scr, mix_ref, carry_ref):
    tm = xa_ref.shape[0]
    i = pl.program_id(0)
    last = pl.num_programs(0) - 2
    slot = i % 2

    @pl.when(i == 0)
    def _():
        carry_ref[...] = jnp.zeros_like(carry_ref)
        x1_scr[...] = jnp.zeros_like(x1_scr)

    x1_prev = x1_scr[1 - slot]
    x1_new = x1_scr.at[slot]
    in_a = jnp.minimum(i, last) < n_a
    n_pieces = D_MODEL // _MIX_COLS

    def mix_piece(p):
        cs = slice(p * _MIX_COLS, (p + 1) * _MIX_COLS)
        gs = slice(D_MODEL + p * _MIX_COLS, D_MODEL + (p + 1) * _MIX_COLS)
        attn = jnp.dot(o_ref[...], wo_ref[:, cs], preferred_element_type=F32)
        conv = jnp.dot(c_ref[...], wpw_ref[:, cs], preferred_element_type=F32) + bpw_ref[:, cs]
        mix_ref[:, cs] = (gate_ref[:, cs].astype(F32) * attn
                          + gate_ref[:, gs].astype(F32) * conv).astype(BF16)

    def out_piece(p):
        cs = slice(p * _MIX_COLS, (p + 1) * _MIX_COLS)
        x = jnp.where(in_a, xa_ref[:, cs], xb_ref[:, cs])
        x1 = x + jnp.dot(mix_ref[...], wout_ref[:, cs], preferred_element_type=F32)
        x1_ref[:, cs] = x1
        x1_new[:, cs] = x1

    mix_piece(0)
    ms = jnp.mean(x1_prev * x1_prev, axis=-1, keepdims=True)
    h2 = x1_prev * lax.rsqrt(ms + NORM_EPS) * gffn_ref[...]
    h_hi = h2.astype(BF16)
    h2_ref[...] = _pack_halves(h_hi)
    mix_piece(1)

    h_lo = (h2 - h_hi.astype(F32)).astype(BF16)
    logits = (lax.dot_general(wrh_ref[...], h_hi, _NT, preferred_element_type=F32)
              + lax.dot_general(wrh_ref[...], h_lo, _NT, preferred_element_type=F32)
              + lax.dot_general(wrl_ref[...], h_hi, _NT, preferred_element_type=F32)
              + br_ref[...])
    for p in range(2, n_pieces):
        mix_piece(p)
    out_piece(0)

    eidx = lax.broadcasted_iota(I32, (N_EXPERTS, tm), 0)
    vals = logits
    picked, top_vals, top_idx = [], [], []
    for _ in range(TOP_K):
        m = jnp.max(vals, axis=0, keepdims=True)
        idx = jnp.min(jnp.where(vals == m, eidx, N_EXPERTS), axis=0, keepdims=True)
        sel = eidx == idx
        vals = jnp.where(sel, -jnp.inf, vals)
        picked.append(sel)
        top_vals.append(m)
        top_idx.append(idx)
    out_piece(1)

    exps = [jnp.exp(v - top_vals[0]) for v in top_vals]
    tot = exps[0] + exps[1] + exps[2] + exps[3]
    onehot = (picked[0] | picked[1] | picked[2] | picked[3])
    prefix = jnp.dot(onehot.astype(BF16), tri_ref[...], preferred_element_type=F32) + carry_ref[...]
    for p in range(2, n_pieces):
        out_piece(p)
    for j in range(TOP_K):
        idx_ref[j:j + 1, :] = top_idx[j]
        gt_ref[j:j + 1, :] = exps[j] / tot
        rank_ref[j:j + 1, :] = jnp.sum(jnp.where(picked[j], prefix, 0.0), axis=0,
                                       keepdims=True).astype(I32)
    counted = jnp.where(i > 0, jnp.sum(onehot.astype(F32), axis=1, keepdims=True), 0.0)
    carry_ref[...] = carry_ref[...] + counted
    cnt_ref[...] = jnp.broadcast_to(carry_ref[...], cnt_ref.shape)


def _mix_route(xa, xb, o, c, gates, wo, wpw, bpw, wout, gffn, wr_hi, wr_lo, br, tri, geo, tm):
    T = geo.total
    n = T // tm
    n_a = geo.rows_a // tm
    head = lambda i: jnp.minimum(i, n - 1)
    tail = lambda i: jnp.maximum(i - 1, 0)
    row = lambda i: (head(i), 0)
    col = lambda i: (0, tail(i))
    const = lambda i: (0, 0)
    wspec = lambda shape: pl.BlockSpec(shape, const, pipeline_mode=pl.Buffered(1))
    x_specs = [pl.BlockSpec((tm, D_MODEL), lambda i: (jnp.minimum(head(i), n_a - 1), 0)),
               pl.BlockSpec((tm, D_MODEL), lambda i: (jnp.maximum(head(i) - n_a, 0), 0))]
    return pl.pallas_call(
        functools.partial(_mix_body, n_a),
        grid=(n + 1,),
        in_specs=x_specs + [
            pl.BlockSpec((tm, Q_COLS), row),
            pl.BlockSpec((tm, D_MODEL), row),
            pl.BlockSpec((tm, 2 * D_MODEL), row),
            wspec((Q_COLS, D_MODEL)),
            wspec((D_MODEL, D_MODEL)),
            pl.BlockSpec((1, D_MODEL), const),
            wspec((D_MODEL, D_MODEL)),
            pl.BlockSpec((1, D_MODEL), const),
            pl.BlockSpec((N_EXPERTS, D_MODEL), const),
            pl.BlockSpec((N_EXPERTS, D_MODEL), const),
            pl.BlockSpec((N_EXPERTS, 1), const),
            pl.BlockSpec((tm, tm), const),
        ],
        out_specs=[
            pl.BlockSpec((tm, D_MODEL), row),
            pl.BlockSpec((tm, PACKED), lambda i: (tail(i), 0)),
            pl.BlockSpec((TOP_K, tm), col),
            pl.BlockSpec((TOP_K, tm), col),
            pl.BlockSpec((TOP_K, tm), col),
            pl.BlockSpec((N_EXPERTS, LANES), const),
        ],
        out_shape=[
            jax.ShapeDtypeStruct((T, D_MODEL), F32),
            jax.ShapeDtypeStruct((T, PACKED), I32),
            jax.ShapeDtypeStruct((TOP_K, T), I32),
            jax.ShapeDtypeStruct((TOP_K, T), F32),
            jax.ShapeDtypeStruct((TOP_K, T), I32),
            jax.ShapeDtypeStruct((N_EXPERTS, LANES), F32),
        ],
        scratch_shapes=[pltpu.VMEM((2, tm, D_MODEL), F32), pltpu.VMEM((tm, D_MODEL), BF16),
                        pltpu.VMEM((N_EXPERTS, 1), F32)],
        compiler_params=_params("arbitrary"),
        name="mix_route",
    )(xa, xb, o, c, gates, wo, wpw, bpw, wout, gffn, wr_hi, wr_lo, br, tri)


def _sc_workers():
    info = plsc.get_sparse_core_info()
    return info.num_cores, info.num_cores * info.num_subcores


def _sc_dispatch(h, dest, n_out):
    t_rows, width = h.shape
    nc, nw = _sc_workers()
    per_w = t_rows // nw
    step = SC_ROWS_PER_STEP
    assert per_w * nw == t_rows and per_w % step == 0
    mesh = plsc.VectorSubcoreMesh(core_axis_name="c", subcore_axis_name="s")

    @functools.partial(
        pl.kernel, mesh=mesh,
        out_type=jax.ShapeDtypeStruct((n_out, width), h.dtype),
        scratch_types=[pltpu.VMEM((step,), I32)] * TOP_K + [pltpu.VMEM((step, width), h.dtype)],
    )
    def scatter_rows(h_hbm, dest_hbm, out_hbm, i0, i1, i2, i3, rows_v):
        base = (lax.axis_index("s") * nc + lax.axis_index("c")) * per_w

        @pl.loop(0, per_w // step)
        def _(i):
            off = base + i * step
            pltpu.sync_copy(h_hbm.at[pl.ds(off, step)], rows_v)
            for j, idx_v in enumerate((i0, i1, i2, i3)):
                pltpu.sync_copy(dest_hbm.at[j, pl.ds(off, step)], idx_v)
                pltpu.sync_copy(rows_v, out_hbm.at[idx_v])

    return scatter_rows(h, dest)


def _sc_gather(table, dest, row0, rows):
    width = table.shape[1]
    nc, nw = _sc_workers()
    per_w = rows // nw
    step = SC_ROWS_PER_STEP
    assert per_w * nw == rows and per_w % step == 0
    mesh = plsc.VectorSubcoreMesh(core_axis_name="c", subcore_axis_name="s")

    @functools.partial(
        pl.kernel, mesh=mesh,
        out_type=jax.ShapeDtypeStruct((TOP_K * rows, width), table.dtype),
        scratch_types=[pltpu.VMEM((step,), I32), pltpu.VMEM((step, width), table.dtype),
                       pltpu.SemaphoreType.DMA],
    )
    def gather_rows(table_hbm, dest_hbm, out_hbm, idx_v, rows_v, sem):
        base = (lax.axis_index("s") * nc + lax.axis_index("c")) * per_w

        @pl.loop(0, per_w // step)
        def _(i):
            off = base + i * step
            for j in range(TOP_K):
                pltpu.sync_copy(dest_hbm.at[j, pl.ds(row0 + off, step)], idx_v)
                pltpu.async_copy(table_hbm.at[idx_v], rows_v, sem).wait()
                pltpu.sync_copy(rows_v, out_hbm.at[pl.ds(j * rows + off, step)])

    return gather_rows(table, dest)


def _expert_body(bexp_ref, nused_ref, valid_ref, xs_ref, wgu_ref, bgu_ref, wd_ref, bd_ref, o_ref,
                 wgu_bf, wd_bf):
    n = pl.program_id(0)
    active = n < nused_ref[0]
    new_expert = (n == 0) | (bexp_ref[n] != bexp_ref[jnp.maximum(n - 1, 0)])

    @pl.when(active & new_expert)
    def _():
        wgu_bf[...] = wgu_ref[...].astype(BF16)
        wd_bf[...] = wd_ref[...].astype(BF16)

    sub = EXPERT_SUB_ROWS
    for r0 in range(0, xs_ref.shape[0], sub):
        rs = slice(r0, r0 + sub)
        live = active & (valid_ref[n] > r0)

        @pl.when(live)
        def _():
            row = r0 + lax.broadcasted_iota(I32, (sub, 1), 0)
            x = _unpack_halves(jnp.where(row < valid_ref[n], xs_ref[rs, :], 0), BF16)
            gu = jnp.dot(x, wgu_bf[...], preferred_element_type=F32) + bgu_ref[...]
            g = jnp.minimum(gu[:, 0:D_FF], SWIGLU_LIMIT)
            u = jnp.clip(gu[:, D_FF:2 * D_FF], -SWIGLU_LIMIT, SWIGLU_LIMIT)
            act = g * jax.nn.sigmoid(SWIGLU_ALPHA * g) * (u + 1.0)
            out = jnp.dot(act.astype(BF16), wd_bf[...], preferred_element_type=F32) + bd_ref[...]
            o_ref[rs, :] = _pack_halves(out.astype(BF16))

        @pl.when(jnp.logical_not(live))
        def _():
            o_ref[rs, :] = jnp.zeros((sub, o_ref.shape[1]), o_ref.dtype)


def _experts(block_exp, n_used, block_valid, xs, wgu, bgu, wd, bd, bm):
    P = xs.shape[0]
    nblk = P // bm
    xmap = lambda n, be, nu, bv: (jnp.minimum(n, nu[0] - 1), 0)
    emap = lambda n, be, nu, bv: (be[n], 0, 0)
    grid_spec = pltpu.PrefetchScalarGridSpec(
        num_scalar_prefetch=3,
        grid=(nblk,),
        in_specs=[
            pl.BlockSpec((bm, PACKED), xmap),
            pl.BlockSpec((None, D_MODEL, 2 * D_FF), emap),
            pl.BlockSpec((None, 1, 2 * D_FF), emap),
            pl.BlockSpec((None, D_FF, D_MODEL), emap),
            pl.BlockSpec((None, 1, D_MODEL), emap),
        ],
        out_specs=pl.BlockSpec((bm, PACKED), lambda n, be, nu, bv: (n, 0)),
        scratch_shapes=[pltpu.VMEM((D_MODEL, 2 * D_FF), BF16), pltpu.VMEM((D_FF, D_MODEL), BF16)],
    )
    return pl.pallas_call(
        _expert_body,
        grid_spec=grid_spec,
        out_shape=jax.ShapeDtypeStruct((P, PACKED), I32),
        compiler_params=_params("arbitrary"),
        name="experts",
    )(block_exp, n_used, block_valid, xs, wgu, bgu, wd, bd)


def _combine_body(x1_ref, g0_ref, g1_ref, g2_ref, g3_ref, gate_ref, gfin_ref, y_ref):
    tf = x1_ref.shape[0]
    gate_rows = jnp.concatenate(
        [gate_ref[...], jnp.zeros((LANES - TOP_K, tf), F32)], axis=0)
    for c in range(tf // LANES):
        rs = slice(c * LANES, (c + 1) * LANES)
        gate = gate_rows[:, rs].T
        y = x1_ref[rs, :]
        for j, g_ref in enumerate((g0_ref, g1_ref, g2_ref, g3_ref)):
            y = y + gate[:, j:j + 1] * _unpack_halves(g_ref[rs, :], F32)
        ms = jnp.mean(y * y, axis=-1, keepdims=True)
        y_ref[rs, :] = y * lax.rsqrt(ms + NORM_EPS) * gfin_ref[...]


def _combine(x1, gathered, gate_t, gfin, row0, rows, tf):
    nt = rows // tf
    t0 = row0 // tf
    row = lambda i: (t0 + i, 0)
    choice = lambda j: pl.BlockSpec((tf, PACKED), lambda i: (j * nt + i, 0))
    return pl.pallas_call(
        _combine_body,
        grid=(nt,),
        in_specs=[pl.BlockSpec((tf, D_MODEL), row)] + [choice(j) for j in range(TOP_K)] + [
            pl.BlockSpec((TOP_K, tf), lambda i: (0, t0 + i)),
            pl.BlockSpec((1, D_MODEL), lambda i: (0, 0)),
        ],
        out_specs=pl.BlockSpec((tf, D_MODEL), lambda i: (i, 0)),
        out_shape=jax.ShapeDtypeStruct((rows, D_MODEL), F32),
        compiler_params=_params("parallel"),
        name="combine",
    )(x1, gathered, gathered, gathered, gathered, gate_t, gfin)


def _permute_in_proj(w):
    lead = w.shape[:-1]
    half = HEAD_DIM // 2
    q = w[..., :Q_COLS].reshape(*lead, N_Q_HEADS // 2, 2, 2, half)
    q = jnp.swapaxes(q, -3, -2).reshape(*lead, Q_COLS)
    k = w[..., Q_COLS:Q_COLS + KV_COLS].reshape(*lead, N_KV_HEADS, 2, 1, half)
    k = jnp.broadcast_to(k, (*lead, N_KV_HEADS, 2, 2, half)).reshape(*lead, KV_DUP_COLS)
    v = w[..., Q_COLS + KV_COLS:Q_COLS + 2 * KV_COLS]
    return jnp.concatenate([q, k, w[..., Q_COLS + 2 * KV_COLS:]], axis=-1), v


def _rope_tables(n_pos):
    half = HEAD_DIM // 2
    inv_freq = 1.0 / (ROPE_THETA ** (jnp.arange(half, dtype=F32) * (2.0 / HEAD_DIM)))
    ang = jnp.arange(n_pos, dtype=F32)[:, None] * inv_freq[None, :]
    cos = jnp.tile(jnp.cos(ang), (1, LANES // half))
    sin = jnp.tile(jnp.sin(ang), (1, LANES // half))
    sign = jnp.where(jnp.arange(LANES) < LANES // 2, -1.0, 1.0).astype(F32)
    return cos, sin * sign[None, :]


def _tiles(geo):
    unit = min(geo.len_a, geo.len_b)
    tile = min(512, unit)
    return dict(tm=tile, tc=tile, tf=tile, bm=2 * EXPERT_SUB_ROWS)


def kernel(x_prompt, x_sample, norm_mix_g, w_in, b_in, attn_sink, w_o_attn, conv_dw_w, conv_dw_b,
           conv_ln_g, conv_ln_b, w_pw2, b_pw2, w_out, norm_ffn_g, w_router, b_router, w_gu, b_gu,
           w_down, b_down, norm_final_g):
    assert w_in.shape[0] == 1, "single trunk layer"
    geo = Geo(x_prompt.shape[0], x_prompt.shape[1], x_sample.shape[0], x_sample.shape[1])
    T = geo.total
    ts = _tiles(geo)
    xa = x_prompt.reshape(-1, D_MODEL)
    xb = x_sample.reshape(-1, D_MODEL)

    w_perm, w_v = _permute_in_proj(w_in[0])
    b_perm, b_v = _permute_in_proj(b_in)
    cos_t, sin_t = _rope_tables(max(geo.len_a, geo.len_b))

    q, kd, vt, glu, gates = _in_proj(xa, xb, norm_mix_g, w_perm.astype(BF16), b_perm,
                                     w_v.T.astype(BF16), b_v.T, cos_t, sin_t, geo, ts["tm"])
    attn = _attention(q, kd, vt, attn_sink[0], geo)
    w_rep = jnp.repeat(conv_dw_w[0], SUBLANES, axis=0)
    conv = _conv_branch(glu, w_rep, conv_dw_b, conv_ln_g, conv_ln_b, geo, ts["tc"])

    wr_t = w_router[0].T
    wr_hi = wr_t.astype(BF16)
    wr_lo = (wr_t - wr_hi.astype(F32)).astype(BF16)
    tri = jnp.triu(jnp.ones((ts["tm"], ts["tm"]), BF16), 1)
    x1, h2, idx, gate_t, rank, counts = _mix_route(
        xa, xb, attn, conv, gates, w_o_attn[0].astype(BF16), w_pw2[0].astype(BF16), b_pw2,
        w_out[0].astype(BF16), norm_ffn_g, wr_hi, wr_lo, b_router[0][:, None], tri, geo, ts["tm"])

    bm = ts["bm"]
    n_blocks = (T * TOP_K) // bm + N_EXPERTS
    cnt = counts[:, 0].astype(I32)
    padded = ((cnt + bm - 1) // bm) * bm
    pad_end = jnp.cumsum(padded)
    pad_start = pad_end - padded
    expert_ids = jnp.arange(N_EXPERTS, dtype=I32)
    dest = rank + jnp.sum(jnp.where(idx[None] == expert_ids[:, None, None],
                                    pad_start[:, None, None], 0), axis=0)
    block_start = jnp.arange(n_blocks, dtype=I32) * bm
    block_exp = jnp.minimum(jnp.sum((pad_end[None, :] <= block_start[:, None]).astype(I32), axis=1),
                            N_EXPERTS - 1)
    n_used = (pad_end[-1:] // bm).astype(I32)
    seg_end = jnp.sum(jnp.where(block_exp[:, None] == expert_ids[None, :],
                                (pad_start + cnt)[None, :], 0), axis=1)
    block_valid = jnp.clip(seg_end - block_start, 0, bm).astype(I32)

    xs = _sc_dispatch(h2, dest, n_blocks * bm)
    ys = _experts(block_exp, n_used, block_valid, xs, w_gu[0], b_gu[0][:, None, :],
                  w_down[0], b_down[0][:, None, :], bm)
    outs = []
    for row0, rows in ((0, geo.rows_a), (geo.rows_a, T - geo.rows_a)):
        gathered = _sc_gather(ys, dest, row0, rows)
        outs.append(_combine(x1, gathered, gate_t, norm_final_g[None, :], row0, rows, ts["tf"]))
    return (outs[0].reshape(x_prompt.shape), outs[1].reshape(x_sample.shape))
```

```python
import functools
import math
from typing import NamedTuple

import jax
import jax.numpy as jnp
from jax import lax
from jax.experimental import pallas as pl
from jax.experimental.pallas import tpu as pltpu
from jax.experimental.pallas import tpu_sc as plsc

F32 = jnp.float32
BF16 = jnp.bfloat16
I32 = jnp.int32

D_MODEL = 1024
HEAD_DIM = 64
N_Q_HEADS = 16
N_KV_HEADS = 4
WINDOW = 128
ATTN_BLOCK = 128
ROPE_THETA = 10000.0
CONV_WIDTH = 31
CONV_PAD = CONV_WIDTH // 2
N_EXPERTS = 32
TOP_K = 4
D_FF = D_MODEL
SWIGLU_LIMIT = 7.0
SWIGLU_ALPHA = 1.702
NORM_EPS = 1e-5
NEG_INF = -1e30

Q_COLS = N_Q_HEADS * HEAD_DIM
KV_COLS = N_KV_HEADS * HEAD_DIM
LANES = 128
SUBLANES = 8
KV_DUP_COLS = N_KV_HEADS * LANES
HALO = 16
VMEM_LIMIT = 56 * 1024 * 1024
SC_ROWS_PER_STEP = 128
EXPERT_SUB_ROWS = 512


class Geo(NamedTuple):
    n_a: int
    len_a: int
    n_b: int
    len_b: int

    @property
    def rows_a(self):
        return self.n_a * self.len_a

    @property
    def total(self):
        return self.rows_a + self.n_b * self.len_b


def _seq_bounds(geo, r):
    in_a = r < geo.rows_a
    start_a = (r // geo.len_a) * geo.len_a
    start_b = geo.rows_a + ((r - geo.rows_a) // geo.len_b) * geo.len_b
    start = jnp.where(in_a, start_a, start_b)
    end = start + jnp.where(in_a, geo.len_a, geo.len_b)
    return start, end


def _params(*sem):
    return pltpu.CompilerParams(dimension_semantics=sem, vmem_limit_bytes=VMEM_LIMIT)


PACKED = D_MODEL // 2


def _pack_halves(x_bf16):
    hi = lax.bitcast_convert_type(x_bf16[:, :PACKED].astype(F32), I32)
    lo = lax.bitcast_convert_type(x_bf16[:, PACKED:].astype(F32), I32)
    return hi | lax.shift_right_logical(lo, 16)


def _unpack_halves(words, dtype):
    hi = lax.bitcast_convert_type(words & jnp.int32(-65536), F32)
    lo = lax.bitcast_convert_type(lax.shift_left(words, 16), F32)
    return jnp.concatenate([hi.astype(dtype), lo.astype(dtype)], axis=1)


def _two_group_specs(geo, tile, width):
    n_a = geo.rows_a // tile
    return n_a, [pl.BlockSpec((tile, width), lambda i: (jnp.minimum(i, n_a - 1), 0)),
                 pl.BlockSpec((tile, width), lambda i: (jnp.maximum(i - n_a, 0), 0))]


_C_Q = 0
_C_K = _C_Q + Q_COLS
_C_GA = _C_K + KV_DUP_COLS
_C_GG = _C_GA + D_MODEL
_C_GATE = _C_GG + D_MODEL
_N_IN = _C_GATE + 2 * D_MODEL
_PROJ_CHUNK = 512
N_SLAB = D_MODEL // LANES
_NT = (((1,), (1,)), ((), ()))
_LOG2E = math.log2(math.e)
_Q_SCALE = HEAD_DIM ** -0.5 * _LOG2E


def _in_proj_body(n_a, xa_ref, xb_ref, g_ref, w_ref, b_ref, wvt_ref, bvt_ref, cos_ref, sin_ref,
                  q_ref, kd_ref, vt_ref, glu_ref, gate_ref):
    x = jnp.where(pl.program_id(0) < n_a, xa_ref[...], xb_ref[...])
    ms = jnp.mean(x * x, axis=-1, keepdims=True)
    h = (x * lax.rsqrt(ms + NORM_EPS) * g_ref[...]).astype(BF16)
    cos = cos_ref[...]
    sin = sin_ref[...]

    def proj(c0):
        return (jnp.dot(h, w_ref[:, c0:c0 + _PROJ_CHUNK], preferred_element_type=F32)
                + b_ref[:, c0:c0 + _PROJ_CHUNK])

    def rope_store(z, out_ref, o0, scale):
        for c in range(_PROJ_CHUNK // LANES):
            zc = z[:, c * LANES:(c + 1) * LANES]
            r = zc * cos + pltpu.roll(zc, LANES // 2, 1) * sin
            if scale != 1.0:
                r = r * scale
            out_ref[:, o0 + c * LANES:o0 + (c + 1) * LANES] = r.astype(out_ref.dtype)

    for c in range(Q_COLS // _PROJ_CHUNK):
        rope_store(proj(_C_Q + c * _PROJ_CHUNK), q_ref, c * _PROJ_CHUNK, _Q_SCALE)
    for c in range(KV_DUP_COLS // _PROJ_CHUNK):
        rope_store(proj(_C_K + c * _PROJ_CHUNK), kd_ref, c * _PROJ_CHUNK, 1.0)
    vt = lax.dot_general(wvt_ref[...], h, _NT, preferred_element_type=F32) + bvt_ref[...]
    vt_ref[...] = vt.astype(BF16)
    per = _PROJ_CHUNK // LANES
    for c in range(D_MODEL // _PROJ_CHUNK):
        a = proj(_C_GA + c * _PROJ_CHUNK)
        g = proj(_C_GG + c * _PROJ_CHUNK)
        glu = a * jax.nn.sigmoid(g)
        for s in range(per):
            glu_ref[c * per + s] = glu[:, s * LANES:(s + 1) * LANES]
    for c in range(2 * D_MODEL // _PROJ_CHUNK):
        gate_ref[:, c * _PROJ_CHUNK:(c + 1) * _PROJ_CHUNK] = jax.nn.sigmoid(
            proj(_C_GATE + c * _PROJ_CHUNK)).astype(BF16)


def _in_proj(xa, xb, g_mix, w_perm, b_perm, wvt, bvt, cos_t, sin_t, geo, tm):
    T = geo.total

    def pos_map(i):
        r0 = i * tm
        start, _ = _seq_bounds(geo, r0)
        return ((r0 - start) // tm, 0)

    const = lambda i: (0, 0)
    row = lambda i: (i, 0)
    n_a, x_specs = _two_group_specs(geo, tm, D_MODEL)
    return pl.pallas_call(
        functools.partial(_in_proj_body, n_a),
        grid=(T // tm,),
        in_specs=x_specs + [
            pl.BlockSpec((1, D_MODEL), const),
            pl.BlockSpec((D_MODEL, _N_IN), const, pipeline_mode=pl.Buffered(1)),
            pl.BlockSpec((1, _N_IN), const),
            pl.BlockSpec((KV_COLS, D_MODEL), const),
            pl.BlockSpec((KV_COLS, 1), const),
            pl.BlockSpec((tm, LANES), pos_map),
            pl.BlockSpec((tm, LANES), pos_map),
        ],
        out_specs=[
            pl.BlockSpec((tm, Q_COLS), row),
            pl.BlockSpec((tm, KV_DUP_COLS), row),
            pl.BlockSpec((KV_COLS, tm), lambda i: (0, i)),
            pl.BlockSpec((N_SLAB, tm, LANES), lambda i: (0, i, 0)),
            pl.BlockSpec((tm, 2 * D_MODEL), row),
        ],
        out_shape=[
            jax.ShapeDtypeStruct((T, Q_COLS), BF16),
            jax.ShapeDtypeStruct((T, KV_DUP_COLS), BF16),
            jax.ShapeDtypeStruct((KV_COLS, T), BF16),
            jax.ShapeDtypeStruct((N_SLAB, T, LANES), F32),
            jax.ShapeDtypeStruct((T, 2 * D_MODEL), BF16),
        ],
        compiler_params=_params("parallel"),
        name="in_proj",
    )(xa, xb, g_mix, w_perm, b_perm, wvt, bvt, cos_t, sin_t)


_ONES_ROWS = 16
_ATTN_STEP_BLOCKS = 4


def _attn_body(geo, sink_ref, q_ref, kp_ref, kc_ref, kn_ref, vp_ref, vc_ref, vn_ref, o_ref):
    group = N_Q_HEADS // N_KV_HEADS
    nq = group * ATTN_BLOCK
    b = ATTN_BLOCK
    hd = HEAD_DIM
    nsub = _ATTN_STEP_BLOCKS

    key = lax.broadcasted_iota(I32, (b, nq), 0)
    qry = lax.broadcasted_iota(I32, (b, nq), 1) % b
    head_of_col = lax.broadcasted_iota(I32, (1, nq), 1) // b
    lane = lax.broadcasted_iota(I32, (b, LANES), 1)
    even_head = (lane % hd) < (hd // 2)
    ones = jnp.ones((_ONES_ROWS, 3 * b), BF16)

    for s in range(nsub):
        rows = slice(s * b, (s + 1) * b)
        r0 = (pl.program_id(0) * nsub + s) * b
        start, end = _seq_bounds(geo, r0)
        bias_prev = jnp.where((key >= qry) & (r0 > start), 0.0, NEG_INF)
        bias_next = jnp.where((key <= qry) & (r0 + b < end), 0.0, NEG_INF)

        sts, vts = [], []
        for g in range(N_KV_HEADS):
            ls = slice(g * LANES, (g + 1) * LANES)
            k_prev = kp_ref[:, ls] if s == 0 else kc_ref[(s - 1) * b:s * b, ls]
            k_next = kn_ref[:, ls] if s == nsub - 1 else kc_ref[(s + 1) * b:(s + 2) * b, ls]
            vs = slice(g * hd, (g + 1) * hd)
            v_prev = vp_ref[vs, :] if s == 0 else vc_ref[vs, (s - 1) * b:s * b]
            v_next = vn_ref[vs, :] if s == nsub - 1 else vc_ref[vs, (s + 1) * b:(s + 2) * b]

            qa = q_ref[rows, (2 * g) * LANES:(2 * g + 1) * LANES]
            qb = q_ref[rows, (2 * g + 1) * LANES:(2 * g + 2) * LANES]
            zero = jnp.zeros_like(qa)
            q4 = jnp.concatenate([jnp.where(even_head, qa, zero), jnp.where(even_head, zero, qa),
                                  jnp.where(even_head, qb, zero), jnp.where(even_head, zero, qb)],
                                 axis=0)
            k = jnp.concatenate([k_prev, kc_ref[rows, ls], k_next], axis=0)
            sts.append(lax.dot_general(k, q4, _NT, preferred_element_type=F32))
            vts.append(jnp.concatenate([v_prev, vc_ref[vs, rows], v_next], axis=1))

        ps, ms, sinks = [], [], []
        for g in range(N_KV_HEADS):
            st = sts[g]
            s_prev = st[0:b] + bias_prev
            s_cur = st[b:2 * b]
            s_next = st[2 * b:3 * b] + bias_next
            sink = jnp.full((1, nq), sink_ref[group * g] * _LOG2E, F32)
            for h in range(1, group):
                sink = jnp.where(head_of_col == h, sink_ref[group * g + h] * _LOG2E, sink)
            m = jnp.maximum(jnp.maximum(jnp.max(s_prev, axis=0, keepdims=True),
                                        jnp.max(s_cur, axis=0, keepdims=True)),
                            jnp.maximum(jnp.max(s_next, axis=0, keepdims=True), sink))
            ps.append(jnp.concatenate([jnp.exp2(s_prev - m).astype(BF16),
                                       jnp.exp2(s_cur - m).astype(BF16),
                                       jnp.exp2(s_next - m).astype(BF16)], axis=0))
            ms.append(m)
            sinks.append(sink)

        for g in range(N_KV_HEADS):
            ot = jnp.dot(jnp.concatenate([vts[g], ones], axis=0), ps[g], preferred_element_type=F32)
            denom = ot[hd:hd + 1] + jnp.exp2(sinks[g] - ms[g])
            on = ot[0:hd] * (1.0 / denom)
            pair_a = jnp.concatenate([on[:, 0:b], on[:, b:2 * b]], axis=0)
            pair_b = jnp.concatenate([on[:, 2 * b:3 * b], on[:, 3 * b:4 * b]], axis=0)
            o_ref[rows, (2 * g) * LANES:(2 * g + 1) * LANES] = pair_a.T.astype(BF16)
            o_ref[rows, (2 * g + 1) * LANES:(2 * g + 2) * LANES] = pair_b.T.astype(BF16)


def _attention(q, kd, vt, sink, geo):
    T = q.shape[0]
    nsub = _ATTN_STEP_BLOCKS
    nb = T // ATTN_BLOCK
    step_rows = nsub * ATTN_BLOCK
    prev = lambda i: jnp.maximum(i * nsub - 1, 0)
    nxt = lambda i: jnp.minimum((i + 1) * nsub, nb - 1)
    k_edge = lambda m: pl.BlockSpec((ATTN_BLOCK, KV_DUP_COLS), lambda i: (m(i), 0))
    v_edge = lambda m: pl.BlockSpec((KV_COLS, ATTN_BLOCK), lambda i: (0, m(i)))
    return pl.pallas_call(
        functools.partial(_attn_body, geo),
        grid=(T // step_rows,),
        in_specs=[
            pl.BlockSpec(memory_space=pltpu.SMEM),
            pl.BlockSpec((step_rows, Q_COLS), lambda i: (i, 0)),
            k_edge(prev), pl.BlockSpec((step_rows, KV_DUP_COLS), lambda i: (i, 0)), k_edge(nxt),
            v_edge(prev), pl.BlockSpec((KV_COLS, step_rows), lambda i: (0, i)), v_edge(nxt),
        ],
        out_specs=pl.BlockSpec((step_rows, Q_COLS), lambda i: (i, 0)),
        out_shape=jax.ShapeDtypeStruct((T, Q_COLS), BF16),
        compiler_params=_params("parallel"),
        name="window_attn",
    )(sink, q, kd, kd, kd, vt, vt, vt)


_CONV_ROWS = 128
_LN_ROWS = 32


def _conv_body(geo, tc, z_ref, zp_ref, zn_ref, w_ref, dwb_ref, lng_ref, lnb_ref, o_ref,
               buf_ref, y_ref):
    i = pl.program_id(0)
    r0 = i * tc
    start, end = _seq_bounds(geo, r0)
    has_prev = (r0 > start).astype(F32)
    has_next = (r0 + tc < end).astype(F32)
    rows = tc + 2 * HALO
    buf_ref[:, 0:HALO, :] = zp_ref[...] * has_prev
    buf_ref[:, HALO:HALO + tc, :] = z_ref[...]
    buf_ref[:, HALO + tc:rows, :] = zn_ref[...] * has_next

    rep = _CONV_ROWS // SUBLANES

    def conv_chunk(j):
        s0 = pl.multiple_of(j * _CONV_ROWS, _CONV_ROWS)
        for c0 in range(0, N_SLAB, 2):
            accs = [jnp.zeros((_CONV_ROWS, LANES), F32) for _ in range(2)]
            for k in range(CONV_WIDTH):
                for d in range(2):
                    ls = slice((c0 + d) * LANES, (c0 + d + 1) * LANES)
                    zt = buf_ref[c0 + d, pl.ds(s0 + (HALO - CONV_PAD + k), _CONV_ROWS, stride=1), :]
                    wt = w_ref[k * SUBLANES:(k + 1) * SUBLANES, ls]
                    accs[d] = accs[d] + zt * jnp.concatenate([wt] * rep, axis=0)
            for d in range(2):
                y_ref[pl.ds(s0, _CONV_ROWS), (c0 + d) * LANES:(c0 + d + 1) * LANES] = accs[d]

    def ln_chunk(j):
        for h in range(_CONV_ROWS // _LN_ROWS):
            s0 = pl.multiple_of(j * _CONV_ROWS + h * _LN_ROWS, _LN_ROWS)
            y = y_ref[pl.ds(s0, _LN_ROWS), :] + dwb_ref[...]
            mu = jnp.mean(y, axis=-1, keepdims=True)
            yc = y - mu
            var = jnp.mean(yc * yc, axis=-1, keepdims=True)
            yn = yc * lax.rsqrt(var + NORM_EPS) * lng_ref[...] + lnb_ref[...]
            o_ref[pl.ds(s0, _LN_ROWS), :] = (yn * jax.nn.sigmoid(yn)).astype(BF16)

    n = tc // _CONV_ROWS
    conv_chunk(0)

    def body(j, carry):
        ln_chunk(j - 1)
        conv_chunk(j)
        return carry

    lax.fori_loop(1, n, body, 0)
    ln_chunk(n - 1)


def _conv_branch(glu, w_rep, dw_b, ln_g, ln_b, geo, tc):
    T = glu.shape[1]
    nh = T // HALO
    per = tc // HALO
    const = lambda i: (0, 0)
    return pl.pallas_call(
        functools.partial(_conv_body, geo, tc),
        grid=(T // tc,),
        in_specs=[
            pl.BlockSpec((N_SLAB, tc, LANES), lambda i: (0, i, 0)),
            pl.BlockSpec((N_SLAB, HALO, LANES), lambda i: (0, jnp.maximum(i * per - 1, 0), 0)),
            pl.BlockSpec((N_SLAB, HALO, LANES), lambda i: (0, jnp.minimum((i + 1) * per, nh - 1), 0)),
            pl.BlockSpec((CONV_WIDTH * SUBLANES, D_MODEL), const),
            pl.BlockSpec((1, D_MODEL), const),
            pl.BlockSpec((1, D_MODEL), const),
            pl.BlockSpec((1, D_MODEL), const),
        ],
        out_specs=pl.BlockSpec((tc, D_MODEL), lambda i: (i, 0)),
        out_shape=jax.ShapeDtypeStruct((T, D_MODEL), BF16),
        scratch_shapes=[pltpu.VMEM((N_SLAB, tc + 2 * HALO, LANES), F32),
                        pltpu.VMEM((tc, D_MODEL), F32)],
        compiler_params=_params("parallel"),
        name="conv_branch",
    )(glu, glu, glu, w_rep, dw_b, ln_g, ln_b)


_MIX_COLS = 256


def _mix_body(n_a, xa_ref, xb_ref, o_ref, c_ref, gate_ref, wo_ref, wpw_ref, bpw_ref, wout_ref,
              gffn_ref, wrh_ref, wrl_ref, br_ref, tri_ref,
              x1_ref, h2_ref, idx_ref, gt_ref, rank_ref, cnt_ref, x1_scr, mix_ref, carry_ref):
    tm = xa_ref.shape[0]
    i = pl.program_id(0)
    last = pl.num_programs(0) - 2
    slot = i % 2

    @pl.when(i == 0)
    def _():
        carry_ref[...] = jnp.zeros_like(carry_ref)
        x1_scr[...] = jnp.zeros_like(x1_scr)

    x1_prev = x1_scr[1 - slot]
    x1_new = x1_scr.at[slot]
    in_a = jnp.minimum(i, last) < n_a
    n_pieces = D_MODEL // _MIX_COLS

    def mix_piece(p):
        cs = slice(p * _MIX_COLS, (p + 1) * _MIX_COLS)
        gs = slice(D_MODEL + p * _MIX_COLS, D_MODEL + (p + 1) * _MIX_COLS)
        attn = jnp.dot(o_ref[...], wo_ref[:, cs], preferred_element_type=F32)
        conv = jnp.dot(c_ref[...], wpw_ref[:, cs], preferred_element_type=F32) + bpw_ref[:, cs]
        mix_ref[:, cs] = (gate_ref[:, cs].astype(F32) * attn
                          + gate_ref[:, gs].astype(F32) * conv).astype(BF16)

    def out_piece(p):
        cs = slice(p * _MIX_COLS, (p + 1) * _MIX_COLS)
        x = jnp.where(in_a, xa_ref[:, cs], xb_ref[:, cs])
        x1 = x + jnp.dot(mix_ref[...], wout_ref[:, cs], preferred_element_type=F32)
        x1_ref[:, cs] = x1
        x1_new[:, cs] = x1

    mix_piece(0)
    ms = jnp.mean(x1_prev * x1_prev, axis=-1, keepdims=True)
    h2 = x1_prev * lax.rsqrt(ms + NORM_EPS) * gffn_ref[...]
    h_hi = h2.astype(BF16)
    h2_ref[...] = _pack_halves(h_hi)
    mix_piece(1)

    h_lo = (h2 - h_hi.astype(F32)).astype(BF16)
    logits = (lax.dot_general(wrh_ref[...], h_hi, _NT, preferred_element_type=F32)
              + lax.dot_general(wrh_ref[...], h_lo, _NT, preferred_element_type=F32)
              + lax.dot_general(wrl_ref[...], h_hi, _NT, preferred_element_type=F32)
              + br_ref[...])
    for p in range(2, n_pieces):
        mix_piece(p)
    out_piece(0)

    eidx = lax.broadcasted_iota(I32, (N_EXPERTS, tm), 0)
    vals = logits
    picked, top_vals, top_idx = [], [], []
    for _ in range(TOP_K):
        m = jnp.max(vals, axis=0, keepdims=True)
        idx = jnp.min(jnp.where(vals == m, eidx, N_EXPERTS), axis=0, keepdims=True)
        sel = eidx == idx
        vals = jnp.where(sel, -jnp.inf, vals)
        picked.append(sel)
        top_vals.append(m)
        top_idx.append(idx)
    out_piece(1)

    exps = [jnp.exp(v - top_vals[0]) for v in top_vals]
    tot = exps[0] + exps[1] + exps[2] + exps[3]
    onehot = (picked[0] | picked[1] | picked[2] | picked[3])
    prefix = jnp.dot(onehot.astype(BF16), tri_ref[...], preferred_element_type=F32) + carry_ref[...]
    for p in range(2, n_pieces):
        out_piece(p)
    for j in range(TOP_K):
        idx_ref[j:j + 1, :] = top_idx[j]
        gt_ref[j:j + 1, :] = exps[j] / tot
        rank_ref[j:j + 1, :] = jnp.sum(jnp.where(picked[j], prefix, 0.0), axis=0,
                                       keepdims=True).astype(I32)
    counted = jnp.where(i > 0, jnp.sum(onehot.astype(F32), axis=1, keepdims=True), 0.0)
    carry_ref[...] = carry_ref[...] + counted
    cnt_ref[...] = jnp.broadcast_to(carry_ref[...], cnt_ref.shape)


def _mix_route(xa, xb, o, c, gates, wo, wpw, bpw, wout, gffn, wr_hi, wr_lo, br, tri, geo, tm):
    T = geo.total
    n = T // tm
    n_a = geo.rows_a // tm
    head = lambda i: jnp.minimum(i, n - 1)
    tail = lambda i: jnp.maximum(i - 1, 0)
    row = lambda i: (head(i), 0)
    col = lambda i: (0, tail(i))
    const = lambda i: (0, 0)
    wspec = lambda shape: pl.BlockSpec(shape, const, pipeline_mode=pl.Buffered(1))
    x_specs = [pl.BlockSpec((tm, D_MODEL), lambda i: (jnp.minimum(head(i), n_a - 1), 0)),
               pl.BlockSpec((tm, D_MODEL), lambda i: (jnp.maximum(head(i) - n_a, 0), 0))]
    return pl.pallas_call(
        functools.partial(_mix_body, n_a),
        grid=(n + 1,),
        in_specs=x_specs + [
            pl.BlockSpec((tm, Q_COLS), row),
            pl.BlockSpec((tm, D_MODEL), row),
            pl.BlockSpec((tm, 2 * D_MODEL), row),
            wspec((Q_COLS, D_MODEL)),
            wspec((D_MODEL, D_MODEL)),
            pl.BlockSpec((1, D_MODEL), const),
            wspec((D_MODEL, D_MODEL)),
            pl.BlockSpec((1, D_MODEL), const),
            pl.BlockSpec((N_EXPERTS, D_MODEL), const),
            pl.BlockSpec((N_EXPERTS, D_MODEL), const),
            pl.BlockSpec((N_EXPERTS, 1), const),
            pl.BlockSpec((tm, tm), const),
        ],
        out_specs=[
            pl.BlockSpec((tm, D_MODEL), row),
            pl.BlockSpec((tm, PACKED), lambda i: (tail(i), 0)),
            pl.BlockSpec((TOP_K, tm), col),
            pl.BlockSpec((TOP_K, tm), col),
            pl.BlockSpec((TOP_K, tm), col),
            pl.BlockSpec((N_EXPERTS, LANES), const),
        ],
        out_shape=[
            jax.ShapeDtypeStruct((T, D_MODEL), F32),
            jax.ShapeDtypeStruct((T, PACKED), I32),
            jax.ShapeDtypeStruct((TOP_K, T), I32),
            jax.ShapeDtypeStruct((TOP_K, T), F32),
            jax.ShapeDtypeStruct((TOP_K, T), I32),
            jax.ShapeDtypeStruct((N_EXPERTS, LANES), F32),
        ],
        scratch_shapes=[pltpu.VMEM((2, tm, D_MODEL), F32), pltpu.VMEM((tm, D_MODEL), BF16),
                        pltpu.VMEM((N_EXPERTS, 1), F32)],
        compiler_params=_params("arbitrary"),
        name="mix_route",
    )(xa, xb, o, c, gates, wo, wpw, bpw, wout, gffn, wr_hi, wr_lo, br, tri)


def _sc_workers():
    info = plsc.get_sparse_core_info()
    return info.num_cores, info.num_cores * info.num_subcores


def _sc_dispatch(h, dest, n_out):
    t_rows, width = h.shape
    nc, nw = _sc_workers()
    per_w = t_rows // nw
    step = SC_ROWS_PER_STEP
    assert per_w * nw == t_rows and per_w % step == 0
    mesh = plsc.VectorSubcoreMesh(core_axis_name="c", subcore_axis_name="s")

    @functools.partial(
        pl.kernel, mesh=mesh,
        out_type=jax.ShapeDtypeStruct((n_out, width), h.dtype),
        scratch_types=[pltpu.VMEM((step,), I32)] * TOP_K + [pltpu.VMEM((step, width), h.dtype)],
    )
    def scatter_rows(h_hbm, dest_hbm, out_hbm, i0, i1, i2, i3, rows_v):
        base = (lax.axis_index("s") * nc + lax.axis_index("c")) * per_w

        @pl.loop(0, per_w // step)
        def _(i):
            off = base + i * step
            pltpu.sync_copy(h_hbm.at[pl.ds(off, step)], rows_v)
            for j, idx_v in enumerate((i0, i1, i2, i3)):
                pltpu.sync_copy(dest_hbm.at[j, pl.ds(off, step)], idx_v)
                pltpu.sync_copy(rows_v, out_hbm.at[idx_v])

    return scatter_rows(h, dest)


def _sc_gather(table, dest, row0, rows):
    width = table.shape[1]
    nc, nw = _sc_workers()
    per_w = rows // nw
    step = SC_ROWS_PER_STEP
    assert per_w * nw == rows and per_w % step == 0
    mesh = plsc.VectorSubcoreMesh(core_axis_name="c", subcore_axis_name="s")

    @functools.partial(
        pl.kernel, mesh=mesh,
        out_type=jax.ShapeDtypeStruct((TOP_K * rows, width), table.dtype),
        scratch_types=[pltpu.VMEM((step,), I32), pltpu.VMEM((step, width), table.dtype),
                       pltpu.SemaphoreType.DMA],
    )
    def gather_rows(table_hbm, dest_hbm, out_hbm, idx_v, rows_v, sem):
        base = (lax.axis_index("s") * nc + lax.axis_index("c")) * per_w

        @pl.loop(0, per_w // step)
        def _(i):
            off = base + i * step
            for j in range(TOP_K):
                pltpu.sync_copy(dest_hbm.at[j, pl.ds(row0 + off, step)], idx_v)
                pltpu.async_copy(table_hbm.at[idx_v], rows_v, sem).wait()
                pltpu.sync_copy(rows_v, out_hbm.at[pl.ds(j * rows + off, step)])

    return gather_rows(table, dest)


def _expert_body(bexp_ref, nused_ref, valid_ref, xs_ref, wgu_ref, bgu_ref, wd_ref, bd_ref, o_ref,
                 wgu_bf, wd_bf):
    n = pl.program_id(0)
    active = n < nused_ref[0]
    new_expert = (n == 0) | (bexp_ref[n] != bexp_ref[jnp.maximum(n - 1, 0)])

    @pl.when(active & new_expert)
    def _():
        wgu_bf[...] = wgu_ref[...].astype(BF16)
        wd_bf[...] = wd_ref[...].astype(BF16)

    sub = EXPERT_SUB_ROWS
    for r0 in range(0, xs_ref.shape[0], sub):
        rs = slice(r0, r0 + sub)
        live = active & (valid_ref[n] > r0)

        @pl.when(live)
        def _():
            row = r0 + lax.broadcasted_iota(I32, (sub, 1), 0)
            x = _unpack_halves(jnp.where(row < valid_ref[n], xs_ref[rs, :], 0), BF16)
            gu = jnp.dot(x, wgu_bf[...], preferred_element_type=F32) + bgu_ref[...]
            g = jnp.minimum(gu[:, 0:D_FF], SWIGLU_LIMIT)
            u = jnp.clip(gu[:, D_FF:2 * D_FF], -SWIGLU_LIMIT, SWIGLU_LIMIT)
            act = g * jax.nn.sigmoid(SWIGLU_ALPHA * g) * (u + 1.0)
            out = jnp.dot(act.astype(BF16), wd_bf[...], preferred_element_type=F32) + bd_ref[...]
            o_ref[rs, :] = _pack_halves(out.astype(BF16))

        @pl.when(jnp.logical_not(live))
        def _():
            o_ref[rs, :] = jnp.zeros((sub, o_ref.shape[1]), o_ref.dtype)


def _experts(block_exp, n_used, block_valid, xs, wgu, bgu, wd, bd, bm):
    P = xs.shape[0]
    nblk = P // bm
    xmap = lambda n, be, nu, bv: (jnp.minimum(n, nu[0] - 1), 0)
    emap = lambda n, be, nu, bv: (be[n], 0, 0)
    grid_spec = pltpu.PrefetchScalarGridSpec(
        num_scalar_prefetch=3,
        grid=(nblk,),
        in_specs=[
            pl.BlockSpec((bm, PACKED), xmap),
            pl.BlockSpec((None, D_MODEL, 2 * D_FF), emap),
            pl.BlockSpec((None, 1, 2 * D_FF), emap),
            pl.BlockSpec((None, D_FF, D_MODEL), emap),
            pl.BlockSpec((None, 1, D_MODEL), emap),
        ],
        out_specs=pl.BlockSpec((bm, PACKED), lambda n, be, nu, bv: (n, 0)),
        scratch_shapes=[pltpu.VMEM((D_MODEL, 2 * D_FF), BF16), pltpu.VMEM((D_FF, D_MODEL), BF16)],
    )
    return pl.pallas_call(
        _expert_body,
        grid_spec=grid_spec,
        out_shape=jax.ShapeDtypeStruct((P, PACKED), I32),
        compiler_params=_params("arbitrary"),
        name="experts",
    )(block_exp, n_used, block_valid, xs, wgu, bgu, wd, bd)


def _combine_body(x1_ref, g0_ref, g1_ref, g2_ref, g3_ref, gate_ref, gfin_ref, y_ref):
    tf = x1_ref.shape[0]
    gate_rows = jnp.concatenate(
        [gate_ref[...], jnp.zeros((LANES - TOP_K, tf), F32)], axis=0)
    for c in range(tf // LANES):
        rs = slice(c * LANES, (c + 1) * LANES)
        gate = gate_rows[:, rs].T
        y = x1_ref[rs, :]
        for j, g_ref in enumerate((g0_ref, g1_ref, g2_ref, g3_ref)):
            y = y + gate[:, j:j + 1] * _unpack_halves(g_ref[rs, :], F32)
        ms = jnp.mean(y * y, axis=-1, keepdims=True)
        y_ref[rs, :] = y * lax.rsqrt(ms + NORM_EPS) * gfin_ref[...]


def _combine(x1, gathered, gate_t, gfin, row0, rows, tf):
    nt = rows // tf
    t0 = row0 // tf
    row = lambda i: (t0 + i, 0)
    choice = lambda j: pl.BlockSpec((tf, PACKED), lambda i: (j * nt + i, 0))
    return pl.pallas_call(
        _combine_body,
        grid=(nt,),
        in_specs=[pl.BlockSpec((tf, D_MODEL), row)] + [choice(j) for j in range(TOP_K)] + [
            pl.BlockSpec((TOP_K, tf), lambda i: (0, t0 + i)),
            pl.BlockSpec((1, D_MODEL), lambda i: (0, 0)),
        ],
        out_specs=pl.BlockSpec((tf, D_MODEL), lambda i: (i, 0)),
        out_shape=jax.ShapeDtypeStruct((rows, D_MODEL), F32),
        compiler_params=_params("parallel"),
        name="combine",
    )(x1, gathered, gathered, gathered, gathered, gate_t, gfin)


def _permute_in_proj(w):
    lead = w.shape[:-1]
    half = HEAD_DIM // 2
    q = w[..., :Q_COLS].reshape(*lead, N_Q_HEADS // 2, 2, 2, half)
    q = jnp.swapaxes(q, -3, -2).reshape(*lead, Q_COLS)
    k = w[..., Q_COLS:Q_COLS + KV_COLS].reshape(*lead, N_KV_HEADS, 2, 1, half)
    k = jnp.broadcast_to(k, (*lead, N_KV_HEADS, 2, 2, half)).reshape(*lead, KV_DUP_COLS)
    v = w[..., Q_COLS + KV_COLS:Q_COLS + 2 * KV_COLS]
    return jnp.concatenate([q, k, w[..., Q_COLS + 2 * KV_COLS:]], axis=-1), v


def _rope_tables(n_pos):
    half = HEAD_DIM // 2
    inv_freq = 1.0 / (ROPE_THETA ** (jnp.arange(half, dtype=F32) * (2.0 / HEAD_DIM)))
    ang = jnp.arange(n_pos, dtype=F32)[:, None] * inv_freq[None, :]
    cos = jnp.tile(jnp.cos(ang), (1, LANES // half))
    sin = jnp.tile(jnp.sin(ang), (1, LANES // half))
    sign = jnp.where(jnp.arange(LANES) < LANES // 2, -1.0, 1.0).astype(F32)
    return cos, sin * sign[None, :]


def _tiles(geo):
    unit = min(geo.len_a, geo.len_b)
    tile = min(512, unit)
    return dict(tm=tile, tc=tile, tf=tile, bm=2 * EXPERT_SUB_ROWS)


def kernel(x_prompt, x_sample, norm_mix_g, w_in, b_in, attn_sink, w_o_attn, conv_dw_w, conv_dw_b,
           conv_ln_g, conv_ln_b, w_pw2, b_pw2, w_out, norm_ffn_g, w_router, b_router, w_gu, b_gu,
           w_down, b_down, norm_final_g):
    assert w_in.shape[0] == 1, "single trunk layer"
    geo = Geo(x_prompt.shape[0], x_prompt.shape[1], x_sample.shape[0], x_sample.shape[1])
    T = geo.total
    ts = _tiles(geo)
    xa = x_prompt.reshape(-1, D_MODEL)
    xb = x_sample.reshape(-1, D_MODEL)

    w_perm, w_v = _permute_in_proj(w_in[0])
    b_perm, b_v = _permute_in_proj(b_in)
    cos_t, sin_t = _rope_tables(max(geo.len_a, geo.len_b))

    q, kd, vt, glu, gates = _in_proj(xa, xb, norm_mix_g, w_perm.astype(BF16), b_perm,
                                     w_v.T.astype(BF16), b_v.T, cos_t, sin_t, geo, ts["tm"])
    attn = _attention(q, kd, vt, attn_sink[0], geo)
    w_rep = jnp.repeat(conv_dw_w[0], SUBLANES, axis=0)
    conv = _conv_branch(glu, w_rep, conv_dw_b, conv_ln_g, conv_ln_b, geo, ts["tc"])

    wr_t = w_router[0].T
    wr_hi = wr_t.astype(BF16)
    wr_lo = (wr_t - wr_hi.astype(F32)).astype(BF16)
    tri = jnp.triu(jnp.ones((ts["tm"], ts["tm"]), BF16), 1)
    x1, h2, idx, gate_t, rank, counts = _mix_route(
        xa, xb, attn, conv, gates, w_o_attn[0].astype(BF16), w_pw2[0].astype(BF16), b_pw2,
        w_out[0].astype(BF16), norm_ffn_g, wr_hi, wr_lo, b_router[0][:, None], tri, geo, ts["tm"])

    bm = ts["bm"]
    n_blocks = (T * TOP_K) // bm + N_EXPERTS
    cnt = counts[:, 0].astype(I32)
    padded = ((cnt + bm - 1) // bm) * bm
    pad_end = jnp.cumsum(padded)
    pad_start = pad_end - padded
    expert_ids = jnp.arange(N_EXPERTS, dtype=I32)
    dest = rank + jnp.sum(jnp.where(idx[None] == expert_ids[:, None, None],
                                    pad_start[:, None, None], 0), axis=0)
    block_start = jnp.arange(n_blocks, dtype=I32) * bm
    block_exp = jnp.minimum(jnp.sum((pad_end[None, :] <= block_start[:, None]).astype(I32), axis=1),
                            N_EXPERTS - 1)
    n_used = (pad_end[-1:] // bm).astype(I32)
    seg_end = jnp.sum(jnp.where(block_exp[:, None] == expert_ids[None, :],
                                (pad_start + cnt)[None, :], 0), axis=1)
    block_valid = jnp.clip(seg_end - block_start, 0, bm).astype(I32)

    xs = _sc_dispatch(h2, dest, n_blocks * bm)
    ys = _experts(block_exp, n_used, block_valid, xs, w_gu[0], b_gu[0][:, None, :],
                  w_down[0], b_down[0][:, None, :], bm)
    outs = []
    for row0, rows in ((0, geo.rows_a), (geo.rows_a, T - geo.rows_a)):
        gathered = _sc_gather(ys, dest, row0, rows)
        outs.append(_combine(x1, gathered, gate_t, norm_final_g[None, :], row0, rows, ts["tf"]))
    return (outs[0].reshape(x_prompt.shape), outs[1].reshape(x_sample.shape))
```
